```python
import math
import jax, jax.numpy as jnp
from jax import lax
import numpy as np

D_MODEL = 1024
BATCH = 16
SEQ = 4096
DEPTH = 1
DEC_BATCH = 8
DEC_SEQ = 64
PAST_LEN = 4096

CHUNK = 64
LEFT_CHUNKS = 8
BAND = LEFT_CHUNKS * CHUNK
HEAD_DIM = 64
MIX_WIDTH = D_MODEL
A_WIDTH = MIX_WIDTH // 2
B_WIDTH = MIX_WIDTH - A_WIDTH
N_HEADS_A = A_WIDTH // HEAD_DIM
N_HEADS_B = B_WIDTH // (2 * HEAD_DIM)
REL_CLIP = 128
ROT_DIM = HEAD_DIM // 4
ROPE_THETA = 500000.0
Q_BLOCK = 128
MEM_LEN = 256
N_HEADS_M = 4
HEAD_DIM_M = D_MODEL // N_HEADS_M
N_EXPERTS = 32
TOP_K = 4
EXPERT_FF = D_MODEL
SWIGLU_LIMIT = 7.0
SWIGLU_ALPHA = 1.702
MOE_BLOCK = 256
RMS_EPS = 1e-5
NEG_INF = -1e30

kernel_name = 'hybrid_streaming_chunkband_diffattn_moe_step'


def rms_norm(x, g):
    xf = x.astype(jnp.float32)
    y = xf * lax.rsqrt(jnp.mean(xf * xf, axis=-1, keepdims=True) + RMS_EPS)
    return (y * g.astype(jnp.float32)).astype(x.dtype)


def partial_rope(x, pos):
    half = ROT_DIM // 2
    inv_freq = jnp.power(ROPE_THETA, -jnp.arange(half, dtype=jnp.float32) / half)
    ang = pos.astype(jnp.float32)[:, None] * inv_freq[None, :]
    cos = jnp.cos(ang)[None, :, None, :]
    sin = jnp.sin(ang)[None, :, None, :]
    xr = x[..., :ROT_DIM].astype(jnp.float32)
    x1, x2 = xr[..., :half], xr[..., half:]
    rot = jnp.concatenate([x1 * cos - x2 * sin, x2 * cos + x1 * sin], axis=-1)
    return jnp.concatenate([rot.astype(x.dtype), x[..., ROT_DIM:]], axis=-1)


def split_mixers(z):
    B, L, _ = z.shape
    aq = z[..., 0:A_WIDTH].reshape(B, L, N_HEADS_A, HEAD_DIM)
    ak = z[..., A_WIDTH:2 * A_WIDTH].reshape(B, L, N_HEADS_A, HEAD_DIM)
    av = z[..., 2 * A_WIDTH:3 * A_WIDTH].reshape(B, L, N_HEADS_A, HEAD_DIM)
    off = 3 * A_WIDTH
    bq = z[..., off:off + B_WIDTH].reshape(B, L, 2 * N_HEADS_B, HEAD_DIM)
    bk = z[..., off + B_WIDTH:off + 2 * B_WIDTH].reshape(B, L, 2 * N_HEADS_B, HEAD_DIM)
    bv = z[..., off + 2 * B_WIDTH:off + 3 * B_WIDTH].reshape(B, L, N_HEADS_B, 2 * HEAD_DIM)
    return aq, ak, av, bq, bk, bv


def band_attention(q, k, v, q_pos, k_pos, rel_bias):
    s = jnp.einsum('bqhd,bkhd->bhqk', q, k).astype(jnp.float32) * HEAD_DIM ** -0.5
    rel = jnp.clip(q_pos[:, None] - k_pos[None, :], -REL_CLIP, REL_CLIP) + REL_CLIP
    s = s + rel_bias[:, rel].astype(jnp.float32)[None]
    qc = (q_pos // CHUNK)[:, None]
    kc = (k_pos // CHUNK)[None, :]
    allowed = (kc <= qc) & (kc >= qc - LEFT_CHUNKS) & (k_pos[None, :] >= 0)
    s = jnp.where(allowed[None, None], s, NEG_INF)
    p = jax.nn.softmax(s, axis=-1).astype(v.dtype)
    return jnp.einsum('bhqk,bkhd->bqhd', p, v)


def chunk_band_prompt(q, k, v, rel_bias):
    B, S, H, d = q.shape
    n_chunks = S // CHUNK
    kp = jnp.pad(k, ((0, 0), (BAND, 0), (0, 0), (0, 0)))
    vp = jnp.pad(v, ((0, 0), (BAND, 0), (0, 0), (0, 0)))

    def one_chunk(c):
        start = c * CHUNK
        qb = lax.dynamic_slice_in_dim(q, start, CHUNK, axis=1)
        kb = lax.dynamic_slice_in_dim(kp, start, BAND + CHUNK, axis=1)
        vb = lax.dynamic_slice_in_dim(vp, start, BAND + CHUNK, axis=1)
        q_pos = start + jnp.arange(CHUNK, dtype=jnp.int32)
        k_pos = start - BAND + jnp.arange(BAND + CHUNK, dtype=jnp.int32)
        return band_attention(qb, kb, vb, q_pos, k_pos, rel_bias)

    out = lax.map(one_chunk, jnp.arange(n_chunks, dtype=jnp.int32))
    return jnp.moveaxis(out, 0, 1).reshape(B, S, H, d)


def diff_attention_block(q, k, v, q_pos, k_pos, lam):
    s = jnp.einsum('bqhd,bkhd->bhqk', q, k).astype(jnp.float32) * HEAD_DIM ** -0.5
    allowed = (k_pos[None, :] // CHUNK) <= (q_pos[:, None] // CHUNK)
    s = jnp.where(allowed[None, None], s, NEG_INF)
    p = jax.nn.softmax(s, axis=-1)
    Bq, _, Lq, Lk = p.shape
    p = p.reshape(Bq, N_HEADS_B, 2, Lq, Lk)
    a = p[:, :, 0] - lam * p[:, :, 1]
    return jnp.einsum('bhqk,bkhe->bqhe', a.astype(v.dtype), v)


def diff_attention_prompt(q, k, v, lam):
    B, S = q.shape[:2]
    n_blocks = S // Q_BLOCK
    k_pos = jnp.arange(S, dtype=jnp.int32)

    def one_block(i):
        start = i * Q_BLOCK
        qb = lax.dynamic_slice_in_dim(q, start, Q_BLOCK, axis=1)
        q_pos = start + jnp.arange(Q_BLOCK, dtype=jnp.int32)
        return diff_attention_block(qb, k, v, q_pos, k_pos, lam)

    out = lax.map(one_block, jnp.arange(n_blocks, dtype=jnp.int32))
    return jnp.moveaxis(out, 0, 1).reshape(B, S, N_HEADS_B, 2 * HEAD_DIM)


def diff_lambda(lq1, lk1, lq2, lk2, lam_init):
    f32 = jnp.float32
    return (jnp.exp(jnp.sum(lq1.astype(f32) * lk1.astype(f32)))
            - jnp.exp(jnp.sum(lq2.astype(f32) * lk2.astype(f32))) + lam_init)


def merge_heads(oa, ob, g_sub, lam_init):
    B, L = oa.shape[:2]
    ob = rms_norm(ob, g_sub) * (1.0 - lam_init)
    return jnp.concatenate([oa.reshape(B, L, A_WIDTH), ob.reshape(B, L, B_WIDTH)], axis=-1)


def memory_kv(mem, g, w_k, w_v):
    B, M, _ = mem.shape
    m = rms_norm(mem, g)
    k = (m @ w_k).reshape(B, M, N_HEADS_M, HEAD_DIM_M)
    v = (m @ w_v).reshape(B, M, N_HEADS_M, HEAD_DIM_M)
    return k, v


def memory_attend(h, mk, mv, w_q, w_o):
    B, L, _ = h.shape
    q = (h @ w_q).reshape(B, L, N_HEADS_M, HEAD_DIM_M)
    s = jnp.einsum('bqhd,bkhd->bhqk', q, mk).astype(jnp.float32) * HEAD_DIM_M ** -0.5
    p = jax.nn.softmax(s, axis=-1).astype(mv.dtype)
    o = jnp.einsum('bhqk,bkhd->bqhd', p, mv).reshape(B, L, N_HEADS_M * HEAD_DIM_M)
    return o @ w_o


def moe_ffn(h, w_router, b_router, w_gate_up, b_gate_up, w_down, b_down):
    B, L, D = h.shape
    xt = h.reshape(B * L, D)
    n = B * L
    logits = (xt @ w_router).astype(jnp.float32) + b_router.astype(jnp.float32)
    top_val, top_idx = lax.top_k(logits, TOP_K)
    gates = jax.nn.softmax(top_val, axis=-1)
    n_slots = n * TOP_K
    e_flat = top_idx.reshape(-1)
    tok_flat = jnp.repeat(jnp.arange(n, dtype=jnp.int32), TOP_K)
    order = jnp.argsort(e_flat)
    e_sorted = e_flat[order]
    tok_sorted = tok_flat[order]
    gate_sorted = gates.reshape(-1)[order]
    counts = jnp.bincount(e_flat, length=N_EXPERTS)
    padded = (counts + MOE_BLOCK - 1) // MOE_BLOCK * MOE_BLOCK
    start_sorted = jnp.cumsum(counts) - counts
    pad_end = jnp.cumsum(padded)
    pad_start = pad_end - padded
    dest = pad_start[e_sorted] + (jnp.arange(n_slots) - start_sorted[e_sorted])
    n_blocks = -(-n_slots // MOE_BLOCK) + N_EXPERTS
    cap = n_blocks * MOE_BLOCK
    row_tok = jnp.full((cap,), n, jnp.int32).at[dest].set(tok_sorted)
    x_pad = jnp.concatenate([xt, jnp.zeros((1, D), xt.dtype)], axis=0)
    x_buf = x_pad[row_tok].reshape(n_blocks, MOE_BLOCK, D)
    block_start = jnp.arange(n_blocks, dtype=pad_end.dtype) * MOE_BLOCK
    block_expert = jnp.minimum(jnp.searchsorted(pad_end, block_start, side='right'), N_EXPERTS - 1)

    def expert_block(args):
        xb, e = args
        gu = xb @ w_gate_up[e] + b_gate_up[e]
        gate = jnp.minimum(gu[:, :EXPERT_FF], SWIGLU_LIMIT)
        up = jnp.clip(gu[:, EXPERT_FF:], -SWIGLU_LIMIT, SWIGLU_LIMIT)
        act = (up + 1.0) * (gate * jax.nn.sigmoid(SWIGLU_ALPHA * gate))
        return act @ w_down[e] + b_down[e]

    y_buf = lax.map(expert_block, (x_buf, block_expert)).reshape(cap, D)
    y_slot = y_buf[dest] * gate_sorted[:, None].astype(y_buf.dtype)
    out = jax.ops.segment_sum(y_slot, tok_sorted, num_segments=n)
    return out.reshape(B, L, D)


def setup_inputs(seed: int = 0) -> dict:
    key = jax.random.key(seed)
    ks = iter(jax.random.split(key, 48))
    f32 = jnp.float32

    def nrm(shape, scale):
        return scale * jax.random.normal(next(ks), shape, f32)

    def gain(shape):
        return 1.0 + 0.01 * jax.random.normal(next(ks), shape, f32)

    a_len = min(BAND, PAST_LEN)
    return {
        'x_prompt': nrm((BATCH, SEQ, D_MODEL), 1.0),
        'x_sample': nrm((DEC_BATCH, DEC_SEQ, D_MODEL), 1.0),
        'mem_prompt': nrm((BATCH, MEM_LEN, D_MODEL), 1.0),
        'cache_a_k': nrm((DEPTH, DEC_BATCH, a_len, N_HEADS_A, HEAD_DIM), 1.0),
        'cache_a_v': nrm((DEPTH, DEC_BATCH, a_len, N_HEADS_A, HEAD_DIM), 1.0),
        'cache_b_k': nrm((DEPTH, DEC_BATCH, PAST_LEN, 2 * N_HEADS_B, HEAD_DIM), 1.0),
        'cache_b_v': nrm((DEPTH, DEC_BATCH, PAST_LEN, N_HEADS_B, 2 * HEAD_DIM), 1.0),
        'cache_mem_k': nrm((DEPTH, DEC_BATCH, MEM_LEN, N_HEADS_M, HEAD_DIM_M), 1.0),
        'cache_mem_v': nrm((DEPTH, DEC_BATCH, MEM_LEN, N_HEADS_M, HEAD_DIM_M), 1.0),
        'g_mix': gain((DEPTH, D_MODEL)),
        'w_in': nrm((DEPTH, D_MODEL, 3 * MIX_WIDTH), D_MODEL ** -0.5),
        'rel_bias': nrm((DEPTH, N_HEADS_A, 2 * REL_CLIP + 1), 0.1),
        'lam_q1': nrm((DEPTH, HEAD_DIM), 0.1),
        'lam_k1': nrm((DEPTH, HEAD_DIM), 0.1),
        'lam_q2': nrm((DEPTH, HEAD_DIM), 0.1),
        'lam_k2': nrm((DEPTH, HEAD_DIM), 0.1),
        'g_subln': gain((DEPTH, 2 * HEAD_DIM)),
        'w_out': nrm((DEPTH, MIX_WIDTH, D_MODEL), MIX_WIDTH ** -0.5),
        'g_xq': gain((DEPTH, D_MODEL)),
        'g_xmem': gain((DEPTH, D_MODEL)),
        'w_xq': nrm((DEPTH, D_MODEL, D_MODEL), D_MODEL ** -0.5),
        'w_xk': nrm((DEPTH, D_MODEL, D_MODEL), D_MODEL ** -0.5),
        'w_xv': nrm((DEPTH, D_MODEL, D_MODEL), D_MODEL ** -0.5),
        'w_xo': nrm((DEPTH, D_MODEL, D_MODEL), D_MODEL ** -0.5),
        'g_moe': gain((DEPTH, D_MODEL)),
        'w_router': nrm((DEPTH, D_MODEL, N_EXPERTS), D_MODEL ** -0.5),
        'b_router': nrm((DEPTH, N_EXPERTS), 0.01),
        'w_gate_up': nrm((DEPTH, N_EXPERTS, D_MODEL, 2 * EXPERT_FF), D_MODEL ** -0.5),
        'b_gate_up': nrm((DEPTH, N_EXPERTS, 2 * EXPERT_FF), 0.01),
        'w_down': nrm((DEPTH, N_EXPERTS, EXPERT_FF, D_MODEL), EXPERT_FF ** -0.5),
        'b_down': nrm((DEPTH, N_EXPERTS, D_MODEL), 0.01),
        'g_final': gain((D_MODEL,)),
    }


def reference(x_prompt, x_sample, mem_prompt, cache_a_k, cache_a_v, cache_b_k, cache_b_v,
              cache_mem_k, cache_mem_v, g_mix, w_in, rel_bias, lam_q1, lam_k1, lam_q2, lam_k2,
              g_subln, w_out, g_xq, g_xmem, w_xq, w_xk, w_xv, w_xo, g_moe, w_router, b_router,
              w_gate_up, b_gate_up, w_down, b_down, g_final):
    S = x_prompt.shape[1]
    L = x_sample.shape[1]
    past_len = cache_b_k.shape[2]
    a_len = cache_a_k.shape[2]
    keep_a = min(BAND, S)
    pos_p = jnp.arange(S, dtype=jnp.int32)
    pos_s = past_len + jnp.arange(L, dtype=jnp.int32)
    k_pos_a_s = past_len - a_len + jnp.arange(a_len + L, dtype=jnp.int32)
    k_pos_b_s = jnp.arange(past_len + L, dtype=jnp.int32)

    hp, hs = x_prompt, x_sample
    pa_k, pa_v, pb_k, pb_v, pm_k, pm_v = [], [], [], [], [], []
    sa_k, sa_v, sb_k, sb_v = [], [], [], []
    for l in range(DEPTH):
        lam_init = 0.8 - 0.6 * math.exp(-0.3 * l)
        lam = diff_lambda(lam_q1[l], lam_k1[l], lam_q2[l], lam_k2[l], lam_init)

        aq, ak, av, bq, bk, bv = split_mixers(rms_norm(hp, g_mix[l]) @ w_in[l])
        bq = partial_rope(bq, pos_p)
        bk = partial_rope(bk, pos_p)
        oa = chunk_band_prompt(aq, ak, av, rel_bias[l])
        ob = diff_attention_prompt(bq, bk, bv, lam)
        hp = hp + merge_heads(oa, ob, g_subln[l], lam_init) @ w_out[l]
        mk, mv = memory_kv(mem_prompt, g_xmem[l], w_xk[l], w_xv[l])
        hp = hp + memory_attend(rms_norm(hp, g_xq[l]), mk, mv, w_xq[l], w_xo[l])
        hp = hp + moe_ffn(rms_norm(hp, g_moe[l]), w_router[l], b_router[l], w_gate_up[l],
                          b_gate_up[l], w_down[l], b_down[l])
        pa_k.append(ak[:, S - keep_a:])
        pa_v.append(av[:, S - keep_a:])
        pb_k.append(bk)
        pb_v.append(bv)
        pm_k.append(mk)
        pm_v.append(mv)

        aq, ak, av, bq, bk, bv = split_mixers(rms_norm(hs, g_mix[l]) @ w_in[l])
        bq = partial_rope(bq, pos_s)
        bk = partial_rope(bk, pos_s)
        ka = jnp.concatenate([cache_a_k[l], ak], axis=1)
        va = jnp.concatenate([cache_a_v[l], av], axis=1)
        oa = band_attention(aq, ka, va, pos_s, k_pos_a_s, rel_bias[l])
        kb = jnp.concatenate([cache_b_k[l], bk], axis=1)
        vb = jnp.concatenate([cache_b_v[l], bv], axis=1)
        ob = diff_attention_block(bq, kb, vb, pos_s, k_pos_b_s, lam)
        hs = hs + merge_heads(oa, ob, g_subln[l], lam_init) @ w_out[l]
        hs = hs + memory_attend(rms_norm(hs, g_xq[l]), cache_mem_k[l], cache_mem_v[l], w_xq[l], w_xo[l])
        hs = hs + moe_ffn(rms_norm(hs, g_moe[l]), w_router[l], b_router[l], w_gate_up[l],
                          b_gate_up[l], w_down[l], b_down[l])
        sa_k.append(ak)
        sa_v.append(av)
        sb_k.append(bk)
        sb_v.append(bv)

    y_prompt = rms_norm(hp, g_final)
    y_sample = rms_norm(hs, g_final)
    return (y_prompt, y_sample, jnp.stack(pa_k), jnp.stack(pa_v), jnp.stack(pb_k), jnp.stack(pb_v),
            jnp.stack(pm_k), jnp.stack(pm_v), jnp.stack(sa_k), jnp.stack(sa_v), jnp.stack(sb_k),
            jnp.stack(sb_v))
```

```python
import functools
import math

import numpy as np
import jax
import jax.numpy as jnp
from jax import lax
from jax.experimental import pallas as pl
from jax.experimental.pallas import tpu as pltpu

F32 = jnp.float32
BF16 = jnp.bfloat16

D_MODEL = 1024
CHUNK = 64
LEFT_CHUNKS = 8
BAND = LEFT_CHUNKS * CHUNK
HEAD_DIM = 64
A_WIDTH = 512
B_WIDTH = 512
N_HEADS_A = 8
N_HEADS_B = 4
REL_CLIP = 128
ROT_DIM = 16
ROPE_THETA = 500000.0
N_HEADS_M = 4
HEAD_DIM_M = 256
N_EXPERTS = 32
TOP_K = 4
EXPERT_FF = 1024
SWIGLU_LIMIT = 7.0
SWIGLU_ALPHA = 1.702
RMS_EPS = 1e-5
NEG_INF = -1e30
LAM_INIT = 0.8 - 0.6 * math.exp(-0.3 * 0)

LANES = 128
A_QBLK = 2 * CHUNK
A_WIN = BAND + A_QBLK
B_QBLK = 256
MOE_ROWS = 256
VMEM_LIMIT = 48 * 1024 * 1024


def _cparams(sem):
    return pltpu.CompilerParams(dimension_semantics=sem, vmem_limit_bytes=VMEM_LIMIT)


def _rms(x, g):
    return x * lax.rsqrt(jnp.mean(x * x, axis=-1, keepdims=True) + RMS_EPS) * g


def _dot_nt(a, b):
    return lax.dot_general(a, b, (((1,), (1,)), ((), ())), preferred_element_type=F32)


def _inproj_kernel(x_ref, g_ref, w_ref, cos_ref, sa_ref, sb_ref,
                   aq_ref, ak_ref, av_ref, bq_ref, bk_ref, bv_ref,
                   akf_ref, avf_ref, bkf_ref, bvf_ref):
    xn = _rms(x_ref[...], g_ref[...]).astype(BF16)

    def seg(s):
        return jnp.dot(xn, w_ref[:, s * A_WIDTH:(s + 1) * A_WIDTH], preferred_element_type=F32)

    def rope(z):
        cols = []
        for c in range(z.shape[1] // LANES):
            zc = z[:, c * LANES:(c + 1) * LANES]
            cols.append(zc * cos_ref[...] + pltpu.roll(zc, 8, 1) * sa_ref[...]
                        + pltpu.roll(zc, LANES - 8, 1) * sb_ref[...])
        return jnp.concatenate(cols, axis=1)

    scale = HEAD_DIM ** -0.5
    aq_ref[...] = (seg(0) * scale).astype(BF16)
    ak = seg(1)
    ak_ref[...] = ak.astype(BF16)
    akf_ref[...] = ak
    av = seg(2)
    av_ref[...] = av.astype(BF16)
    avf_ref[...] = av
    bq_ref[...] = (rope(seg(3)) * scale).astype(BF16)
    bk = rope(seg(4))
    bk_ref[...] = bk.astype(BF16)
    bkf_ref[...] = bk
    bv = seg(5)
    bv_ref[...] = bv.astype(BF16)
    bvf_ref[...] = bv


def _inproj(x2d, g, w_bf, cos_t, sa_t, sb_t, *, tm, n_pos_blocks, tail_div):
    r = x2d.shape[0]
    n_tiles = r // tm
    row = lambda i: (i, 0)
    const = lambda i: (0, 0)
    pos = lambda i: (i % n_pos_blocks, 0)
    tail = lambda i: (i // tail_div, 0)
    bf = jax.ShapeDtypeStruct((r, A_WIDTH), BF16)
    f_all = jax.ShapeDtypeStruct((r, A_WIDTH), F32)
    f_tail = jax.ShapeDtypeStruct((r // tail_div, A_WIDTH), F32)
    blk = lambda im: pl.BlockSpec((tm, A_WIDTH), im)
    return pl.pallas_call(
        _inproj_kernel,
        grid=(n_tiles,),
        in_specs=[pl.BlockSpec((tm, D_MODEL), row), pl.BlockSpec((1, D_MODEL), const),
                  pl.BlockSpec((D_MODEL, 3 * D_MODEL), const),
                  pl.BlockSpec((tm, LANES), pos), pl.BlockSpec((tm, LANES), pos),
                  pl.BlockSpec((tm, LANES), pos)],
        out_specs=[blk(row)] * 6 + [blk(tail), blk(tail), blk(row), blk(row)],
        out_shape=[bf] * 6 + [f_tail, f_tail, f_all, f_all],
        compiler_params=_cparams(("arbitrary",)),
        name="inproj",
    )(x2d, g, w_bf, cos_t, sa_t, sb_t)


def _rope_tables(positions):
    half = ROT_DIM // 2
    lane = np.arange(LANES) % HEAD_DIM
    inv_freq = jnp.power(ROPE_THETA, -jnp.arange(half, dtype=F32) / half)
    ang = positions.astype(F32)[:, None] * inv_freq[None, :]
    cos = jnp.cos(ang)[:, lane % half]
    sin = jnp.sin(ang)[:, lane % half]
    in_rot = jnp.asarray(lane < ROT_DIM)[None, :]
    first = jnp.asarray(lane < half)[None, :]
    cos_t = jnp.where(in_rot, cos, 1.0)
    sa_t = jnp.where(in_rot & ~first, sin, 0.0)
    sb_t = jnp.where(first, -sin, 0.0)
    return cos_t.astype(F32), sa_t.astype(F32), sb_t.astype(F32)


def _band_table(rel_bias):
    r = np.arange(A_QBLK)[:, None]
    j = np.arange(BAND + A_WIN)[None, :]
    rel = np.clip(BAND + r - j, -REL_CLIP, REL_CLIP) + REL_CLIP
    dchunk = LEFT_CHUNKS + r // CHUNK - j // CHUNK
    allowed = (dchunk >= 0) & (dchunk <= LEFT_CHUNKS)
    tab = jnp.where(jnp.asarray(allowed)[None], rel_bias.astype(F32)[:, rel], NEG_INF)
    return tab.reshape(N_HEADS_A, A_QBLK, -1, LANES).transpose(0, 2, 1, 3)


def _table_cols(tab_ref, h, first_blk, n_blk, rows):
    return jnp.concatenate([tab_ref[h, first_blk + c, 0:rows, :] for c in range(n_blk)], axis=1)


def _softmax_pv(s_list, v_list):
    m = s_list[0].max(axis=-1, keepdims=True)
    for s in s_list[1:]:
        m = jnp.maximum(m, s.max(axis=-1, keepdims=True))
    l = None
    o = None
    for s, v in zip(s_list, v_list):
        p = jnp.exp(s - m)
        ls = p.sum(axis=-1, keepdims=True)
        os_ = jnp.dot(p.astype(BF16), v, preferred_element_type=F32)
        l = ls if l is None else l + ls
        o = os_ if o is None else o + os_
    return o / l


def _attn_a_prompt_kernel(q_ref, k_ref, v_ref, tab_ref, o_ref):
    i = pl.program_id(1)
    n_left = BAND // A_QBLK
    start = pl.multiple_of(jnp.maximum(i - n_left, 0) * A_QBLK, A_QBLK)
    off_blk = jnp.maximum(n_left - i, 0) * (A_QBLK // LANES)
    kw = k_ref[0, pl.ds(start, A_WIN), :]
    vw = v_ref[0, pl.ds(start, A_WIN), :]
    q = q_ref[0]
    outs = []
    for h in range(N_HEADS_A):
        sl = slice(h * HEAD_DIM, (h + 1) * HEAD_DIM)
        s = _dot_nt(q[:, sl], kw[:, sl]) + _table_cols(tab_ref, h, off_blk, A_WIN // LANES, A_QBLK)
        outs.append(_softmax_pv([s], [vw[:, sl]]))
    o_ref[0] = jnp.concatenate(outs, axis=1).astype(BF16)


def _attn_a_prompt(aq, ak, av, table):
    b, l, _ = aq.shape
    return pl.pallas_call(
        _attn_a_prompt_kernel,
        grid=(b, l // A_QBLK),
        in_specs=[pl.BlockSpec((1, A_QBLK, A_WIDTH), lambda bb, i: (bb, i, 0)),
                  pl.BlockSpec((1, l, A_WIDTH), lambda bb, i: (bb, 0, 0)),
                  pl.BlockSpec((1, l, A_WIDTH), lambda bb, i: (bb, 0, 0)),
                  pl.BlockSpec(table.shape, lambda bb, i: (0, 0, 0, 0))],
        out_specs=pl.BlockSpec((1, A_QBLK, A_WIDTH), lambda bb, i: (bb, i, 0)),
        out_shape=jax.ShapeDtypeStruct((b, l, A_WIDTH), BF16),
        compiler_params=_cparams(("arbitrary", "arbitrary")),
        name="attn_a_prompt",
    )(aq, ak, av, table)


def _attn_a_sample_kernel(q_ref, kc_ref, vc_ref, k_ref, v_ref, tab_ref, o_ref):
    q = q_ref[0]
    kc = kc_ref[0].astype(BF16)
    vc = vc_ref[0].astype(BF16)
    ko = k_ref[0]
    vo = v_ref[0]
    a_len = kc.shape[0]
    lq = q.shape[0]
    c0 = BAND - a_len
    outs = []
    for h in range(N_HEADS_A):
        sl = slice(h * HEAD_DIM, (h + 1) * HEAD_DIM)
        tab = _table_cols(tab_ref, h, 0, (BAND + A_QBLK) // LANES, lq)
        s_c = _dot_nt(q[:, sl], kc[:, sl]) + tab[:, c0:BAND]
        s_o = _dot_nt(q[:, sl], ko[:, sl]) + tab[:, BAND:BAND + lq]
        outs.append(_softmax_pv([s_c, s_o], [vc[:, sl], vo[:, sl]]))
    o_ref[0] = jnp.concatenate(outs, axis=1).astype(BF16)


def _attn_a_sample(aq, cache_k, cache_v, ak, av, table):
    b, l, _ = aq.shape
    a_len = cache_k.shape[1]
    bs = lambda n: pl.BlockSpec((1, n, A_WIDTH), lambda bb: (bb, 0, 0))
    return pl.pallas_call(
        _attn_a_sample_kernel,
        grid=(b,),
        in_specs=[bs(l), bs(a_len), bs(a_len), bs(l), bs(l),
                  pl.BlockSpec(table.shape, lambda bb: (0, 0, 0, 0))],
        out_specs=bs(l),
        out_shape=jax.ShapeDtypeStruct((b, l, A_WIDTH), BF16),
        compiler_params=_cparams(("arbitrary",)),
        name="attn_a_sample",
    )(aq, cache_k, cache_v, ak, av, table)


def _subnorm(o, g):
    return _rms(o, g) * (1.0 - LAM_INIT)


def _attn_b_prompt_kernel(lam_ref, q_ref, k_ref, v_ref, g_ref, o_ref, m_sc, l_sc, acc_sc):
    i = pl.program_id(2)
    tq = q_ref.shape[1]
    q = q_ref[0]
    qs = (q[:, :HEAD_DIM], q[:, HEAD_DIM:])
    m_sc[...] = jnp.full(m_sc.shape, NEG_INF, F32)
    l_sc[...] = jnp.zeros(l_sc.shape, F32)
    acc_sc[...] = jnp.zeros(acc_sc.shape, F32)

    def step(j, masked):
        kb = k_ref[0, pl.ds(pl.multiple_of(j * tq, tq), tq), :]
        vb = v_ref[0, pl.ds(pl.multiple_of(j * tq, tq), tq), :]
        for a in range(2):
            s = _dot_nt(qs[a], kb[:, a * HEAD_DIM:(a + 1) * HEAD_DIM])
            if masked:
                rq = lax.broadcasted_iota(jnp.int32, s.shape, 0) // CHUNK
                ck = lax.broadcasted_iota(jnp.int32, s.shape, 1) // CHUNK
                s = jnp.where(ck <= rq, s, NEG_INF)
            m_old = m_sc[a]
            m_new = jnp.maximum(m_old, s.max(axis=-1, keepdims=True))
            alpha = jnp.exp(m_old - m_new)
            p = jnp.exp(s - m_new)
            l_sc[a] = alpha * l_sc[a] + p.sum(axis=-1, keepdims=True)
            acc_sc[a] = alpha * acc_sc[a] + jnp.dot(p.astype(BF16), vb, preferred_element_type=F32)
            m_sc[a] = m_new

    def body(j, c):
        step(j, False)
        return c

    lax.fori_loop(0, i, body, 0)
    step(i, True)
    o = acc_sc[0] / l_sc[0] - lam_ref[0] * (acc_sc[1] / l_sc[1])
    o_ref[0] = _subnorm(o, g_ref[...]).astype(BF16)


def _attn_b_prompt(lam, bq, bk, bv, g_sub):
    b, l, _ = bq.shape
    tq = B_QBLK
    hw = 2 * HEAD_DIM
    return pl.pallas_call(
        _attn_b_prompt_kernel,
        grid_spec=pltpu.PrefetchScalarGridSpec(
            num_scalar_prefetch=1,
            grid=(b, N_HEADS_B, l // tq),
            in_specs=[pl.BlockSpec((1, tq, hw), lambda bb, h, i, lam_: (bb, i, h)),
                      pl.BlockSpec((1, l, hw), lambda bb, h, i, lam_: (bb, 0, h)),
                      pl.BlockSpec((1, l, hw), lambda bb, h, i, lam_: (bb, 0, h)),
                      pl.BlockSpec((1, hw), lambda bb, h, i, lam_: (0, 0))],
            out_specs=pl.BlockSpec((1, tq, hw), lambda bb, h, i, lam_: (bb, i, h)),
            scratch_shapes=[pltpu.VMEM((2, tq, 1), F32), pltpu.VMEM((2, tq, 1), F32),
                            pltpu.VMEM((2, tq, hw), F32)]),
        out_shape=jax.ShapeDtypeStruct((b, l, B_WIDTH), BF16),
        compiler_params=_cparams(("arbitrary", "arbitrary", "arbitrary")),
        name="attn_b_prompt",
    )(lam, bq, bk, bv, g_sub)


def _attn_b_sample_kernel(lam_ref, q_ref, kc_ref, vc_ref, k_ref, v_ref, g_ref, o_ref):
    q = q_ref[0]
    kc = kc_ref[0].astype(BF16)
    vc = vc_ref[0].astype(BF16)
    ko = k_ref[0]
    vo = v_ref[0]
    outs = []
    for a in range(2):
        sl = slice(a * HEAD_DIM, (a + 1) * HEAD_DIM)
        s_c = _dot_nt(q[:, sl], kc[:, sl])
        s_o = _dot_nt(q[:, sl], ko[:, sl])
        outs.append(_softmax_pv([s_c, s_o], [vc, vo]))
    o = outs[0] - lam_ref[0] * outs[1]
    o_ref[0] = _subnorm(o, g_ref[...]).astype(BF16)


def _attn_b_sample(lam, bq, cache_k, cache_v, bk, bv, g_sub):
    b, l, _ = bq.shape
    p = cache_k.shape[1]
    hw = 2 * HEAD_DIM
    bs = lambda n: pl.BlockSpec((1, n, hw), lambda bb, h, lam_: (bb, 0, h))
    return pl.pallas_call(
        _attn_b_sample_kernel,
        grid_spec=pltpu.PrefetchScalarGridSpec(
            num_scalar_prefetch=1,
            grid=(b, N_HEADS_B),
            in_specs=[bs(l), bs(p), bs(p), bs(l), bs(l),
                      pl.BlockSpec((1, hw), lambda bb, h, lam_: (0, 0))],
            out_specs=bs(l)),
        out_shape=jax.ShapeDtypeStruct((b, l, B_WIDTH), BF16),
        compiler_params=_cparams(("arbitrary", "arbitrary")),
        name="attn_b_sample",
    )(lam, bq, cache_k, cache_v, bk, bv, g_sub)


def _memkv_kernel(m_ref, g_ref, wk_ref, wv_ref, kf_ref, vf_ref, kb_ref, vb_ref):
    mn = _rms(m_ref[...], g_ref[...]).astype(BF16)
    k = jnp.dot(mn, wk_ref[...], preferred_element_type=F32)
    v = jnp.dot(mn, wv_ref[...], preferred_element_type=F32)
    kf_ref[...] = k
    vf_ref[...] = v
    kb_ref[...] = k.astype(BF16)
    vb_ref[...] = v.astype(BF16)


def _memkv(mem2d, g, wk, wv, *, tm):
    r = mem2d.shape[0]
    row = lambda i: (i, 0)
    const = lambda i: (0, 0)
    blk = pl.BlockSpec((tm, D_MODEL), row)
    wspec = pl.BlockSpec((D_MODEL, D_MODEL), const)
    f = jax.ShapeDtypeStruct((r, D_MODEL), F32)
    bf = jax.ShapeDtypeStruct((r, D_MODEL), BF16)
    return pl.pallas_call(
        _memkv_kernel,
        grid=(r // tm,),
        in_specs=[blk, pl.BlockSpec((1, D_MODEL), const), wspec, wspec],
        out_specs=[blk] * 4,
        out_shape=[f, f, bf, bf],
        compiler_params=_cparams(("arbitrary",)),
        name="memkv",
    )(mem2d, g, wk, wv)


def _post_kernel(x_ref, oa_ref, ob_ref, woa_ref, wob_ref, gq_ref, wq_ref, mk_ref, mv_ref, wo_ref,
                 gm_ref, h_ref, xn_ref):
    h1 = (x_ref[...] + jnp.dot(oa_ref[...], woa_ref[...], preferred_element_type=F32)
          + jnp.dot(ob_ref[...], wob_ref[...], preferred_element_type=F32))
    xq = _rms(h1, gq_ref[...]).astype(BF16)
    q = (jnp.dot(xq, wq_ref[...], preferred_element_type=F32) * HEAD_DIM_M ** -0.5).astype(BF16)
    outs = []
    for h in range(N_HEADS_M):
        sl = slice(h * HEAD_DIM_M, (h + 1) * HEAD_DIM_M)
        s = _dot_nt(q[:, sl], mk_ref[0, :, sl])
        outs.append(_softmax_pv([s], [mv_ref[0, :, sl]]).astype(BF16))
    o = jnp.concatenate(outs, axis=1)
    h2 = h1 + jnp.dot(o, wo_ref[...], preferred_element_type=F32)
    h_ref[...] = h2
    xn_ref[...] = _rms(h2, gm_ref[...])


def _post(x2d, oa, ob, woa, wob, gq, wq, mk, mv, wo, gm, *, tm, tiles_per_batch):
    r = x2d.shape[0]
    mlen = mk.shape[1]
    row = lambda i: (i, 0)
    const = lambda i: (0, 0)
    memb = lambda i: (i // tiles_per_batch, 0, 0)
    gspec = pl.BlockSpec((1, D_MODEL), const)
    wspec = pl.BlockSpec((D_MODEL, D_MODEL), const)
    hspec = pl.BlockSpec((A_WIDTH, D_MODEL), const)
    return pl.pallas_call(
        _post_kernel,
        grid=(r // tm,),
        in_specs=[pl.BlockSpec((tm, D_MODEL), row),
                  pl.BlockSpec((tm, A_WIDTH), row), pl.BlockSpec((tm, B_WIDTH), row),
                  hspec, hspec, gspec, wspec,
                  pl.BlockSpec((1, mlen, D_MODEL), memb), pl.BlockSpec((1, mlen, D_MODEL), memb),
                  wspec, gspec],
        out_specs=[pl.BlockSpec((tm, D_MODEL), row)] * 2,
        out_shape=[jax.ShapeDtypeStruct((r, D_MODEL), F32)] * 2,
        compiler_params=_cparams(("arbitrary",)),
        name="post_attn",
    )(x2d, oa, ob, woa, wob, gq, wq, mk, mv, wo, gm)


def _router_kernel(xn_ref, wr_ref, br_ref, tri_ref, idx_ref, gate_ref, rank_ref, cnt_ref, base_sc):
    @pl.when(pl.program_id(0) == 0)
    def _():
        base_sc[...] = jnp.zeros(base_sc.shape, F32)

    tm = xn_ref.shape[0]
    logits = _dot_nt(wr_ref[...], xn_ref[...].astype(BF16)) + br_ref[...]
    iota_e = lax.broadcasted_iota(jnp.int32, (N_EXPERTS, tm), 0).astype(F32)
    work = logits
    vals, idxs, hots = [], [], []
    for _k in range(TOP_K):
        mx = work.max(axis=0, keepdims=True)
        ix = jnp.where(work == mx, iota_e, float(N_EXPERTS)).min(axis=0, keepdims=True)
        hot = iota_e == ix
        vals.append(mx)
        idxs.append(ix)
        hots.append(hot)
        work = jnp.where(hot, -jnp.inf, work)
    es = [jnp.exp(v - vals[0]) for v in vals]
    den = es[0] + es[1] + es[2] + es[3]
    cnt = (hots[0] | hots[1] | hots[2] | hots[3]).astype(F32)
    before = jnp.dot(cnt.astype(BF16), tri_ref[...], preferred_element_type=F32) + base_sc[...]
    for k in range(TOP_K):
        idx_ref[k:k + 1, :] = idxs[k].astype(jnp.int32)
        gate_ref[k:k + 1, :] = es[k] / den
        rank_ref[k:k + 1, :] = jnp.where(hots[k], before, 0.0).sum(axis=0, keepdims=True).astype(jnp.int32)
    base_sc[...] = base_sc[...] + cnt.sum(axis=1, keepdims=True)
    cnt_ref[...] = jnp.broadcast_to(base_sc[...], cnt_ref.shape).astype(jnp.int32)


def _router(xn2d, wr_t, br, tri, *, tm):
    r = xn2d.shape[0]
    col = lambda i: (0, i)
    const = lambda i: (0, 0)
    return pl.pallas_call(
        _router_kernel,
        grid=(r // tm,),
        in_specs=[pl.BlockSpec((tm, D_MODEL), lambda i: (i, 0)),
                  pl.BlockSpec((N_EXPERTS, D_MODEL), const),
                  pl.BlockSpec((N_EXPERTS, 1), const),
                  pl.BlockSpec((tm, tm), const)],
        out_specs=[pl.BlockSpec((TOP_K, tm), col)] * 3 + [pl.BlockSpec((N_EXPERTS, LANES), const)],
        out_shape=[jax.ShapeDtypeStruct((TOP_K, r), jnp.int32),
                   jax.ShapeDtypeStruct((TOP_K, r), F32),
                   jax.ShapeDtypeStruct((TOP_K, r), jnp.int32),
                   jax.ShapeDtypeStruct((N_EXPERTS, LANES), jnp.int32)],
        scratch_shapes=[pltpu.VMEM((N_EXPERTS, 1), F32)],
        compiler_params=_cparams(("arbitrary",)),
        name="router",
    )(xn2d, wr_t, br, tri)


def _row_copy(src, dst, sem):
    return pltpu.make_async_copy(src, dst, sem)


def _scatter_kernel(pos_ref, xn_ref, xs_ref, sem):
    tm = xn_ref.shape[0]

    def start(t, c):
        for k in range(TOP_K):
            _row_copy(xn_ref.at[pl.ds(t, 1), :], xs_ref.at[pl.ds(pos_ref[0, k, t], 1), :], sem).start()
        return c

    lax.fori_loop(0, tm, start, 0)

    def wait(t, c):
        for k in range(TOP_K):
            _row_copy(xn_ref.at[pl.ds(0, 1), :], xs_ref.at[pl.ds(0, 1), :], sem).wait()
        return c

    lax.fori_loop(0, tm, wait, 0)


def _scatter(pos, xn2d, *, tm):
    r = xn2d.shape[0]
    return pl.pallas_call(
        _scatter_kernel,
        grid=(r // tm,),
        in_specs=[pl.BlockSpec((1, TOP_K, tm), lambda i: (i, 0, 0), memory_space=pltpu.SMEM),
                  pl.BlockSpec((tm, D_MODEL), lambda i: (i, 0))],
        out_specs=pl.BlockSpec(memory_space=pl.ANY),
        out_shape=jax.ShapeDtypeStruct((TOP_K * r, D_MODEL), F32),
        scratch_shapes=[pltpu.SemaphoreType.DMA],
        compiler_params=_cparams(("arbitrary",)),
        name="moe_scatter",
    )(pos, xn2d)


def _experts_kernel(blk_ref, exp_ref, lo_ref, hi_ref, first_ref,
                    x_ref, wgu_ref, bgu_ref, wd_ref, bd_ref, y_ref):
    i = pl.program_id(0)
    x = x_ref[...].astype(BF16)
    gu = jnp.dot(x, wgu_ref[0], preferred_element_type=F32) + bgu_ref[0]
    gate = jnp.minimum(gu[:, :EXPERT_FF], SWIGLU_LIMIT)
    up = jnp.clip(gu[:, EXPERT_FF:], -SWIGLU_LIMIT, SWIGLU_LIMIT)
    act = (up + 1.0) * (gate * jax.nn.sigmoid(SWIGLU_ALPHA * gate))
    y = jnp.dot(act.astype(BF16), wd_ref[0], preferred_element_type=F32) + bd_ref[0]
    rows = blk_ref[i] * MOE_ROWS + lax.broadcasted_iota(jnp.int32, (MOE_ROWS, 1), 0)
    mine = (rows >= lo_ref[i]) & (rows < hi_ref[i])

    @pl.when(first_ref[i] == 1)
    def _():
        y_ref[...] = jnp.zeros(y_ref.shape, F32)

    y_ref[...] = jnp.where(mine, y, y_ref[...])


def _experts(items, xs, wgu, bgu, wd, bd):
    blk, exp, lo, hi, first = items
    n_items = blk.shape[0]
    n_rows = xs.shape[0]
    return pl.pallas_call(
        _experts_kernel,
        grid_spec=pltpu.PrefetchScalarGridSpec(
            num_scalar_prefetch=5,
            grid=(n_items,),
            in_specs=[pl.BlockSpec((MOE_ROWS, D_MODEL), lambda i, b, e, l, h, f: (b[i], 0)),
                      pl.BlockSpec((1, D_MODEL, 2 * EXPERT_FF), lambda i, b, e, l, h, f: (e[i], 0, 0)),
                      pl.BlockSpec((1, 1, 2 * EXPERT_FF), lambda i, b, e, l, h, f: (e[i], 0, 0)),
                      pl.BlockSpec((1, EXPERT_FF, D_MODEL), lambda i, b, e, l, h, f: (e[i], 0, 0)),
                      pl.BlockSpec((1, 1, D_MODEL), lambda i, b, e, l, h, f: (e[i], 0, 0))],
            out_specs=pl.BlockSpec((MOE_ROWS, D_MODEL), lambda i, b, e, l, h, f: (b[i], 0))),
        out_shape=jax.ShapeDtypeStruct((n_rows, D_MODEL), F32),
        compiler_params=_cparams(("arbitrary",)),
        name="moe_experts",
    )(blk, exp, lo, hi, first, xs, wgu, bgu, wd, bd)


def _work_items(counts, n_rows):
    n_blocks = n_rows // MOE_ROWS
    n_items = n_blocks + N_EXPERTS - 1
    ends = jnp.cumsum(counts)
    starts = ends - counts
    nb = jnp.where(counts > 0, (ends - 1) // MOE_ROWS - starts // MOE_ROWS + 1, 0)
    item_end = jnp.cumsum(nb)
    item_start = item_end - nb
    total = item_end[-1]
    i = jnp.arange(n_items, dtype=jnp.int32)
    e = jnp.minimum(jnp.searchsorted(item_end, i, side="right"), N_EXPERTS - 1).astype(jnp.int32)
    blk = (starts[e] // MOE_ROWS + (i - item_start[e])).astype(jnp.int32)
    valid = i < total
    lo = jnp.maximum(starts[e], blk * MOE_ROWS)
    hi = jnp.minimum(ends[e], (blk + 1) * MOE_ROWS)
    last = jnp.maximum(total - 1, 0)
    blk = jnp.where(valid, blk, blk[last])
    e = jnp.where(valid, e, e[last])
    lo = jnp.where(valid, lo, 0).astype(jnp.int32)
    hi = jnp.where(valid, hi, 0).astype(jnp.int32)
    prev = jnp.concatenate([jnp.full((1,), -1, jnp.int32), blk[:-1]])
    first = (blk != prev).astype(jnp.int32)
    return blk, e, lo, hi, first, starts


def _combine_kernel(pos_ref, h_ref, gate_ref, g_ref, ys_ref, o_ref, buf, sem):
    tm = h_ref.shape[0]

    def start(t, c):
        for k in range(TOP_K):
            _row_copy(ys_ref.at[pl.ds(pos_ref[0, k, t], 1), :], buf.at[k, pl.ds(t, 1), :], sem).start()
        return c

    lax.fori_loop(0, tm, start, 0)

    def wait(t, c):
        for k in range(TOP_K):
            _row_copy(ys_ref.at[pl.ds(0, 1), :], buf.at[0, pl.ds(0, 1), :], sem).wait()
        return c

    lax.fori_loop(0, tm, wait, 0)
    acc = h_ref[...]
    for k in range(TOP_K):
        acc = acc + gate_ref[:, k:k + 1] * buf[k]
    o_ref[...] = _rms(acc, g_ref[...])


def _combine(pos, h2d, gates_t, g_final, ys, *, tm):
    r = h2d.shape[0]
    row = lambda i: (i, 0)
    return pl.pallas_call(
        _combine_kernel,
        grid=(r // tm,),
        in_specs=[pl.BlockSpec((1, TOP_K, tm), lambda i: (i, 0, 0), memory_space=pltpu.SMEM),
                  pl.BlockSpec((tm, D_MODEL), row),
                  pl.BlockSpec((tm, TOP_K), row),
                  pl.BlockSpec((1, D_MODEL), lambda i: (0, 0)),
                  pl.BlockSpec(memory_space=pl.ANY)],
        out_specs=pl.BlockSpec((tm, D_MODEL), row),
        out_shape=jax.ShapeDtypeStruct((r, D_MODEL), F32),
        scratch_shapes=[pltpu.VMEM((TOP_K, tm, D_MODEL), F32), pltpu.SemaphoreType.DMA],
        compiler_params=_cparams(("arbitrary",)),
        name="moe_combine",
    )(pos, h2d, gates_t, g_final, ys)


def _moe(h2d, xn2d, w, *, tm_router, tm_rows):
    idx, gates, rank, cnt = _router(xn2d, w["wr_t"], w["br"], w["tri"][:tm_router, :tm_router], tm=tm_router)
    counts = cnt[:, 0]
    n_rows = TOP_K * h2d.shape[0]
    blk, e, lo, hi, first, starts = _work_items(counts, n_rows)
    pos = starts[idx].astype(jnp.int32) + rank
    pos_tiles = pos.reshape(TOP_K, -1, tm_rows).transpose(1, 0, 2)
    xs = _scatter(pos_tiles, xn2d, tm=tm_rows)
    ys = _experts((blk, e, lo, hi, first), xs, w["wgu"], w["bgu"], w["wd"], w["bd"])
    return _combine(pos_tiles, h2d, gates.T, w["g_final"], ys, tm=tm_rows)


def _pick_tile(n, cap):
    t = cap
    while n % t:
        t //= 2
    return t


def kernel(x_prompt, x_sample, mem_prompt, cache_a_k, cache_a_v, cache_b_k, cache_b_v, cache_mem_k, cache_mem_v, g_mix, w_in, rel_bias, lam_q1, lam_k1, lam_q2, lam_k2, g_subln, w_out, g_xq, g_xmem, w_xq, w_xk, w_xv, w_xo, g_moe, w_router, b_router, w_gate_up, b_gate_up, w_down, b_down, g_final):
    assert g_mix.shape[0] == 1, "single-layer kernel"
    b, s, d = x_prompt.shape
    db, dl, _ = x_sample.shape
    past = cache_b_k.shape[2]
    a_len = cache_a_k.shape[2]
    mlen = mem_prompt.shape[1]
    keep_a = min(BAND, s)
    assert d == D_MODEL and s % B_QBLK == 0 and s >= A_WIN and dl == CHUNK and past % CHUNK == 0

    row1 = lambda v: v.reshape(1, -1).astype(F32)
    w_in_bf = w_in[0].astype(BF16)
    w_out_bf = w_out[0].astype(BF16)
    weights = dict(
        wr_t=w_router[0].T.astype(BF16), br=b_router[0].reshape(N_EXPERTS, 1).astype(F32),
        tri=jnp.asarray(np.triu(np.ones((512, 512), np.float32), 1), BF16),
        wgu=w_gate_up[0].astype(BF16), bgu=b_gate_up[0].reshape(N_EXPERTS, 1, -1).astype(F32),
        wd=w_down[0].astype(BF16), bd=b_down[0].reshape(N_EXPERTS, 1, -1).astype(F32),
        g_final=row1(g_final))
    lam = (jnp.exp(jnp.sum(lam_q1[0].astype(F32) * lam_k1[0].astype(F32)))
           - jnp.exp(jnp.sum(lam_q2[0].astype(F32) * lam_k2[0].astype(F32))) + LAM_INIT).reshape(1)
    table = _band_table(rel_bias[0])
    g_sub = row1(g_subln[0])

    tm_p = _pick_tile(s, 512)
    tabs_p = _rope_tables(jnp.arange(s, dtype=jnp.int32))
    p_in = _inproj(x_prompt.reshape(b * s, d), row1(g_mix[0]), w_in_bf, *tabs_p,
                   tm=tm_p, n_pos_blocks=s // tm_p, tail_div=s // tm_p if keep_a == tm_p else 1)
    tabs_s = _rope_tables(jnp.tile(past + jnp.arange(dl, dtype=jnp.int32), db))
    s_in = _inproj(x_sample.reshape(db * dl, d), row1(g_mix[0]), w_in_bf, *tabs_s,
                   tm=db * dl, n_pos_blocks=1, tail_div=1)

    def split(outs, nb, nl):
        return [o.reshape(nb, nl, A_WIDTH) for o in outs[:6]]

    aq, ak, av, bq, bk, bv = split(p_in, b, s)
    saq, sak, sav, sbq, sbk, sbv = split(s_in, db, dl)

    oa = _attn_a_prompt(aq, ak, av, table)
    ob = _attn_b_prompt(lam, bq, bk, bv, g_sub)
    soa = _attn_a_sample(saq, cache_a_k[0].reshape(db, a_len, A_WIDTH), cache_a_v[0].reshape(db, a_len, A_WIDTH),
                         sak, sav, table)
    sob = _attn_b_sample(lam, sbq, cache_b_k[0].reshape(db, past, B_WIDTH), cache_b_v[0].reshape(db, past, B_WIDTH),
                         sbk, sbv, g_sub)

    mkf, mvf, mkb, mvb = _memkv(mem_prompt.reshape(b * mlen, d), row1(g_xmem[0]),
                                w_xk[0].astype(BF16), w_xv[0].astype(BF16), tm=mlen)
    post_w = (w_out_bf[:A_WIDTH], w_out_bf[A_WIDTH:], row1(g_xq[0]), w_xq[0].astype(BF16))
    tm_post = _pick_tile(s, 256)
    hp, xnp_ = _post(x_prompt.reshape(b * s, d), oa.reshape(b * s, A_WIDTH), ob.reshape(b * s, B_WIDTH),
                     *post_w, mkb.reshape(b, mlen, d), mvb.reshape(b, mlen, d),
                     w_xo[0].astype(BF16), row1(g_moe[0]), tm=tm_post, tiles_per_batch=s // tm_post)
    hs, xns = _post(x_sample.reshape(db * dl, d), soa.reshape(db * dl, A_WIDTH), sob.reshape(db * dl, B_WIDTH),
                    *post_w, cache_mem_k[0].reshape(db, mlen, d).astype(BF16),
                    cache_mem_v[0].reshape(db, mlen, d).astype(BF16),
                    w_xo[0].astype(BF16), row1(g_moe[0]), tm=dl, tiles_per_batch=1)

    y_prompt = _moe(hp, xnp_, weights, tm_router=512, tm_rows=128).reshape(b, s, d)
    y_sample = _moe(hs, xns, weights, tm_router=_pick_tile(db * dl, 512), tm_rows=128).reshape(db, dl, d)

    def heads(x2d, nb, nl, nh):
        return x2d.reshape(1, nb, nl, nh, -1)

    akf, avf, bkf, bvf = p_in[6:]
    if keep_a != tm_p:
        akf = akf.reshape(b, s, A_WIDTH)[:, s - keep_a:]
        avf = avf.reshape(b, s, A_WIDTH)[:, s - keep_a:]
    sakf, savf, sbkf, sbvf = s_in[6:]
    return (y_prompt, y_sample,
            heads(akf, b, keep_a, N_HEADS_A), heads(avf, b, keep_a, N_HEADS_A),
            heads(bkf, b, s, 2 * N_HEADS_B), heads(bvf, b, s, N_HEADS_B),
            heads(mkf, b, mlen, N_HEADS_M), heads(mvf, b, mlen, N_HEADS_M),
            heads(sakf, db, dl, N_HEADS_A), heads(savf, db, dl, N_HEADS_A),
            heads(sbkf, db, dl, 2 * N_HEADS_B), heads(sbvf, db, dl, N_HEADS_B))
```

```python
import functools
import math

import numpy as np
import jax
import jax.numpy as jnp
from jax import lax
from jax.experimental import pallas as pl
from jax.experimental.pallas import tpu as pltpu

F32 = jnp.float32
BF16 = jnp.bfloat16

D_MODEL = 1024
CHUNK = 64
LEFT_CHUNKS = 8
BAND = LEFT_CHUNKS * CHUNK
HEAD_DIM = 64
A_WIDTH = 512
B_WIDTH = 512
N_HEADS_A = 8
N_HEADS_B = 4
REL_CLIP = 128
ROT_DIM = 16
ROPE_THETA = 500000.0
N_HEADS_M = 4
HEAD_DIM_M = 256
N_EXPERTS = 32
TOP_K = 4
EXPERT_FF = 1024
SWIGLU_LIMIT = 7.0
SWIGLU_ALPHA = 1.702
RMS_EPS = 1e-5
NEG_INF = -1e30
LAM_INIT = 0.8 - 0.6 * math.exp(-0.3 * 0)

LANES = 128
A_QBLK = 4 * CHUNK
A_WIN = BAND + A_QBLK
B_QBLK = 256
B_KBLK = 512
MOE_ROWS = 256
VMEM_LIMIT = 48 * 1024 * 1024


def _cparams(sem):
    return pltpu.CompilerParams(dimension_semantics=sem, vmem_limit_bytes=VMEM_LIMIT)


def _rms(x, g):
    return x * lax.rsqrt(jnp.mean(x * x, axis=-1, keepdims=True) + RMS_EPS) * g


def _dot_nt(a, b):
    return lax.dot_general(a, b, (((1,), (1,)), ((), ())), preferred_element_type=F32)


def _inproj_kernel(x_ref, g_ref, w_ref, wvt_ref, cos_ref, sa_ref, sb_ref,
                   aq_ref, ak_ref, av_ref, bq_ref, bk_ref, bv_ref,
                   akf_ref, avf_ref, bkf_ref, bvf_ref, *maybe_vt_refs):
    xn = _rms(x_ref[...], g_ref[...]).astype(BF16)
    for n, vt_ref in enumerate(maybe_vt_refs):
        vt = _dot_nt(wvt_ref[n], xn).astype(BF16)
        blk = vt_ref.shape[2]
        for c in range(vt_ref.shape[0]):
            vt_ref[c] = vt[:, c * blk:(c + 1) * blk]

    def seg(s):
        return jnp.dot(xn, w_ref[:, s * A_WIDTH:(s + 1) * A_WIDTH], preferred_element_type=F32)

    def rope(z):
        cols = []
        for c in range(z.shape[1] // LANES):
            zc = z[:, c * LANES:(c + 1) * LANES]
            cols.append(zc * cos_ref[...] + pltpu.roll(zc, 8, 1) * sa_ref[...]
                        + pltpu.roll(zc, LANES - 8, 1) * sb_ref[...])
        return jnp.concatenate(cols, axis=1)

    scale = HEAD_DIM ** -0.5
    aq_ref[...] = (seg(0) * scale).astype(BF16)
    ak = seg(1)
    ak_ref[...] = ak.astype(BF16)
    akf_ref[...] = ak
    av = seg(2)
    av_ref[...] = av.astype(BF16)
    avf_ref[...] = av
    bq_ref[...] = (rope(seg(3)) * scale).astype(BF16)
    bk = rope(seg(4))
    bk_ref[...] = bk.astype(BF16)
    bkf_ref[...] = bk
    bv = seg(5)
    bv_ref[...] = bv.astype(BF16)
    bvf_ref[...] = bv


def _inproj(x2d, g, w_bf, wvt_bf, cos_t, sa_t, sb_t, *, tm, n_pos_blocks, tail_div, emit_vt):
    r = x2d.shape[0]
    n_tiles = r // tm
    row = lambda i: (i, 0)
    const = lambda i: (0, 0)
    pos = lambda i: (i % n_pos_blocks, 0)
    tail = lambda i: (i // tail_div, 0)
    bf = jax.ShapeDtypeStruct((r, A_WIDTH), BF16)
    f_all = jax.ShapeDtypeStruct((r, A_WIDTH), F32)
    f_tail = jax.ShapeDtypeStruct((r // tail_div, A_WIDTH), F32)
    blk = lambda im: pl.BlockSpec((tm, A_WIDTH), im)
    out_specs = [blk(row)] * 6 + [blk(tail), blk(tail), blk(row), blk(row)]
    out_shape = [bf] * 6 + [f_tail, f_tail, f_all, f_all]
    if emit_vt:
        for kb in (A_QBLK, B_KBLK):
            out_specs.append(pl.BlockSpec((tm // kb, A_WIDTH, kb), lambda i: (i, 0, 0)))
            out_shape.append(jax.ShapeDtypeStruct((r // kb, A_WIDTH, kb), BF16))
    return pl.pallas_call(
        _inproj_kernel,
        grid=(n_tiles,),
        in_specs=[pl.BlockSpec((tm, D_MODEL), row), pl.BlockSpec((1, D_MODEL), const),
                  pl.BlockSpec((D_MODEL, 3 * D_MODEL), const),
                  pl.BlockSpec((2, A_WIDTH, D_MODEL), lambda i: (0, 0, 0)),
                  pl.BlockSpec((tm, LANES), pos), pl.BlockSpec((tm, LANES), pos),
                  pl.BlockSpec((tm, LANES), pos)],
        out_specs=out_specs,
        out_shape=out_shape,
        compiler_params=_cparams(("arbitrary",)),
        name="inproj",
    )(x2d, g, w_bf, wvt_bf, cos_t, sa_t, sb_t)


def _rope_tables(positions):
    half = ROT_DIM // 2
    lane = np.arange(LANES) % HEAD_DIM
    inv_freq = jnp.power(ROPE_THETA, -jnp.arange(half, dtype=F32) / half)
    ang = positions.astype(F32)[:, None] * inv_freq[None, :]
    cos = jnp.cos(ang)[:, lane % half]
    sin = jnp.sin(ang)[:, lane % half]
    in_rot = jnp.asarray(lane < ROT_DIM)[None, :]
    first = jnp.asarray(lane < half)[None, :]
    cos_t = jnp.where(in_rot, cos, 1.0)
    sa_t = jnp.where(in_rot & ~first, sin, 0.0)
    sb_t = jnp.where(first, -sin, 0.0)
    return cos_t.astype(F32), sa_t.astype(F32), sb_t.astype(F32)


def _band_table(rel_bias):
    r = np.arange(A_QBLK)[:, None]
    j = np.arange(BAND + A_WIN)[None, :]
    dchunk = LEFT_CHUNKS + r // CHUNK - j // CHUNK
    allowed = (dchunk >= 0) & (dchunk <= LEFT_CHUNKS)
    bias = rel_bias.astype(F32)
    n_seq = BAND + A_WIN + A_QBLK - 1
    n_hi = BAND + A_QBLK - 1 - REL_CLIP
    seq = jnp.concatenate([jnp.broadcast_to(bias[:, -1:], (N_HEADS_A, n_hi)), bias[:, ::-1],
                           jnp.broadcast_to(bias[:, :1], (N_HEADS_A, n_seq - n_hi - bias.shape[1]))], axis=1)
    seq = jnp.roll(seq, -(A_QBLK - 1), axis=1)
    toe = jnp.tile(seq, (1, A_QBLK))[:, :A_QBLK * (n_seq - 1)].reshape(N_HEADS_A, A_QBLK, n_seq - 1)
    return jnp.where(jnp.asarray(allowed)[None], toe[:, :, :BAND + A_WIN], NEG_INF)


def _softmax_pv(s_list, v_list):
    m = s_list[0].max(axis=-1, keepdims=True)
    for s in s_list[1:]:
        m = jnp.maximum(m, s.max(axis=-1, keepdims=True))
    l = None
    o = None
    for s, v in zip(s_list, v_list):
        p = jnp.exp(s - m)
        ls = p.sum(axis=-1, keepdims=True)
        os_ = jnp.dot(p.astype(BF16), v, preferred_element_type=F32)
        l = ls if l is None else l + ls
        o = os_ if o is None else o + os_
    return o / l


def _attn_a_prompt_kernel(q_ref, k_ref, vt_ref, tab_ref, o_ref):
    i = pl.program_id(1)
    n_left = BAND // A_QBLK
    n_sub = A_WIN // A_QBLK
    blk0 = jnp.maximum(i - n_left, 0)
    off = pl.multiple_of(jnp.maximum(n_left - i, 0) * A_QBLK, A_QBLK)
    q = q_ref[0]
    lane = lax.broadcasted_iota(jnp.int32, (A_QBLK, LANES), 1)
    pair_outs = []
    for g in range(N_HEADS_A // 2):
        cols = slice(g * LANES, (g + 1) * LANES)
        qp = q[:, cols]
        zero = jnp.zeros_like(qp)
        kbs = [k_ref[0, pl.ds(pl.multiple_of((blk0 + c) * A_QBLK, A_QBLK), A_QBLK), cols] for c in range(n_sub)]
        vts = [vt_ref[0, blk0 + c, cols, :] for c in range(n_sub)]
        halves = []
        for e in range(2):
            h = 2 * g + e
            qh = jnp.where((lane >= HEAD_DIM) == bool(e), qp, zero)
            ss = [_dot_nt(kbs[c], qh) + tab_ref[h, pl.ds(off + c * A_QBLK, A_QBLK), :] for c in range(n_sub)]
            m = _col_reduce(ss[0], jnp.maximum, jnp.max)
            for s in ss[1:]:
                m = jnp.maximum(m, _col_reduce(s, jnp.maximum, jnp.max))
            l = None
            o_t = None
            for c in range(n_sub):
                p = jnp.exp(ss[c] - m)
                ls = _col_reduce(p, jnp.add, jnp.sum)
                os_ = jnp.dot(vts[c], p.astype(BF16), preferred_element_type=F32)
                l = ls if l is None else l + ls
                o_t = os_ if o_t is None else o_t + os_
            halves.append((o_t / l)[e * HEAD_DIM:(e + 1) * HEAD_DIM])
        pair_outs.append(jnp.concatenate(halves, axis=0).T)
    o_ref[0] = jnp.concatenate(pair_outs, axis=1).astype(BF16)


def _attn_a_prompt(aq, ak, avt, table_t):
    b, l, _ = aq.shape
    n_kb = avt.shape[1]
    return pl.pallas_call(
        _attn_a_prompt_kernel,
        grid=(b, l // A_QBLK),
        in_specs=[pl.BlockSpec((1, A_QBLK, A_WIDTH), lambda bb, i: (bb, i, 0)),
                  pl.BlockSpec((1, l, A_WIDTH), lambda bb, i: (bb, 0, 0)),
                  pl.BlockSpec((1, n_kb, A_WIDTH, A_QBLK), lambda bb, i: (bb, 0, 0, 0)),
                  pl.BlockSpec(table_t.shape, lambda bb, i: (0, 0, 0))],
        out_specs=pl.BlockSpec((1, A_QBLK, A_WIDTH), lambda bb, i: (bb, i, 0)),
        out_shape=jax.ShapeDtypeStruct((b, l, A_WIDTH), BF16),
        compiler_params=_cparams(("arbitrary", "arbitrary")),
        name="attn_a_prompt",
    )(aq, ak, avt, table_t)


def _attn_a_sample_kernel(q_ref, kc_ref, vc_ref, k_ref, v_ref, tab_ref, o_ref):
    q = q_ref[0]
    kc = kc_ref[0].astype(BF16)
    vc = vc_ref[0].astype(BF16)
    ko = k_ref[0]
    vo = v_ref[0]
    a_len = kc.shape[0]
    lq = q.shape[0]
    c0 = BAND - a_len
    outs = []
    for h in range(N_HEADS_A):
        sl = slice(h * HEAD_DIM, (h + 1) * HEAD_DIM)
        s_c = _dot_nt(q[:, sl], kc[:, sl]) + tab_ref[h, :, c0:BAND]
        s_o = _dot_nt(q[:, sl], ko[:, sl]) + tab_ref[h, :, BAND:BAND + lq]
        outs.append(_softmax_pv([s_c, s_o], [vc[:, sl], vo[:, sl]]))
    o_ref[0] = jnp.concatenate(outs, axis=1).astype(BF16)


def _attn_a_sample(aq, cache_k, cache_v, ak, av, table):
    b, l, _ = aq.shape
    a_len = cache_k.shape[1]
    bs = lambda n: pl.BlockSpec((1, n, A_WIDTH), lambda bb: (bb, 0, 0))
    return pl.pallas_call(
        _attn_a_sample_kernel,
        grid=(b,),
        in_specs=[bs(l), bs(a_len), bs(a_len), bs(l), bs(l),
                  pl.BlockSpec(table.shape, lambda bb: (0, 0, 0))],
        out_specs=bs(l),
        out_shape=jax.ShapeDtypeStruct((b, l, A_WIDTH), BF16),
        compiler_params=_cparams(("arbitrary",)),
        name="attn_a_sample",
    )(aq, cache_k, cache_v, ak, av, table)


def _subnorm(o, g):
    return _rms(o, g) * (1.0 - LAM_INIT)


def _col_reduce(x, op, reduce_fn):
    n = x.shape[0]
    while n > 8:
        n //= 2
        x = op(x[:n], x[n:])
    return reduce_fn(x, axis=0, keepdims=True)


def _attn_b_prompt_kernel(lam_ref, q_ref, k_ref, vt_ref, mask_ref, g_ref, o_ref,
                          sa_sc, sb_sc, m_sc, l_sc, acc_sc):
    i = pl.program_id(2)
    tk = vt_ref.shape[3]
    q = q_ref[0]
    lane = lax.broadcasted_iota(jnp.int32, q.shape, 1)
    zero = jnp.zeros_like(q)
    qs = (jnp.where(lane < HEAD_DIM, q, zero), jnp.where(lane >= HEAD_DIM, q, zero))
    m_sc[...] = jnp.full(m_sc.shape, NEG_INF, F32)
    l_sc[...] = jnp.zeros(l_sc.shape, F32)
    acc_sc[...] = jnp.zeros(acc_sc.shape, F32)

    def scores(j, s_sc):
        kb = k_ref[0, pl.ds(pl.multiple_of(j * tk, tk), tk), :]
        for a in range(2):
            s_sc[a] = _dot_nt(kb, qs[a])

    def consume(j, s_sc, mask):
        vt = vt_ref[0, j]
        for a in range(2):
            s = s_sc[a]
            if mask is not None:
                s = s + mask
            m_old = m_sc[a]
            m_new = jnp.maximum(m_old, _col_reduce(s, jnp.maximum, jnp.max))
            alpha = jnp.exp(m_old - m_new)
            p = jnp.exp(s - m_new)
            l_sc[a] = alpha * l_sc[a] + _col_reduce(p, jnp.add, jnp.sum)
            m_sc[a] = m_new
            acc_sc[a] = alpha * acc_sc[a] + jnp.dot(vt, p.astype(BF16), preferred_element_type=F32)

    per = tk // q.shape[0]
    last = i // per
    mask = mask_ref[i % per]
    scores(0, sa_sc)

    def body(jj, c):
        j = 2 * jj
        scores(j + 1, sb_sc)
        consume(j, sa_sc, None)
        scores(j + 2, sa_sc)
        consume(j + 1, sb_sc, None)
        return c

    lax.fori_loop(0, last // 2, body, 0)

    @pl.when(last % 2 == 0)
    def _():
        consume(last, sa_sc, mask)

    @pl.when(last % 2 == 1)
    def _():
        scores(last, sb_sc)
        consume(last - 1, sa_sc, None)
        consume(last, sb_sc, mask)

    o_t = acc_sc[0] / l_sc[0] - lam_ref[0] * (acc_sc[1] / l_sc[1])
    o_ref[0] = _subnorm(o_t.T, g_ref[...]).astype(BF16)


def _chunk_causal_masks_t(tk, tq):
    kc = np.arange(tk)[None, :, None] // CHUNK
    qc = (np.arange(tk // tq)[:, None, None] * tq + np.arange(tq)[None, None, :]) // CHUNK
    return jnp.asarray(np.where(kc <= qc, 0.0, NEG_INF), F32)


def _attn_b_prompt(lam, bq, bk, bvt, g_sub):
    b, l, _ = bq.shape
    tq = B_QBLK
    tk = B_KBLK
    hw = 2 * HEAD_DIM
    n_kb = bvt.shape[1]
    return pl.pallas_call(
        _attn_b_prompt_kernel,
        grid_spec=pltpu.PrefetchScalarGridSpec(
            num_scalar_prefetch=1,
            grid=(b, N_HEADS_B, l // tq),
            in_specs=[pl.BlockSpec((1, tq, hw), lambda bb, h, i, lam_: (bb, i, h)),
                      pl.BlockSpec((1, l, hw), lambda bb, h, i, lam_: (bb, 0, h)),
                      pl.BlockSpec((1, n_kb, hw, tk), lambda bb, h, i, lam_: (bb, 0, h, 0)),
                      pl.BlockSpec((tk // tq, tk, tq), lambda bb, h, i, lam_: (0, 0, 0)),
                      pl.BlockSpec((1, hw), lambda bb, h, i, lam_: (0, 0))],
            out_specs=pl.BlockSpec((1, tq, hw), lambda bb, h, i, lam_: (bb, i, h)),
            scratch_shapes=[pltpu.VMEM((2, tk, tq), F32), pltpu.VMEM((2, tk, tq), F32),
                            pltpu.VMEM((2, 1, tq), F32), pltpu.VMEM((2, 1, tq), F32),
                            pltpu.VMEM((2, hw, tq), F32)]),
        out_shape=jax.ShapeDtypeStruct((b, l, B_WIDTH), BF16),
        compiler_params=_cparams(("arbitrary", "arbitrary", "arbitrary")),
        name="attn_b_prompt",
    )(lam, bq, bk, bvt, _chunk_causal_masks_t(tk, tq), g_sub)


def _attn_b_sample_kernel(lam_ref, q_ref, kc_ref, vc_ref, k_ref, v_ref, g_ref, o_ref):
    q = q_ref[0]
    kc = kc_ref[0].astype(BF16)
    vc = vc_ref[0].astype(BF16)
    ko = k_ref[0]
    vo = v_ref[0]
    outs = []
    for a in range(2):
        sl = slice(a * HEAD_DIM, (a + 1) * HEAD_DIM)
        s_c = _dot_nt(q[:, sl], kc[:, sl])
        s_o = _dot_nt(q[:, sl], ko[:, sl])
        outs.append(_softmax_pv([s_c, s_o], [vc, vo]))
    o = outs[0] - lam_ref[0] * outs[1]
    o_ref[0] = _subnorm(o, g_ref[...]).astype(BF16)


def _attn_b_sample(lam, bq, cache_k, cache_v, bk, bv, g_sub):
    b, l, _ = bq.shape
    p = cache_k.shape[1]
    hw = 2 * HEAD_DIM
    bs = lambda n: pl.BlockSpec((1, n, hw), lambda bb, h, lam_: (bb, 0, h))
    return pl.pallas_call(
        _attn_b_sample_kernel,
        grid_spec=pltpu.PrefetchScalarGridSpec(
            num_scalar_prefetch=1,
            grid=(b, N_HEADS_B),
            in_specs=[bs(l), bs(p), bs(p), bs(l), bs(l),
                      pl.BlockSpec((1, hw), lambda bb, h, lam_: (0, 0))],
            out_specs=bs(l)),
        out_shape=jax.ShapeDtypeStruct((b, l, B_WIDTH), BF16),
        compiler_params=_cparams(("arbitrary", "arbitrary")),
        name="attn_b_sample",
    )(lam, bq, cache_k, cache_v, bk, bv, g_sub)


def _memkv_kernel(m_ref, g_ref, wk_ref, wv_ref, kf_ref, vf_ref, kb_ref, vb_ref):
    mn = _rms(m_ref[...], g_ref[...]).astype(BF16)
    k = jnp.dot(mn, wk_ref[...], preferred_element_type=F32)
    v = jnp.dot(mn, wv_ref[...], preferred_element_type=F32)
    kf_ref[...] = k
    vf_ref[...] = v
    kb_ref[...] = k.astype(BF16)
    vb_ref[...] = v.astype(BF16)


def _memkv(mem2d, g, wk, wv, *, tm):
    r = mem2d.shape[0]
    row = lambda i: (i, 0)
    const = lambda i: (0, 0)
    blk = pl.BlockSpec((tm, D_MODEL), row)
    wspec = pl.BlockSpec((D_MODEL, D_MODEL), const)
    f = jax.ShapeDtypeStruct((r, D_MODEL), F32)
    bf = jax.ShapeDtypeStruct((r, D_MODEL), BF16)
    return pl.pallas_call(
        _memkv_kernel,
        grid=(r // tm,),
        in_specs=[blk, pl.BlockSpec((1, D_MODEL), const), wspec, wspec],
        out_specs=[blk] * 4,
        out_shape=[f, f, bf, bf],
        compiler_params=_cparams(("arbitrary",)),
        name="memkv",
    )(mem2d, g, wk, wv)


def _post_kernel(x_ref, oa_ref, ob_ref, woa_ref, wob_ref, gq_ref, wq_ref, mk_ref, mv_ref, wo_ref,
                 gm_ref, h_ref, xn_ref):
    h1 = (x_ref[...] + jnp.dot(oa_ref[...], woa_ref[...], preferred_element_type=F32)
          + jnp.dot(ob_ref[...], wob_ref[...], preferred_element_type=F32))
    xq = _rms(h1, gq_ref[...]).astype(BF16)
    q = (jnp.dot(xq, wq_ref[...], preferred_element_type=F32) * HEAD_DIM_M ** -0.5).astype(BF16)
    outs = []
    for h in range(N_HEADS_M):
        sl = slice(h * HEAD_DIM_M, (h + 1) * HEAD_DIM_M)
        s = _dot_nt(q[:, sl], mk_ref[0, :, sl])
        outs.append(_softmax_pv([s], [mv_ref[0, :, sl]]).astype(BF16))
    o = jnp.concatenate(outs, axis=1)
    h2 = h1 + jnp.dot(o, wo_ref[...], preferred_element_type=F32)
    h_ref[...] = h2
    xn_ref[...] = _rms(h2, gm_ref[...])


def _post(x2d, oa, ob, woa, wob, gq, wq, mk, mv, wo, gm, *, tm, tiles_per_batch):
    r = x2d.shape[0]
    mlen = mk.shape[1]
    row = lambda i: (i, 0)
    const = lambda i: (0, 0)
    memb = lambda i: (i // tiles_per_batch, 0, 0)
    gspec = pl.BlockSpec((1, D_MODEL), const)
    wspec = pl.BlockSpec((D_MODEL, D_MODEL), const)
    hspec = pl.BlockSpec((A_WIDTH, D_MODEL), const)
    return pl.pallas_call(
        _post_kernel,
        grid=(r // tm,),
        in_specs=[pl.BlockSpec((tm, D_MODEL), row),
                  pl.BlockSpec((tm, A_WIDTH), row), pl.BlockSpec((tm, B_WIDTH), row),
                  hspec, hspec, gspec, wspec,
                  pl.BlockSpec((1, mlen, D_MODEL), memb), pl.BlockSpec((1, mlen, D_MODEL), memb),
                  wspec, gspec],
        out_specs=[pl.BlockSpec((tm, D_MODEL), row)] * 2,
        out_shape=[jax.ShapeDtypeStruct((r, D_MODEL), F32)] * 2,
        compiler_params=_cparams(("arbitrary",)),
        name="post_attn",
    )(x2d, oa, ob, woa, wob, gq, wq, mk, mv, wo, gm)


def _router_kernel(xn_ref, wr_ref, br_ref, tri_ref, idx_ref, gate_ref, rank_ref, cnt_ref, base_sc):
    @pl.when(pl.program_id(0) == 0)
    def _():
        base_sc[...] = jnp.zeros(base_sc.shape, F32)

    tm = xn_ref.shape[0]
    logits = _dot_nt(wr_ref[...], xn_ref[...].astype(BF16)) + br_ref[...]
    iota_e = lax.broadcasted_iota(jnp.int32, (N_EXPERTS, tm), 0).astype(F32)
    work = logits
    vals, idxs, hots = [], [], []
    for _k in range(TOP_K):
        mx = work.max(axis=0, keepdims=True)
        ix = jnp.where(work == mx, iota_e, float(N_EXPERTS)).min(axis=0, keepdims=True)
        hot = iota_e == ix
        vals.append(mx)
        idxs.append(ix)
        hots.append(hot)
        work = jnp.where(hot, -jnp.inf, work)
    es = [jnp.exp(v - vals[0]) for v in vals]
    den = es[0] + es[1] + es[2] + es[3]
    cnt = (hots[0] | hots[1] | hots[2] | hots[3]).astype(F32)
    before = jnp.dot(cnt.astype(BF16), tri_ref[...], preferred_element_type=F32) + base_sc[...]
    for k in range(TOP_K):
        idx_ref[k:k + 1, :] = idxs[k].astype(jnp.int32)
        gate_ref[k:k + 1, :] = es[k] / den
        rank_ref[k:k + 1, :] = jnp.where(hots[k], before, 0.0).sum(axis=0, keepdims=True).astype(jnp.int32)
    base_sc[...] = base_sc[...] + cnt.sum(axis=1, keepdims=True)
    cnt_ref[...] = jnp.broadcast_to(base_sc[...], cnt_ref.shape).astype(jnp.int32)


def _router(xn2d, wr_t, br, tri, *, tm):
    r = xn2d.shape[0]
    col = lambda i: (0, i)
    const = lambda i: (0, 0)
    return pl.pallas_call(
        _router_kernel,
        grid=(r // tm,),
        in_specs=[pl.BlockSpec((tm, D_MODEL), lambda i: (i, 0)),
                  pl.BlockSpec((N_EXPERTS, D_MODEL), const),
                  pl.BlockSpec((N_EXPERTS, 1), const),
                  pl.BlockSpec((tm, tm), const)],
        out_specs=[pl.BlockSpec((TOP_K, tm), col)] * 3 + [pl.BlockSpec((N_EXPERTS, LANES), const)],
        out_shape=[jax.ShapeDtypeStruct((TOP_K, r), jnp.int32),
                   jax.ShapeDtypeStruct((TOP_K, r), F32),
                   jax.ShapeDtypeStruct((TOP_K, r), jnp.int32),
                   jax.ShapeDtypeStruct((N_EXPERTS, LANES), jnp.int32)],
        scratch_shapes=[pltpu.VMEM((N_EXPERTS, 1), F32)],
        compiler_params=_cparams(("arbitrary",)),
        name="router",
    )(xn2d, wr_t, br, tri)


def _row_copy(src, dst, sem):
    return pltpu.make_async_copy(src, dst, sem)


def _scatter_kernel(pos_ref, xn_ref, xs_ref, sem):
    tm = xn_ref.shape[0]

    def start(t, c):
        for k in range(TOP_K):
            _row_copy(xn_ref.at[pl.ds(t, 1), :], xs_ref.at[pl.ds(pos_ref[0, k, t], 1), :], sem).start()
        return c

    lax.fori_loop(0, tm, start, 0)

    def wait(t, c):
        for k in range(TOP_K):
            _row_copy(xn_ref.at[pl.ds(0, 1), :], xs_ref.at[pl.ds(0, 1), :], sem).wait()
        return c

    lax.fori_loop(0, tm, wait, 0)


def _scatter(pos, xn2d, *, tm):
    r = xn2d.shape[0]
    return pl.pallas_call(
        _scatter_kernel,
        grid=(r // tm,),
        in_specs=[pl.BlockSpec((1, TOP_K, tm), lambda i: (i, 0, 0), memory_space=pltpu.SMEM),
                  pl.BlockSpec((tm, D_MODEL), lambda i: (i, 0))],
        out_specs=pl.BlockSpec(memory_space=pl.ANY),
        out_shape=jax.ShapeDtypeStruct((TOP_K * r, D_MODEL), F32),
        scratch_shapes=[pltpu.SemaphoreType.DMA],
        compiler_params=_cparams(("arbitrary",)),
        name="moe_scatter",
    )(pos, xn2d)


def _experts_kernel(blk_ref, exp_ref, lo_ref, hi_ref, first_ref,
                    x_ref, wgu_ref, bgu_ref, wd_ref, bd_ref, y_ref):
    i = pl.program_id(0)
    x = x_ref[...].astype(BF16)
    gu = jnp.dot(x, wgu_ref[0], preferred_element_type=F32) + bgu_ref[0]
    gate = jnp.minimum(gu[:, :EXPERT_FF], SWIGLU_LIMIT)
    up = jnp.clip(gu[:, EXPERT_FF:], -SWIGLU_LIMIT, SWIGLU_LIMIT)
    act = (up + 1.0) * (gate * jax.nn.sigmoid(SWIGLU_ALPHA * gate))
    y = jnp.dot(act.astype(BF16), wd_ref[0], preferred_element_type=F32) + bd_ref[0]
    rows = blk_ref[i] * MOE_ROWS + lax.broadcasted_iota(jnp.int32, (MOE_ROWS, 1), 0)
    mine = (rows >= lo_ref[i]) & (rows < hi_ref[i])

    @pl.when(first_ref[i] == 1)
    def _():
        y_ref[...] = jnp.zeros(y_ref.shape, F32)

    y_ref[...] = jnp.where(mine, y, y_ref[...])


def _experts(items, xs, wgu, bgu, wd, bd):
    blk, exp, lo, hi, first = items
    n_items = blk.shape[0]
    n_rows = xs.shape[0]
    return pl.pallas_call(
        _experts_kernel,
        grid_spec=pltpu.PrefetchScalarGridSpec(
            num_scalar_prefetch=5,
            grid=(n_items,),
            in_specs=[pl.BlockSpec((MOE_ROWS, D_MODEL), lambda i, b, e, l, h, f: (b[i], 0)),
                      pl.BlockSpec((1, D_MODEL, 2 * EXPERT_FF), lambda i, b, e, l, h, f: (e[i], 0, 0)),
                      pl.BlockSpec((1, 1, 2 * EXPERT_FF), lambda i, b, e, l, h, f: (e[i], 0, 0)),
                      pl.BlockSpec((1, EXPERT_FF, D_MODEL), lambda i, b, e, l, h, f: (e[i], 0, 0)),
                      pl.BlockSpec((1, 1, D_MODEL), lambda i, b, e, l, h, f: (e[i], 0, 0))],
            out_specs=pl.BlockSpec((MOE_ROWS, D_MODEL), lambda i, b, e, l, h, f: (b[i], 0))),
        out_shape=jax.ShapeDtypeStruct((n_rows, D_MODEL), F32),
        compiler_params=_cparams(("arbitrary",)),
        name="moe_experts",
    )(blk, exp, lo, hi, first, xs, wgu, bgu, wd, bd)


def _work_items(counts, n_rows):
    n_blocks = n_rows // MOE_ROWS
    n_items = n_blocks + N_EXPERTS - 1
    ends = jnp.cumsum(counts)
    starts = ends - counts
    nb = jnp.where(counts > 0, (ends - 1) // MOE_ROWS - starts // MOE_ROWS + 1, 0)
    item_end = jnp.cumsum(nb)
    item_start = item_end - nb
    total = item_end[-1]
    i = jnp.arange(n_items, dtype=jnp.int32)
    e = jnp.minimum(jnp.sum(item_end[None, :] <= i[:, None], axis=1), N_EXPERTS - 1).astype(jnp.int32)
    blk = (starts[e] // MOE_ROWS + (i - item_start[e])).astype(jnp.int32)
    valid = i < total
    lo = jnp.maximum(starts[e], blk * MOE_ROWS)
    hi = jnp.minimum(ends[e], (blk + 1) * MOE_ROWS)
    last = jnp.maximum(total - 1, 0)
    blk = jnp.where(valid, blk, blk[last])
    e = jnp.where(valid, e, e[last])
    lo = jnp.where(valid, lo, 0).astype(jnp.int32)
    hi = jnp.where(valid, hi, 0).astype(jnp.int32)
    prev = jnp.concatenate([jnp.full((1,), -1, jnp.int32), blk[:-1]])
    first = (blk != prev).astype(jnp.int32)
    return blk, e, lo, hi, first, starts


def _combine_kernel(pos_ref, h_ref, gate_ref, g_ref, ys_ref, o_ref, buf, sem):
    tm = h_ref.shape[0]

    def start(t, c):
        for k in range(TOP_K):
            _row_copy(ys_ref.at[pl.ds(pos_ref[0, k, t], 1), :], buf.at[k, pl.ds(t, 1), :], sem).start()
        return c

    lax.fori_loop(0, tm, start, 0)

    def wait(t, c):
        for k in range(TOP_K):
            _row_copy(ys_ref.at[pl.ds(0, 1), :], buf.at[0, pl.ds(0, 1), :], sem).wait()
        return c

    lax.fori_loop(0, tm, wait, 0)
    acc = h_ref[...]
    for k in range(TOP_K):
        acc = acc + gate_ref[:, k:k + 1] * buf[k]
    o_ref[...] = _rms(acc, g_ref[...])


def _combine(pos, h2d, gates_t, g_final, ys, *, tm):
    r = h2d.shape[0]
    row = lambda i: (i, 0)
    return pl.pallas_call(
        _combine_kernel,
        grid=(r // tm,),
        in_specs=[pl.BlockSpec((1, TOP_K, tm), lambda i: (i, 0, 0), memory_space=pltpu.SMEM),
                  pl.BlockSpec((tm, D_MODEL), row),
                  pl.BlockSpec((tm, TOP_K), row),
                  pl.BlockSpec((1, D_MODEL), lambda i: (0, 0)),
                  pl.BlockSpec(memory_space=pl.ANY)],
        out_specs=pl.BlockSpec((tm, D_MODEL), row),
        out_shape=jax.ShapeDtypeStruct((r, D_MODEL), F32),
        scratch_shapes=[pltpu.VMEM((TOP_K, tm, D_MODEL), F32), pltpu.SemaphoreType.DMA],
        compiler_params=_cparams(("arbitrary",)),
        name="moe_combine",
    )(pos, h2d, gates_t, g_final, ys)


def _moe(h2d, xn2d, w, *, tm_router, tm_rows):
    idx, gates, rank, cnt = _router(xn2d, w["wr_t"], w["br"], w["tri"][:tm_router, :tm_router], tm=tm_router)
    counts = cnt[:, 0]
    n_rows = TOP_K * h2d.shape[0]
    blk, e, lo, hi, first, starts = _work_items(counts, n_rows)
    hot = idx[:, :, None] == jnp.arange(N_EXPERTS, dtype=jnp.int32)
    pos = jnp.sum(jnp.where(hot, starts.astype(jnp.int32), 0), axis=-1) + rank
    pos_tiles = pos.reshape(TOP_K, -1, tm_rows).transpose(1, 0, 2)
    xs = _scatter(pos_tiles, xn2d, tm=tm_rows)
    ys = _experts((blk, e, lo, hi, first), xs, w["wgu"], w["bgu"], w["wd"], w["bd"])
    return _combine(pos_tiles, h2d, gates.T, w["g_final"], ys, tm=tm_rows)


def _pick_tile(n, cap):
    t = cap
    while n % t:
        t //= 2
    return t


def kernel(x_prompt, x_sample, mem_prompt, cache_a_k, cache_a_v, cache_b_k, cache_b_v, cache_mem_k, cache_mem_v, g_mix, w_in, rel_bias, lam_q1, lam_k1, lam_q2, lam_k2, g_subln, w_out, g_xq, g_xmem, w_xq, w_xk, w_xv, w_xo, g_moe, w_router, b_router, w_gate_up, b_gate_up, w_down, b_down, g_final):
    assert g_mix.shape[0] == 1, "single-layer kernel"
    b, s, d = x_prompt.shape
    db, dl, _ = x_sample.shape
    past = cache_b_k.shape[2]
    a_len = cache_a_k.shape[2]
    mlen = mem_prompt.shape[1]
    keep_a = min(BAND, s)
    assert d == D_MODEL and s % B_KBLK == 0 and s >= A_WIN and dl == CHUNK and past % CHUNK == 0

    row1 = lambda v: v.reshape(1, -1).astype(F32)
    w_in_bf = w_in[0].astype(BF16)
    w_out_bf = w_out[0].astype(BF16)
    weights = dict(
        wr_t=w_router[0].T.astype(BF16), br=b_router[0].reshape(N_EXPERTS, 1).astype(F32),
        tri=jnp.asarray(np.triu(np.ones((512, 512), np.float32), 1), BF16),
        wgu=w_gate_up[0].astype(BF16), bgu=b_gate_up[0].reshape(N_EXPERTS, 1, -1).astype(F32),
        wd=w_down[0].astype(BF16), bd=b_down[0].reshape(N_EXPERTS, 1, -1).astype(F32),
        g_final=row1(g_final))
    lam = (jnp.exp(jnp.sum(lam_q1[0].astype(F32) * lam_k1[0].astype(F32)))
           - jnp.exp(jnp.sum(lam_q2[0].astype(F32) * lam_k2[0].astype(F32))) + LAM_INIT).reshape(1)
    table = _band_table(rel_bias[0])
    g_sub = row1(g_subln[0])

    tm_p = _pick_tile(s, 512)
    tabs_p = _rope_tables(jnp.arange(s, dtype=jnp.int32))
    w_bvt_bf = jnp.stack([w_in[0][:, 2 * A_WIDTH:3 * A_WIDTH].T,
                          w_in[0][:, 3 * A_WIDTH + 2 * B_WIDTH:].T]).astype(BF16)
    p_in = _inproj(x_prompt.reshape(b * s, d), row1(g_mix[0]), w_in_bf, w_bvt_bf, *tabs_p,
                   tm=tm_p, n_pos_blocks=s // tm_p, tail_div=s // tm_p if keep_a == tm_p else 1, emit_vt=True)
    tabs_s = _rope_tables(jnp.tile(past + jnp.arange(dl, dtype=jnp.int32), db))
    s_in = _inproj(x_sample.reshape(db * dl, d), row1(g_mix[0]), w_in_bf, w_bvt_bf, *tabs_s,
                   tm=db * dl, n_pos_blocks=1, tail_div=1, emit_vt=False)

    def split(outs, nb, nl):
        return [o.reshape(nb, nl, A_WIDTH) for o in outs[:6]]

    aq, ak, _, bq, bk, _ = split(p_in, b, s)
    avt = p_in[10].reshape(b, s // A_QBLK, A_WIDTH, A_QBLK)
    bvt = p_in[11].reshape(b, s // B_KBLK, B_WIDTH, B_KBLK)
    saq, sak, sav, sbq, sbk, sbv = split(s_in, db, dl)

    oa = _attn_a_prompt(aq, ak, avt, table.transpose(0, 2, 1))
    ob = _attn_b_prompt(lam, bq, bk, bvt, g_sub)
    soa = _attn_a_sample(saq, cache_a_k[0].reshape(db, a_len, A_WIDTH), cache_a_v[0].reshape(db, a_len, A_WIDTH),
                         sak, sav, table[:, :dl, :BAND + dl])
    sob = _attn_b_sample(lam, sbq, cache_b_k[0].reshape(db, past, B_WIDTH), cache_b_v[0].reshape(db, past, B_WIDTH),
                         sbk, sbv, g_sub)

    mkf, mvf, mkb, mvb = _memkv(mem_prompt.reshape(b * mlen, d), row1(g_xmem[0]),
                                w_xk[0].astype(BF16), w_xv[0].astype(BF16), tm=mlen)
    post_w = (w_out_bf[:A_WIDTH], w_out_bf[A_WIDTH:], row1(g_xq[0]), w_xq[0].astype(BF16))
    tm_post = _pick_tile(s, 256)
    hp, xnp_ = _post(x_prompt.reshape(b * s, d), oa.reshape(b * s, A_WIDTH), ob.reshape(b * s, B_WIDTH),
                     *post_w, mkb.reshape(b, mlen, d), mvb.reshape(b, mlen, d),
                     w_xo[0].astype(BF16), row1(g_moe[0]), tm=tm_post, tiles_per_batch=s // tm_post)
    hs, xns = _post(x_sample.reshape(db * dl, d), soa.reshape(db * dl, A_WIDTH), sob.reshape(db * dl, B_WIDTH),
                    *post_w, cache_mem_k[0].reshape(db, mlen, d).astype(BF16),
                    cache_mem_v[0].reshape(db, mlen, d).astype(BF16),
                    w_xo[0].astype(BF16), row1(g_moe[0]), tm=dl, tiles_per_batch=1)

    y_prompt = _moe(hp, xnp_, weights, tm_router=512, tm_rows=128).reshape(b, s, d)
    y_sample = _moe(hs, xns, weights, tm_router=_pick_tile(db * dl, 512), tm_rows=128).reshape(db, dl, d)

    def heads(x2d, nb, nl, nh):
        return x2d.reshape(1, nb, nl, nh, -1)

    akf, avf, bkf, bvf = p_in[6:10]
    if keep_a != tm_p:
        akf = akf.reshape(b, s, A_WIDTH)[:, s - keep_a:]
        avf = avf.reshape(b, s, A_WIDTH)[:, s - keep_a:]
    sakf, savf, sbkf, sbvf = s_in[6:]
    return (y_prompt, y_sample,
            heads(akf, b, keep_a, N_HEADS_A), heads(avf, b, keep_a, N_HEADS_A),
            heads(bkf, b, s, 2 * N_HEADS_B), heads(bvf, b, s, N_HEADS_B),
            heads(mkf, b, mlen, N_HEADS_M), heads(mvf, b, mlen, N_HEADS_M),
            heads(sakf, db, dl, N_HEADS_A), heads(savf, db, dl, N_HEADS_A),
            heads(sbkf, db, dl, 2 * N_HEADS_B), heads(sbvf, db, dl, N_HEADS_B))
```

```python
import functools
import math

import numpy as np
import jax
import jax.numpy as jnp
from jax import lax
from jax.experimental import pallas as pl
from jax.experimental.pallas import tpu as pltpu

F32 = jnp.float32
BF16 = jnp.bfloat16

D_MODEL = 1024
CHUNK = 64
LEFT_CHUNKS = 8
BAND = LEFT_CHUNKS * CHUNK
HEAD_DIM = 64
A_WIDTH = 512
B_WIDTH = 512
N_HEADS_A = 8
N_HEADS_B = 4
REL_CLIP = 128
ROT_DIM = 16
ROPE_THETA = 500000.0
N_HEADS_M = 4
HEAD_DIM_M = 256
N_EXPERTS = 32
TOP_K = 4
EXPERT_FF = 1024
SWIGLU_LIMIT = 7.0
SWIGLU_ALPHA = 1.702
RMS_EPS = 1e-5
NEG_INF = -1e30
LAM_INIT = 0.8 - 0.6 * math.exp(-0.3 * 0)
LOG2E = math.log2(math.e)

LANES = 128
A_QBLK = 4 * CHUNK
A_WIN = BAND + A_QBLK
B_QBLK = 256
B_KBLK = 512
MOE_ROWS = 256
X_SLABS = D_MODEL // 2 // LANES
Y_SLABS = D_MODEL // LANES
VMEM_LIMIT = 48 * 1024 * 1024


def _cparams(sem):
    return pltpu.CompilerParams(dimension_semantics=sem, vmem_limit_bytes=VMEM_LIMIT)


def _rms(x, g):
    return x * lax.rsqrt(jnp.mean(x * x, axis=-1, keepdims=True) + RMS_EPS) * g


HI16 = 0xFFFF0000


def _pack_bf16_pairs(x):
    n = x.shape[1] // 2
    bits = lambda v: lax.bitcast_convert_type(v.astype(BF16).astype(F32), jnp.uint32)
    return (bits(x[:, :n]) >> 16) | (bits(x[:, n:]) & jnp.uint32(HI16))


def _unpack_bf16_pairs(w):
    lo = lax.bitcast_convert_type(w << 16, F32)
    hi = lax.bitcast_convert_type(w & jnp.uint32(HI16), F32)
    return jnp.concatenate([lo, hi], axis=1).astype(BF16)


def _store_row_slabs(ref, x, lead=()):
    m, n = x.shape[0], x.shape[1] // LANES
    for c in range(n):
        ref[lead + (pl.ds(c, m, stride=n), slice(None))] = x[:, c * LANES:(c + 1) * LANES]


def _load_row_slabs(ref, m, n, lead=()):
    return jnp.concatenate([ref[lead + (pl.ds(c, m, stride=n), slice(None))] for c in range(n)], axis=1)


def _dot_nt(a, b):
    return lax.dot_general(a, b, (((1,), (1,)), ((), ())), preferred_element_type=F32)


def _inproj_kernel(x_ref, g_ref, w_ref, wvt_ref, cos_ref, sa_ref, sb_ref,
                   aq_ref, ak_ref, av_ref, bq_ref, bk_ref, bv_ref,
                   akf_ref, avf_ref, bkf_ref, bvf_ref, *maybe_vt_refs):
    xn = _rms(x_ref[...], g_ref[...]).astype(BF16)
    for n, vt_ref in enumerate(maybe_vt_refs):
        vt = _dot_nt(wvt_ref[n], xn).astype(BF16)
        blk = vt_ref.shape[2]
        for c in range(vt_ref.shape[0]):
            vt_ref[c] = vt[:, c * blk:(c + 1) * blk]

    def seg(s):
        return jnp.dot(xn, w_ref[:, s * A_WIDTH:(s + 1) * A_WIDTH], preferred_element_type=F32)

    def rope(z):
        cols = []
        for c in range(z.shape[1] // LANES):
            zc = z[:, c * LANES:(c + 1) * LANES]
            cols.append(zc * cos_ref[...] + pltpu.roll(zc, 8, 1) * sa_ref[...]
                        + pltpu.roll(zc, LANES - 8, 1) * sb_ref[...])
        return jnp.concatenate(cols, axis=1)

    scale = HEAD_DIM ** -0.5
    aq_ref[...] = (seg(0) * scale).astype(BF16)
    ak = seg(1)
    ak_ref[...] = ak.astype(BF16)
    akf_ref[...] = ak
    av = seg(2)
    av_ref[...] = av.astype(BF16)
    avf_ref[...] = av
    bq_ref[...] = (rope(seg(3)) * (scale * LOG2E)).astype(BF16)
    bk = rope(seg(4))
    bk_ref[...] = bk.astype(BF16)
    bkf_ref[...] = bk
    bv = seg(5)
    bv_ref[...] = bv.astype(BF16)
    bvf_ref[...] = bv


def _inproj(x2d, g, w_bf, wvt_bf, cos_t, sa_t, sb_t, *, tm, n_pos_blocks, tail_div, emit_vt):
    r = x2d.shape[0]
    n_tiles = r // tm
    row = lambda i: (i, 0)
    const = lambda i: (0, 0)
    pos = lambda i: (i % n_pos_blocks, 0)
    tail = lambda i: (i // tail_div, 0)
    bf = jax.ShapeDtypeStruct((r, A_WIDTH), BF16)
    f_all = jax.ShapeDtypeStruct((r, A_WIDTH), F32)
    f_tail = jax.ShapeDtypeStruct((r // tail_div, A_WIDTH), F32)
    blk = lambda im: pl.BlockSpec((tm, A_WIDTH), im)
    out_specs = [blk(row)] * 6 + [blk(tail), blk(tail), blk(row), blk(row)]
    out_shape = [bf] * 6 + [f_tail, f_tail, f_all, f_all]
    if emit_vt:
        for kb in (A_QBLK, B_KBLK):
            out_specs.append(pl.BlockSpec((tm // kb, A_WIDTH, kb), lambda i: (i, 0, 0)))
            out_shape.append(jax.ShapeDtypeStruct((r // kb, A_WIDTH, kb), BF16))
    return pl.pallas_call(
        _inproj_kernel,
        grid=(n_tiles,),
        in_specs=[pl.BlockSpec((tm, D_MODEL), row), pl.BlockSpec((1, D_MODEL), const),
                  pl.BlockSpec((D_MODEL, 3 * D_MODEL), const),
                  pl.BlockSpec((2, A_WIDTH, D_MODEL), lambda i: (0, 0, 0)),
                  pl.BlockSpec((tm, LANES), pos), pl.BlockSpec((tm, LANES), pos),
                  pl.BlockSpec((tm, LANES), pos)],
        out_specs=out_specs,
        out_shape=out_shape,
        compiler_params=_cparams(("arbitrary",)),
        name="inproj",
    )(x2d, g, w_bf, wvt_bf, cos_t, sa_t, sb_t)


def _rope_tables(positions):
    half = ROT_DIM // 2
    lane = np.arange(LANES) % HEAD_DIM
    inv_freq = jnp.power(ROPE_THETA, -jnp.arange(half, dtype=F32) / half)
    ang = positions.astype(F32)[:, None] * inv_freq[None, :]
    cos = jnp.cos(ang)[:, lane % half]
    sin = jnp.sin(ang)[:, lane % half]
    in_rot = jnp.asarray(lane < ROT_DIM)[None, :]
    first = jnp.asarray(lane < half)[None, :]
    cos_t = jnp.where(in_rot, cos, 1.0)
    sa_t = jnp.where(in_rot & ~first, sin, 0.0)
    sb_t = jnp.where(first, -sin, 0.0)
    return cos_t.astype(F32), sa_t.astype(F32), sb_t.astype(F32)


def _band_table(rel_bias):
    r = np.arange(A_QBLK)[:, None]
    j = np.arange(BAND + A_WIN)[None, :]
    dchunk = LEFT_CHUNKS + r // CHUNK - j // CHUNK
    allowed = (dchunk >= 0) & (dchunk <= LEFT_CHUNKS)
    bias = rel_bias.astype(F32)
    n_seq = BAND + A_WIN + A_QBLK - 1
    n_hi = BAND + A_QBLK - 1 - REL_CLIP
    seq = jnp.concatenate([jnp.broadcast_to(bias[:, -1:], (N_HEADS_A, n_hi)), bias[:, ::-1],
                           jnp.broadcast_to(bias[:, :1], (N_HEADS_A, n_seq - n_hi - bias.shape[1]))], axis=1)
    seq = jnp.roll(seq, -(A_QBLK - 1), axis=1)
    toe = jnp.tile(seq, (1, A_QBLK))[:, :A_QBLK * (n_seq - 1)].reshape(N_HEADS_A, A_QBLK, n_seq - 1)
    return jnp.where(jnp.asarray(allowed)[None], toe[:, :, :BAND + A_WIN], NEG_INF)


def _softmax_pv(s_list, v_list, exp_fn=jnp.exp):
    m = s_list[0].max(axis=-1, keepdims=True)
    for s in s_list[1:]:
        m = jnp.maximum(m, s.max(axis=-1, keepdims=True))
    l = None
    o = None
    for s, v in zip(s_list, v_list):
        p = exp_fn(s - m)
        ls = p.sum(axis=-1, keepdims=True)
        os_ = jnp.dot(p.astype(BF16), v, preferred_element_type=F32)
        l = ls if l is None else l + ls
        o = os_ if o is None else o + os_
    return o / l


def _attn_a_prompt_kernel(q_ref, k_ref, vt_ref, tab_ref, o_ref):
    i = pl.program_id(1)
    n_left = BAND // A_QBLK
    n_sub = A_WIN // A_QBLK
    blk0 = jnp.maximum(i - n_left, 0)
    off = pl.multiple_of(jnp.maximum(n_left - i, 0) * A_QBLK, A_QBLK)
    q = q_ref[0]
    lane = lax.broadcasted_iota(jnp.int32, (A_QBLK, LANES), 1)
    pair_outs = []
    for g in range(N_HEADS_A // 2):
        cols = slice(g * LANES, (g + 1) * LANES)
        qp = q[:, cols]
        zero = jnp.zeros_like(qp)
        kbs = [k_ref[0, pl.ds(pl.multiple_of((blk0 + c) * A_QBLK, A_QBLK), A_QBLK), cols] for c in range(n_sub)]
        vts = [vt_ref[0, blk0 + c, cols, :] for c in range(n_sub)]
        halves = []
        for e in range(2):
            h = 2 * g + e
            qh = jnp.where((lane >= HEAD_DIM) == bool(e), qp, zero)
            ss = [_dot_nt(kbs[c], qh) + tab_ref[h, pl.ds(off + c * A_QBLK, A_QBLK), :] for c in range(n_sub)]
            m = _col_reduce(ss[0], jnp.maximum, jnp.max)
            for s in ss[1:]:
                m = jnp.maximum(m, _col_reduce(s, jnp.maximum, jnp.max))
            l = None
            o_t = None
            for c in range(n_sub):
                p = jnp.exp(ss[c] - m)
                ls = _col_reduce(p, jnp.add, jnp.sum)
                os_ = jnp.dot(vts[c], p.astype(BF16), preferred_element_type=F32)
                l = ls if l is None else l + ls
                o_t = os_ if o_t is None else o_t + os_
            halves.append((o_t / l)[e * HEAD_DIM:(e + 1) * HEAD_DIM])
        pair_outs.append(jnp.concatenate(halves, axis=0).T)
    o_ref[0] = jnp.concatenate(pair_outs, axis=1).astype(BF16)


def _attn_a_prompt(aq, ak, avt, table_t):
    b, l, _ = aq.shape
    n_kb = avt.shape[1]
    return pl.pallas_call(
        _attn_a_prompt_kernel,
        grid=(b, l // A_QBLK),
        in_specs=[pl.BlockSpec((1, A_QBLK, A_WIDTH), lambda bb, i: (bb, i, 0)),
                  pl.BlockSpec((1, l, A_WIDTH), lambda bb, i: (bb, 0, 0)),
                  pl.BlockSpec((1, n_kb, A_WIDTH, A_QBLK), lambda bb, i: (bb, 0, 0, 0)),
                  pl.BlockSpec(table_t.shape, lambda bb, i: (0, 0, 0))],
        out_specs=pl.BlockSpec((1, A_QBLK, A_WIDTH), lambda bb, i: (bb, i, 0)),
        out_shape=jax.ShapeDtypeStruct((b, l, A_WIDTH), BF16),
        compiler_params=_cparams(("arbitrary", "arbitrary")),
        name="attn_a_prompt",
    )(aq, ak, avt, table_t)


def _attn_a_sample_kernel(q_ref, kc_ref, vc_ref, k_ref, v_ref, tab_ref, o_ref):
    q = q_ref[0]
    kc = kc_ref[0].astype(BF16)
    vc = vc_ref[0].astype(BF16)
    ko = k_ref[0]
    vo = v_ref[0]
    a_len = kc.shape[0]
    lq = q.shape[0]
    c0 = BAND - a_len
    outs = []
    for h in range(N_HEADS_A):
        sl = slice(h * HEAD_DIM, (h + 1) * HEAD_DIM)
        s_c = _dot_nt(q[:, sl], kc[:, sl]) + tab_ref[h, :, c0:BAND]
        s_o = _dot_nt(q[:, sl], ko[:, sl]) + tab_ref[h, :, BAND:BAND + lq]
        outs.append(_softmax_pv([s_c, s_o], [vc[:, sl], vo[:, sl]]))
    o_ref[0] = jnp.concatenate(outs, axis=1).astype(BF16)


def _attn_a_sample(aq, cache_k, cache_v, ak, av, table):
    b, l, _ = aq.shape
    a_len = cache_k.shape[1]
    bs = lambda n: pl.BlockSpec((1, n, A_WIDTH), lambda bb: (bb, 0, 0))
    return pl.pallas_call(
        _attn_a_sample_kernel,
        grid=(b,),
        in_specs=[bs(l), bs(a_len), bs(a_len), bs(l), bs(l),
                  pl.BlockSpec(table.shape, lambda bb: (0, 0, 0))],
        out_specs=bs(l),
        out_shape=jax.ShapeDtypeStruct((b, l, A_WIDTH), BF16),
        compiler_params=_cparams(("arbitrary",)),
        name="attn_a_sample",
    )(aq, cache_k, cache_v, ak, av, table)


def _subnorm(o, g):
    return _rms(o, g) * (1.0 - LAM_INIT)


def _col_reduce(x, op, reduce_fn):
    n = x.shape[0]
    while n > 8:
        n //= 2
        x = op(x[:n], x[n:])
    return reduce_fn(x, axis=0, keepdims=True)


def _attn_b_prompt_kernel(lam_ref, q_ref, k_ref, vt_ref, mask_ref, g_ref, o_ref,
                          sa_sc, sb_sc, m_sc, l_sc, acc_sc):
    i = pl.program_id(2)
    tk = vt_ref.shape[3]
    q = q_ref[0]
    lane = lax.broadcasted_iota(jnp.int32, q.shape, 1)
    zero = jnp.zeros_like(q)
    qs = (jnp.where(lane < HEAD_DIM, q, zero), jnp.where(lane >= HEAD_DIM, q, zero))
    m_sc[...] = jnp.full(m_sc.shape, NEG_INF, F32)
    l_sc[...] = jnp.zeros(l_sc.shape, F32)
    acc_sc[...] = jnp.zeros(acc_sc.shape, F32)

    def scores(j, s_sc):
        kb = k_ref[0, pl.ds(pl.multiple_of(j * tk, tk), tk), :]
        for a in range(2):
            s_sc[a] = _dot_nt(kb, qs[a])

    def consume(j, s_sc, mask):
        vt = vt_ref[0, j]
        for a in range(2):
            s = s_sc[a]
            if mask is not None:
                s = s + mask
            m_old = m_sc[a]
            m_new = jnp.maximum(m_old, _col_reduce(s, jnp.maximum, jnp.max))
            alpha = jnp.exp2(m_old - m_new)
            p = jnp.exp2(s - m_new)
            l_sc[a] = alpha * l_sc[a] + _col_reduce(p, jnp.add, jnp.sum)
            m_sc[a] = m_new
            acc_sc[a] = alpha * acc_sc[a] + jnp.dot(vt, p.astype(BF16), preferred_element_type=F32)

    per = tk // q.shape[0]
    last = i // per
    mask = mask_ref[i % per]
    scores(0, sa_sc)

    def body(jj, c):
        j = 2 * jj
        scores(j + 1, sb_sc)
        consume(j, sa_sc, None)
        scores(j + 2, sa_sc)
        consume(j + 1, sb_sc, None)
        return c

    lax.fori_loop(0, last // 2, body, 0)

    @pl.when(last % 2 == 0)
    def _():
        consume(last, sa_sc, mask)

    @pl.when(last % 2 == 1)
    def _():
        scores(last, sb_sc)
        consume(last - 1, sa_sc, None)
        consume(last, sb_sc, mask)

    o_t = acc_sc[0] / l_sc[0] - lam_ref[0] * (acc_sc[1] / l_sc[1])
    o_ref[0] = _subnorm(o_t.T, g_ref[...]).astype(BF16)


def _chunk_causal_masks_t(tk, tq):
    kc = np.arange(tk)[None, :, None] // CHUNK
    qc = (np.arange(tk // tq)[:, None, None] * tq + np.arange(tq)[None, None, :]) // CHUNK
    return jnp.asarray(np.where(kc <= qc, 0.0, NEG_INF), F32)


def _attn_b_prompt(lam, bq, bk, bvt, g_sub):
    b, l, _ = bq.shape
    tq = B_QBLK
    tk = B_KBLK
    hw = 2 * HEAD_DIM
    n_kb = bvt.shape[1]
    return pl.pallas_call(
        _attn_b_prompt_kernel,
        grid_spec=pltpu.PrefetchScalarGridSpec(
            num_scalar_prefetch=1,
            grid=(b, N_HEADS_B, l // tq),
            in_specs=[pl.BlockSpec((1, tq, hw), lambda bb, h, i, lam_: (bb, i, h)),
                      pl.BlockSpec((1, l, hw), lambda bb, h, i, lam_: (bb, 0, h)),
                      pl.BlockSpec((1, n_kb, hw, tk), lambda bb, h, i, lam_: (bb, 0, h, 0)),
                      pl.BlockSpec((tk // tq, tk, tq), lambda bb, h, i, lam_: (0, 0, 0)),
                      pl.BlockSpec((1, hw), lambda bb, h, i, lam_: (0, 0))],
            out_specs=pl.BlockSpec((1, tq, hw), lambda bb, h, i, lam_: (bb, i, h)),
            scratch_shapes=[pltpu.VMEM((2, tk, tq), F32), pltpu.VMEM((2, tk, tq), F32),
                            pltpu.VMEM((2, 1, tq), F32), pltpu.VMEM((2, 1, tq), F32),
                            pltpu.VMEM((2, hw, tq), F32)]),
        out_shape=jax.ShapeDtypeStruct((b, l, B_WIDTH), BF16),
        compiler_params=_cparams(("arbitrary", "arbitrary", "arbitrary")),
        name="attn_b_prompt",
    )(lam, bq, bk, bvt, _chunk_causal_masks_t(tk, tq), g_sub)


def _attn_b_sample_kernel(lam_ref, q_ref, kc_ref, vc_ref, k_ref, v_ref, g_ref, o_ref):
    q = q_ref[0]
    kc = kc_ref[0].astype(BF16)
    vc = vc_ref[0].astype(BF16)
    ko = k_ref[0]
    vo = v_ref[0]
    outs = []
    for a in range(2):
        sl = slice(a * HEAD_DIM, (a + 1) * HEAD_DIM)
        s_c = _dot_nt(q[:, sl], kc[:, sl])
        s_o = _dot_nt(q[:, sl], ko[:, sl])
        outs.append(_softmax_pv([s_c, s_o], [vc, vo], exp_fn=jnp.exp2))
    o = outs[0] - lam_ref[0] * outs[1]
    o_ref[0] = _subnorm(o, g_ref[...]).astype(BF16)


def _attn_b_sample(lam, bq, cache_k, cache_v, bk, bv, g_sub):
    b, l, _ = bq.shape
    p = cache_k.shape[1]
    hw = 2 * HEAD_DIM
    bs = lambda n: pl.BlockSpec((1, n, hw), lambda bb, h, lam_: (bb, 0, h))
    return pl.pallas_call(
        _attn_b_sample_kernel,
        grid_spec=pltpu.PrefetchScalarGridSpec(
            num_scalar_prefetch=1,
            grid=(b, N_HEADS_B),
            in_specs=[bs(l), bs(p), bs(p), bs(l), bs(l),
                      pl.BlockSpec((1, hw), lambda bb, h, lam_: (0, 0))],
            out_specs=bs(l)),
        out_shape=jax.ShapeDtypeStruct((b, l, B_WIDTH), BF16),
        compiler_params=_cparams(("arbitrary", "arbitrary")),
        name="attn_b_sample",
    )(lam, bq, cache_k, cache_v, bk, bv, g_sub)


def _memkv_kernel(m_ref, g_ref, wk_ref, wv_ref, kf_ref, vf_ref, kb_ref, vb_ref):
    mn = _rms(m_ref[...], g_ref[...]).astype(BF16)
    k = jnp.dot(mn, wk_ref[...], preferred_element_type=F32)
    v = jnp.dot(mn, wv_ref[...], preferred_element_type=F32)
    kf_ref[...] = k
    vf_ref[...] = v
    kb_ref[...] = k.astype(BF16)
    vb_ref[...] = v.astype(BF16)


def _memkv(mem2d, g, wk, wv, *, tm):
    r = mem2d.shape[0]
    row = lambda i: (i, 0)
    const = lambda i: (0, 0)
    blk = pl.BlockSpec((tm, D_MODEL), row)
    wspec = pl.BlockSpec((D_MODEL, D_MODEL), const)
    f = jax.ShapeDtypeStruct((r, D_MODEL), F32)
    bf = jax.ShapeDtypeStruct((r, D_MODEL), BF16)
    return pl.pallas_call(
        _memkv_kernel,
        grid=(r // tm,),
        in_specs=[blk, pl.BlockSpec((1, D_MODEL), const), wspec, wspec],
        out_specs=[blk] * 4,
        out_shape=[f, f, bf, bf],
        compiler_params=_cparams(("arbitrary",)),
        name="memkv",
    )(mem2d, g, wk, wv)


def _post_kernel(x_ref, oa_ref, ob_ref, woa_ref, wob_ref, gq_ref, wq_ref, mk_ref, mv_ref, wo_ref,
                 gm_ref, h_ref, xn_ref):
    h1 = (x_ref[...] + jnp.dot(oa_ref[...], woa_ref[...], preferred_element_type=F32)
          + jnp.dot(ob_ref[...], wob_ref[...], preferred_element_type=F32))
    xq = _rms(h1, gq_ref[...]).astype(BF16)
    q = (jnp.dot(xq, wq_ref[...], preferred_element_type=F32) * HEAD_DIM_M ** -0.5).astype(BF16)
    outs = []
    for h in range(N_HEADS_M):
        sl = slice(h * HEAD_DIM_M, (h + 1) * HEAD_DIM_M)
        s = _dot_nt(q[:, sl], mk_ref[0, :, sl])
        outs.append(_softmax_pv([s], [mv_ref[0, :, sl]]).astype(BF16))
    o = jnp.concatenate(outs, axis=1)
    h2 = h1 + jnp.dot(o, wo_ref[...], preferred_element_type=F32)
    h_ref[...] = h2
    _store_row_slabs(xn_ref, _pack_bf16_pairs(_rms(h2, gm_ref[...])))


def _post(x2d, oa, ob, woa, wob, gq, wq, mk, mv, wo, gm, *, tm, tiles_per_batch):
    r = x2d.shape[0]
    mlen = mk.shape[1]
    row = lambda i: (i, 0)
    const = lambda i: (0, 0)
    memb = lambda i: (i // tiles_per_batch, 0, 0)
    gspec = pl.BlockSpec((1, D_MODEL), const)
    wspec = pl.BlockSpec((D_MODEL, D_MODEL), const)
    hspec = pl.BlockSpec((A_WIDTH, D_MODEL), const)
    return pl.pallas_call(
        _post_kernel,
        grid=(r // tm,),
        in_specs=[pl.BlockSpec((tm, D_MODEL), row),
                  pl.BlockSpec((tm, A_WIDTH), row), pl.BlockSpec((tm, B_WIDTH), row),
                  hspec, hspec, gspec, wspec,
                  pl.BlockSpec((1, mlen, D_MODEL), memb), pl.BlockSpec((1, mlen, D_MODEL), memb),
                  wspec, gspec],
        out_specs=[pl.BlockSpec((tm, D_MODEL), row), pl.BlockSpec((tm * X_SLABS, LANES), row)],
        out_shape=[jax.ShapeDtypeStruct((r, D_MODEL), F32),
                   jax.ShapeDtypeStruct((r * X_SLABS, LANES), jnp.uint32)],
        compiler_params=_cparams(("arbitrary",)),
        name="post_attn",
    )(x2d, oa, ob, woa, wob, gq, wq, mk, mv, wo, gm)


def _router_kernel(xn_ref, wr_ref, br_ref, tri_ref, idx_ref, gate_ref, rank_ref, cnt_ref, base_sc):
    @pl.when(pl.program_id(0) == 0)
    def _():
        base_sc[...] = jnp.zeros(base_sc.shape, F32)

    tm = xn_ref.shape[0] // X_SLABS
    xn = _unpack_bf16_pairs(_load_row_slabs(xn_ref, tm, X_SLABS))
    logits = _dot_nt(wr_ref[...], xn) + br_ref[...]
    iota_e = lax.broadcasted_iota(jnp.int32, (N_EXPERTS, tm), 0).astype(F32)
    work = logits
    vals, idxs, hots = [], [], []
    for _k in range(TOP_K):
        mx = work.max(axis=0, keepdims=True)
        ix = jnp.where(work == mx, iota_e, float(N_EXPERTS)).min(axis=0, keepdims=True)
        hot = iota_e == ix
        vals.append(mx)
        idxs.append(ix)
        hots.append(hot)
        work = jnp.where(hot, -jnp.inf, work)
    es = [jnp.exp(v - vals[0]) for v in vals]
    den = es[0] + es[1] + es[2] + es[3]
    cnt = (hots[0] | hots[1] | hots[2] | hots[3]).astype(F32)
    before = jnp.dot(cnt.astype(BF16), tri_ref[...], preferred_element_type=F32) + base_sc[...]
    for k in range(TOP_K):
        idx_ref[k:k + 1, :] = idxs[k].astype(jnp.int32)
        gate_ref[k:k + 1, :] = es[k] / den
        rank_ref[k:k + 1, :] = jnp.where(hots[k], before, 0.0).sum(axis=0, keepdims=True).astype(jnp.int32)
    base_sc[...] = base_sc[...] + cnt.sum(axis=1, keepdims=True)
    cnt_ref[...] = jnp.broadcast_to(base_sc[...], cnt_ref.shape).astype(jnp.int32)


def _router(xn2d, wr_t, br, tri, *, tm):
    r = xn2d.shape[0] // X_SLABS
    col = lambda i: (0, i)
    const = lambda i: (0, 0)
    return pl.pallas_call(
        _router_kernel,
        grid=(r // tm,),
        in_specs=[pl.BlockSpec((tm * X_SLABS, LANES), lambda i: (i, 0)),
                  pl.BlockSpec((N_EXPERTS, D_MODEL), const),
                  pl.BlockSpec((N_EXPERTS, 1), const),
                  pl.BlockSpec((tm, tm), const)],
        out_specs=[pl.BlockSpec((TOP_K, tm), col)] * 3 + [pl.BlockSpec((N_EXPERTS, LANES), const)],
        out_shape=[jax.ShapeDtypeStruct((TOP_K, r), jnp.int32),
                   jax.ShapeDtypeStruct((TOP_K, r), F32),
                   jax.ShapeDtypeStruct((TOP_K, r), jnp.int32),
                   jax.ShapeDtypeStruct((N_EXPERTS, LANES), jnp.int32)],
        scratch_shapes=[pltpu.VMEM((N_EXPERTS, 1), F32)],
        compiler_params=_cparams(("arbitrary",)),
        name="router",
    )(xn2d, wr_t, br, tri)


def _row_copy(src, dst, sem):
    return pltpu.make_async_copy(src, dst, sem)


def _slab(ref, row, n, lead=()):
    return ref.at[lead + (pl.ds(pl.multiple_of(row * n, n), n), slice(None))]


def _scatter_kernel(pos_ref, xn_ref, xs_ref, sem):
    tm = xn_ref.shape[0] // X_SLABS

    def start(t, c):
        for k in range(TOP_K):
            _row_copy(_slab(xn_ref, t, X_SLABS), _slab(xs_ref, pos_ref[0, k, t], X_SLABS),
                      sem).start(priority=k % 2)
        return c

    lax.fori_loop(0, tm, start, 0, unroll=2)

    def wait(t, c):
        for k in range(TOP_K):
            _row_copy(_slab(xn_ref, 0, X_SLABS), _slab(xs_ref, 0, X_SLABS), sem).wait()
        return c

    lax.fori_loop(0, tm, wait, 0, unroll=4)


def _scatter(pos, xn2d, *, tm):
    r = xn2d.shape[0] // X_SLABS
    return pl.pallas_call(
        _scatter_kernel,
        grid=(r // tm,),
        in_specs=[pl.BlockSpec((1, TOP_K, tm), lambda i: (i, 0, 0), memory_space=pltpu.SMEM),
                  pl.BlockSpec((tm * X_SLABS, LANES), lambda i: (i, 0))],
        out_specs=pl.BlockSpec(memory_space=pl.ANY),
        out_shape=jax.ShapeDtypeStruct((TOP_K * r * X_SLABS, LANES), jnp.uint32),
        scratch_shapes=[pltpu.SemaphoreType.DMA],
        compiler_params=_cparams(("arbitrary",)),
        name="moe_scatter",
    )(pos, xn2d)


def _experts_kernel(blk_ref, exp_ref, lo_ref, hi_ref, first_ref,
                    x_ref, wgu_ref, bgu_ref, wd_ref, bd_ref, y_ref):
    i = pl.program_id(0)
    x = _unpack_bf16_pairs(_load_row_slabs(x_ref, MOE_ROWS, X_SLABS))
    gu = jnp.dot(x, wgu_ref[0], preferred_element_type=F32) + bgu_ref[0]
    gate = jnp.minimum(gu[:, :EXPERT_FF], SWIGLU_LIMIT)
    up = jnp.clip(gu[:, EXPERT_FF:], -SWIGLU_LIMIT, SWIGLU_LIMIT)
    act = (up + 1.0) * (gate * jax.nn.sigmoid(SWIGLU_ALPHA * gate))
    y = jnp.dot(act.astype(BF16), wd_ref[0], preferred_element_type=F32) + bd_ref[0]
    row0 = blk_ref[i] * MOE_ROWS
    whole = (lo_ref[i] <= row0) & (hi_ref[i] >= row0 + MOE_ROWS)

    @pl.when(whole)
    def _():
        _store_row_slabs(y_ref, y)

    @pl.when(jnp.logical_not(whole))
    def _():
        @pl.when(first_ref[i] == 1)
        def _():
            y_ref[...] = jnp.zeros(y_ref.shape, F32)

        rows = row0 + lax.broadcasted_iota(jnp.int32, (MOE_ROWS, 1), 0)
        mine = (rows >= lo_ref[i]) & (rows < hi_ref[i])
        _store_row_slabs(y_ref, jnp.where(mine, y, _load_row_slabs(y_ref, MOE_ROWS, Y_SLABS)))


def _experts(items, xs, wgu, bgu, wd, bd):
    blk, exp, lo, hi, first = items
    n_items = blk.shape[0]
    n_rows = xs.shape[0] // X_SLABS
    return pl.pallas_call(
        _experts_kernel,
        grid_spec=pltpu.PrefetchScalarGridSpec(
            num_scalar_prefetch=5,
            grid=(n_items,),
            in_specs=[pl.BlockSpec((MOE_ROWS * X_SLABS, LANES), lambda i, b, e, l, h, f: (b[i], 0)),
                      pl.BlockSpec((1, D_MODEL, 2 * EXPERT_FF), lambda i, b, e, l, h, f: (e[i], 0, 0)),
                      pl.BlockSpec((1, 1, 2 * EXPERT_FF), lambda i, b, e, l, h, f: (e[i], 0, 0)),
                      pl.BlockSpec((1, EXPERT_FF, D_MODEL), lambda i, b, e, l, h, f: (e[i], 0, 0)),
                      pl.BlockSpec((1, 1, D_MODEL), lambda i, b, e, l, h, f: (e[i], 0, 0))],
            out_specs=pl.BlockSpec((MOE_ROWS * Y_SLABS, LANES), lambda i, b, e, l, h, f: (b[i], 0))),
        out_shape=jax.ShapeDtypeStruct((n_rows * Y_SLABS, LANES), F32),
        compiler_params=_cparams(("arbitrary",)),
        name="moe_experts",
    )(blk, exp, lo, hi, first, xs, wgu, bgu, wd, bd)


def _work_items(counts, n_rows):
    n_blocks = n_rows // MOE_ROWS
    n_items = n_blocks + N_EXPERTS - 1
    ends = jnp.cumsum(counts)
    starts = ends - counts
    nb = jnp.where(counts > 0, (ends - 1) // MOE_ROWS - starts // MOE_ROWS + 1, 0)
    item_end = jnp.cumsum(nb)
    item_start = item_end - nb
    total = item_end[-1]
    i = jnp.arange(n_items, dtype=jnp.int32)
    e = jnp.minimum(jnp.sum(item_end[None, :] <= i[:, None], axis=1), N_EXPERTS - 1).astype(jnp.int32)
    blk = (starts[e] // MOE_ROWS + (i - item_start[e])).astype(jnp.int32)
    valid = i < total
    lo = jnp.maximum(starts[e], blk * MOE_ROWS)
    hi = jnp.minimum(ends[e], (blk + 1) * MOE_ROWS)
    last = jnp.maximum(total - 1, 0)
    blk = jnp.where(valid, blk, blk[last])
    e = jnp.where(valid, e, e[last])
    lo = jnp.where(valid, lo, 0).astype(jnp.int32)
    hi = jnp.where(valid, hi, 0).astype(jnp.int32)
    prev = jnp.concatenate([jnp.full((1,), -1, jnp.int32), blk[:-1]])
    first = (blk != prev).astype(jnp.int32)
    return blk, e, lo, hi, first, starts


def _combine_kernel(pos_ref, pos_next_ref, h_ref, gate_ref, g_ref, ys_ref, o_ref, buf, sem):
    tm = h_ref.shape[0]
    i = pl.program_id(0)
    n = pl.num_programs(0)
    cur = i % 2

    def fetch(p_ref, slot):
        def start(t, c):
            for k in range(TOP_K):
                _row_copy(_slab(ys_ref, p_ref[0, k, t], Y_SLABS), _slab(buf, t, Y_SLABS, (slot, k)),
                          sem.at[slot]).start(priority=k % 2)
            return c

        lax.fori_loop(0, tm, start, 0, unroll=2)

    @pl.when(i == 0)
    def _():
        fetch(pos_ref, 0)

    @pl.when(i + 1 < n)
    def _():
        fetch(pos_next_ref, 1 - cur)

    def wait(t, c):
        for k in range(TOP_K):
            _row_copy(_slab(ys_ref, 0, Y_SLABS), _slab(buf, 0, Y_SLABS, (cur, 0)), sem.at[cur]).wait()
        return c

    lax.fori_loop(0, tm, wait, 0, unroll=4)
    acc = h_ref[...]
    for k in range(TOP_K):
        acc = acc + gate_ref[:, k:k + 1] * _load_row_slabs(buf, tm, Y_SLABS, (cur, k))
    o_ref[...] = _rms(acc, g_ref[...])


def _combine(pos, h2d, gates_t, g_final, ys, *, tm):
    r = h2d.shape[0]
    n_tiles = r // tm
    row = lambda i: (i, 0)
    return pl.pallas_call(
        _combine_kernel,
        grid=(n_tiles,),
        in_specs=[pl.BlockSpec((1, TOP_K, tm), lambda i: (i, 0, 0), memory_space=pltpu.SMEM),
                  pl.BlockSpec((1, TOP_K, tm), lambda i: (jnp.minimum(i + 1, n_tiles - 1), 0, 0),
                               memory_space=pltpu.SMEM),
                  pl.BlockSpec((tm, D_MODEL), row),
                  pl.BlockSpec((tm, TOP_K), row),
                  pl.BlockSpec((1, D_MODEL), lambda i: (0, 0)),
                  pl.BlockSpec(memory_space=pl.ANY)],
        out_specs=pl.BlockSpec((tm, D_MODEL), row),
        out_shape=jax.ShapeDtypeStruct((r, D_MODEL), F32),
        scratch_shapes=[pltpu.VMEM((2, TOP_K, tm * Y_SLABS, LANES), F32), pltpu.SemaphoreType.DMA((2,))],
        compiler_params=_cparams(("arbitrary",)),
        name="moe_combine",
    )(pos, pos, h2d, gates_t, g_final, ys)


def _moe(h2d, xn2d, w):
    r = h2d.shape[0]
    tm_router = _pick_tile(r, 512)
    tm_scatter = _pick_tile(r, 512)
    tm_combine = _pick_tile(r, 256)
    idx, gates, rank, cnt = _router(xn2d, w["wr_t"], w["br"], w["tri"][:tm_router, :tm_router], tm=tm_router)
    counts = cnt[:, 0]
    blk, e, lo, hi, first, starts = _work_items(counts, TOP_K * r)
    hot = idx[:, :, None] == jnp.arange(N_EXPERTS, dtype=jnp.int32)
    pos = jnp.sum(jnp.where(hot, starts.astype(jnp.int32), 0), axis=-1) + rank
    tiles = lambda tm: pos.reshape(TOP_K, -1, tm).transpose(1, 0, 2)
    xs = _scatter(tiles(tm_scatter), xn2d, tm=tm_scatter)
    ys = _experts((blk, e, lo, hi, first), xs, w["wgu"], w["bgu"], w["wd"], w["bd"])
    return _combine(tiles(tm_combine), h2d, gates.T, w["g_final"], ys, tm=tm_combine)


def _pick_tile(n, cap):
    t = cap
    while n % t:
        t //= 2
    return t


def kernel(x_prompt, x_sample, mem_prompt, cache_a_k, cache_a_v, cache_b_k, cache_b_v, cache_mem_k, cache_mem_v, g_mix, w_in, rel_bias, lam_q1, lam_k1, lam_q2, lam_k2, g_subln, w_out, g_xq, g_xmem, w_xq, w_xk, w_xv, w_xo, g_moe, w_router, b_router, w_gate_up, b_gate_up, w_down, b_down, g_final):
    assert g_mix.shape[0] == 1, "single-layer kernel"
    b, s, d = x_prompt.shape
    db, dl, _ = x_sample.shape
    past = cache_b_k.shape[2]
    a_len = cache_a_k.shape[2]
    mlen = mem_prompt.shape[1]
    keep_a = min(BAND, s)
    assert d == D_MODEL and s % B_KBLK == 0 and s >= A_WIN and dl == CHUNK and past % CHUNK == 0

    row1 = lambda v: v.reshape(1, -1).astype(F32)
    w_in_bf = w_in[0].astype(BF16)
    w_out_bf = w_out[0].astype(BF16)
    weights = dict(
        wr_t=w_router[0].T.astype(BF16), br=b_router[0].reshape(N_EXPERTS, 1).astype(F32),
        tri=jnp.asarray(np.triu(np.ones((512, 512), np.float32), 1), BF16),
        wgu=w_gate_up[0].astype(BF16), bgu=b_gate_up[0].reshape(N_EXPERTS, 1, -1).astype(F32),
        wd=w_down[0].astype(BF16), bd=b_down[0].reshape(N_EXPERTS, 1, -1).astype(F32),
        g_final=row1(g_final))
    lam = (jnp.exp(jnp.sum(lam_q1[0].astype(F32) * lam_k1[0].astype(F32)))
           - jnp.exp(jnp.sum(lam_q2[0].astype(F32) * lam_k2[0].astype(F32))) + LAM_INIT).reshape(1)
    table = _band_table(rel_bias[0])
    g_sub = row1(g_subln[0])

    tm_p = _pick_tile(s, 512)
    tabs_p = _rope_tables(jnp.arange(s, dtype=jnp.int32))
    w_bvt_bf = jnp.stack([w_in[0][:, 2 * A_WIDTH:3 * A_WIDTH].T,
                          w_in[0][:, 3 * A_WIDTH + 2 * B_WIDTH:].T]).astype(BF16)
    p_in = _inproj(x_prompt.reshape(b * s, d), row1(g_mix[0]), w_in_bf, w_bvt_bf, *tabs_p,
                   tm=tm_p, n_pos_blocks=s // tm_p, tail_div=s // tm_p if keep_a == tm_p else 1, emit_vt=True)
    tabs_s = _rope_tables(jnp.tile(past + jnp.arange(dl, dtype=jnp.int32), db))
    s_in = _inproj(x_sample.reshape(db * dl, d), row1(g_mix[0]), w_in_bf, w_bvt_bf, *tabs_s,
                   tm=db * dl, n_pos_blocks=1, tail_div=1, emit_vt=False)

    def split(outs, nb, nl):
        return [o.reshape(nb, nl, A_WIDTH) for o in outs[:6]]

    aq, ak, _, bq, bk, _ = split(p_in, b, s)
    avt = p_in[10].reshape(b, s // A_QBLK, A_WIDTH, A_QBLK)
    bvt = p_in[11].reshape(b, s // B_KBLK, B_WIDTH, B_KBLK)
    saq, sak, sav, sbq, sbk, sbv = split(s_in, db, dl)

    oa = _attn_a_prompt(aq, ak, avt, table.transpose(0, 2, 1))
    ob = _attn_b_prompt(lam, bq, bk, bvt, g_sub)
    soa = _attn_a_sample(saq, cache_a_k[0].reshape(db, a_len, A_WIDTH), cache_a_v[0].reshape(db, a_len, A_WIDTH),
                         sak, sav, table[:, :dl, :BAND + dl])
    sob = _attn_b_sample(lam, sbq, cache_b_k[0].reshape(db, past, B_WIDTH), cache_b_v[0].reshape(db, past, B_WIDTH),
                         sbk, sbv, g_sub)

    mkf, mvf, mkb, mvb = _memkv(mem_prompt.reshape(b * mlen, d), row1(g_xmem[0]),
                                w_xk[0].astype(BF16), w_xv[0].astype(BF16), tm=mlen)
    post_w = (w_out_bf[:A_WIDTH], w_out_bf[A_WIDTH:], row1(g_xq[0]), w_xq[0].astype(BF16))
    tm_post = _pick_tile(s, 256)
    hp, xnp_ = _post(x_prompt.reshape(b * s, d), oa.reshape(b * s, A_WIDTH), ob.reshape(b * s, B_WIDTH),
                     *post_w, mkb.reshape(b, mlen, d), mvb.reshape(b, mlen, d),
                     w_xo[0].astype(BF16), row1(g_moe[0]), tm=tm_post, tiles_per_batch=s // tm_post)
    hs, xns = _post(x_sample.reshape(db * dl, d), soa.reshape(db * dl, A_WIDTH), sob.reshape(db * dl, B_WIDTH),
                    *post_w, cache_mem_k[0].reshape(db, mlen, d).astype(BF16),
                    cache_mem_v[0].reshape(db, mlen, d).astype(BF16),
                    w_xo[0].astype(BF16), row1(g_moe[0]), tm=dl, tiles_per_batch=1)

    y_prompt = _moe(hp, xnp_, weights).reshape(b, s, d)
    y_sample = _moe(hs, xns, weights).reshape(db, dl, d)

    def heads(x2d, nb, nl, nh):
        return x2d.reshape(1, nb, nl, nh, -1)

    akf, avf, bkf, bvf = p_in[6:10]
    if keep_a != tm_p:
        akf = akf.reshape(b, s, A_WIDTH)[:, s - keep_a:]
        avf = avf.reshape(b, s, A_WIDTH)[:, s - keep_a:]
    sakf, savf, sbkf, sbvf = s_in[6:]
    return (y_prompt, y_sample,
            heads(akf, b, keep_a, N_HEADS_A), heads(avf, b, keep_a, N_HEADS_A),
            heads(bkf, b, s, 2 * N_HEADS_B), heads(bvf, b, s, N_HEADS_B),
            heads(mkf, b, mlen, N_HEADS_M), heads(mvf, b, mlen, N_HEADS_M),
            heads(sakf, db, dl, N_HEADS_A), heads(savf, db, dl, N_HEADS_A),
            heads(sbkf, db, dl, 2 * N_HEADS_B), heads(sbvf, db, dl, N_HEADS_B))
```

```python
import functools
import math

import numpy as np
import jax
import jax.numpy as jnp
from jax import lax
from jax.experimental import pallas as pl
from jax.experimental.pallas import tpu as pltpu

F32 = jnp.float32
BF16 = jnp.bfloat16

D_MODEL = 1024
CHUNK = 64
LEFT_CHUNKS = 8
BAND = LEFT_CHUNKS * CHUNK
HEAD_DIM = 64
A_WIDTH = 512
B_WIDTH = 512
N_HEADS_A = 8
N_HEADS_B = 4
REL_CLIP = 128
ROT_DIM = 16
ROPE_THETA = 500000.0
N_HEADS_M = 4
HEAD_DIM_M = 256
N_EXPERTS = 32
TOP_K = 4
EXPERT_FF = 1024
SWIGLU_LIMIT = 7.0
SWIGLU_ALPHA = 1.702
RMS_EPS = 1e-5
NEG_INF = -1e30
LAM_INIT = 0.8 - 0.6 * math.exp(-0.3 * 0)
LOG2E = math.log2(math.e)

LANES = 128
A_QBLK = 4 * CHUNK
A_WIN = BAND + A_QBLK
B_QBLK = 512
B_KBLK = 256
MOE_ROWS = 256
X_SLABS = D_MODEL // 2 // LANES
Y_SLABS = D_MODEL // LANES
VMEM_LIMIT = 48 * 1024 * 1024
EXPERTS_VMEM_LIMIT = 56 * 1024 * 1024


def _cparams(sem):
    return pltpu.CompilerParams(dimension_semantics=sem, vmem_limit_bytes=VMEM_LIMIT)


def _rms(x, g):
    return x * lax.rsqrt(jnp.mean(x * x, axis=-1, keepdims=True) + RMS_EPS) * g


HI16 = 0xFFFF0000


def _pack_bf16_pairs(x):
    n = x.shape[1] // 2
    bits = lambda v: lax.bitcast_convert_type(v.astype(BF16).astype(F32), jnp.uint32)
    return (bits(x[:, :n]) >> 16) | (bits(x[:, n:]) & jnp.uint32(HI16))


def _unpack_bf16_pairs(w):
    lo = lax.bitcast_convert_type(w << 16, F32)
    hi = lax.bitcast_convert_type(w & jnp.uint32(HI16), F32)
    return jnp.concatenate([lo, hi], axis=1).astype(BF16)


def _store_row_slabs(ref, x, lead=()):
    m, n = x.shape[0], x.shape[1] // LANES
    for c in range(n):
        ref[lead + (pl.ds(c, m, stride=n), slice(None))] = x[:, c * LANES:(c + 1) * LANES]


def _load_row_slabs(ref, m, n, lead=()):
    return jnp.concatenate([ref[lead + (pl.ds(c, m, stride=n), slice(None))] for c in range(n)], axis=1)


def _dot_nt(a, b):
    return lax.dot_general(a, b, (((1,), (1,)), ((), ())), preferred_element_type=F32)


def _inproj_kernel(x_ref, g_ref, w_ref, wvt_ref, cos_ref, sa_ref, sb_ref,
                   aq_ref, ak_ref, av_ref, bq_ref, bk_ref, bv_ref,
                   akf_ref, avf_ref, bkf_ref, bvf_ref, *maybe_vt_refs):
    xn = _rms(x_ref[...], g_ref[...]).astype(BF16)
    for n, vt_ref in enumerate(maybe_vt_refs):
        vt = _dot_nt(wvt_ref[n], xn).astype(BF16)
        blk = vt_ref.shape[2]
        for c in range(vt_ref.shape[0]):
            vt_ref[c] = vt[:, c * blk:(c + 1) * blk]

    def seg(s):
        return jnp.dot(xn, w_ref[:, s * A_WIDTH:(s + 1) * A_WIDTH], preferred_element_type=F32)

    def rope(z):
        cols = []
        for c in range(z.shape[1] // LANES):
            zc = z[:, c * LANES:(c + 1) * LANES]
            cols.append(zc * cos_ref[...] + pltpu.roll(zc, 8, 1) * sa_ref[...]
                        + pltpu.roll(zc, LANES - 8, 1) * sb_ref[...])
        return jnp.concatenate(cols, axis=1)

    scale = HEAD_DIM ** -0.5
    aq_ref[...] = (seg(0) * scale).astype(BF16)
    ak = seg(1)
    ak_ref[...] = ak.astype(BF16)
    akf_ref[...] = ak
    av = seg(2)
    av_ref[...] = av.astype(BF16)
    avf_ref[...] = av
    bq_ref[...] = (rope(seg(3)) * (scale * LOG2E)).astype(BF16)
    bk = rope(seg(4))
    bk_ref[...] = bk.astype(BF16)
    bkf_ref[...] = bk
    bv = seg(5)
    bv_ref[...] = bv.astype(BF16)
    bvf_ref[...] = bv


def _inproj(x2d, g, w_bf, wvt_bf, cos_t, sa_t, sb_t, *, tm, n_pos_blocks, tail_div, emit_vt):
    r = x2d.shape[0]
    n_tiles = r // tm
    row = lambda i: (i, 0)
    const = lambda i: (0, 0)
    pos = lambda i: (i % n_pos_blocks, 0)
    tail = lambda i: (i // tail_div, 0)
    bf = jax.ShapeDtypeStruct((r, A_WIDTH), BF16)
    f_all = jax.ShapeDtypeStruct((r, A_WIDTH), F32)
    f_tail = jax.ShapeDtypeStruct((r // tail_div, A_WIDTH), F32)
    blk = lambda im: pl.BlockSpec((tm, A_WIDTH), im)
    out_specs = [blk(row)] * 6 + [blk(tail), blk(tail), blk(row), blk(row)]
    out_shape = [bf] * 6 + [f_tail, f_tail, f_all, f_all]
    if emit_vt:
        for kb in (A_QBLK, B_KBLK):
            out_specs.append(pl.BlockSpec((tm // kb, A_WIDTH, kb), lambda i: (i, 0, 0)))
            out_shape.append(jax.ShapeDtypeStruct((r // kb, A_WIDTH, kb), BF16))
    return pl.pallas_call(
        _inproj_kernel,
        grid=(n_tiles,),
        in_specs=[pl.BlockSpec((tm, D_MODEL), row), pl.BlockSpec((1, D_MODEL), const),
                  pl.BlockSpec((D_MODEL, 3 * D_MODEL), const),
                  pl.BlockSpec((2, A_WIDTH, D_MODEL), lambda i: (0, 0, 0)),
                  pl.BlockSpec((tm, LANES), pos), pl.BlockSpec((tm, LANES), pos),
                  pl.BlockSpec((tm, LANES), pos)],
        out_specs=out_specs,
        out_shape=out_shape,
        compiler_params=_cparams(("arbitrary",)),
        name="inproj",
    )(x2d, g, w_bf, wvt_bf, cos_t, sa_t, sb_t)


def _rope_tables(positions):
    half = ROT_DIM // 2
    lane = np.arange(LANES) % HEAD_DIM
    inv_freq = jnp.power(ROPE_THETA, -jnp.arange(half, dtype=F32) / half)
    ang = positions.astype(F32)[:, None] * inv_freq[None, :]
    cos = jnp.cos(ang)[:, lane % half]
    sin = jnp.sin(ang)[:, lane % half]
    in_rot = jnp.asarray(lane < ROT_DIM)[None, :]
    first = jnp.asarray(lane < half)[None, :]
    cos_t = jnp.where(in_rot, cos, 1.0)
    sa_t = jnp.where(in_rot & ~first, sin, 0.0)
    sb_t = jnp.where(first, -sin, 0.0)
    return cos_t.astype(F32), sa_t.astype(F32), sb_t.astype(F32)


def _band_table(rel_bias):
    r = np.arange(A_QBLK)[:, None]
    j = np.arange(BAND + A_WIN)[None, :]
    dchunk = LEFT_CHUNKS + r // CHUNK - j // CHUNK
    allowed = (dchunk >= 0) & (dchunk <= LEFT_CHUNKS)
    bias = rel_bias.astype(F32)
    n_seq = BAND + A_WIN + A_QBLK - 1
    n_hi = BAND + A_QBLK - 1 - REL_CLIP
    seq = jnp.concatenate([jnp.broadcast_to(bias[:, -1:], (N_HEADS_A, n_hi)), bias[:, ::-1],
                           jnp.broadcast_to(bias[:, :1], (N_HEADS_A, n_seq - n_hi - bias.shape[1]))], axis=1)
    seq = jnp.roll(seq, -(A_QBLK - 1), axis=1)
    toe = jnp.tile(seq, (1, A_QBLK))[:, :A_QBLK * (n_seq - 1)].reshape(N_HEADS_A, A_QBLK, n_seq - 1)
    return jnp.where(jnp.asarray(allowed)[None], toe[:, :, :BAND + A_WIN], NEG_INF)


def _softmax_pv(s_list, v_list, exp_fn=jnp.exp):
    m = s_list[0].max(axis=-1, keepdims=True)
    for s in s_list[1:]:
        m = jnp.maximum(m, s.max(axis=-1, keepdims=True))
    l = None
    o = None
    for s, v in zip(s_list, v_list):
        p = exp_fn(s - m)
        ls = p.sum(axis=-1, keepdims=True)
        os_ = jnp.dot(p.astype(BF16), v, preferred_element_type=F32)
        l = ls if l is None else l + ls
        o = os_ if o is None else o + os_
    return o / l


def _attn_a_prompt_kernel(q_ref, k_ref, vt_ref, tab_ref, o_ref):
    i = pl.program_id(1)
    n_left = BAND // A_QBLK
    n_sub = A_WIN // A_QBLK
    blk0 = jnp.maximum(i - n_left, 0)
    off = pl.multiple_of(jnp.maximum(n_left - i, 0) * A_QBLK, A_QBLK)
    q = q_ref[0]
    lane = lax.broadcasted_iota(jnp.int32, (A_QBLK, LANES), 1)

    def scores(h):
        cols = slice(h // 2 * LANES, (h // 2 + 1) * LANES)
        qp = q[:, cols]
        qh = jnp.where((lane >= HEAD_DIM) == bool(h % 2), qp, jnp.zeros_like(qp))
        return [_dot_nt(k_ref[0, pl.ds(pl.multiple_of((blk0 + c) * A_QBLK, A_QBLK), A_QBLK), cols], qh)
                + tab_ref[h, pl.ds(off + c * A_QBLK, A_QBLK), :] for c in range(n_sub)]

    def attend(h, ss):
        cols = slice(h // 2 * LANES, (h // 2 + 1) * LANES)
        m = _col_reduce(ss[0], jnp.maximum, jnp.max)
        for s in ss[1:]:
            m = jnp.maximum(m, _col_reduce(s, jnp.maximum, jnp.max))
        l = None
        o_t = None
        for c in range(n_sub):
            p = jnp.exp(ss[c] - m)
            ls = _col_reduce(p, jnp.add, jnp.sum)
            os_ = jnp.dot(vt_ref[0, blk0 + c, cols, :], p.astype(BF16), preferred_element_type=F32)
            l = ls if l is None else l + ls
            o_t = os_ if o_t is None else o_t + os_
        return (o_t / l)[h % 2 * HEAD_DIM:(h % 2 + 1) * HEAD_DIM]

    halves = []
    ss = scores(0)
    for h in range(N_HEADS_A):
        ss_next = scores(h + 1) if h + 1 < N_HEADS_A else None
        halves.append(attend(h, ss))
        ss = ss_next
    pair_outs = [jnp.concatenate(halves[2 * g:2 * g + 2], axis=0).T for g in range(N_HEADS_A // 2)]
    o_ref[0] = jnp.concatenate(pair_outs, axis=1).astype(BF16)


def _attn_a_prompt(aq, ak, avt, table_t):
    b, l, _ = aq.shape
    n_kb = avt.shape[1]
    return pl.pallas_call(
        _attn_a_prompt_kernel,
        grid=(b, l // A_QBLK),
        in_specs=[pl.BlockSpec((1, A_QBLK, A_WIDTH), lambda bb, i: (bb, i, 0)),
                  pl.BlockSpec((1, l, A_WIDTH), lambda bb, i: (bb, 0, 0)),
                  pl.BlockSpec((1, n_kb, A_WIDTH, A_QBLK), lambda bb, i: (bb, 0, 0, 0)),
                  pl.BlockSpec(table_t.shape, lambda bb, i: (0, 0, 0))],
        out_specs=pl.BlockSpec((1, A_QBLK, A_WIDTH), lambda bb, i: (bb, i, 0)),
        out_shape=jax.ShapeDtypeStruct((b, l, A_WIDTH), BF16),
        compiler_params=_cparams(("arbitrary", "arbitrary")),
        name="attn_a_prompt",
    )(aq, ak, avt, table_t)


def _attn_a_sample_kernel(q_ref, kc_ref, vc_ref, k_ref, v_ref, tab_ref, o_ref):
    q = q_ref[0]
    kc = kc_ref[0].astype(BF16)
    vc = vc_ref[0].astype(BF16)
    ko = k_ref[0]
    vo = v_ref[0]
    a_len = kc.shape[0]
    lq = q.shape[0]
    c0 = BAND - a_len
    outs = []
    for h in range(N_HEADS_A):
        sl = slice(h * HEAD_DIM, (h + 1) * HEAD_DIM)
        s_c = _dot_nt(q[:, sl], kc[:, sl]) + tab_ref[h, :, c0:BAND]
        s_o = _dot_nt(q[:, sl], ko[:, sl]) + tab_ref[h, :, BAND:BAND + lq]
        outs.append(_softmax_pv([s_c, s_o], [vc[:, sl], vo[:, sl]]))
    o_ref[0] = jnp.concatenate(outs, axis=1).astype(BF16)


def _attn_a_sample(aq, cache_k, cache_v, ak, av, table):
    b, l, _ = aq.shape
    a_len = cache_k.shape[1]
    bs = lambda n: pl.BlockSpec((1, n, A_WIDTH), lambda bb: (bb, 0, 0))
    return pl.pallas_call(
        _attn_a_sample_kernel,
        grid=(b,),
        in_specs=[bs(l), bs(a_len), bs(a_len), bs(l), bs(l),
                  pl.BlockSpec(table.shape, lambda bb: (0, 0, 0))],
        out_specs=bs(l),
        out_shape=jax.ShapeDtypeStruct((b, l, A_WIDTH), BF16),
        compiler_params=_cparams(("arbitrary",)),
        name="attn_a_sample",
    )(aq, cache_k, cache_v, ak, av, table)


def _subnorm(o, g):
    return _rms(o, g) * (1.0 - LAM_INIT)


def _col_reduce(x, op, reduce_fn):
    n = x.shape[0]
    while n > 8:
        n //= 2
        x = op(x[:n], x[n:])
    return reduce_fn(x, axis=0, keepdims=True)


def _attn_b_prompt_kernel(lam_ref, q_ref, k_ref, vt_ref, mask_ref, g_ref, o_ref,
                          sa_sc, sb_sc, m_sc, l_sc, acc_sc):
    i = pl.program_id(2)
    tk = vt_ref.shape[3]
    q = q_ref[0]
    lane = lax.broadcasted_iota(jnp.int32, q.shape, 1)
    zero = jnp.zeros_like(q)
    qs = (jnp.where(lane < HEAD_DIM, q, zero), jnp.where(lane >= HEAD_DIM, q, zero))
    m_sc[...] = jnp.full(m_sc.shape, NEG_INF, F32)
    l_sc[...] = jnp.zeros(l_sc.shape, F32)
    acc_sc[...] = jnp.zeros(acc_sc.shape, F32)

    def scores(j, s_sc):
        kb = k_ref[0, pl.ds(pl.multiple_of(j * tk, tk), tk), :]
        for a in range(2):
            s_sc[a] = _dot_nt(kb, qs[a])

    def consume(j, s_sc, mask):
        vt = vt_ref[0, j]
        for a in range(2):
            s = s_sc[a]
            if mask is not None:
                s = s + mask
            m_old = m_sc[a]
            m_new = jnp.maximum(m_old, _col_reduce(s, jnp.maximum, jnp.max))
            alpha = jnp.exp2(m_old - m_new)
            p = jnp.exp2(s - m_new)
            l_sc[a] = alpha * l_sc[a] + _col_reduce(p, jnp.add, jnp.sum)
            m_sc[a] = m_new
            acc_sc[a] = alpha * acc_sc[a] + jnp.dot(vt, p.astype(BF16), preferred_element_type=F32)

    scores(0, sa_sc)

    def body(jj, c):
        j = 2 * jj
        scores(j + 1, sb_sc)
        consume(j, sa_sc, None)
        scores(j + 2, sa_sc)
        consume(j + 1, sb_sc, None)
        return c

    lax.fori_loop(0, i, body, 0)
    scores(2 * i + 1, sb_sc)
    consume(2 * i, sa_sc, mask_ref[0])
    consume(2 * i + 1, sb_sc, mask_ref[1])
    o_t = acc_sc[0] / l_sc[0] - lam_ref[0] * (acc_sc[1] / l_sc[1])
    o_ref[0] = _subnorm(o_t.T, g_ref[...]).astype(BF16)


def _chunk_causal_masks_t(tk, tq):
    kc = (np.arange(tq // tk)[:, None, None] * tk + np.arange(tk)[None, :, None]) // CHUNK
    qc = np.arange(tq)[None, None, :] // CHUNK
    return jnp.asarray(np.where(kc <= qc, 0.0, NEG_INF), F32)


def _attn_b_prompt(lam, bq, bk, bvt, g_sub):
    b, l, _ = bq.shape
    tq = B_QBLK
    tk = B_KBLK
    hw = 2 * HEAD_DIM
    n_kb = bvt.shape[1]
    return pl.pallas_call(
        _attn_b_prompt_kernel,
        grid_spec=pltpu.PrefetchScalarGridSpec(
            num_scalar_prefetch=1,
            grid=(b, N_HEADS_B, l // tq),
            in_specs=[pl.BlockSpec((1, tq, hw), lambda bb, h, i, lam_: (bb, i, h)),
                      pl.BlockSpec((1, l, hw), lambda bb, h, i, lam_: (bb, 0, h)),
                      pl.BlockSpec((1, n_kb, hw, tk), lambda bb, h, i, lam_: (bb, 0, h, 0)),
                      pl.BlockSpec((tq // tk, tk, tq), lambda bb, h, i, lam_: (0, 0, 0)),
                      pl.BlockSpec((1, hw), lambda bb, h, i, lam_: (0, 0))],
            out_specs=pl.BlockSpec((1, tq, hw), lambda bb, h, i, lam_: (bb, i, h)),
            scratch_shapes=[pltpu.VMEM((2, tk, tq), F32), pltpu.VMEM((2, tk, tq), F32),
                            pltpu.VMEM((2, 1, tq), F32), pltpu.VMEM((2, 1, tq), F32),
                            pltpu.VMEM((2, hw, tq), F32)]),
        out_shape=jax.ShapeDtypeStruct((b, l, B_WIDTH), BF16),
        compiler_params=_cparams(("arbitrary", "arbitrary", "arbitrary")),
        name="attn_b_prompt",
    )(lam, bq, bk, bvt, _chunk_causal_masks_t(tk, tq), g_sub)


def _attn_b_sample_kernel(lam_ref, q_ref, kc_ref, vc_ref, k_ref, v_ref, g_ref, o_ref):
    q = q_ref[0]
    kc = kc_ref[0].astype(BF16)
    vc = vc_ref[0].astype(BF16)
    ko = k_ref[0]
    vo = v_ref[0]
    outs = []
    for a in range(2):
        sl = slice(a * HEAD_DIM, (a + 1) * HEAD_DIM)
        s_c = _dot_nt(q[:, sl], kc[:, sl])
        s_o = _dot_nt(q[:, sl], ko[:, sl])
        outs.append(_softmax_pv([s_c, s_o], [vc, vo], exp_fn=jnp.exp2))
    o = outs[0] - lam_ref[0] * outs[1]
    o_ref[0] = _subnorm(o, g_ref[...]).astype(BF16)


def _attn_b_sample(lam, bq, cache_k, cache_v, bk, bv, g_sub):
    b, l, _ = bq.shape
    p = cache_k.shape[1]
    hw = 2 * HEAD_DIM
    bs = lambda n: pl.BlockSpec((1, n, hw), lambda bb, h, lam_: (bb, 0, h))
    return pl.pallas_call(
        _attn_b_sample_kernel,
        grid_spec=pltpu.PrefetchScalarGridSpec(
            num_scalar_prefetch=1,
            grid=(b, N_HEADS_B),
            in_specs=[bs(l), bs(p), bs(p), bs(l), bs(l),
                      pl.BlockSpec((1, hw), lambda bb, h, lam_: (0, 0))],
            out_specs=bs(l)),
        out_shape=jax.ShapeDtypeStruct((b, l, B_WIDTH), BF16),
        compiler_params=_cparams(("arbitrary", "arbitrary")),
        name="attn_b_sample",
    )(lam, bq, cache_k, cache_v, bk, bv, g_sub)


def _memkv_kernel(m_ref, g_ref, wk_ref, wv_ref, kf_ref, vf_ref, kb_ref, vb_ref):
    mn = _rms(m_ref[...], g_ref[...]).astype(BF16)
    k = jnp.dot(mn, wk_ref[...], preferred_element_type=F32)
    v = jnp.dot(mn, wv_ref[...], preferred_element_type=F32)
    kf_ref[...] = k
    vf_ref[...] = v
    kb_ref[...] = k.astype(BF16)
    vb_ref[...] = v.astype(BF16)


def _memkv(mem2d, g, wk, wv, *, tm):
    r = mem2d.shape[0]
    row = lambda i: (i, 0)
    const = lambda i: (0, 0)
    blk = pl.BlockSpec((tm, D_MODEL), row)
    wspec = pl.BlockSpec((D_MODEL, D_MODEL), const)
    f = jax.ShapeDtypeStruct((r, D_MODEL), F32)
    bf = jax.ShapeDtypeStruct((r, D_MODEL), BF16)
    return pl.pallas_call(
        _memkv_kernel,
        grid=(r // tm,),
        in_specs=[blk, pl.BlockSpec((1, D_MODEL), const), wspec, wspec],
        out_specs=[blk] * 4,
        out_shape=[f, f, bf, bf],
        compiler_params=_cparams(("arbitrary",)),
        name="memkv",
    )(mem2d, g, wk, wv)


def _post_kernel(x_ref, oa_ref, ob_ref, woa_ref, wob_ref, gq_ref, wq_ref, mk_ref, mv_ref, wo_ref,
                 gm_ref, h_ref, xn_ref):
    h1 = (x_ref[...] + jnp.dot(oa_ref[...], woa_ref[...], preferred_element_type=F32)
          + jnp.dot(ob_ref[...], wob_ref[...], preferred_element_type=F32))
    xq = _rms(h1, gq_ref[...]).astype(BF16)
    q = (jnp.dot(xq, wq_ref[...], preferred_element_type=F32) * HEAD_DIM_M ** -0.5).astype(BF16)
    outs = []
    for h in range(N_HEADS_M):
        sl = slice(h * HEAD_DIM_M, (h + 1) * HEAD_DIM_M)
        s = _dot_nt(q[:, sl], mk_ref[0, :, sl])
        outs.append(_softmax_pv([s], [mv_ref[0, :, sl]]).astype(BF16))
    o = jnp.concatenate(outs, axis=1)
    h2 = h1 + jnp.dot(o, wo_ref[...], preferred_element_type=F32)
    h_ref[...] = h2
    _store_row_slabs(xn_ref, _pack_bf16_pairs(_rms(h2, gm_ref[...])))


def _post(x2d, oa, ob, woa, wob, gq, wq, mk, mv, wo, gm, *, tm, tiles_per_batch):
    r = x2d.shape[0]
    mlen = mk.shape[1]
    row = lambda i: (i, 0)
    const = lambda i: (0, 0)
    memb = lambda i: (i // tiles_per_batch, 0, 0)
    gspec = pl.BlockSpec((1, D_MODEL), const)
    wspec = pl.BlockSpec((D_MODEL, D_MODEL), const)
    hspec = pl.BlockSpec((A_WIDTH, D_MODEL), const)
    return pl.pallas_call(
        _post_kernel,
        grid=(r // tm,),
        in_specs=[pl.BlockSpec((tm, D_MODEL), row),
                  pl.BlockSpec((tm, A_WIDTH), row), pl.BlockSpec((tm, B_WIDTH), row),
                  hspec, hspec, gspec, wspec,
                  pl.BlockSpec((1, mlen, D_MODEL), memb), pl.BlockSpec((1, mlen, D_MODEL), memb),
                  wspec, gspec],
        out_specs=[pl.BlockSpec((tm, D_MODEL), row), pl.BlockSpec((tm * X_SLABS, LANES), row)],
        out_shape=[jax.ShapeDtypeStruct((r, D_MODEL), F32),
                   jax.ShapeDtypeStruct((r * X_SLABS, LANES), jnp.uint32)],
        compiler_params=_cparams(("arbitrary",)),
        name="post_attn",
    )(x2d, oa, ob, woa, wob, gq, wq, mk, mv, wo, gm)


def _router_kernel(xn_ref, wr_ref, br_ref, tri_ref, idx_ref, gate_ref, rank_ref, cnt_ref, base_sc):
    @pl.when(pl.program_id(0) == 0)
    def _():
        base_sc[...] = jnp.zeros(base_sc.shape, F32)

    tm = xn_ref.shape[0] // X_SLABS
    xn = _unpack_bf16_pairs(_load_row_slabs(xn_ref, tm, X_SLABS))
    logits = _dot_nt(wr_ref[...], xn) + br_ref[...]
    iota_e = lax.broadcasted_iota(jnp.int32, (N_EXPERTS, tm), 0).astype(F32)
    work = logits
    vals, idxs, hots = [], [], []
    for _k in range(TOP_K):
        mx = work.max(axis=0, keepdims=True)
        ix = jnp.where(work == mx, iota_e, float(N_EXPERTS)).min(axis=0, keepdims=True)
        hot = iota_e == ix
        vals.append(mx)
        idxs.append(ix)
        hots.append(hot)
        work = jnp.where(hot, -jnp.inf, work)
    es = [jnp.exp(v - vals[0]) for v in vals]
    den = es[0] + es[1] + es[2] + es[3]
    cnt = (hots[0] | hots[1] | hots[2] | hots[3]).astype(F32)
    before = jnp.dot(cnt.astype(BF16), tri_ref[...], preferred_element_type=F32) + base_sc[...]
    for k in range(TOP_K):
        idx_ref[k:k + 1, :] = idxs[k].astype(jnp.int32)
        gate_ref[k:k + 1, :] = es[k] / den
        rank_ref[k:k + 1, :] = jnp.where(hots[k], before, 0.0).sum(axis=0, keepdims=True).astype(jnp.int32)
    base_sc[...] = base_sc[...] + cnt.sum(axis=1, keepdims=True)
    cnt_ref[...] = jnp.broadcast_to(base_sc[...], cnt_ref.shape).astype(jnp.int32)


def _router(xn2d, wr_t, br, tri, *, tm):
    r = xn2d.shape[0] // X_SLABS
    col = lambda i: (0, i)
    const = lambda i: (0, 0)
    return pl.pallas_call(
        _router_kernel,
        grid=(r // tm,),
        in_specs=[pl.BlockSpec((tm * X_SLABS, LANES), lambda i: (i, 0)),
                  pl.BlockSpec((N_EXPERTS, D_MODEL), const),
                  pl.BlockSpec((N_EXPERTS, 1), const),
                  pl.BlockSpec((tm, tm), const)],
        out_specs=[pl.BlockSpec((TOP_K, tm), col)] * 3 + [pl.BlockSpec((N_EXPERTS, LANES), const)],
        out_shape=[jax.ShapeDtypeStruct((TOP_K, r), jnp.int32),
                   jax.ShapeDtypeStruct((TOP_K, r), F32),
                   jax.ShapeDtypeStruct((TOP_K, r), jnp.int32),
                   jax.ShapeDtypeStruct((N_EXPERTS, LANES), jnp.int32)],
        scratch_shapes=[pltpu.VMEM((N_EXPERTS, 1), F32)],
        compiler_params=_cparams(("arbitrary",)),
        name="router",
    )(xn2d, wr_t, br, tri)


def _row_copy(src, dst, sem):
    return pltpu.make_async_copy(src, dst, sem)


def _slab(ref, row, n, lead=()):
    return ref.at[lead + (pl.ds(pl.multiple_of(row * n, n), n), slice(None))]


def _scatter_kernel(pos_ref, xn_ref, xs_ref, sem):
    tm = xn_ref.shape[0] // X_SLABS

    def start(t, c):
        for k in range(TOP_K):
            _row_copy(_slab(xn_ref, t, X_SLABS), _slab(xs_ref, pos_ref[0, k, t], X_SLABS),
                      sem).start(priority=k % 2)
        return c

    lax.fori_loop(0, tm, start, 0, unroll=2)

    def wait(t, c):
        for k in range(TOP_K):
            _row_copy(_slab(xn_ref, 0, X_SLABS), _slab(xs_ref, 0, X_SLABS), sem).wait()
        return c

    lax.fori_loop(0, tm, wait, 0, unroll=4)


def _scatter(pos, xn2d, *, tm):
    r = xn2d.shape[0] // X_SLABS
    return pl.pallas_call(
        _scatter_kernel,
        grid=(r // tm,),
        in_specs=[pl.BlockSpec((1, TOP_K, tm), lambda i: (i, 0, 0), memory_space=pltpu.SMEM),
                  pl.BlockSpec((tm * X_SLABS, LANES), lambda i: (i, 0))],
        out_specs=pl.BlockSpec(memory_space=pl.ANY),
        out_shape=jax.ShapeDtypeStruct((TOP_K * r * X_SLABS, LANES), jnp.uint32),
        scratch_shapes=[pltpu.SemaphoreType.DMA],
        compiler_params=_cparams(("arbitrary",)),
        name="moe_scatter",
    )(pos, xn2d)


def _experts_kernel(blk_ref, exp_ref, lo_ref, hi_ref, first_ref, newexp_ref,
                    x_ref, wgu_ref, bgu_ref, wd_ref, bd_ref, y_ref, wgu_bf, wd_bf):
    i = pl.program_id(0)

    @pl.when(newexp_ref[i] == 1)
    def _():
        wgu_bf[...] = wgu_ref[0].astype(BF16)
        wd_bf[...] = wd_ref[0].astype(BF16)

    x = _unpack_bf16_pairs(_load_row_slabs(x_ref, MOE_ROWS, X_SLABS))
    gu = jnp.dot(x, wgu_bf[...], preferred_element_type=F32) + bgu_ref[0]
    gate = jnp.minimum(gu[:, :EXPERT_FF], SWIGLU_LIMIT)
    up = jnp.clip(gu[:, EXPERT_FF:], -SWIGLU_LIMIT, SWIGLU_LIMIT)
    act = (up + 1.0) * (gate * jax.nn.sigmoid(SWIGLU_ALPHA * gate))
    y = jnp.dot(act.astype(BF16), wd_bf[...], preferred_element_type=F32) + bd_ref[0]
    row0 = blk_ref[i] * MOE_ROWS
    whole = (lo_ref[i] <= row0) & (hi_ref[i] >= row0 + MOE_ROWS)

    @pl.when(whole)
    def _():
        _store_row_slabs(y_ref, y)

    @pl.when(jnp.logical_not(whole))
    def _():
        @pl.when(first_ref[i] == 1)
        def _():
            y_ref[...] = jnp.zeros(y_ref.shape, F32)

        rows = row0 + lax.broadcasted_iota(jnp.int32, (MOE_ROWS, 1), 0)
        mine = (rows >= lo_ref[i]) & (rows < hi_ref[i])
        _store_row_slabs(y_ref, jnp.where(mine, y, _load_row_slabs(y_ref, MOE_ROWS, Y_SLABS)))


def _experts(items, xs, wgu, bgu, wd, bd):
    blk, exp, lo, hi, first = items
    newexp = jnp.concatenate([jnp.ones((1,), jnp.int32), (exp[1:] != exp[:-1]).astype(jnp.int32)])
    n_items = blk.shape[0]
    n_rows = xs.shape[0] // X_SLABS
    return pl.pallas_call(
        _experts_kernel,
        grid_spec=pltpu.PrefetchScalarGridSpec(
            num_scalar_prefetch=6,
            grid=(n_items,),
            in_specs=[pl.BlockSpec((MOE_ROWS * X_SLABS, LANES), lambda i, b, e, *_: (b[i], 0)),
                      pl.BlockSpec((1, D_MODEL, 2 * EXPERT_FF), lambda i, b, e, *_: (e[i], 0, 0)),
                      pl.BlockSpec((1, 1, 2 * EXPERT_FF), lambda i, b, e, *_: (e[i], 0, 0)),
                      pl.BlockSpec((1, EXPERT_FF, D_MODEL), lambda i, b, e, *_: (e[i], 0, 0)),
                      pl.BlockSpec((1, 1, D_MODEL), lambda i, b, e, *_: (e[i], 0, 0))],
            out_specs=pl.BlockSpec((MOE_ROWS * Y_SLABS, LANES), lambda i, b, e, *_: (b[i], 0)),
            scratch_shapes=[pltpu.VMEM((D_MODEL, 2 * EXPERT_FF), BF16), pltpu.VMEM((EXPERT_FF, D_MODEL), BF16)]),
        out_shape=jax.ShapeDtypeStruct((n_rows * Y_SLABS, LANES), F32),
        compiler_params=pltpu.CompilerParams(dimension_semantics=("arbitrary",),
                                             vmem_limit_bytes=EXPERTS_VMEM_LIMIT),
        name="moe_experts",
    )(blk, exp, lo, hi, first, newexp, xs, wgu, bgu, wd, bd)


def _work_items(counts, n_rows):
    n_blocks = n_rows // MOE_ROWS
    n_items = n_blocks + N_EXPERTS - 1
    ends = jnp.cumsum(counts)
    starts = ends - counts
    nb = jnp.where(counts > 0, (ends - 1) // MOE_ROWS - starts // MOE_ROWS + 1, 0)
    item_end = jnp.cumsum(nb)
    item_start = item_end - nb
    total = item_end[-1]
    i = jnp.arange(n_items, dtype=jnp.int32)
    e = jnp.minimum(jnp.sum(item_end[None, :] <= i[:, None], axis=1), N_EXPERTS - 1).astype(jnp.int32)
    blk = (starts[e] // MOE_ROWS + (i - item_start[e])).astype(jnp.int32)
    valid = i < total
    lo = jnp.maximum(starts[e], blk * MOE_ROWS)
    hi = jnp.minimum(ends[e], (blk + 1) * MOE_ROWS)
    last = jnp.maximum(total - 1, 0)
    blk = jnp.where(valid, blk, blk[last])
    e = jnp.where(valid, e, e[last])
    lo = jnp.where(valid, lo, 0).astype(jnp.int32)
    hi = jnp.where(valid, hi, 0).astype(jnp.int32)
    prev = jnp.concatenate([jnp.full((1,), -1, jnp.int32), blk[:-1]])
    first = (blk != prev).astype(jnp.int32)
    return blk, e, lo, hi, first, starts


def _combine_kernel(pos_ref, pos_next_ref, h_ref, gate_ref, g_ref, ys_ref, o_ref, buf, sem):
    tm = h_ref.shape[0]
    i = pl.program_id(0)
    n = pl.num_programs(0)
    cur = i % 2

    def fetch(p_ref, slot):
        def start(t, c):
            for k in range(TOP_K):
                _row_copy(_slab(ys_ref, p_ref[0, k, t], Y_SLABS), _slab(buf, t, Y_SLABS, (slot, k)),
                          sem.at[slot]).start(priority=k % 2)
            return c

        lax.fori_loop(0, tm, start, 0, unroll=2)

    @pl.when(i == 0)
    def _():
        fetch(pos_ref, 0)

    @pl.when(i + 1 < n)
    def _():
        fetch(pos_next_ref, 1 - cur)

    def wait(t, c):
        for k in range(TOP_K):
            _row_copy(_slab(ys_ref, 0, Y_SLABS), _slab(buf, 0, Y_SLABS, (cur, 0)), sem.at[cur]).wait()
        return c

    lax.fori_loop(0, tm, wait, 0, unroll=4)
    acc = h_ref[...]
    for k in range(TOP_K):
        acc = acc + gate_ref[:, k:k + 1] * _load_row_slabs(buf, tm, Y_SLABS, (cur, k))
    o_ref[...] = _rms(acc, g_ref[...])


def _combine(pos, h2d, gates_t, g_final, ys, *, tm):
    r = h2d.shape[0]
    n_tiles = r // tm
    row = lambda i: (i, 0)
    return pl.pallas_call(
        _combine_kernel,
        grid=(n_tiles,),
        in_specs=[pl.BlockSpec((1, TOP_K, tm), lambda i: (i, 0, 0), memory_space=pltpu.SMEM),
                  pl.BlockSpec((1, TOP_K, tm), lambda i: (jnp.minimum(i + 1, n_tiles - 1), 0, 0),
                               memory_space=pltpu.SMEM),
                  pl.BlockSpec((tm, D_MODEL), row),
                  pl.BlockSpec((tm, TOP_K), row),
                  pl.BlockSpec((1, D_MODEL), lambda i: (0, 0)),
                  pl.BlockSpec(memory_space=pl.ANY)],
        out_specs=pl.BlockSpec((tm, D_MODEL), row),
        out_shape=jax.ShapeDtypeStruct((r, D_MODEL), F32),
        scratch_shapes=[pltpu.VMEM((2, TOP_K, tm * Y_SLABS, LANES), F32), pltpu.SemaphoreType.DMA((2,))],
        compiler_params=_cparams(("arbitrary",)),
        name="moe_combine",
    )(pos, pos, h2d, gates_t, g_final, ys)


def _moe(h2d, xn2d, w):
    r = h2d.shape[0]
    tm_router = _pick_tile(r, 512)
    tm_scatter = _pick_tile(r, 512)
    tm_combine = _pick_tile(r, 256)
    idx, gates, rank, cnt = _router(xn2d, w["wr_t"], w["br"], w["tri"][:tm_router, :tm_router], tm=tm_router)
    counts = cnt[:, 0]
    blk, e, lo, hi, first, starts = _work_items(counts, TOP_K * r)
    hot = idx[:, :, None] == jnp.arange(N_EXPERTS, dtype=jnp.int32)
    pos = jnp.sum(jnp.where(hot, starts.astype(jnp.int32), 0), axis=-1) + rank
    tiles = lambda tm: pos.reshape(TOP_K, -1, tm).transpose(1, 0, 2)
    xs = _scatter(tiles(tm_scatter), xn2d, tm=tm_scatter)
    ys = _experts((blk, e, lo, hi, first), xs, w["wgu"], w["bgu"], w["wd"], w["bd"])
    return _combine(tiles(tm_combine), h2d, gates.T, w["g_final"], ys, tm=tm_combine)


def _pick_tile(n, cap):
    t = cap
    while n % t:
        t //= 2
    return t


def kernel(x_prompt, x_sample, mem_prompt, cache_a_k, cache_a_v, cache_b_k, cache_b_v, cache_mem_k, cache_mem_v, g_mix, w_in, rel_bias, lam_q1, lam_k1, lam_q2, lam_k2, g_subln, w_out, g_xq, g_xmem, w_xq, w_xk, w_xv, w_xo, g_moe, w_router, b_router, w_gate_up, b_gate_up, w_down, b_down, g_final):
    assert g_mix.shape[0] == 1, "single-layer kernel"
    b, s, d = x_prompt.shape
    db, dl, _ = x_sample.shape
    past = cache_b_k.shape[2]
    a_len = cache_a_k.shape[2]
    mlen = mem_prompt.shape[1]
    keep_a = min(BAND, s)
    assert d == D_MODEL and s % B_QBLK == 0 and s >= A_WIN and dl == CHUNK and past % CHUNK == 0
    assert B_QBLK == 2 * B_KBLK

    row1 = lambda v: v.reshape(1, -1).astype(F32)
    w_in_bf = w_in[0].astype(BF16)
    w_out_bf = w_out[0].astype(BF16)
    weights = dict(
        wr_t=w_router[0].T.astype(BF16), br=b_router[0].reshape(N_EXPERTS, 1).astype(F32),
        tri=jnp.asarray(np.triu(np.ones((512, 512), np.float32), 1), BF16),
        wgu=w_gate_up[0].astype(F32), bgu=b_gate_up[0].reshape(N_EXPERTS, 1, -1).astype(F32),
        wd=w_down[0].astype(F32), bd=b_down[0].reshape(N_EXPERTS, 1, -1).astype(F32),
        g_final=row1(g_final))
    lam = (jnp.exp(jnp.sum(lam_q1[0].astype(F32) * lam_k1[0].astype(F32)))
           - jnp.exp(jnp.sum(lam_q2[0].astype(F32) * lam_k2[0].astype(F32))) + LAM_INIT).reshape(1)
    table = _band_table(rel_bias[0])
    g_sub = row1(g_subln[0])

    tm_p = _pick_tile(s, 512)
    tabs_p = _rope_tables(jnp.arange(s, dtype=jnp.int32))
    w_bvt_bf = jnp.stack([w_in[0][:, 2 * A_WIDTH:3 * A_WIDTH].T,
                          w_in[0][:, 3 * A_WIDTH + 2 * B_WIDTH:].T]).astype(BF16)
    p_in = _inproj(x_prompt.reshape(b * s, d), row1(g_mix[0]), w_in_bf, w_bvt_bf, *tabs_p,
                   tm=tm_p, n_pos_blocks=s // tm_p, tail_div=s // tm_p if keep_a == tm_p else 1, emit_vt=True)
    tabs_s = _rope_tables(jnp.tile(past + jnp.arange(dl, dtype=jnp.int32), db))
    s_in = _inproj(x_sample.reshape(db * dl, d), row1(g_mix[0]), w_in_bf, w_bvt_bf, *tabs_s,
                   tm=db * dl, n_pos_blocks=1, tail_div=1, emit_vt=False)

    def split(outs, nb, nl):
        return [o.reshape(nb, nl, A_WIDTH) for o in outs[:6]]

    aq, ak, _, bq, bk, _ = split(p_in, b, s)
    avt = p_in[10].reshape(b, s // A_QBLK, A_WIDTH, A_QBLK)
    bvt = p_in[11].reshape(b, s // B_KBLK, B_WIDTH, B_KBLK)
    saq, sak, sav, sbq, sbk, sbv = split(s_in, db, dl)

    oa = _attn_a_prompt(aq, ak, avt, table.transpose(0, 2, 1))
    ob = _attn_b_prompt(lam, bq, bk, bvt, g_sub)
    soa = _attn_a_sample(saq, cache_a_k[0].reshape(db, a_len, A_WIDTH), cache_a_v[0].reshape(db, a_len, A_WIDTH),
                         sak, sav, table[:, :dl, :BAND + dl])
    sob = _attn_b_sample(lam, sbq, cache_b_k[0].reshape(db, past, B_WIDTH), cache_b_v[0].reshape(db, past, B_WIDTH),
                         sbk, sbv, g_sub)

    mkf, mvf, mkb, mvb = _memkv(mem_prompt.reshape(b * mlen, d), row1(g_xmem[0]),
                                w_xk[0].astype(BF16), w_xv[0].astype(BF16), tm=mlen)
    post_w = (w_out_bf[:A_WIDTH], w_out_bf[A_WIDTH:], row1(g_xq[0]), w_xq[0].astype(BF16))
    tm_post = _pick_tile(s, 256)
    hp, xnp_ = _post(x_prompt.reshape(b * s, d), oa.reshape(b * s, A_WIDTH), ob.reshape(b * s, B_WIDTH),
                     *post_w, mkb.reshape(b, mlen, d), mvb.reshape(b, mlen, d),
                     w_xo[0].astype(BF16), row1(g_moe[0]), tm=tm_post, tiles_per_batch=s // tm_post)
    hs, xns = _post(x_sample.reshape(db * dl, d), soa.reshape(db * dl, A_WIDTH), sob.reshape(db * dl, B_WIDTH),
                    *post_w, cache_mem_k[0].reshape(db, mlen, d).astype(BF16),
                    cache_mem_v[0].reshape(db, mlen, d).astype(BF16),
                    w_xo[0].astype(BF16), row1(g_moe[0]), tm=dl, tiles_per_batch=1)

    y_prompt = _moe(hp, xnp_, weights).reshape(b, s, d)
    y_sample = _moe(hs, xns, weights).reshape(db, dl, d)

    def heads(x2d, nb, nl, nh):
        return x2d.reshape(1, nb, nl, nh, -1)

    akf, avf, bkf, bvf = p_in[6:10]
    if keep_a != tm_p:
        akf = akf.reshape(b, s, A_WIDTH)[:, s - keep_a:]
        avf = avf.reshape(b, s, A_WIDTH)[:, s - keep_a:]
    sakf, savf, sbkf, sbvf = s_in[6:]
    return (y_prompt, y_sample,
            heads(akf, b, keep_a, N_HEADS_A), heads(avf, b, keep_a, N_HEADS_A),
            heads(bkf, b, s, 2 * N_HEADS_B), heads(bvf, b, s, N_HEADS_B),
            heads(mkf, b, mlen, N_HEADS_M), heads(mvf, b, mlen, N_HEADS_M),
            heads(sakf, db, dl, N_HEADS_A), heads(savf, db, dl, N_HEADS_A),
            heads(sbkf, db, dl, 2 * N_HEADS_B), heads(sbvf, db, dl, N_HEADS_B))
```

```python
import functools
import math

import numpy as np
import jax
import jax.numpy as jnp
from jax import lax
from jax.experimental import pallas as pl
from jax.experimental.pallas import tpu as pltpu

F32 = jnp.float32
BF16 = jnp.bfloat16

D_MODEL = 1024
CHUNK = 64
LEFT_CHUNKS = 8
BAND = LEFT_CHUNKS * CHUNK
HEAD_DIM = 64
A_WIDTH = 512
B_WIDTH = 512
N_HEADS_A = 8
N_HEADS_B = 4
REL_CLIP = 128
ROT_DIM = 16
ROPE_THETA = 500000.0
N_HEADS_M = 4
HEAD_DIM_M = 256
N_EXPERTS = 32
TOP_K = 4
EXPERT_FF = 1024
SWIGLU_LIMIT = 7.0
SWIGLU_ALPHA = 1.702
RMS_EPS = 1e-5
NEG_INF = -1e30
LAM_INIT = 0.8 - 0.6 * math.exp(-0.3 * 0)
LOG2E = math.log2(math.e)

LANES = 128
A_QBLK = 4 * CHUNK
A_WIN = BAND + A_QBLK
B_QBLK = 512
B_KBLK = 256
VT_ONES = 16
MOE_ROWS = 256
X_SLABS = D_MODEL // 2 // LANES
Y_SLABS = D_MODEL // LANES
VMEM_LIMIT = 48 * 1024 * 1024
EXPERTS_VMEM_LIMIT = 56 * 1024 * 1024


def _cparams(sem):
    return pltpu.CompilerParams(dimension_semantics=sem, vmem_limit_bytes=VMEM_LIMIT)


def _rms(x, g):
    return x * lax.rsqrt(jnp.mean(x * x, axis=-1, keepdims=True) + RMS_EPS) * g


HI16 = 0xFFFF0000


def _pack_bf16_pairs(x):
    n = x.shape[1] // 2
    bits = lambda v: lax.bitcast_convert_type(v.astype(BF16).astype(F32), jnp.uint32)
    return (bits(x[:, :n]) >> 16) | (bits(x[:, n:]) & jnp.uint32(HI16))


def _unpack_bf16_pairs(w):
    lo = lax.bitcast_convert_type(w << 16, F32)
    hi = lax.bitcast_convert_type(w & jnp.uint32(HI16), F32)
    return jnp.concatenate([lo, hi], axis=1).astype(BF16)


def _store_row_slabs(ref, x, lead=()):
    m, n = x.shape[0], x.shape[1] // LANES
    for c in range(n):
        ref[lead + (pl.ds(c, m, stride=n), slice(None))] = x[:, c * LANES:(c + 1) * LANES]


def _load_row_slabs(ref, m, n, lead=()):
    return jnp.concatenate([ref[lead + (pl.ds(c, m, stride=n), slice(None))] for c in range(n)], axis=1)


def _dot_nt(a, b):
    return lax.dot_general(a, b, (((1,), (1,)), ((), ())), preferred_element_type=F32)


def _inproj_kernel(x_ref, g_ref, w_ref, wvt_ref, cos_ref, sa_ref, sb_ref,
                   aq_ref, ak_ref, av_ref, bq_ref, bk_ref, bv_ref,
                   akf_ref, avf_ref, bkf_ref, bvf_ref, *maybe_vt_refs):
    xn = _rms(x_ref[...], g_ref[...]).astype(BF16)
    for n, vt_ref in enumerate(maybe_vt_refs):
        vt = _dot_nt(wvt_ref[n], xn).astype(BF16)
        blk = vt_ref.shape[3]
        for c in range(vt_ref.shape[0]):
            for g in range(vt_ref.shape[1]):
                vt_ref[c, g, 0:LANES, :] = vt[g * LANES:(g + 1) * LANES, c * blk:(c + 1) * blk]
                vt_ref[c, g, LANES:LANES + VT_ONES, :] = jnp.ones((VT_ONES, blk), BF16)

    def seg(s):
        return jnp.dot(xn, w_ref[:, s * A_WIDTH:(s + 1) * A_WIDTH], preferred_element_type=F32)

    def rope(z):
        cols = []
        for c in range(z.shape[1] // LANES):
            zc = z[:, c * LANES:(c + 1) * LANES]
            cols.append(zc * cos_ref[...] + pltpu.roll(zc, 8, 1) * sa_ref[...]
                        + pltpu.roll(zc, LANES - 8, 1) * sb_ref[...])
        return jnp.concatenate(cols, axis=1)

    scale = HEAD_DIM ** -0.5
    aq_ref[...] = (seg(0) * scale).astype(BF16)
    ak = seg(1)
    ak_ref[...] = ak.astype(BF16)
    akf_ref[...] = ak
    av = seg(2)
    av_ref[...] = av.astype(BF16)
    avf_ref[...] = av
    bq_ref[...] = (rope(seg(3)) * (scale * LOG2E)).astype(BF16)
    bk = rope(seg(4))
    bk_ref[...] = bk.astype(BF16)
    bkf_ref[...] = bk
    bv = seg(5)
    bv_ref[...] = bv.astype(BF16)
    _store_row_slabs(bvf_ref, bv)


def _inproj(x2d, g, w_bf, wvt_bf, cos_t, sa_t, sb_t, *, tm, n_pos_blocks, tail_div, emit_vt):
    r = x2d.shape[0]
    n_tiles = r // tm
    row = lambda i: (i, 0)
    const = lambda i: (0, 0)
    pos = lambda i: (i % n_pos_blocks, 0)
    tail = lambda i: (i // tail_div, 0)
    bf = jax.ShapeDtypeStruct((r, A_WIDTH), BF16)
    f_all = jax.ShapeDtypeStruct((r, A_WIDTH), F32)
    f_tail = jax.ShapeDtypeStruct((r // tail_div, A_WIDTH), F32)
    blk = lambda im: pl.BlockSpec((tm, A_WIDTH), im)
    out_specs = [blk(row)] * 6 + [blk(tail), blk(tail), blk(row),
                                  pl.BlockSpec((tm * N_HEADS_B, 2 * HEAD_DIM), row)]
    out_shape = [bf] * 6 + [f_tail, f_tail, f_all,
                            jax.ShapeDtypeStruct((r * N_HEADS_B, 2 * HEAD_DIM), F32)]
    if emit_vt:
        for kb in (A_QBLK, B_KBLK):
            grp = (A_WIDTH // LANES, LANES + VT_ONES, kb)
            out_specs.append(pl.BlockSpec((tm // kb,) + grp, lambda i: (i, 0, 0, 0)))
            out_shape.append(jax.ShapeDtypeStruct((r // kb,) + grp, BF16))
    return pl.pallas_call(
        _inproj_kernel,
        grid=(n_tiles,),
        in_specs=[pl.BlockSpec((tm, D_MODEL), row), pl.BlockSpec((1, D_MODEL), const),
                  pl.BlockSpec((D_MODEL, 3 * D_MODEL), const),
                  pl.BlockSpec((2, A_WIDTH, D_MODEL), lambda i: (0, 0, 0)),
                  pl.BlockSpec((tm, LANES), pos), pl.BlockSpec((tm, LANES), pos),
                  pl.BlockSpec((tm, LANES), pos)],
        out_specs=out_specs,
        out_shape=out_shape,
        compiler_params=_cparams(("arbitrary",)),
        name="inproj",
    )(x2d, g, w_bf, wvt_bf, cos_t, sa_t, sb_t)


def _rope_tables(positions):
    half = ROT_DIM // 2
    lane = np.arange(LANES) % HEAD_DIM
    inv_freq = jnp.power(ROPE_THETA, -jnp.arange(half, dtype=F32) / half)
    ang = positions.astype(F32)[:, None] * inv_freq[None, :]
    cos = jnp.cos(ang)[:, lane % half]
    sin = jnp.sin(ang)[:, lane % half]
    in_rot = jnp.asarray(lane < ROT_DIM)[None, :]
    first = jnp.asarray(lane < half)[None, :]
    cos_t = jnp.where(in_rot, cos, 1.0)
    sa_t = jnp.where(in_rot & ~first, sin, 0.0)
    sb_t = jnp.where(first, -sin, 0.0)
    return cos_t.astype(F32), sa_t.astype(F32), sb_t.astype(F32)


def _band_table(rel_bias):
    r = np.arange(A_QBLK)[:, None]
    j = np.arange(BAND + A_WIN)[None, :]
    dchunk = LEFT_CHUNKS + r // CHUNK - j // CHUNK
    allowed = (dchunk >= 0) & (dchunk <= LEFT_CHUNKS)
    bias = rel_bias.astype(F32)
    n_seq = BAND + A_WIN + A_QBLK - 1
    n_hi = BAND + A_QBLK - 1 - REL_CLIP
    seq = jnp.concatenate([jnp.broadcast_to(bias[:, -1:], (N_HEADS_A, n_hi)), bias[:, ::-1],
                           jnp.broadcast_to(bias[:, :1], (N_HEADS_A, n_seq - n_hi - bias.shape[1]))], axis=1)
    seq = jnp.roll(seq, -(A_QBLK - 1), axis=1)
    toe = jnp.tile(seq, (1, A_QBLK))[:, :A_QBLK * (n_seq - 1)].reshape(N_HEADS_A, A_QBLK, n_seq - 1)
    return jnp.where(jnp.asarray(allowed)[None], toe[:, :, :BAND + A_WIN], NEG_INF)


def _softmax_pv(s_list, v_list, exp_fn=jnp.exp):
    m = s_list[0].max(axis=-1, keepdims=True)
    for s in s_list[1:]:
        m = jnp.maximum(m, s.max(axis=-1, keepdims=True))
    l = None
    o = None
    for s, v in zip(s_list, v_list):
        p = exp_fn(s - m)
        ls = p.sum(axis=-1, keepdims=True)
        os_ = jnp.dot(p.astype(BF16), v, preferred_element_type=F32)
        l = ls if l is None else l + ls
        o = os_ if o is None else o + os_
    return o / l


def _attn_a_prompt_kernel(q_ref, k_ref, vt_ref, tab_ref, o_ref):
    i = pl.program_id(1)
    n_left = BAND // A_QBLK
    n_sub = A_WIN // A_QBLK
    blk0 = jnp.maximum(i - n_left, 0)
    off = pl.multiple_of(jnp.maximum(n_left - i, 0) * A_QBLK, A_QBLK)
    q = q_ref[0]
    lane = lax.broadcasted_iota(jnp.int32, (A_QBLK, LANES), 1)

    def scores(h):
        cols = slice(h // 2 * LANES, (h // 2 + 1) * LANES)
        qp = q[:, cols]
        qh = jnp.where((lane >= HEAD_DIM) == bool(h % 2), qp, jnp.zeros_like(qp))
        return [_dot_nt(k_ref[0, pl.ds(pl.multiple_of((blk0 + c) * A_QBLK, A_QBLK), A_QBLK), cols], qh)
                + tab_ref[h, pl.ds(off + c * A_QBLK, A_QBLK), :] for c in range(n_sub)]

    def attend(h, ss):
        m = _col_reduce(ss[0], jnp.maximum, jnp.max)
        for s in ss[1:]:
            m = jnp.maximum(m, _col_reduce(s, jnp.maximum, jnp.max))
        o_t = None
        for c in range(n_sub):
            p = jnp.exp(ss[c] - m).astype(BF16)
            os_ = jnp.dot(vt_ref[0, blk0 + c, h // 2], p, preferred_element_type=F32)
            o_t = os_ if o_t is None else o_t + os_
        return o_t[h % 2 * HEAD_DIM:(h % 2 + 1) * HEAD_DIM] / o_t[LANES:LANES + 1]

    halves = []
    ss = scores(0)
    for h in range(N_HEADS_A):
        ss_next = scores(h + 1) if h + 1 < N_HEADS_A else None
        halves.append(attend(h, ss))
        ss = ss_next
    pair_outs = [jnp.concatenate(halves[2 * g:2 * g + 2], axis=0).T for g in range(N_HEADS_A // 2)]
    o_ref[0] = jnp.concatenate(pair_outs, axis=1).astype(BF16)


def _attn_a_prompt(aq, ak, avt, table_t):
    b, l, _ = aq.shape
    n_kb = avt.shape[1]
    return pl.pallas_call(
        _attn_a_prompt_kernel,
        grid=(b, l // A_QBLK),
        in_specs=[pl.BlockSpec((1, A_QBLK, A_WIDTH), lambda bb, i: (bb, i, 0)),
                  pl.BlockSpec((1, l, A_WIDTH), lambda bb, i: (bb, 0, 0)),
                  pl.BlockSpec((1,) + avt.shape[1:], lambda bb, i: (bb, 0, 0, 0, 0)),
                  pl.BlockSpec(table_t.shape, lambda bb, i: (0, 0, 0))],
        out_specs=pl.BlockSpec((1, A_QBLK, A_WIDTH), lambda bb, i: (bb, i, 0)),
        out_shape=jax.ShapeDtypeStruct((b, l, A_WIDTH), BF16),
        compiler_params=_cparams(("arbitrary", "arbitrary")),
        name="attn_a_prompt",
    )(aq, ak, avt, table_t)


def _attn_a_sample_kernel(q_ref, kc_ref, vc_ref, k_ref, v_ref, tab_ref, o_ref):
    q = q_ref[0]
    kc = kc_ref[0].astype(BF16)
    vc = vc_ref[0].astype(BF16)
    ko = k_ref[0]
    vo = v_ref[0]
    a_len = kc.shape[0]
    lq = q.shape[0]
    c0 = BAND - a_len
    outs = []
    for h in range(N_HEADS_A):
        sl = slice(h * HEAD_DIM, (h + 1) * HEAD_DIM)
        s_c = _dot_nt(q[:, sl], kc[:, sl]) + tab_ref[h, :, c0:BAND]
        s_o = _dot_nt(q[:, sl], ko[:, sl]) + tab_ref[h, :, BAND:BAND + lq]
        outs.append(_softmax_pv([s_c, s_o], [vc[:, sl], vo[:, sl]]))
    o_ref[0] = jnp.concatenate(outs, axis=1).astype(BF16)


def _attn_a_sample(aq, cache_k, cache_v, ak, av, table):
    b, l, _ = aq.shape
    a_len = cache_k.shape[1]
    bs = lambda n: pl.BlockSpec((1, n, A_WIDTH), lambda bb: (bb, 0, 0))
    return pl.pallas_call(
        _attn_a_sample_kernel,
        grid=(b,),
        in_specs=[bs(l), bs(a_len), bs(a_len), bs(l), bs(l),
                  pl.BlockSpec(table.shape, lambda bb: (0, 0, 0))],
        out_specs=bs(l),
        out_shape=jax.ShapeDtypeStruct((b, l, A_WIDTH), BF16),
        compiler_params=_cparams(("arbitrary",)),
        name="attn_a_sample",
    )(aq, cache_k, cache_v, ak, av, table)


def _subnorm(o, g):
    return _rms(o, g) * (1.0 - LAM_INIT)


def _col_reduce(x, op, reduce_fn):
    n = x.shape[0]
    while n > 8:
        n //= 2
        x = op(x[:n], x[n:])
    return reduce_fn(x, axis=0, keepdims=True)


def _attn_b_prompt_kernel(lam_ref, q_ref, k_ref, vt_ref, mask_ref, g_ref, o_ref,
                          sa_sc, sb_sc, m_sc, acc_sc):
    i = pl.program_id(2)
    tk = vt_ref.shape[4]
    q = q_ref[0]
    lane = lax.broadcasted_iota(jnp.int32, q.shape, 1)
    zero = jnp.zeros_like(q)
    qs = (jnp.where(lane < HEAD_DIM, q, zero), jnp.where(lane >= HEAD_DIM, q, zero))
    m_sc[...] = jnp.full(m_sc.shape, NEG_INF, F32)
    acc_sc[...] = jnp.zeros(acc_sc.shape, F32)

    def scores(j, s_sc):
        kb = k_ref[0, pl.ds(pl.multiple_of(j * tk, tk), tk), :]
        for a in range(2):
            s_sc[a] = _dot_nt(kb, qs[a])

    def consume(j, s_sc, mask):
        vt = vt_ref[0, j, 0]
        for a in range(2):
            s = s_sc[a]
            if mask is not None:
                s = s + mask
            m_old = m_sc[a]
            m_new = jnp.maximum(m_old, _col_reduce(s, jnp.maximum, jnp.max))
            alpha = jnp.exp2(m_old - m_new)
            p = jnp.exp2(s - m_new).astype(BF16)
            m_sc[a] = m_new
            acc_sc[a] = alpha * acc_sc[a] + jnp.dot(vt, p, preferred_element_type=F32)

    scores(0, sa_sc)

    def body(jj, c):
        j = 2 * jj
        scores(j + 1, sb_sc)
        consume(j, sa_sc, None)
        scores(j + 2, sa_sc)
        consume(j + 1, sb_sc, None)
        return c

    lax.fori_loop(0, i, body, 0)
    scores(2 * i + 1, sb_sc)
    consume(2 * i, sa_sc, mask_ref[0])
    consume(2 * i + 1, sb_sc, mask_ref[1])
    o_t = (acc_sc[0, 0:LANES] / acc_sc[0, LANES:LANES + 1]
           - lam_ref[0] * (acc_sc[1, 0:LANES] / acc_sc[1, LANES:LANES + 1]))
    o_ref[0] = _subnorm(o_t.T, g_ref[...]).astype(BF16)


def _chunk_causal_masks_t(tk, tq):
    kc = (np.arange(tq // tk)[:, None, None] * tk + np.arange(tk)[None, :, None]) // CHUNK
    qc = np.arange(tq)[None, None, :] // CHUNK
    return jnp.asarray(np.where(kc <= qc, 0.0, NEG_INF), F32)


def _attn_b_prompt(lam, bq, bk, bvt, g_sub):
    b, l, _ = bq.shape
    tq = B_QBLK
    tk = B_KBLK
    hw = 2 * HEAD_DIM
    n_kb = bvt.shape[1]
    return pl.pallas_call(
        _attn_b_prompt_kernel,
        grid_spec=pltpu.PrefetchScalarGridSpec(
            num_scalar_prefetch=1,
            grid=(b, N_HEADS_B, l // tq),
            in_specs=[pl.BlockSpec((1, tq, hw), lambda bb, h, i, lam_: (bb, i, h)),
                      pl.BlockSpec((1, l, hw), lambda bb, h, i, lam_: (bb, 0, h)),
                      pl.BlockSpec((1, n_kb, 1, hw + VT_ONES, tk), lambda bb, h, i, lam_: (bb, 0, h, 0, 0)),
                      pl.BlockSpec((tq // tk, tk, tq), lambda bb, h, i, lam_: (0, 0, 0)),
                      pl.BlockSpec((1, hw), lambda bb, h, i, lam_: (0, 0))],
            out_specs=pl.BlockSpec((1, tq, hw), lambda bb, h, i, lam_: (bb, i, h)),
            scratch_shapes=[pltpu.VMEM((2, tk, tq), F32), pltpu.VMEM((2, tk, tq), F32),
                            pltpu.VMEM((2, 1, tq), F32), pltpu.VMEM((2, hw + VT_ONES, tq), F32)]),
        out_shape=jax.ShapeDtypeStruct((b, l, B_WIDTH), BF16),
        compiler_params=_cparams(("arbitrary", "arbitrary", "arbitrary")),
        name="attn_b_prompt",
    )(lam, bq, bk, bvt, _chunk_causal_masks_t(tk, tq), g_sub)


def _attn_b_sample_kernel(lam_ref, q_ref, kc_ref, vc_ref, k_ref, v_ref, g_ref, o_ref):
    q = q_ref[0]
    kc = kc_ref[0].astype(BF16)
    vc = vc_ref[0].astype(BF16)
    ko = k_ref[0]
    vo = v_ref[0]
    outs = []
    for a in range(2):
        sl = slice(a * HEAD_DIM, (a + 1) * HEAD_DIM)
        s_c = _dot_nt(q[:, sl], kc[:, sl])
        s_o = _dot_nt(q[:, sl], ko[:, sl])
        outs.append(_softmax_pv([s_c, s_o], [vc, vo], exp_fn=jnp.exp2))
    o = outs[0] - lam_ref[0] * outs[1]
    o_ref[0] = _subnorm(o, g_ref[...]).astype(BF16)


def _attn_b_sample(lam, bq, cache_k, cache_v, bk, bv, g_sub):
    b, l, _ = bq.shape
    p = cache_k.shape[1]
    hw = 2 * HEAD_DIM
    bs = lambda n: pl.BlockSpec((1, n, hw), lambda bb, h, lam_: (bb, 0, h))
    return pl.pallas_call(
        _attn_b_sample_kernel,
        grid_spec=pltpu.PrefetchScalarGridSpec(
            num_scalar_prefetch=1,
            grid=(b, N_HEADS_B),
            in_specs=[bs(l), bs(p), bs(p), bs(l), bs(l),
                      pl.BlockSpec((1, hw), lambda bb, h, lam_: (0, 0))],
            out_specs=bs(l)),
        out_shape=jax.ShapeDtypeStruct((b, l, B_WIDTH), BF16),
        compiler_params=_cparams(("arbitrary", "arbitrary")),
        name="attn_b_sample",
    )(lam, bq, cache_k, cache_v, bk, bv, g_sub)


def _memkv_kernel(m_ref, g_ref, wk_ref, wv_ref, kf_ref, vf_ref, kb_ref, vb_ref):
    mn = _rms(m_ref[...], g_ref[...]).astype(BF16)
    k = jnp.dot(mn, wk_ref[...], preferred_element_type=F32)
    v = jnp.dot(mn, wv_ref[...], preferred_element_type=F32)
    kf_ref[...] = k
    vf_ref[...] = v
    kb_ref[...] = k.astype(BF16)
    vb_ref[...] = v.astype(BF16)


def _memkv(mem2d, g, wk, wv, *, tm):
    r = mem2d.shape[0]
    row = lambda i: (i, 0)
    const = lambda i: (0, 0)
    blk = pl.BlockSpec((tm, D_MODEL), row)
    wspec = pl.BlockSpec((D_MODEL, D_MODEL), const)
    f = jax.ShapeDtypeStruct((r, D_MODEL), F32)
    bf = jax.ShapeDtypeStruct((r, D_MODEL), BF16)
    return pl.pallas_call(
        _memkv_kernel,
        grid=(r // tm,),
        in_specs=[blk, pl.BlockSpec((1, D_MODEL), const), wspec, wspec],
        out_specs=[blk] * 4,
        out_shape=[f, f, bf, bf],
        compiler_params=_cparams(("arbitrary",)),
        name="memkv",
    )(mem2d, g, wk, wv)


def _post_kernel(*refs, n_real):
    h_ref, xn_ref = refs[-2:]

    @pl.when(pl.program_id(0) < n_real)
    def _():
        _post_tile(*refs[:11], h_ref, xn_ref)

    @pl.when(pl.program_id(0) >= n_real)
    def _():
        h_ref[...] = jnp.zeros(h_ref.shape, h_ref.dtype)
        xn_ref[...] = jnp.zeros(xn_ref.shape, xn_ref.dtype)


def _post_tile(x_ref, oa_ref, ob_ref, woa_ref, wob_ref, gq_ref, wq_ref, mk_ref, mv_ref, wo_ref,
               gm_ref, h_ref, xn_ref):
    h1 = (x_ref[...] + jnp.dot(oa_ref[...], woa_ref[...], preferred_element_type=F32)
          + jnp.dot(ob_ref[...], wob_ref[...], preferred_element_type=F32))
    xq = _rms(h1, gq_ref[...]).astype(BF16)
    q = (jnp.dot(xq, wq_ref[...], preferred_element_type=F32) * HEAD_DIM_M ** -0.5).astype(BF16)
    outs = []
    for h in range(N_HEADS_M):
        sl = slice(h * HEAD_DIM_M, (h + 1) * HEAD_DIM_M)
        s = _dot_nt(q[:, sl], mk_ref[0, :, sl])
        outs.append(_softmax_pv([s], [mv_ref[0, :, sl]]).astype(BF16))
    o = jnp.concatenate(outs, axis=1)
    h2 = h1 + jnp.dot(o, wo_ref[...], preferred_element_type=F32)
    h_ref[...] = h2
    _store_row_slabs(xn_ref, _pack_bf16_pairs(_rms(h2, gm_ref[...])))


def _post(x2d, oa, ob, woa, wob, gq, wq, mk, mv, wo, gm, *, tm, tiles_per_batch, total_rows,
          row_offset=0, into=None):
    r = x2d.shape[0]
    n_real = r // tm
    n_steps = n_real if into is not None else total_rows // tm
    mlen = mk.shape[1]
    row = lambda i: (jnp.minimum(i, n_real - 1), 0)
    out_row = lambda i: (row_offset // tm + i, 0)
    const = lambda i: (0, 0)
    memb = lambda i: (jnp.minimum(i, n_real - 1) // tiles_per_batch, 0, 0)
    gspec = pl.BlockSpec((1, D_MODEL), const)
    wspec = pl.BlockSpec((D_MODEL, D_MODEL), const)
    hspec = pl.BlockSpec((A_WIDTH, D_MODEL), const)
    in_specs = [pl.BlockSpec((tm, D_MODEL), row),
                pl.BlockSpec((tm, A_WIDTH), row), pl.BlockSpec((tm, B_WIDTH), row),
                hspec, hspec, gspec, wspec,
                pl.BlockSpec((1, mlen, D_MODEL), memb), pl.BlockSpec((1, mlen, D_MODEL), memb),
                wspec, gspec]
    args = [x2d, oa, ob, woa, wob, gq, wq, mk, mv, wo, gm]
    aliases = {}
    if into is not None:
        aliases = {len(args): 0, len(args) + 1: 1}
        in_specs += [pl.BlockSpec(memory_space=pl.ANY)] * 2
        args += list(into)
    return pl.pallas_call(
        functools.partial(_post_kernel, n_real=n_real),
        grid=(n_steps,),
        in_specs=in_specs,
        out_specs=[pl.BlockSpec((tm, D_MODEL), out_row), pl.BlockSpec((tm * X_SLABS, LANES), out_row)],
        out_shape=[jax.ShapeDtypeStruct((total_rows, D_MODEL), F32),
                   jax.ShapeDtypeStruct((total_rows * X_SLABS, LANES), jnp.uint32)],
        input_output_aliases=aliases,
        compiler_params=_cparams(("arbitrary",)),
        name="post_attn",
    )(*args)


def _router_kernel(xn_ref, wr_ref, br_ref, tri_ref, idx_ref, gate_ref, rank_ref, cnt_ref, base_sc):
    @pl.when(pl.program_id(0) == 0)
    def _():
        base_sc[...] = jnp.zeros(base_sc.shape, F32)

    tm = xn_ref.shape[0] // X_SLABS
    xn = _unpack_bf16_pairs(_load_row_slabs(xn_ref, tm, X_SLABS))
    logits = _dot_nt(wr_ref[...], xn) + br_ref[...]
    iota_e = lax.broadcasted_iota(jnp.int32, (N_EXPERTS, tm), 0).astype(F32)
    work = logits
    vals, idxs, hots = [], [], []
    for _k in range(TOP_K):
        mx = work.max(axis=0, keepdims=True)
        ix = jnp.where(work == mx, iota_e, float(N_EXPERTS)).min(axis=0, keepdims=True)
        hot = iota_e == ix
        vals.append(mx)
        idxs.append(ix)
        hots.append(hot)
        work = jnp.where(hot, -jnp.inf, work)
    es = [jnp.exp(v - vals[0]) for v in vals]
    den = es[0] + es[1] + es[2] + es[3]
    cnt = (hots[0] | hots[1] | hots[2] | hots[3]).astype(F32)
    before = jnp.dot(cnt.astype(BF16), tri_ref[...], preferred_element_type=F32) + base_sc[...]
    for k in range(TOP_K):
        idx_ref[k:k + 1, :] = idxs[k].astype(jnp.int32)
        gate_ref[k:k + 1, :] = es[k] / den
        rank_ref[k:k + 1, :] = jnp.where(hots[k], before, 0.0).sum(axis=0, keepdims=True).astype(jnp.int32)
    base_sc[...] = base_sc[...] + cnt.sum(axis=1, keepdims=True)
    cnt_ref[...] = jnp.broadcast_to(base_sc[...], cnt_ref.shape).astype(jnp.int32)


def _router(xn2d, wr_t, br, tri, *, tm):
    r = xn2d.shape[0] // X_SLABS
    col = lambda i: (0, i)
    const = lambda i: (0, 0)
    return pl.pallas_call(
        _router_kernel,
        grid=(r // tm,),
        in_specs=[pl.BlockSpec((tm * X_SLABS, LANES), lambda i: (i, 0)),
                  pl.BlockSpec((N_EXPERTS, D_MODEL), const),
                  pl.BlockSpec((N_EXPERTS, 1), const),
                  pl.BlockSpec((tm, tm), const)],
        out_specs=[pl.BlockSpec((TOP_K, tm), col)] * 3 + [pl.BlockSpec((N_EXPERTS, LANES), const)],
        out_shape=[jax.ShapeDtypeStruct((TOP_K, r), jnp.int32),
                   jax.ShapeDtypeStruct((TOP_K, r), F32),
                   jax.ShapeDtypeStruct((TOP_K, r), jnp.int32),
                   jax.ShapeDtypeStruct((N_EXPERTS, LANES), jnp.int32)],
        scratch_shapes=[pltpu.VMEM((N_EXPERTS, 1), F32)],
        compiler_params=_cparams(("arbitrary",)),
        name="router",
    )(xn2d, wr_t, br, tri)


def _row_copy(src, dst, sem):
    return pltpu.make_async_copy(src, dst, sem)


def _slab(ref, row, n, lead=()):
    return ref.at[lead + (pl.ds(pl.multiple_of(row * n, n), n), slice(None))]


def _scatter_kernel(pos_ref, xn_ref, xs_ref, sem):
    tm = xn_ref.shape[0] // X_SLABS

    def start(t, c):
        for k in range(TOP_K):
            _row_copy(_slab(xn_ref, t, X_SLABS), _slab(xs_ref, pos_ref[0, k, t], X_SLABS),
                      sem).start(priority=k % 2)
        return c

    lax.fori_loop(0, tm, start, 0, unroll=2)

    def wait(t, c):
        for k in range(TOP_K):
            _row_copy(_slab(xn_ref, 0, X_SLABS), _slab(xs_ref, 0, X_SLABS), sem).wait()
        return c

    lax.fori_loop(0, tm, wait, 0, unroll=4)


def _scatter(pos, xn2d, *, tm):
    r = xn2d.shape[0] // X_SLABS
    return pl.pallas_call(
        _scatter_kernel,
        grid=(r // tm,),
        in_specs=[pl.BlockSpec((1, TOP_K, tm), lambda i: (i, 0, 0), memory_space=pltpu.SMEM),
                  pl.BlockSpec((tm * X_SLABS, LANES), lambda i: (i, 0))],
        out_specs=pl.BlockSpec(memory_space=pl.ANY),
        out_shape=jax.ShapeDtypeStruct((TOP_K * r * X_SLABS, LANES), jnp.uint32),
        scratch_shapes=[pltpu.SemaphoreType.DMA],
        compiler_params=_cparams(("arbitrary",)),
        name="moe_scatter",
    )(pos, xn2d)


def _experts_kernel(blk_ref, exp_ref, lo_ref, hi_ref, first_ref, newexp_ref,
                    x_ref, wgu_ref, bgu_ref, wd_ref, bd_ref, y_ref, wgu_bf, wd_bf):
    i = pl.program_id(0)

    @pl.when(newexp_ref[i] == 1)
    def _():
        wgu_bf[...] = wgu_ref[0].astype(BF16)
        wd_bf[...] = wd_ref[0].astype(BF16)

    x = _unpack_bf16_pairs(_load_row_slabs(x_ref, MOE_ROWS, X_SLABS))
    gu = jnp.dot(x, wgu_bf[...], preferred_element_type=F32) + bgu_ref[0]
    gate = jnp.minimum(gu[:, :EXPERT_FF], SWIGLU_LIMIT)
    up = jnp.clip(gu[:, EXPERT_FF:], -SWIGLU_LIMIT, SWIGLU_LIMIT)
    act = (up + 1.0) * (gate * jax.nn.sigmoid(SWIGLU_ALPHA * gate))
    y = jnp.dot(act.astype(BF16), wd_bf[...], preferred_element_type=F32) + bd_ref[0]
    row0 = blk_ref[i] * MOE_ROWS
    whole = (lo_ref[i] <= row0) & (hi_ref[i] >= row0 + MOE_ROWS)

    @pl.when(whole)
    def _():
        _store_row_slabs(y_ref, y)

    @pl.when(jnp.logical_not(whole))
    def _():
        @pl.when(first_ref[i] == 1)
        def _():
            y_ref[...] = jnp.zeros(y_ref.shape, F32)

        rows = row0 + lax.broadcasted_iota(jnp.int32, (MOE_ROWS, 1), 0)
        mine = (rows >= lo_ref[i]) & (rows < hi_ref[i])
        _store_row_slabs(y_ref, jnp.where(mine, y, _load_row_slabs(y_ref, MOE_ROWS, Y_SLABS)))


def _experts(items, xs, wgu, bgu, wd, bd):
    blk, exp, lo, hi, first = items
    newexp = jnp.concatenate([jnp.ones((1,), jnp.int32), (exp[1:] != exp[:-1]).astype(jnp.int32)])
    n_items = blk.shape[0]
    n_rows = xs.shape[0] // X_SLABS
    return pl.pallas_call(
        _experts_kernel,
        grid_spec=pltpu.PrefetchScalarGridSpec(
            num_scalar_prefetch=6,
            grid=(n_items,),
            in_specs=[pl.BlockSpec((MOE_ROWS * X_SLABS, LANES), lambda i, b, e, *_: (b[i], 0)),
                      pl.BlockSpec((1, D_MODEL, 2 * EXPERT_FF), lambda i, b, e, *_: (e[i], 0, 0)),
                      pl.BlockSpec((1, 1, 2 * EXPERT_FF), lambda i, b, e, *_: (e[i], 0, 0)),
                      pl.BlockSpec((1, EXPERT_FF, D_MODEL), lambda i, b, e, *_: (e[i], 0, 0)),
                      pl.BlockSpec((1, 1, D_MODEL), lambda i, b, e, *_: (e[i], 0, 0))],
            out_specs=pl.BlockSpec((MOE_ROWS * Y_SLABS, LANES), lambda i, b, e, *_: (b[i], 0)),
            scratch_shapes=[pltpu.VMEM((D_MODEL, 2 * EXPERT_FF), BF16), pltpu.VMEM((EXPERT_FF, D_MODEL), BF16)]),
        out_shape=jax.ShapeDtypeStruct((n_rows * Y_SLABS, LANES), F32),
        compiler_params=pltpu.CompilerParams(dimension_semantics=("arbitrary",),
                                             vmem_limit_bytes=EXPERTS_VMEM_LIMIT),
        name="moe_experts",
    )(blk, exp, lo, hi, first, newexp, xs, wgu, bgu, wd, bd)


def _work_items(counts, n_rows):
    n_blocks = n_rows // MOE_ROWS
    n_items = n_blocks + N_EXPERTS - 1
    ends = jnp.cumsum(counts)
    starts = ends - counts
    nb = jnp.where(counts > 0, (ends - 1) // MOE_ROWS - starts // MOE_ROWS + 1, 0)
    item_end = jnp.cumsum(nb)
    item_start = item_end - nb
    total = item_end[-1]
    i = jnp.arange(n_items, dtype=jnp.int32)
    e = jnp.minimum(jnp.sum(item_end[None, :] <= i[:, None], axis=1), N_EXPERTS - 1).astype(jnp.int32)
    blk = (starts[e] // MOE_ROWS + (i - item_start[e])).astype(jnp.int32)
    valid = i < total
    lo = jnp.maximum(starts[e], blk * MOE_ROWS)
    hi = jnp.minimum(ends[e], (blk + 1) * MOE_ROWS)
    last = jnp.maximum(total - 1, 0)
    blk = jnp.where(valid, blk, blk[last])
    e = jnp.where(valid, e, e[last])
    lo = jnp.where(valid, lo, 0).astype(jnp.int32)
    hi = jnp.where(valid, hi, 0).astype(jnp.int32)
    prev = jnp.concatenate([jnp.full((1,), -1, jnp.int32), blk[:-1]])
    first = (blk != prev).astype(jnp.int32)
    return blk, e, lo, hi, first, starts


def _combine_kernel(pos_ref, pos_next_ref, h_ref, gate_ref, g_ref, ys_ref, o_first_ref, o_rest_ref, buf, sem,
                    *, n_first):
    tm = h_ref.shape[0]
    i = pl.program_id(0)
    n = pl.num_programs(0)
    cur = i % 2

    def fetch(p_ref, slot):
        def start(t, c):
            for k in range(TOP_K):
                _row_copy(_slab(ys_ref, p_ref[0, k, t], Y_SLABS), _slab(buf, t, Y_SLABS, (slot, k)),
                          sem.at[slot]).start(priority=k % 2)
            return c

        lax.fori_loop(0, tm, start, 0, unroll=2)

    @pl.when(i == 0)
    def _():
        fetch(pos_ref, 0)

    @pl.when(i + 1 < n)
    def _():
        fetch(pos_next_ref, 1 - cur)

    def wait(t, c):
        for k in range(TOP_K):
            _row_copy(_slab(ys_ref, 0, Y_SLABS), _slab(buf, 0, Y_SLABS, (cur, 0)), sem.at[cur]).wait()
        return c

    lax.fori_loop(0, tm, wait, 0, unroll=4)
    acc = h_ref[...]
    for k in range(TOP_K):
        acc = acc + gate_ref[:, k:k + 1] * _load_row_slabs(buf, tm, Y_SLABS, (cur, k))
    res = _rms(acc, g_ref[...])

    @pl.when(i < n_first)
    def _():
        o_first_ref[...] = res

    @pl.when(i >= n_first)
    def _():
        o_rest_ref[...] = res


def _combine(pos, h2d, gates_t, g_final, ys, *, tm, rows_first):
    r = h2d.shape[0]
    n_tiles = r // tm
    n_first = rows_first // tm
    row = lambda i: (i, 0)
    return pl.pallas_call(
        functools.partial(_combine_kernel, n_first=n_first),
        grid=(n_tiles,),
        in_specs=[pl.BlockSpec((1, TOP_K, tm), lambda i: (i, 0, 0), memory_space=pltpu.SMEM),
                  pl.BlockSpec((1, TOP_K, tm), lambda i: (jnp.minimum(i + 1, n_tiles - 1), 0, 0),
                               memory_space=pltpu.SMEM),
                  pl.BlockSpec((tm, D_MODEL), row),
                  pl.BlockSpec((tm, TOP_K), row),
                  pl.BlockSpec((1, D_MODEL), lambda i: (0, 0)),
                  pl.BlockSpec(memory_space=pl.ANY)],
        out_specs=[pl.BlockSpec((tm, D_MODEL), lambda i: (jnp.minimum(i, n_first - 1), 0)),
                   pl.BlockSpec((tm, D_MODEL), lambda i: (jnp.maximum(i - n_first, 0), 0))],
        out_shape=[jax.ShapeDtypeStruct((rows_first, D_MODEL), F32),
                   jax.ShapeDtypeStruct((r - rows_first, D_MODEL), F32)],
        scratch_shapes=[pltpu.VMEM((2, TOP_K, tm * Y_SLABS, LANES), F32), pltpu.SemaphoreType.DMA((2,))],
        compiler_params=_cparams(("arbitrary",)),
        name="moe_combine",
    )(pos, pos, h2d, gates_t, g_final, ys)


def _moe(h2d, xn2d, w, rows_first):
    r = h2d.shape[0]
    tm_router = _pick_tile(r, 512)
    tm_scatter = _pick_tile(r, 512)
    tm_combine = _pick_tile(r, 256)
    idx, gates, rank, cnt = _router(xn2d, w["wr_t"], w["br"], w["tri"][:tm_router, :tm_router], tm=tm_router)
    counts = cnt[:, 0]
    blk, e, lo, hi, first, starts = _work_items(counts, TOP_K * r)
    hot = idx[:, :, None] == jnp.arange(N_EXPERTS, dtype=jnp.int32)
    pos = jnp.sum(jnp.where(hot, starts.astype(jnp.int32), 0), axis=-1) + rank
    tiles = lambda tm: pos.reshape(TOP_K, -1, tm).transpose(1, 0, 2)
    xs = _scatter(tiles(tm_scatter), xn2d, tm=tm_scatter)
    ys = _experts((blk, e, lo, hi, first), xs, w["wgu"], w["bgu"], w["wd"], w["bd"])
    return _combine(tiles(tm_combine), h2d, gates.T, w["g_final"], ys, tm=tm_combine, rows_first=rows_first)


def _pick_tile(n, cap):
    t = cap
    while n % t:
        t //= 2
    return t


def kernel(x_prompt, x_sample, mem_prompt, cache_a_k, cache_a_v, cache_b_k, cache_b_v, cache_mem_k, cache_mem_v, g_mix, w_in, rel_bias, lam_q1, lam_k1, lam_q2, lam_k2, g_subln, w_out, g_xq, g_xmem, w_xq, w_xk, w_xv, w_xo, g_moe, w_router, b_router, w_gate_up, b_gate_up, w_down, b_down, g_final):
    assert g_mix.shape[0] == 1, "single-layer kernel"
    b, s, d = x_prompt.shape
    db, dl, _ = x_sample.shape
    past = cache_b_k.shape[2]
    a_len = cache_a_k.shape[2]
    mlen = mem_prompt.shape[1]
    keep_a = min(BAND, s)
    assert d == D_MODEL and s % B_QBLK == 0 and s >= A_WIN and dl == CHUNK and past % CHUNK == 0
    assert B_QBLK == 2 * B_KBLK

    row1 = lambda v: v.reshape(1, -1).astype(F32)
    w_in_bf = w_in[0].astype(BF16)
    w_out_bf = w_out[0].astype(BF16)
    weights = dict(
        wr_t=w_router[0].T.astype(BF16), br=b_router[0].reshape(N_EXPERTS, 1).astype(F32),
        tri=jnp.asarray(np.triu(np.ones((512, 512), np.float32), 1), BF16),
        wgu=w_gate_up[0].astype(F32), bgu=b_gate_up[0].reshape(N_EXPERTS, 1, -1).astype(F32),
        wd=w_down[0].astype(F32), bd=b_down[0].reshape(N_EXPERTS, 1, -1).astype(F32),
        g_final=row1(g_final))
    lam = (jnp.exp(jnp.sum(lam_q1[0].astype(F32) * lam_k1[0].astype(F32)))
           - jnp.exp(jnp.sum(lam_q2[0].astype(F32) * lam_k2[0].astype(F32))) + LAM_INIT).reshape(1)
    table = _band_table(rel_bias[0])
    g_sub = row1(g_subln[0])

    tm_p = _pick_tile(s, 512)
    tabs_p = _rope_tables(jnp.arange(s, dtype=jnp.int32))
    w_bvt_bf = jnp.stack([w_in[0][:, 2 * A_WIDTH:3 * A_WIDTH].T,
                          w_in[0][:, 3 * A_WIDTH + 2 * B_WIDTH:].T]).astype(BF16)
    p_in = _inproj(x_prompt.reshape(b * s, d), row1(g_mix[0]), w_in_bf, w_bvt_bf, *tabs_p,
                   tm=tm_p, n_pos_blocks=s // tm_p, tail_div=s // tm_p if keep_a == tm_p else 1, emit_vt=True)
    tabs_s = _rope_tables(jnp.tile(past + jnp.arange(dl, dtype=jnp.int32), db))
    s_in = _inproj(x_sample.reshape(db * dl, d), row1(g_mix[0]), w_in_bf, w_bvt_bf, *tabs_s,
                   tm=db * dl, n_pos_blocks=1, tail_div=1, emit_vt=False)

    def split(outs, nb, nl):
        return [o.reshape(nb, nl, A_WIDTH) for o in outs[:6]]

    aq, ak, _, bq, bk, _ = split(p_in, b, s)
    avt = p_in[10].reshape((b, s // A_QBLK) + p_in[10].shape[1:])
    bvt = p_in[11].reshape((b, s // B_KBLK) + p_in[11].shape[1:])
    saq, sak, sav, sbq, sbk, sbv = split(s_in, db, dl)

    oa = _attn_a_prompt(aq, ak, avt, table.transpose(0, 2, 1))
    ob = _attn_b_prompt(lam, bq, bk, bvt, g_sub)
    soa = _attn_a_sample(saq, cache_a_k[0].reshape(db, a_len, A_WIDTH), cache_a_v[0].reshape(db, a_len, A_WIDTH),
                         sak, sav, table[:, :dl, :BAND + dl])
    sob = _attn_b_sample(lam, sbq, cache_b_k[0].reshape(db, past, B_WIDTH), cache_b_v[0].reshape(db, past, B_WIDTH),
                         sbk, sbv, g_sub)

    mkf, mvf, mkb, mvb = _memkv(mem_prompt.reshape(b * mlen, d), row1(g_xmem[0]),
                                w_xk[0].astype(BF16), w_xv[0].astype(BF16), tm=mlen)
    post_w = (w_out_bf[:A_WIDTH], w_out_bf[A_WIDTH:], row1(g_xq[0]), w_xq[0].astype(BF16))
    tm_post = _pick_tile(math.gcd(s, b * s + db * dl), 256)
    rows_p, rows_s = b * s, db * dl
    h_xn = _post(x_prompt.reshape(rows_p, d), oa.reshape(rows_p, A_WIDTH), ob.reshape(rows_p, B_WIDTH),
                 *post_w, mkb.reshape(b, mlen, d), mvb.reshape(b, mlen, d),
                 w_xo[0].astype(BF16), row1(g_moe[0]), tm=tm_post, tiles_per_batch=s // tm_post,
                 total_rows=rows_p + rows_s)
    h_all, xn_all = _post(x_sample.reshape(rows_s, d), soa.reshape(rows_s, A_WIDTH), sob.reshape(rows_s, B_WIDTH),
                          *post_w, cache_mem_k[0].reshape(db, mlen, d).astype(BF16),
                          cache_mem_v[0].reshape(db, mlen, d).astype(BF16),
                          w_xo[0].astype(BF16), row1(g_moe[0]), tm=dl, tiles_per_batch=1,
                          total_rows=rows_p + rows_s, row_offset=rows_p, into=h_xn)

    y_prompt, y_sample = _moe(h_all, xn_all, weights, rows_p)
    y_prompt = y_prompt.reshape(b, s, d)
    y_sample = y_sample.reshape(db, dl, d)

    def heads(x2d, nb, nl, nh):
        return x2d.reshape(1, nb, nl, nh, -1)

    akf, avf, bkf, bvf = p_in[6:10]
    if keep_a != tm_p:
        akf = akf.reshape(b, s, A_WIDTH)[:, s - keep_a:]
        avf = avf.reshape(b, s, A_WIDTH)[:, s - keep_a:]
    sakf, savf, sbkf, sbvf = s_in[6:]
    return (y_prompt, y_sample,
            heads(akf, b, keep_a, N_HEADS_A), heads(avf, b, keep_a, N_HEADS_A),
            heads(bkf, b, s, 2 * N_HEADS_B), heads(bvf, b, s, N_HEADS_B),
            heads(mkf, b, mlen, N_HEADS_M), heads(mvf, b, mlen, N_HEADS_M),
            heads(sakf, db, dl, N_HEADS_A), heads(savf, db, dl, N_HEADS_A),
            heads(sbkf, db, dl, 2 * N_HEADS_B), heads(sbvf, db, dl, N_HEADS_B))
```

```python
import functools
import math

import numpy as np
import jax
import jax.numpy as jnp
from jax import lax
from jax.experimental import pallas as pl
from jax.experimental.pallas import tpu as pltpu

F32 = jnp.float32
BF16 = jnp.bfloat16

D_MODEL = 1024
CHUNK = 64
LEFT_CHUNKS = 8
BAND = LEFT_CHUNKS * CHUNK
HEAD_DIM = 64
A_WIDTH = 512
B_WIDTH = 512
N_HEADS_A = 8
N_HEADS_B = 4
REL_CLIP = 128
ROT_DIM = 16
ROPE_THETA = 500000.0
N_HEADS_M = 4
HEAD_DIM_M = 256
N_EXPERTS = 32
TOP_K = 4
EXPERT_FF = 1024
SWIGLU_LIMIT = 7.0
SWIGLU_ALPHA = 1.702
RMS_EPS = 1e-5
NEG_INF = -1e30
LAM_INIT = 0.8 - 0.6 * math.exp(-0.3 * 0)
LOG2E = math.log2(math.e)

LANES = 128
A_QBLK = 4 * CHUNK
A_WIN = BAND + A_QBLK
B_QBLK = 512
B_KBLK = 256
VT_ONES = 16
MOE_ROWS = 256
X_SLABS = D_MODEL // 2 // LANES
Y_SLABS = D_MODEL // LANES
VMEM_LIMIT = 48 * 1024 * 1024
EXPERTS_VMEM_LIMIT = 56 * 1024 * 1024


def _cparams(sem):
    return pltpu.CompilerParams(dimension_semantics=sem, vmem_limit_bytes=VMEM_LIMIT)


def _rms(x, g):
    return x * lax.rsqrt(jnp.mean(x * x, axis=-1, keepdims=True) + RMS_EPS) * g


HI16 = 0xFFFF0000


def _pack_bf16_pairs(x):
    n = x.shape[1] // 2
    bits = lambda v: lax.bitcast_convert_type(v.astype(BF16).astype(F32), jnp.uint32)
    return (bits(x[:, :n]) >> 16) | (bits(x[:, n:]) & jnp.uint32(HI16))


def _unpack_bf16_pairs(w):
    lo = lax.bitcast_convert_type(w << 16, F32)
    hi = lax.bitcast_convert_type(w & jnp.uint32(HI16), F32)
    return jnp.concatenate([lo, hi], axis=1).astype(BF16)


def _store_row_slabs(ref, x, lead=()):
    m, n = x.shape[0], x.shape[1] // LANES
    for c in range(n):
        ref[lead + (pl.ds(c, m, stride=n), slice(None))] = x[:, c * LANES:(c + 1) * LANES]


def _load_row_slabs(ref, m, n, lead=()):
    return jnp.concatenate([ref[lead + (pl.ds(c, m, stride=n), slice(None))] for c in range(n)], axis=1)


def _dot_nt(a, b):
    return lax.dot_general(a, b, (((1,), (1,)), ((), ())), preferred_element_type=F32)


def _inproj_kernel(x_ref, g_ref, w_ref, cos_ref, sa_ref, sb_ref,
                   aq_ref, ak_ref, av_ref, bq_ref, bk_ref, bv_ref,
                   akf_ref, avf_ref, bkf_ref, bvf_ref, *maybe_vt_refs):
    xn = _rms(x_ref[...], g_ref[...]).astype(BF16)

    def seg(s):
        return jnp.dot(xn, w_ref[:, s * A_WIDTH:(s + 1) * A_WIDTH], preferred_element_type=F32)

    def rope(z):
        cols = []
        for c in range(z.shape[1] // LANES):
            zc = z[:, c * LANES:(c + 1) * LANES]
            cols.append(zc * cos_ref[...] + pltpu.roll(zc, 8, 1) * sa_ref[...]
                        + pltpu.roll(zc, LANES - 8, 1) * sb_ref[...])
        return jnp.concatenate(cols, axis=1)

    scale = HEAD_DIM ** -0.5
    aq_ref[...] = (seg(0) * scale).astype(BF16)
    ak = seg(1)
    ak_ref[...] = ak.astype(BF16)
    akf_ref[...] = ak
    av = seg(2)
    av_ref[...] = av.astype(BF16)
    avf_ref[...] = av
    bq_ref[...] = (rope(seg(3)) * (scale * LOG2E)).astype(BF16)
    bk = rope(seg(4))
    bk_ref[...] = bk.astype(BF16)
    bkf_ref[...] = bk
    bv = seg(5)
    bv_ref[...] = bv.astype(BF16)
    _store_row_slabs(bvf_ref, bv)
    for vt_ref, v in zip(maybe_vt_refs, (av, bv)):
        vt = v.T.astype(BF16)
        blk = vt_ref.shape[3]
        for c in range(vt_ref.shape[0]):
            for g in range(vt_ref.shape[1]):
                vt_ref[c, g, 0:LANES, :] = vt[g * LANES:(g + 1) * LANES, c * blk:(c + 1) * blk]
                vt_ref[c, g, LANES:LANES + VT_ONES, :] = jnp.ones((VT_ONES, blk), BF16)


def _inproj(x2d, g, w_bf, cos_t, sa_t, sb_t, *, tm, n_pos_blocks, tail_div, emit_vt):
    r = x2d.shape[0]
    n_tiles = r // tm
    row = lambda i: (i, 0)
    const = lambda i: (0, 0)
    pos = lambda i: (i % n_pos_blocks, 0)
    tail = lambda i: (i // tail_div, 0)
    bf = jax.ShapeDtypeStruct((r, A_WIDTH), BF16)
    f_all = jax.ShapeDtypeStruct((r, A_WIDTH), F32)
    f_tail = jax.ShapeDtypeStruct((r // tail_div, A_WIDTH), F32)
    blk = lambda im: pl.BlockSpec((tm, A_WIDTH), im)
    out_specs = [blk(row)] * 6 + [blk(tail), blk(tail), blk(row),
                                  pl.BlockSpec((tm * N_HEADS_B, 2 * HEAD_DIM), row)]
    out_shape = [bf] * 6 + [f_tail, f_tail, f_all,
                            jax.ShapeDtypeStruct((r * N_HEADS_B, 2 * HEAD_DIM), F32)]
    if emit_vt:
        for kb in (A_QBLK, B_KBLK):
            grp = (A_WIDTH // LANES, LANES + VT_ONES, kb)
            out_specs.append(pl.BlockSpec((tm // kb,) + grp, lambda i: (i, 0, 0, 0)))
            out_shape.append(jax.ShapeDtypeStruct((r // kb,) + grp, BF16))
    return pl.pallas_call(
        _inproj_kernel,
        grid=(n_tiles,),
        in_specs=[pl.BlockSpec((tm, D_MODEL), row), pl.BlockSpec((1, D_MODEL), const),
                  pl.BlockSpec((D_MODEL, 3 * D_MODEL), const),
                  pl.BlockSpec((tm, LANES), pos), pl.BlockSpec((tm, LANES), pos),
                  pl.BlockSpec((tm, LANES), pos)],
        out_specs=out_specs,
        out_shape=out_shape,
        compiler_params=_cparams(("arbitrary",)),
        name="inproj",
    )(x2d, g, w_bf, cos_t, sa_t, sb_t)


def _rope_tables(positions):
    half = ROT_DIM // 2
    lane = np.arange(LANES) % HEAD_DIM
    inv_freq = jnp.power(ROPE_THETA, -jnp.arange(half, dtype=F32) / half)
    ang = positions.astype(F32)[:, None] * inv_freq[None, :]
    cos = jnp.cos(ang)[:, lane % half]
    sin = jnp.sin(ang)[:, lane % half]
    in_rot = jnp.asarray(lane < ROT_DIM)[None, :]
    first = jnp.asarray(lane < half)[None, :]
    cos_t = jnp.where(in_rot, cos, 1.0)
    sa_t = jnp.where(in_rot & ~first, sin, 0.0)
    sb_t = jnp.where(first, -sin, 0.0)
    return cos_t.astype(F32), sa_t.astype(F32), sb_t.astype(F32)


def _band_table(rel_bias):
    r = np.arange(A_QBLK)[:, None]
    j = np.arange(BAND + A_WIN)[None, :]
    dchunk = LEFT_CHUNKS + r // CHUNK - j // CHUNK
    allowed = (dchunk >= 0) & (dchunk <= LEFT_CHUNKS)
    bias = rel_bias.astype(F32)
    n_seq = BAND + A_WIN + A_QBLK - 1
    n_hi = BAND + A_QBLK - 1 - REL_CLIP
    seq = jnp.concatenate([jnp.broadcast_to(bias[:, -1:], (N_HEADS_A, n_hi)), bias[:, ::-1],
                           jnp.broadcast_to(bias[:, :1], (N_HEADS_A, n_seq - n_hi - bias.shape[1]))], axis=1)
    seq = jnp.roll(seq, -(A_QBLK - 1), axis=1)
    toe = jnp.tile(seq, (1, A_QBLK))[:, :A_QBLK * (n_seq - 1)].reshape(N_HEADS_A, A_QBLK, n_seq - 1)
    return jnp.where(jnp.asarray(allowed)[None], toe[:, :, :BAND + A_WIN], NEG_INF)


def _softmax_pv(s_list, v_list, exp_fn=jnp.exp):
    m = s_list[0].max(axis=-1, keepdims=True)
    for s in s_list[1:]:
        m = jnp.maximum(m, s.max(axis=-1, keepdims=True))
    l = None
    o = None
    for s, v in zip(s_list, v_list):
        p = exp_fn(s - m)
        ls = p.sum(axis=-1, keepdims=True)
        os_ = jnp.dot(p.astype(BF16), v, preferred_element_type=F32)
        l = ls if l is None else l + ls
        o = os_ if o is None else o + os_
    return o / l


def _attn_a_prompt_kernel(q_ref, k_ref, vt_ref, tab_ref, o_ref):
    i = pl.program_id(1)
    n_left = BAND // A_QBLK
    n_sub = A_WIN // A_QBLK
    blk0 = jnp.maximum(i - n_left, 0)
    off = pl.multiple_of(jnp.maximum(n_left - i, 0) * A_QBLK, A_QBLK)
    q = q_ref[0]
    lane = lax.broadcasted_iota(jnp.int32, (A_QBLK, LANES), 1)

    def scores(h):
        cols = slice(h // 2 * LANES, (h // 2 + 1) * LANES)
        qp = q[:, cols]
        qh = jnp.where((lane >= HEAD_DIM) == bool(h % 2), qp, jnp.zeros_like(qp))
        return [_dot_nt(k_ref[0, pl.ds(pl.multiple_of((blk0 + c) * A_QBLK, A_QBLK), A_QBLK), cols], qh)
                + tab_ref[h, pl.ds(off + c * A_QBLK, A_QBLK), :] for c in range(n_sub)]

    def attend(h, ss):
        m = _col_reduce(ss[0], jnp.maximum, jnp.max)
        for s in ss[1:]:
            m = jnp.maximum(m, _col_reduce(s, jnp.maximum, jnp.max))
        o_t = None
        for c in range(n_sub):
            p = jnp.exp(ss[c] - m).astype(BF16)
            os_ = jnp.dot(vt_ref[0, blk0 + c, h // 2], p, preferred_element_type=F32)
            o_t = os_ if o_t is None else o_t + os_
        return o_t[h % 2 * HEAD_DIM:(h % 2 + 1) * HEAD_DIM] / o_t[LANES:LANES + 1]

    halves = []
    ss = scores(0)
    for h in range(N_HEADS_A):
        ss_next = scores(h + 1) if h + 1 < N_HEADS_A else None
        halves.append(attend(h, ss))
        ss = ss_next
    pair_outs = [jnp.concatenate(halves[2 * g:2 * g + 2], axis=0).T for g in range(N_HEADS_A // 2)]
    o_ref[0] = jnp.concatenate(pair_outs, axis=1).astype(BF16)


def _attn_a_prompt(aq, ak, avt, table_t):
    b, l, _ = aq.shape
    n_kb = avt.shape[1]
    return pl.pallas_call(
        _attn_a_prompt_kernel,
        grid=(b, l // A_QBLK),
        in_specs=[pl.BlockSpec((1, A_QBLK, A_WIDTH), lambda bb, i: (bb, i, 0)),
                  pl.BlockSpec((1, l, A_WIDTH), lambda bb, i: (bb, 0, 0)),
                  pl.BlockSpec((1,) + avt.shape[1:], lambda bb, i: (bb, 0, 0, 0, 0)),
                  pl.BlockSpec(table_t.shape, lambda bb, i: (0, 0, 0))],
        out_specs=pl.BlockSpec((1, A_QBLK, A_WIDTH), lambda bb, i: (bb, i, 0)),
        out_shape=jax.ShapeDtypeStruct((b, l, A_WIDTH), BF16),
        compiler_params=_cparams(("arbitrary", "arbitrary")),
        name="attn_a_prompt",
    )(aq, ak, avt, table_t)


def _attn_a_sample_kernel(q_ref, kc_ref, vc_ref, k_ref, v_ref, tab_ref, o_ref):
    q = q_ref[0]
    kc = kc_ref[0].astype(BF16)
    vc = vc_ref[0].astype(BF16)
    ko = k_ref[0]
    vo = v_ref[0]
    a_len = kc.shape[0]
    lq = q.shape[0]
    c0 = BAND - a_len
    outs = []
    for h in range(N_HEADS_A):
        sl = slice(h * HEAD_DIM, (h + 1) * HEAD_DIM)
        s_c = _dot_nt(q[:, sl], kc[:, sl]) + tab_ref[h, :, c0:BAND]
        s_o = _dot_nt(q[:, sl], ko[:, sl]) + tab_ref[h, :, BAND:BAND + lq]
        outs.append(_softmax_pv([s_c, s_o], [vc[:, sl], vo[:, sl]]))
    o_ref[0] = jnp.concatenate(outs, axis=1).astype(BF16)


def _attn_a_sample(aq, cache_k, cache_v, ak, av, table):
    b, l, _ = aq.shape
    a_len = cache_k.shape[1]
    bs = lambda n: pl.BlockSpec((1, n, A_WIDTH), lambda bb: (bb, 0, 0))
    return pl.pallas_call(
        _attn_a_sample_kernel,
        grid=(b,),
        in_specs=[bs(l), bs(a_len), bs(a_len), bs(l), bs(l),
                  pl.BlockSpec(table.shape, lambda bb: (0, 0, 0))],
        out_specs=bs(l),
        out_shape=jax.ShapeDtypeStruct((b, l, A_WIDTH), BF16),
        compiler_params=_cparams(("arbitrary",)),
        name="attn_a_sample",
    )(aq, cache_k, cache_v, ak, av, table)


def _subnorm(o, g):
    return _rms(o, g) * (1.0 - LAM_INIT)


def _col_reduce(x, op, reduce_fn):
    n = x.shape[0]
    while n > 8:
        n //= 2
        x = op(x[:n], x[n:])
    return reduce_fn(x, axis=0, keepdims=True)


def _attn_b_prompt_kernel(lam_ref, q_ref, k_ref, vt_ref, mask_ref, g_ref, o_ref,
                          sa_sc, sb_sc, m_sc, acc_sc):
    i = pl.program_id(2)
    tk = vt_ref.shape[4]
    q = q_ref[0]
    lane = lax.broadcasted_iota(jnp.int32, q.shape, 1)
    zero = jnp.zeros_like(q)
    qs = (jnp.where(lane < HEAD_DIM, q, zero), jnp.where(lane >= HEAD_DIM, q, zero))
    m_sc[...] = jnp.full(m_sc.shape, NEG_INF, F32)
    acc_sc[...] = jnp.zeros(acc_sc.shape, F32)

    def scores(j, s_sc):
        kb = k_ref[0, pl.ds(pl.multiple_of(j * tk, tk), tk), :]
        for a in range(2):
            s_sc[a] = _dot_nt(kb, qs[a])

    def consume(j, s_sc, mask):
        vt = vt_ref[0, j, 0]
        for a in range(2):
            s = s_sc[a]
            if mask is not None:
                s = s + mask
            m_old = m_sc[a]
            m_new = jnp.maximum(m_old, _col_reduce(s, jnp.maximum, jnp.max))
            alpha = jnp.exp2(m_old - m_new)
            p = jnp.exp2(s - m_new).astype(BF16)
            m_sc[a] = m_new
            acc_sc[a] = alpha * acc_sc[a] + jnp.dot(vt, p, preferred_element_type=F32)

    scores(0, sa_sc)

    def body(jj, c):
        j = 2 * jj
        scores(j + 1, sb_sc)
        consume(j, sa_sc, None)
        scores(j + 2, sa_sc)
        consume(j + 1, sb_sc, None)
        return c

    lax.fori_loop(0, i, body, 0)
    scores(2 * i + 1, sb_sc)
    consume(2 * i, sa_sc, mask_ref[0])
    consume(2 * i + 1, sb_sc, mask_ref[1])
    o_t = (acc_sc[0, 0:LANES] / acc_sc[0, LANES:LANES + 1]
           - lam_ref[0] * (acc_sc[1, 0:LANES] / acc_sc[1, LANES:LANES + 1]))
    o_ref[0] = _subnorm(o_t.T, g_ref[...]).astype(BF16)


def _chunk_causal_masks_t(tk, tq):
    kc = (np.arange(tq // tk)[:, None, None] * tk + np.arange(tk)[None, :, None]) // CHUNK
    qc = np.arange(tq)[None, None, :] // CHUNK
    return jnp.asarray(np.where(kc <= qc, 0.0, NEG_INF), F32)


def _attn_b_prompt(lam, bq, bk, bvt, g_sub):
    b, l, _ = bq.shape
    tq = B_QBLK
    tk = B_KBLK
    hw = 2 * HEAD_DIM
    n_kb = bvt.shape[1]
    return pl.pallas_call(
        _attn_b_prompt_kernel,
        grid_spec=pltpu.PrefetchScalarGridSpec(
            num_scalar_prefetch=1,
            grid=(b, N_HEADS_B, l // tq),
            in_specs=[pl.BlockSpec((1, tq, hw), lambda bb, h, i, lam_: (bb, i, h)),
                      pl.BlockSpec((1, l, hw), lambda bb, h, i, lam_: (bb, 0, h)),
                      pl.BlockSpec((1, n_kb, 1, hw + VT_ONES, tk), lambda bb, h, i, lam_: (bb, 0, h, 0, 0)),
                      pl.BlockSpec((tq // tk, tk, tq), lambda bb, h, i, lam_: (0, 0, 0)),
                      pl.BlockSpec((1, hw), lambda bb, h, i, lam_: (0, 0))],
            out_specs=pl.BlockSpec((1, tq, hw), lambda bb, h, i, lam_: (bb, i, h)),
            scratch_shapes=[pltpu.VMEM((2, tk, tq), F32), pltpu.VMEM((2, tk, tq), F32),
                            pltpu.VMEM((2, 1, tq), F32), pltpu.VMEM((2, hw + VT_ONES, tq), F32)]),
        out_shape=jax.ShapeDtypeStruct((b, l, B_WIDTH), BF16),
        compiler_params=_cparams(("arbitrary", "arbitrary", "arbitrary")),
        name="attn_b_prompt",
    )(lam, bq, bk, bvt, _chunk_causal_masks_t(tk, tq), g_sub)


def _attn_b_sample_kernel(lam_ref, q_ref, kc_ref, vc_ref, k_ref, v_ref, g_ref, o_ref):
    q = q_ref[0]
    kc = kc_ref[0].astype(BF16)
    n_past = vc_ref.shape[1] // N_HEADS_B
    vc = vc_ref[0, pl.ds(pl.program_id(1), n_past, stride=N_HEADS_B), :].astype(BF16)
    ko = k_ref[0]
    vo = v_ref[0]
    outs = []
    for a in range(2):
        sl = slice(a * HEAD_DIM, (a + 1) * HEAD_DIM)
        s_c = _dot_nt(q[:, sl], kc[:, sl])
        s_o = _dot_nt(q[:, sl], ko[:, sl])
        outs.append(_softmax_pv([s_c, s_o], [vc, vo], exp_fn=jnp.exp2))
    o = outs[0] - lam_ref[0] * outs[1]
    o_ref[0] = _subnorm(o, g_ref[...]).astype(BF16)


def _attn_b_sample(lam, bq, cache_k, cache_v, bk, bv, g_sub):
    b, l, _ = bq.shape
    p = cache_k.shape[1]
    hw = 2 * HEAD_DIM
    bs = lambda n: pl.BlockSpec((1, n, hw), lambda bb, h, lam_: (bb, 0, h))
    return pl.pallas_call(
        _attn_b_sample_kernel,
        grid_spec=pltpu.PrefetchScalarGridSpec(
            num_scalar_prefetch=1,
            grid=(b, N_HEADS_B),
            in_specs=[bs(l), bs(p),
                      pl.BlockSpec((1, p * N_HEADS_B, hw), lambda bb, h, lam_: (bb, 0, 0)),
                      bs(l), bs(l),
                      pl.BlockSpec((1, hw), lambda bb, h, lam_: (0, 0))],
            out_specs=bs(l)),
        out_shape=jax.ShapeDtypeStruct((b, l, B_WIDTH), BF16),
        compiler_params=_cparams(("arbitrary", "arbitrary")),
        name="attn_b_sample",
    )(lam, bq, cache_k, cache_v, bk, bv, g_sub)


def _memkv_kernel(m_ref, g_ref, wk_ref, wv_ref, kf_ref, vf_ref, kb_ref, vb_ref):
    mn = _rms(m_ref[...], g_ref[...]).astype(BF16)
    k = jnp.dot(mn, wk_ref[...], preferred_element_type=F32)
    v = jnp.dot(mn, wv_ref[...], preferred_element_type=F32)
    kf_ref[...] = k
    vf_ref[...] = v
    kb_ref[...] = k.astype(BF16)
    vb_ref[...] = v.astype(BF16)


def _memkv(mem2d, g, wk, wv, *, tm):
    r = mem2d.shape[0]
    row = lambda i: (i, 0)
    const = lambda i: (0, 0)
    blk = pl.BlockSpec((tm, D_MODEL), row)
    wspec = pl.BlockSpec((D_MODEL, D_MODEL), const)
    f = jax.ShapeDtypeStruct((r, D_MODEL), F32)
    bf = jax.ShapeDtypeStruct((r, D_MODEL), BF16)
    return pl.pallas_call(
        _memkv_kernel,
        grid=(r // tm,),
        in_specs=[blk, pl.BlockSpec((1, D_MODEL), const), wspec, wspec],
        out_specs=[blk] * 4,
        out_shape=[f, f, bf, bf],
        compiler_params=_cparams(("arbitrary",)),
        name="memkv",
    )(mem2d, g, wk, wv)


def _post_kernel(*refs, n_real):
    h_ref, xn_ref = refs[-2:]

    @pl.when(pl.program_id(0) < n_real)
    def _():
        _post_tile(*refs[:11], h_ref, xn_ref)

    @pl.when(pl.program_id(0) >= n_real)
    def _():
        h_ref[...] = jnp.zeros(h_ref.shape, h_ref.dtype)
        xn_ref[...] = jnp.zeros(xn_ref.shape, xn_ref.dtype)


def _post_tile(x_ref, oa_ref, ob_ref, woa_ref, wob_ref, gq_ref, wq_ref, mk_ref, mv_ref, wo_ref,
               gm_ref, h_ref, xn_ref):
    h1 = (x_ref[...] + jnp.dot(oa_ref[...], woa_ref[...], preferred_element_type=F32)
          + jnp.dot(ob_ref[...], wob_ref[...], preferred_element_type=F32))
    xq = _rms(h1, gq_ref[...]).astype(BF16)
    q = (jnp.dot(xq, wq_ref[...], preferred_element_type=F32) * HEAD_DIM_M ** -0.5).astype(BF16)
    outs = []
    for h in range(N_HEADS_M):
        sl = slice(h * HEAD_DIM_M, (h + 1) * HEAD_DIM_M)
        s = _dot_nt(q[:, sl], mk_ref[0, :, sl])
        outs.append(_softmax_pv([s], [mv_ref[0, :, sl]]).astype(BF16))
    o = jnp.concatenate(outs, axis=1)
    h2 = h1 + jnp.dot(o, wo_ref[...], preferred_element_type=F32)
    h_ref[...] = h2
    _store_row_slabs(xn_ref, _pack_bf16_pairs(_rms(h2, gm_ref[...])))


def _post(x2d, oa, ob, woa, wob, gq, wq, mk, mv, wo, gm, *, tm, tiles_per_batch, total_rows,
          row_offset=0, into=None):
    r = x2d.shape[0]
    n_real = r // tm
    n_steps = n_real if into is not None else total_rows // tm
    mlen = mk.shape[1]
    row = lambda i: (jnp.minimum(i, n_real - 1), 0)
    out_row = lambda i: (row_offset // tm + i, 0)
    const = lambda i: (0, 0)
    memb = lambda i: (jnp.minimum(i, n_real - 1) // tiles_per_batch, 0, 0)
    gspec = pl.BlockSpec((1, D_MODEL), const)
    wspec = pl.BlockSpec((D_MODEL, D_MODEL), const)
    hspec = pl.BlockSpec((A_WIDTH, D_MODEL), const)
    in_specs = [pl.BlockSpec((tm, D_MODEL), row),
                pl.BlockSpec((tm, A_WIDTH), row), pl.BlockSpec((tm, B_WIDTH), row),
                hspec, hspec, gspec, wspec,
                pl.BlockSpec((1, mlen, D_MODEL), memb), pl.BlockSpec((1, mlen, D_MODEL), memb),
                wspec, gspec]
    args = [x2d, oa, ob, woa, wob, gq, wq, mk, mv, wo, gm]
    aliases = {}
    if into is not None:
        aliases = {len(args): 0, len(args) + 1: 1}
        in_specs += [pl.BlockSpec(memory_space=pl.ANY)] * 2
        args += list(into)
    return pl.pallas_call(
        functools.partial(_post_kernel, n_real=n_real),
        grid=(n_steps,),
        in_specs=in_specs,
        out_specs=[pl.BlockSpec((tm, D_MODEL), out_row), pl.BlockSpec((tm * X_SLABS, LANES), out_row)],
        out_shape=[jax.ShapeDtypeStruct((total_rows, D_MODEL), F32),
                   jax.ShapeDtypeStruct((total_rows * X_SLABS, LANES), jnp.uint32)],
        input_output_aliases=aliases,
        compiler_params=_cparams(("arbitrary",)),
        name="post_attn",
    )(*args)


def _router_kernel(xn_ref, wr_ref, br_ref, tri_ref, idx_ref, gate_ref, rank_ref, cnt_ref, base_sc):
    @pl.when(pl.program_id(0) == 0)
    def _():
        base_sc[...] = jnp.zeros(base_sc.shape, F32)

    tm = xn_ref.shape[0] // X_SLABS
    xn = _unpack_bf16_pairs(_load_row_slabs(xn_ref, tm, X_SLABS))
    logits = _dot_nt(wr_ref[...], xn) + br_ref[...]
    iota_e = lax.broadcasted_iota(jnp.int32, (N_EXPERTS, tm), 0).astype(F32)
    work = logits
    vals, idxs, hots = [], [], []
    for _k in range(TOP_K):
        mx = work.max(axis=0, keepdims=True)
        ix = jnp.where(work == mx, iota_e, float(N_EXPERTS)).min(axis=0, keepdims=True)
        hot = iota_e == ix
        vals.append(mx)
        idxs.append(ix)
        hots.append(hot)
        work = jnp.where(hot, -jnp.inf, work)
    es = [jnp.exp(v - vals[0]) for v in vals]
    den = es[0] + es[1] + es[2] + es[3]
    cnt = (hots[0] | hots[1] | hots[2] | hots[3]).astype(F32)
    before = jnp.dot(cnt.astype(BF16), tri_ref[...], preferred_element_type=F32) + base_sc[...]
    for k in range(TOP_K):
        idx_ref[k:k + 1, :] = idxs[k].astype(jnp.int32)
        gate_ref[k:k + 1, :] = es[k] / den
        rank_ref[k:k + 1, :] = jnp.where(hots[k], before, 0.0).sum(axis=0, keepdims=True).astype(jnp.int32)
    base_sc[...] = base_sc[...] + cnt.sum(axis=1, keepdims=True)
    cnt_ref[...] = jnp.broadcast_to(base_sc[...], cnt_ref.shape).astype(jnp.int32)


def _router(xn2d, wr_t, br, tri, *, tm):
    r = xn2d.shape[0] // X_SLABS
    col = lambda i: (0, i)
    const = lambda i: (0, 0)
    return pl.pallas_call(
        _router_kernel,
        grid=(r // tm,),
        in_specs=[pl.BlockSpec((tm * X_SLABS, LANES), lambda i: (i, 0)),
                  pl.BlockSpec((N_EXPERTS, D_MODEL), const),
                  pl.BlockSpec((N_EXPERTS, 1), const),
                  pl.BlockSpec((tm, tm), const)],
        out_specs=[pl.BlockSpec((TOP_K, tm), col)] * 3 + [pl.BlockSpec((N_EXPERTS, LANES), const)],
        out_shape=[jax.ShapeDtypeStruct((TOP_K, r), jnp.int32),
                   jax.ShapeDtypeStruct((TOP_K, r), F32),
                   jax.ShapeDtypeStruct((TOP_K, r), jnp.int32),
                   jax.ShapeDtypeStruct((N_EXPERTS, LANES), jnp.int32)],
        scratch_shapes=[pltpu.VMEM((N_EXPERTS, 1), F32)],
        compiler_params=_cparams(("arbitrary",)),
        name="router",
    )(xn2d, wr_t, br, tri)


def _row_copy(src, dst, sem):
    return pltpu.make_async_copy(src, dst, sem)


def _slab(ref, row, n, lead=()):
    return ref.at[lead + (pl.ds(pl.multiple_of(row * n, n), n), slice(None))]


def _scatter_kernel(pos_ref, xn_ref, xs_ref, sem):
    tm = xn_ref.shape[0] // X_SLABS

    def start(t, c):
        for k in range(TOP_K):
            _row_copy(_slab(xn_ref, t, X_SLABS), _slab(xs_ref, pos_ref[0, k, t], X_SLABS),
                      sem).start(priority=k % 2)
        return c

    lax.fori_loop(0, tm, start, 0, unroll=2)

    def wait(t, c):
        for k in range(TOP_K):
            _row_copy(_slab(xn_ref, 0, X_SLABS), _slab(xs_ref, 0, X_SLABS), sem).wait()
        return c

    lax.fori_loop(0, tm, wait, 0, unroll=4)


def _scatter(pos, xn2d, *, tm):
    r = xn2d.shape[0] // X_SLABS
    return pl.pallas_call(
        _scatter_kernel,
        grid=(r // tm,),
        in_specs=[pl.BlockSpec((1, TOP_K, tm), lambda i: (i, 0, 0), memory_space=pltpu.SMEM),
                  pl.BlockSpec((tm * X_SLABS, LANES), lambda i: (i, 0))],
        out_specs=pl.BlockSpec(memory_space=pl.ANY),
        out_shape=jax.ShapeDtypeStruct((TOP_K * r * X_SLABS, LANES), jnp.uint32),
        scratch_shapes=[pltpu.SemaphoreType.DMA],
        compiler_params=_cparams(("arbitrary",)),
        name="moe_scatter",
    )(pos, xn2d)


def _experts_kernel(blk_ref, exp_ref, lo_ref, hi_ref, first_ref, newexp_ref,
                    x_ref, wgu_ref, bgu_ref, wd_ref, bd_ref, y_ref, wgu_bf, wd_bf):
    i = pl.program_id(0)

    @pl.when(newexp_ref[i] == 1)
    def _():
        wgu_bf[...] = wgu_ref[0].astype(BF16)
        wd_bf[...] = wd_ref[0].astype(BF16)

    x = _unpack_bf16_pairs(_load_row_slabs(x_ref, MOE_ROWS, X_SLABS))
    gu = jnp.dot(x, wgu_bf[...], preferred_element_type=F32) + bgu_ref[0]
    gate = jnp.minimum(gu[:, :EXPERT_FF], SWIGLU_LIMIT)
    up = jnp.clip(gu[:, EXPERT_FF:], -SWIGLU_LIMIT, SWIGLU_LIMIT)
    act = (up + 1.0) * (gate * jax.nn.sigmoid(SWIGLU_ALPHA * gate))
    y = jnp.dot(act.astype(BF16), wd_bf[...], preferred_element_type=F32) + bd_ref[0]
    row0 = blk_ref[i] * MOE_ROWS
    whole = (lo_ref[i] <= row0) & (hi_ref[i] >= row0 + MOE_ROWS)

    @pl.when(whole)
    def _():
        _store_row_slabs(y_ref, y)

    @pl.when(jnp.logical_not(whole))
    def _():
        @pl.when(first_ref[i] == 1)
        def _():
            y_ref[...] = jnp.zeros(y_ref.shape, F32)

        rows = row0 + lax.broadcasted_iota(jnp.int32, (MOE_ROWS, 1), 0)
        mine = (rows >= lo_ref[i]) & (rows < hi_ref[i])
        _store_row_slabs(y_ref, jnp.where(mine, y, _load_row_slabs(y_ref, MOE_ROWS, Y_SLABS)))


def _experts(items, xs, wgu, bgu, wd, bd):
    blk, exp, lo, hi, first = items
    newexp = jnp.concatenate([jnp.ones((1,), jnp.int32), (exp[1:] != exp[:-1]).astype(jnp.int32)])
    n_items = blk.shape[0]
    n_rows = xs.shape[0] // X_SLABS
    return pl.pallas_call(
        _experts_kernel,
        grid_spec=pltpu.PrefetchScalarGridSpec(
            num_scalar_prefetch=6,
            grid=(n_items,),
            in_specs=[pl.BlockSpec((MOE_ROWS * X_SLABS, LANES), lambda i, b, e, *_: (b[i], 0)),
                      pl.BlockSpec((1, D_MODEL, 2 * EXPERT_FF), lambda i, b, e, *_: (e[i], 0, 0)),
                      pl.BlockSpec((1, 1, 2 * EXPERT_FF), lambda i, b, e, *_: (e[i], 0, 0)),
                      pl.BlockSpec((1, EXPERT_FF, D_MODEL), lambda i, b, e, *_: (e[i], 0, 0)),
                      pl.BlockSpec((1, 1, D_MODEL), lambda i, b, e, *_: (e[i], 0, 0))],
            out_specs=pl.BlockSpec((MOE_ROWS * Y_SLABS, LANES), lambda i, b, e, *_: (b[i], 0)),
            scratch_shapes=[pltpu.VMEM((D_MODEL, 2 * EXPERT_FF), BF16), pltpu.VMEM((EXPERT_FF, D_MODEL), BF16)]),
        out_shape=jax.ShapeDtypeStruct((n_rows * Y_SLABS, LANES), F32),
        compiler_params=pltpu.CompilerParams(dimension_semantics=("arbitrary",),
                                             vmem_limit_bytes=EXPERTS_VMEM_LIMIT),
        name="moe_experts",
    )(blk, exp, lo, hi, first, newexp, xs, wgu, bgu, wd, bd)


def _work_items(counts, n_rows):
    n_blocks = n_rows // MOE_ROWS
    n_items = n_blocks + N_EXPERTS - 1
    ends = jnp.cumsum(counts)
    starts = ends - counts
    nb = jnp.where(counts > 0, (ends - 1) // MOE_ROWS - starts // MOE_ROWS + 1, 0)
    item_end = jnp.cumsum(nb)
    item_start = item_end - nb
    total = item_end[-1]
    i = jnp.arange(n_items, dtype=jnp.int32)
    e = jnp.minimum(jnp.sum(item_end[None, :] <= i[:, None], axis=1), N_EXPERTS - 1).astype(jnp.int32)
    blk = (starts[e] // MOE_ROWS + (i - item_start[e])).astype(jnp.int32)
    valid = i < total
    lo = jnp.maximum(starts[e], blk * MOE_ROWS)
    hi = jnp.minimum(ends[e], (blk + 1) * MOE_ROWS)
    last = jnp.maximum(total - 1, 0)
    blk = jnp.where(valid, blk, blk[last])
    e = jnp.where(valid, e, e[last])
    lo = jnp.where(valid, lo, 0).astype(jnp.int32)
    hi = jnp.where(valid, hi, 0).astype(jnp.int32)
    prev = jnp.concatenate([jnp.full((1,), -1, jnp.int32), blk[:-1]])
    first = (blk != prev).astype(jnp.int32)
    return blk, e, lo, hi, first, starts


def _combine_kernel(pos_ref, pos_next_ref, h_ref, gate_ref, g_ref, ys_ref, o_first_ref, o_rest_ref, buf, sem,
                    *, n_first):
    tm = h_ref.shape[0]
    i = pl.program_id(0)
    n = pl.num_programs(0)
    cur = i % 2

    def fetch(p_ref, slot):
        def start(t, c):
            for k in range(TOP_K):
                _row_copy(_slab(ys_ref, p_ref[0, k, t], Y_SLABS), _slab(buf, t, Y_SLABS, (slot, k)),
                          sem.at[slot]).start(priority=k % 2)
            return c

        lax.fori_loop(0, tm, start, 0, unroll=2)

    @pl.when(i == 0)
    def _():
        fetch(pos_ref, 0)

    for slot in range(2):
        @pl.when((i + 1 < n) & (cur != slot))
        def _():
            fetch(pos_next_ref, slot)

    def wait(t, c):
        for k in range(TOP_K):
            _row_copy(_slab(ys_ref, 0, Y_SLABS), _slab(buf, 0, Y_SLABS, (cur, 0)), sem.at[cur]).wait()
        return c

    lax.fori_loop(0, tm, wait, 0, unroll=4)
    acc = h_ref[...]
    for k in range(TOP_K):
        acc = acc + gate_ref[:, k:k + 1] * _load_row_slabs(buf, tm, Y_SLABS, (cur, k))
    res = _rms(acc, g_ref[...])

    @pl.when(i < n_first)
    def _():
        o_first_ref[...] = res

    @pl.when(i >= n_first)
    def _():
        o_rest_ref[...] = res


def _combine(pos, h2d, gates_t, g_final, ys, *, tm, rows_first):
    r = h2d.shape[0]
    n_tiles = r // tm
    n_first = rows_first // tm
    row = lambda i: (i, 0)
    return pl.pallas_call(
        functools.partial(_combine_kernel, n_first=n_first),
        grid=(n_tiles,),
        in_specs=[pl.BlockSpec((1, TOP_K, tm), lambda i: (i, 0, 0), memory_space=pltpu.SMEM),
                  pl.BlockSpec((1, TOP_K, tm), lambda i: (jnp.minimum(i + 1, n_tiles - 1), 0, 0),
                               memory_space=pltpu.SMEM),
                  pl.BlockSpec((tm, D_MODEL), row),
                  pl.BlockSpec((tm, TOP_K), row),
                  pl.BlockSpec((1, D_MODEL), lambda i: (0, 0)),
                  pl.BlockSpec(memory_space=pl.ANY)],
        out_specs=[pl.BlockSpec((tm, D_MODEL), lambda i: (jnp.minimum(i, n_first - 1), 0)),
                   pl.BlockSpec((tm, D_MODEL), lambda i: (jnp.maximum(i - n_first, 0), 0))],
        out_shape=[jax.ShapeDtypeStruct((rows_first, D_MODEL), F32),
                   jax.ShapeDtypeStruct((r - rows_first, D_MODEL), F32)],
        scratch_shapes=[pltpu.VMEM((2, TOP_K, tm * Y_SLABS, LANES), F32), pltpu.SemaphoreType.DMA((2,))],
        compiler_params=_cparams(("arbitrary",)),
        name="moe_combine",
    )(pos, pos, h2d, gates_t, g_final, ys)


def _moe(h2d, xn2d, w, rows_first):
    r = h2d.shape[0]
    tm_router = _pick_tile(r, 512)
    tm_scatter = _pick_tile(r, 512)
    tm_combine = _pick_tile(r, 256)
    idx, gates, rank, cnt = _router(xn2d, w["wr_t"], w["br"], w["tri"][:tm_router, :tm_router], tm=tm_router)
    counts = cnt[:, 0]
    blk, e, lo, hi, first, starts = _work_items(counts, TOP_K * r)
    hot = idx[:, :, None] == jnp.arange(N_EXPERTS, dtype=jnp.int32)
    pos = jnp.sum(jnp.where(hot, starts.astype(jnp.int32), 0), axis=-1) + rank
    tiles = lambda tm: pos.reshape(TOP_K, -1, tm).transpose(1, 0, 2)
    xs = _scatter(tiles(tm_scatter), xn2d, tm=tm_scatter)
    ys = _experts((blk, e, lo, hi, first), xs, w["wgu"], w["bgu"], w["wd"], w["bd"])
    return _combine(tiles(tm_combine), h2d, gates.T, w["g_final"], ys, tm=tm_combine, rows_first=rows_first)


def _pick_tile(n, cap):
    t = cap
    while n % t:
        t //= 2
    return t


def kernel(x_prompt, x_sample, mem_prompt, cache_a_k, cache_a_v, cache_b_k, cache_b_v, cache_mem_k, cache_mem_v, g_mix, w_in, rel_bias, lam_q1, lam_k1, lam_q2, lam_k2, g_subln, w_out, g_xq, g_xmem, w_xq, w_xk, w_xv, w_xo, g_moe, w_router, b_router, w_gate_up, b_gate_up, w_down, b_down, g_final):
    assert g_mix.shape[0] == 1, "single-layer kernel"
    b, s, d = x_prompt.shape
    db, dl, _ = x_sample.shape
    past = cache_b_k.shape[2]
    a_len = cache_a_k.shape[2]
    mlen = mem_prompt.shape[1]
    keep_a = min(BAND, s)
    assert d == D_MODEL and s % B_QBLK == 0 and s >= A_WIN and dl == CHUNK and past % CHUNK == 0
    assert B_QBLK == 2 * B_KBLK

    row1 = lambda v: v.reshape(1, -1).astype(F32)
    w_in_bf = w_in[0].astype(BF16)
    w_out_bf = w_out[0].astype(BF16)
    weights = dict(
        wr_t=w_router[0].T.astype(BF16), br=b_router[0].reshape(N_EXPERTS, 1).astype(F32),
        tri=jnp.asarray(np.triu(np.ones((512, 512), np.float32), 1), BF16),
        wgu=w_gate_up[0].astype(F32), bgu=b_gate_up[0].reshape(N_EXPERTS, 1, -1).astype(F32),
        wd=w_down[0].astype(F32), bd=b_down[0].reshape(N_EXPERTS, 1, -1).astype(F32),
        g_final=row1(g_final))
    lam = (jnp.exp(jnp.sum(lam_q1[0].astype(F32) * lam_k1[0].astype(F32)))
           - jnp.exp(jnp.sum(lam_q2[0].astype(F32) * lam_k2[0].astype(F32))) + LAM_INIT).reshape(1)
    table = _band_table(rel_bias[0])
    g_sub = row1(g_subln[0])

    tm_p = _pick_tile(s, 512)
    tabs_p = _rope_tables(jnp.arange(s, dtype=jnp.int32))
    p_in = _inproj(x_prompt.reshape(b * s, d), row1(g_mix[0]), w_in_bf, *tabs_p,
                   tm=tm_p, n_pos_blocks=s // tm_p, tail_div=s // tm_p if keep_a == tm_p else 1, emit_vt=True)
    tabs_s = _rope_tables(jnp.tile(past + jnp.arange(dl, dtype=jnp.int32), db))
    s_in = _inproj(x_sample.reshape(db * dl, d), row1(g_mix[0]), w_in_bf, *tabs_s,
                   tm=db * dl, n_pos_blocks=1, tail_div=1, emit_vt=False)

    def split(outs, nb, nl):
        return [o.reshape(nb, nl, A_WIDTH) for o in outs[:6]]

    aq, ak, _, bq, bk, _ = split(p_in, b, s)
    avt = p_in[10].reshape((b, s // A_QBLK) + p_in[10].shape[1:])
    bvt = p_in[11].reshape((b, s // B_KBLK) + p_in[11].shape[1:])
    saq, sak, sav, sbq, sbk, sbv = split(s_in, db, dl)

    oa = _attn_a_prompt(aq, ak, avt, table.transpose(0, 2, 1))
    ob = _attn_b_prompt(lam, bq, bk, bvt, g_sub)
    soa = _attn_a_sample(saq, cache_a_k[0].reshape(db, a_len, A_WIDTH), cache_a_v[0].reshape(db, a_len, A_WIDTH),
                         sak, sav, table[:, :dl, :BAND + dl])
    sob = _attn_b_sample(lam, sbq, cache_b_k[0].reshape(db, past, B_WIDTH),
                         cache_b_v[0].reshape(db, past * N_HEADS_B, 2 * HEAD_DIM), sbk, sbv, g_sub)

    mkf, mvf, mkb, mvb = _memkv(mem_prompt.reshape(b * mlen, d), row1(g_xmem[0]),
                                w_xk[0].astype(BF16), w_xv[0].astype(BF16), tm=mlen)
    post_w = (w_out_bf[:A_WIDTH], w_out_bf[A_WIDTH:], row1(g_xq[0]), w_xq[0].astype(BF16))
    tm_post = _pick_tile(math.gcd(s, b * s + db * dl), 256)
    rows_p, rows_s = b * s, db * dl
    h_xn = _post(x_prompt.reshape(rows_p, d), oa.reshape(rows_p, A_WIDTH), ob.reshape(rows_p, B_WIDTH),
                 *post_w, mkb.reshape(b, mlen, d), mvb.reshape(b, mlen, d),
                 w_xo[0].astype(BF16), row1(g_moe[0]), tm=tm_post, tiles_per_batch=s // tm_post,
                 total_rows=rows_p + rows_s)
    h_all, xn_all = _post(x_sample.reshape(rows_s, d), soa.reshape(rows_s, A_WIDTH), sob.reshape(rows_s, B_WIDTH),
                          *post_w, cache_mem_k[0].reshape(db, mlen, d).astype(BF16),
                          cache_mem_v[0].reshape(db, mlen, d).astype(BF16),
                          w_xo[0].astype(BF16), row1(g_moe[0]), tm=dl, tiles_per_batch=1,
                          total_rows=rows_p + rows_s, row_offset=rows_p, into=h_xn)

    y_prompt, y_sample = _moe(h_all, xn_all, weights, rows_p)
    y_prompt = y_prompt.reshape(b, s, d)
    y_sample = y_sample.reshape(db, dl, d)

    def heads(x2d, nb, nl, nh):
        return x2d.reshape(1, nb, nl, nh, -1)

    akf, avf, bkf, bvf = p_in[6:10]
    if keep_a != tm_p:
        akf = akf.reshape(b, s, A_WIDTH)[:, s - keep_a:]
        avf = avf.reshape(b, s, A_WIDTH)[:, s - keep_a:]
    sakf, savf, sbkf, sbvf = s_in[6:]
    return (y_prompt, y_sample,
            heads(akf, b, keep_a, N_HEADS_A), heads(avf, b, keep_a, N_HEADS_A),
            heads(bkf, b, s, 2 * N_HEADS_B), heads(bvf, b, s, N_HEADS_B),
            heads(mkf, b, mlen, N_HEADS_M), heads(mvf, b, mlen, N_HEADS_M),
            heads(sakf, db, dl, N_HEADS_A), heads(savf, db, dl, N_HEADS_A),
            heads(sbkf, db, dl, 2 * N_HEADS_B), heads(sbvf, db, dl, N_HEADS_B))
```

```python
import functools
import math

import numpy as np
import jax
import jax.numpy as jnp
from jax import lax
from jax.experimental import pallas as pl
from jax.experimental.pallas import tpu as pltpu

F32 = jnp.float32
BF16 = jnp.bfloat16

D_MODEL = 1024
CHUNK = 64
LEFT_CHUNKS = 8
BAND = LEFT_CHUNKS * CHUNK
HEAD_DIM = 64
A_WIDTH = 512
B_WIDTH = 512
N_HEADS_A = 8
N_HEADS_B = 4
REL_CLIP = 128
ROT_DIM = 16
ROPE_THETA = 500000.0
N_HEADS_M = 4
HEAD_DIM_M = 256
N_EXPERTS = 32
TOP_K = 4
EXPERT_FF = 1024
SWIGLU_LIMIT = 7.0
SWIGLU_ALPHA = 1.702
RMS_EPS = 1e-5
NEG_INF = -1e30
LAM_INIT = 0.8 - 0.6 * math.exp(-0.3 * 0)
LOG2E = math.log2(math.e)

LANES = 128
A_QBLK = 4 * CHUNK
A_WIN = BAND + A_QBLK
B_QBLK = 512
B_KBLK = 256
VT_ONES = 16
MOE_ROWS = 256
X_SLABS = D_MODEL // 2 // LANES
Y_SLABS = D_MODEL // LANES
VMEM_LIMIT = 48 * 1024 * 1024
EXPERTS_VMEM_LIMIT = 56 * 1024 * 1024


def _cparams(sem):
    return pltpu.CompilerParams(dimension_semantics=sem, vmem_limit_bytes=VMEM_LIMIT)


def _rms(x, g):
    return x * lax.rsqrt(jnp.mean(x * x, axis=-1, keepdims=True) + RMS_EPS) * g


HI16 = 0xFFFF0000


def _pack_bf16_pairs(x):
    n = x.shape[1] // 2
    bits = lambda v: lax.bitcast_convert_type(v.astype(BF16).astype(F32), jnp.uint32)
    return (bits(x[:, :n]) >> 16) | (bits(x[:, n:]) & jnp.uint32(HI16))


def _unpack_bf16_pairs(w):
    lo = lax.bitcast_convert_type(w << 16, F32)
    hi = lax.bitcast_convert_type(w & jnp.uint32(HI16), F32)
    return jnp.concatenate([lo, hi], axis=1).astype(BF16)


def _store_row_slabs(ref, x, lead=()):
    m, n = x.shape[0], x.shape[1] // LANES
    for c in range(n):
        ref[lead + (pl.ds(c, m, stride=n), slice(None))] = x[:, c * LANES:(c + 1) * LANES]


def _load_row_slabs(ref, m, n, lead=()):
    return jnp.concatenate([ref[lead + (pl.ds(c, m, stride=n), slice(None))] for c in range(n)], axis=1)


def _dot_nt(a, b):
    return lax.dot_general(a, b, (((1,), (1,)), ((), ())), preferred_element_type=F32)


def _inproj_kernel(x_ref, g_ref, w_ref, cos_ref, sa_ref, sb_ref,
                   aq_ref, ak_ref, av_ref, bq_ref, bk_ref, bv_ref,
                   akf_ref, avf_ref, bkf_ref, bvf_ref, *maybe_vt_refs):
    xn = _rms(x_ref[...], g_ref[...]).astype(BF16)

    def seg(s):
        return jnp.dot(xn, w_ref[:, s * A_WIDTH:(s + 1) * A_WIDTH], preferred_element_type=F32)

    def rope(z):
        cols = []
        for c in range(z.shape[1] // LANES):
            zc = z[:, c * LANES:(c + 1) * LANES]
            cols.append(zc * cos_ref[...] + pltpu.roll(zc, 8, 1) * sa_ref[...]
                        + pltpu.roll(zc, LANES - 8, 1) * sb_ref[...])
        return jnp.concatenate(cols, axis=1)

    scale = HEAD_DIM ** -0.5
    aq_ref[...] = (seg(0) * (scale * LOG2E)).astype(BF16)
    ak = seg(1)
    ak_ref[...] = ak.astype(BF16)
    akf_ref[...] = ak
    av = seg(2)
    av_ref[...] = av.astype(BF16)
    avf_ref[...] = av
    bq_ref[...] = (rope(seg(3)) * (scale * LOG2E)).astype(BF16)
    bk = rope(seg(4))
    bk_ref[...] = bk.astype(BF16)
    bkf_ref[...] = bk
    bv = seg(5)
    bv_ref[...] = bv.astype(BF16)
    _store_row_slabs(bvf_ref, bv)
    for vt_ref, v in zip(maybe_vt_refs, (av, bv)):
        vt = v.T.astype(BF16)
        blk = vt_ref.shape[3]
        for c in range(vt_ref.shape[0]):
            for g in range(vt_ref.shape[1]):
                vt_ref[c, g, 0:LANES, :] = vt[g * LANES:(g + 1) * LANES, c * blk:(c + 1) * blk]
                vt_ref[c, g, LANES:LANES + VT_ONES, :] = jnp.ones((VT_ONES, blk), BF16)


def _inproj(x2d, g, w_bf, cos_t, sa_t, sb_t, *, tm, n_pos_blocks, tail_div, emit_vt):
    r = x2d.shape[0]
    n_tiles = r // tm
    row = lambda i: (i, 0)
    const = lambda i: (0, 0)
    pos = lambda i: (i % n_pos_blocks, 0)
    tail = lambda i: (i // tail_div, 0)
    bf = jax.ShapeDtypeStruct((r, A_WIDTH), BF16)
    f_all = jax.ShapeDtypeStruct((r, A_WIDTH), F32)
    f_tail = jax.ShapeDtypeStruct((r // tail_div, A_WIDTH), F32)
    blk = lambda im: pl.BlockSpec((tm, A_WIDTH), im)
    out_specs = [blk(row)] * 6 + [blk(tail), blk(tail), blk(row),
                                  pl.BlockSpec((tm * N_HEADS_B, 2 * HEAD_DIM), row)]
    out_shape = [bf] * 6 + [f_tail, f_tail, f_all,
                            jax.ShapeDtypeStruct((r * N_HEADS_B, 2 * HEAD_DIM), F32)]
    if emit_vt:
        for kb in (A_QBLK, B_KBLK):
            grp = (A_WIDTH // LANES, LANES + VT_ONES, kb)
            out_specs.append(pl.BlockSpec((tm // kb,) + grp, lambda i: (i, 0, 0, 0)))
            out_shape.append(jax.ShapeDtypeStruct((r // kb,) + grp, BF16))
    return pl.pallas_call(
        _inproj_kernel,
        grid=(n_tiles,),
        in_specs=[pl.BlockSpec((tm, D_MODEL), row), pl.BlockSpec((1, D_MODEL), const),
                  pl.BlockSpec((D_MODEL, 3 * D_MODEL), const),
                  pl.BlockSpec((tm, LANES), pos), pl.BlockSpec((tm, LANES), pos),
                  pl.BlockSpec((tm, LANES), pos)],
        out_specs=out_specs,
        out_shape=out_shape,
        compiler_params=_cparams(("arbitrary",)),
        name="inproj",
    )(x2d, g, w_bf, cos_t, sa_t, sb_t)


def _rope_tables(positions):
    half = ROT_DIM // 2
    lane = np.arange(LANES) % HEAD_DIM
    inv_freq = jnp.power(ROPE_THETA, -jnp.arange(half, dtype=F32) / half)
    ang = positions.astype(F32)[:, None] * inv_freq[None, :]
    cos = jnp.cos(ang)[:, lane % half]
    sin = jnp.sin(ang)[:, lane % half]
    in_rot = jnp.asarray(lane < ROT_DIM)[None, :]
    first = jnp.asarray(lane < half)[None, :]
    cos_t = jnp.where(in_rot, cos, 1.0)
    sa_t = jnp.where(in_rot & ~first, sin, 0.0)
    sb_t = jnp.where(first, -sin, 0.0)
    return cos_t.astype(F32), sa_t.astype(F32), sb_t.astype(F32)


def _band_table(rel_bias):
    r = np.arange(A_QBLK)[:, None]
    j = np.arange(BAND + A_WIN)[None, :]
    dchunk = LEFT_CHUNKS + r // CHUNK - j // CHUNK
    allowed = (dchunk >= 0) & (dchunk <= LEFT_CHUNKS)
    bias = rel_bias.astype(F32) * LOG2E
    n_seq = BAND + A_WIN + A_QBLK - 1
    n_hi = BAND + A_QBLK - 1 - REL_CLIP
    seq = jnp.concatenate([jnp.broadcast_to(bias[:, -1:], (N_HEADS_A, n_hi)), bias[:, ::-1],
                           jnp.broadcast_to(bias[:, :1], (N_HEADS_A, n_seq - n_hi - bias.shape[1]))], axis=1)
    seq = jnp.roll(seq, -(A_QBLK - 1), axis=1)
    toe = jnp.tile(seq, (1, A_QBLK))[:, :A_QBLK * (n_seq - 1)].reshape(N_HEADS_A, A_QBLK, n_seq - 1)
    return jnp.where(jnp.asarray(allowed)[None], toe[:, :, :BAND + A_WIN], NEG_INF)


def _softmax_pv(s_list, v_list, exp_fn=jnp.exp):
    m = s_list[0].max(axis=-1, keepdims=True)
    for s in s_list[1:]:
        m = jnp.maximum(m, s.max(axis=-1, keepdims=True))
    l = None
    o = None
    for s, v in zip(s_list, v_list):
        p = exp_fn(s - m)
        ls = p.sum(axis=-1, keepdims=True)
        os_ = jnp.dot(p.astype(BF16), v, preferred_element_type=F32)
        l = ls if l is None else l + ls
        o = os_ if o is None else o + os_
    return o / l


def _attn_a_prompt_kernel(q_ref, k_ref, vt_ref, tab_ref, o_ref):
    i = pl.program_id(1)
    n_left = BAND // A_QBLK
    n_sub = A_WIN // A_QBLK
    blk0 = jnp.maximum(i - n_left, 0)
    off = pl.multiple_of(jnp.maximum(n_left - i, 0) * A_QBLK, A_QBLK)
    q = q_ref[0]
    lane = lax.broadcasted_iota(jnp.int32, (A_QBLK, LANES), 1)

    def scores(h):
        cols = slice(h // 2 * LANES, (h // 2 + 1) * LANES)
        qp = q[:, cols]
        qh = jnp.where((lane >= HEAD_DIM) == bool(h % 2), qp, jnp.zeros_like(qp))
        return [_dot_nt(k_ref[0, pl.ds(pl.multiple_of((blk0 + c) * A_QBLK, A_QBLK), A_QBLK), cols], qh)
                + tab_ref[h, pl.ds(off + c * A_QBLK, A_QBLK), :] for c in range(n_sub)]

    def attend(h, ss):
        m = _col_reduce(ss[0], jnp.maximum, jnp.max)
        for s in ss[1:]:
            m = jnp.maximum(m, _col_reduce(s, jnp.maximum, jnp.max))
        o_t = None
        for c in range(n_sub):
            p = jnp.exp2(ss[c] - m).astype(BF16)
            os_ = jnp.dot(vt_ref[0, blk0 + c, h // 2], p, preferred_element_type=F32)
            o_t = os_ if o_t is None else o_t + os_
        return o_t[h % 2 * HEAD_DIM:(h % 2 + 1) * HEAD_DIM] / o_t[LANES:LANES + 1]

    halves = []
    ss = scores(0)
    for h in range(N_HEADS_A):
        ss_next = scores(h + 1) if h + 1 < N_HEADS_A else None
        halves.append(attend(h, ss))
        ss = ss_next
    pair_outs = [jnp.concatenate(halves[2 * g:2 * g + 2], axis=0).T for g in range(N_HEADS_A // 2)]
    o_ref[0] = jnp.concatenate(pair_outs, axis=1).astype(BF16)


def _attn_a_prompt(aq, ak, avt, table_t):
    b, l, _ = aq.shape
    n_kb = avt.shape[1]
    return pl.pallas_call(
        _attn_a_prompt_kernel,
        grid=(b, l // A_QBLK),
        in_specs=[pl.BlockSpec((1, A_QBLK, A_WIDTH), lambda bb, i: (bb, i, 0)),
                  pl.BlockSpec((1, l, A_WIDTH), lambda bb, i: (bb, 0, 0)),
                  pl.BlockSpec((1,) + avt.shape[1:], lambda bb, i: (bb, 0, 0, 0, 0)),
                  pl.BlockSpec(table_t.shape, lambda bb, i: (0, 0, 0))],
        out_specs=pl.BlockSpec((1, A_QBLK, A_WIDTH), lambda bb, i: (bb, i, 0)),
        out_shape=jax.ShapeDtypeStruct((b, l, A_WIDTH), BF16),
        compiler_params=_cparams(("arbitrary", "arbitrary")),
        name="attn_a_prompt",
    )(aq, ak, avt, table_t)


def _attn_a_sample_kernel(q_ref, kc_ref, vc_ref, k_ref, v_ref, tab_ref, o_ref):
    q = q_ref[0]
    kc = kc_ref[0].astype(BF16)
    vc = vc_ref[0].astype(BF16)
    ko = k_ref[0]
    vo = v_ref[0]
    a_len = kc.shape[0]
    lq = q.shape[0]
    c0 = BAND - a_len
    outs = []
    for h in range(N_HEADS_A):
        sl = slice(h * HEAD_DIM, (h + 1) * HEAD_DIM)
        s_c = _dot_nt(q[:, sl], kc[:, sl]) + tab_ref[h, :, c0:BAND]
        s_o = _dot_nt(q[:, sl], ko[:, sl]) + tab_ref[h, :, BAND:BAND + lq]
        outs.append(_softmax_pv([s_c, s_o], [vc[:, sl], vo[:, sl]], exp_fn=jnp.exp2))
    o_ref[0] = jnp.concatenate(outs, axis=1).astype(BF16)


def _attn_a_sample(aq, cache_k, cache_v, ak, av, table):
    b, l, _ = aq.shape
    a_len = cache_k.shape[1]
    bs = lambda n: pl.BlockSpec((1, n, A_WIDTH), lambda bb: (bb, 0, 0))
    return pl.pallas_call(
        _attn_a_sample_kernel,
        grid=(b,),
        in_specs=[bs(l), bs(a_len), bs(a_len), bs(l), bs(l),
                  pl.BlockSpec(table.shape, lambda bb: (0, 0, 0))],
        out_specs=bs(l),
        out_shape=jax.ShapeDtypeStruct((b, l, A_WIDTH), BF16),
        compiler_params=_cparams(("arbitrary",)),
        name="attn_a_sample",
    )(aq, cache_k, cache_v, ak, av, table)


def _subnorm(o, g):
    return _rms(o, g) * (1.0 - LAM_INIT)


def _col_reduce(x, op, reduce_fn):
    n = x.shape[0]
    while n > 8:
        n //= 2
        x = op(x[:n], x[n:])
    return reduce_fn(x, axis=0, keepdims=True)


def _attn_b_prompt_kernel(lam_ref, q_ref, k_ref, vt_ref, mask_ref, g_ref, o_ref,
                          sa_sc, sb_sc, m_sc, acc_sc):
    i = pl.program_id(2)
    tk = vt_ref.shape[4]
    q = q_ref[0]
    lane = lax.broadcasted_iota(jnp.int32, q.shape, 1)
    zero = jnp.zeros_like(q)
    qs = (jnp.where(lane < HEAD_DIM, q, zero), jnp.where(lane >= HEAD_DIM, q, zero))
    m_sc[...] = jnp.full(m_sc.shape, NEG_INF, F32)
    acc_sc[...] = jnp.zeros(acc_sc.shape, F32)

    def scores(j, s_sc):
        kb = k_ref[0, pl.ds(pl.multiple_of(j * tk, tk), tk), :]
        for a in range(2):
            s_sc[a] = _dot_nt(kb, qs[a])

    def consume(j, s_sc, mask):
        vt = vt_ref[0, j, 0]
        for a in range(2):
            s = s_sc[a]
            if mask is not None:
                s = s + mask
            m_old = m_sc[a]
            m_new = jnp.maximum(m_old, _col_reduce(s, jnp.maximum, jnp.max))
            alpha = jnp.exp2(m_old - m_new)
            p = jnp.exp2(s - m_new).astype(BF16)
            m_sc[a] = m_new
            acc_sc[a] = alpha * acc_sc[a] + jnp.dot(vt, p, preferred_element_type=F32)

    scores(0, sa_sc)

    def body(jj, c):
        j = 2 * jj
        scores(j + 1, sb_sc)
        consume(j, sa_sc, None)
        scores(j + 2, sa_sc)
        consume(j + 1, sb_sc, None)
        return c

    lax.fori_loop(0, i, body, 0)
    scores(2 * i + 1, sb_sc)
    consume(2 * i, sa_sc, mask_ref[0])
    consume(2 * i + 1, sb_sc, mask_ref[1])
    o_t = (acc_sc[0, 0:LANES] / acc_sc[0, LANES:LANES + 1]
           - lam_ref[0] * (acc_sc[1, 0:LANES] / acc_sc[1, LANES:LANES + 1]))
    inv = lax.rsqrt(jnp.mean(o_t * o_t, axis=0, keepdims=True) + RMS_EPS)
    o_ref[0, 0] = (o_t * inv * (g_ref[...] * (1.0 - LAM_INIT))).astype(BF16)


def _chunk_causal_masks_t(tk, tq):
    kc = (np.arange(tq // tk)[:, None, None] * tk + np.arange(tk)[None, :, None]) // CHUNK
    qc = np.arange(tq)[None, None, :] // CHUNK
    return jnp.asarray(np.where(kc <= qc, 0.0, NEG_INF), F32)


def _attn_b_prompt(lam, bq, bk, bvt, g_sub):
    b, l, _ = bq.shape
    tq = B_QBLK
    tk = B_KBLK
    hw = 2 * HEAD_DIM
    n_kb = bvt.shape[1]
    return pl.pallas_call(
        _attn_b_prompt_kernel,
        grid_spec=pltpu.PrefetchScalarGridSpec(
            num_scalar_prefetch=1,
            grid=(b, N_HEADS_B, l // tq),
            in_specs=[pl.BlockSpec((1, tq, hw), lambda bb, h, i, lam_: (bb, i, h)),
                      pl.BlockSpec((1, l, hw), lambda bb, h, i, lam_: (bb, 0, h)),
                      pl.BlockSpec((1, n_kb, 1, hw + VT_ONES, tk), lambda bb, h, i, lam_: (bb, 0, h, 0, 0)),
                      pl.BlockSpec((tq // tk, tk, tq), lambda bb, h, i, lam_: (0, 0, 0)),
                      pl.BlockSpec((hw, 1), lambda bb, h, i, lam_: (0, 0))],
            out_specs=pl.BlockSpec((1, 1, hw, tq), lambda bb, h, i, lam_: (bb, h, 0, i)),
            scratch_shapes=[pltpu.VMEM((2, tk, tq), F32), pltpu.VMEM((2, tk, tq), F32),
                            pltpu.VMEM((2, 1, tq), F32), pltpu.VMEM((2, hw + VT_ONES, tq), F32)]),
        out_shape=jax.ShapeDtypeStruct((b, N_HEADS_B, hw, l), BF16),
        compiler_params=_cparams(("arbitrary", "arbitrary", "arbitrary")),
        name="attn_b_prompt",
    )(lam, bq, bk, bvt, _chunk_causal_masks_t(tk, tq), g_sub.reshape(hw, 1))


def _attn_b_sample_kernel(lam_ref, q_ref, kc_ref, vc_ref, k_ref, v_ref, g_ref, o_ref):
    q = q_ref[0]
    kc = kc_ref[0].astype(BF16)
    n_past = vc_ref.shape[1] // N_HEADS_B
    vc = vc_ref[0, pl.ds(pl.program_id(1), n_past, stride=N_HEADS_B), :].astype(BF16)
    ko = k_ref[0]
    vo = v_ref[0]
    outs = []
    for a in range(2):
        sl = slice(a * HEAD_DIM, (a + 1) * HEAD_DIM)
        s_c = _dot_nt(q[:, sl], kc[:, sl])
        s_o = _dot_nt(q[:, sl], ko[:, sl])
        outs.append(_softmax_pv([s_c, s_o], [vc, vo], exp_fn=jnp.exp2))
    o = outs[0] - lam_ref[0] * outs[1]
    o_ref[0] = _subnorm(o, g_ref[...]).astype(BF16)


def _attn_b_sample(lam, bq, cache_k, cache_v, bk, bv, g_sub):
    b, l, _ = bq.shape
    p = cache_k.shape[1]
    hw = 2 * HEAD_DIM
    bs = lambda n: pl.BlockSpec((1, n, hw), lambda bb, h, lam_: (bb, 0, h))
    return pl.pallas_call(
        _attn_b_sample_kernel,
        grid_spec=pltpu.PrefetchScalarGridSpec(
            num_scalar_prefetch=1,
            grid=(b, N_HEADS_B),
            in_specs=[bs(l), bs(p),
                      pl.BlockSpec((1, p * N_HEADS_B, hw), lambda bb, h, lam_: (bb, 0, 0)),
                      bs(l), bs(l),
                      pl.BlockSpec((1, hw), lambda bb, h, lam_: (0, 0))],
            out_specs=bs(l)),
        out_shape=jax.ShapeDtypeStruct((b, l, B_WIDTH), BF16),
        compiler_params=_cparams(("arbitrary", "arbitrary")),
        name="attn_b_sample",
    )(lam, bq, cache_k, cache_v, bk, bv, g_sub)


def _memkv_kernel(m_ref, g_ref, wk_ref, wv_ref, kf_ref, vf_ref, kb_ref, vb_ref):
    mn = _rms(m_ref[...], g_ref[...]).astype(BF16)
    k = jnp.dot(mn, wk_ref[...], preferred_element_type=F32)
    v = jnp.dot(mn, wv_ref[...], preferred_element_type=F32)
    kf_ref[...] = k
    vf_ref[...] = v
    kb_ref[...] = k.astype(BF16)
    vb_ref[...] = v.astype(BF16)


def _memkv(mem2d, g, wk, wv, *, tm):
    r = mem2d.shape[0]
    row = lambda i: (i, 0)
    const = lambda i: (0, 0)
    blk = pl.BlockSpec((tm, D_MODEL), row)
    wspec = pl.BlockSpec((D_MODEL, D_MODEL), const)
    f = jax.ShapeDtypeStruct((r, D_MODEL), F32)
    bf = jax.ShapeDtypeStruct((r, D_MODEL), BF16)
    return pl.pallas_call(
        _memkv_kernel,
        grid=(r // tm,),
        in_specs=[blk, pl.BlockSpec((1, D_MODEL), const), wspec, wspec],
        out_specs=[blk] * 4,
        out_shape=[f, f, bf, bf],
        compiler_params=_cparams(("arbitrary",)),
        name="memkv",
    )(mem2d, g, wk, wv)


def _post_kernel(*refs, n_real):
    h_ref, xn_ref = refs[-2:]

    @pl.when(pl.program_id(0) < n_real)
    def _():
        _post_tile(*refs[:11], h_ref, xn_ref)

    @pl.when(pl.program_id(0) >= n_real)
    def _():
        h_ref[...] = jnp.zeros(h_ref.shape, h_ref.dtype)
        xn_ref[...] = jnp.zeros(xn_ref.shape, xn_ref.dtype)


def _post_tile(x_ref, oa_ref, ob_ref, woa_ref, wob_ref, gq_ref, wq_ref, mk_ref, mv_ref, wo_ref,
               gm_ref, h_ref, xn_ref):
    ob_t = ob_ref[0].reshape(B_WIDTH, ob_ref.shape[3])
    h1 = (x_ref[...] + jnp.dot(oa_ref[...], woa_ref[...], preferred_element_type=F32)
          + lax.dot_general(ob_t, wob_ref[...], (((0,), (0,)), ((), ())), preferred_element_type=F32))
    xq = _rms(h1, gq_ref[...]).astype(BF16)
    q = (jnp.dot(xq, wq_ref[...], preferred_element_type=F32) * HEAD_DIM_M ** -0.5).astype(BF16)
    outs = []
    for h in range(N_HEADS_M):
        sl = slice(h * HEAD_DIM_M, (h + 1) * HEAD_DIM_M)
        s = _dot_nt(q[:, sl], mk_ref[0, :, sl])
        outs.append(_softmax_pv([s], [mv_ref[0, :, sl]]).astype(BF16))
    o = jnp.concatenate(outs, axis=1)
    h2 = h1 + jnp.dot(o, wo_ref[...], preferred_element_type=F32)
    h_ref[...] = h2
    _store_row_slabs(xn_ref, _pack_bf16_pairs(_rms(h2, gm_ref[...])))


def _post(x2d, oa, ob, woa, wob, gq, wq, mk, mv, wo, gm, *, tm, tiles_per_batch, total_rows,
          row_offset=0, into=None):
    r = x2d.shape[0]
    n_real = r // tm
    n_steps = n_real if into is not None else total_rows // tm
    mlen = mk.shape[1]
    row = lambda i: (jnp.minimum(i, n_real - 1), 0)
    out_row = lambda i: (row_offset // tm + i, 0)
    const = lambda i: (0, 0)
    memb = lambda i: (jnp.minimum(i, n_real - 1) // tiles_per_batch, 0, 0)
    gspec = pl.BlockSpec((1, D_MODEL), const)
    wspec = pl.BlockSpec((D_MODEL, D_MODEL), const)
    hspec = pl.BlockSpec((A_WIDTH, D_MODEL), const)
    obt_blk = lambda i: (jnp.minimum(i, n_real - 1) // tiles_per_batch, 0, 0,
                         jnp.minimum(i, n_real - 1) % tiles_per_batch)
    in_specs = [pl.BlockSpec((tm, D_MODEL), row),
                pl.BlockSpec((tm, A_WIDTH), row),
                pl.BlockSpec((1, N_HEADS_B, 2 * HEAD_DIM, tm), obt_blk),
                hspec, hspec, gspec, wspec,
                pl.BlockSpec((1, mlen, D_MODEL), memb), pl.BlockSpec((1, mlen, D_MODEL), memb),
                wspec, gspec]
    args = [x2d, oa, ob, woa, wob, gq, wq, mk, mv, wo, gm]
    aliases = {}
    if into is not None:
        aliases = {len(args): 0, len(args) + 1: 1}
        in_specs += [pl.BlockSpec(memory_space=pl.ANY)] * 2
        args += list(into)
    return pl.pallas_call(
        functools.partial(_post_kernel, n_real=n_real),
        grid=(n_steps,),
        in_specs=in_specs,
        out_specs=[pl.BlockSpec((tm, D_MODEL), out_row), pl.BlockSpec((tm * X_SLABS, LANES), out_row)],
        out_shape=[jax.ShapeDtypeStruct((total_rows, D_MODEL), F32),
                   jax.ShapeDtypeStruct((total_rows * X_SLABS, LANES), jnp.uint32)],
        input_output_aliases=aliases,
        compiler_params=_cparams(("arbitrary",)),
        name="post_attn",
    )(*args)


def _router_kernel(xn_ref, wr_ref, br_ref, tri_ref, idx_ref, gate_ref, rank_ref, cnt_ref, base_sc):
    @pl.when(pl.program_id(0) == 0)
    def _():
        base_sc[...] = jnp.zeros(base_sc.shape, F32)

    tm = xn_ref.shape[0] // X_SLABS
    xn = _unpack_bf16_pairs(_load_row_slabs(xn_ref, tm, X_SLABS))
    logits = _dot_nt(wr_ref[...], xn) + br_ref[...]
    iota_e = lax.broadcasted_iota(jnp.int32, (N_EXPERTS, tm), 0).astype(F32)
    work = logits
    vals, idxs, hots = [], [], []
    for _k in range(TOP_K):
        mx = work.max(axis=0, keepdims=True)
        ix = jnp.where(work == mx, iota_e, float(N_EXPERTS)).min(axis=0, keepdims=True)
        hot = iota_e == ix
        vals.append(mx)
        idxs.append(ix)
        hots.append(hot)
        work = jnp.where(hot, -jnp.inf, work)
    es = [jnp.exp(v - vals[0]) for v in vals]
    den = es[0] + es[1] + es[2] + es[3]
    cnt = (hots[0] | hots[1] | hots[2] | hots[3]).astype(F32)
    before = jnp.dot(cnt.astype(BF16), tri_ref[...], preferred_element_type=F32) + base_sc[...]
    for k in range(TOP_K):
        idx_ref[k:k + 1, :] = idxs[k].astype(jnp.int32)
        gate_ref[k:k + 1, :] = es[k] / den
        rank_ref[k:k + 1, :] = jnp.where(hots[k], before, 0.0).sum(axis=0, keepdims=True).astype(jnp.int32)
    base_sc[...] = base_sc[...] + cnt.sum(axis=1, keepdims=True)
    cnt_ref[...] = jnp.broadcast_to(base_sc[...], cnt_ref.shape).astype(jnp.int32)


def _router(xn2d, wr_t, br, tri, *, tm):
    r = xn2d.shape[0] // X_SLABS
    col = lambda i: (0, i)
    const = lambda i: (0, 0)
    return pl.pallas_call(
        _router_kernel,
        grid=(r // tm,),
        in_specs=[pl.BlockSpec((tm * X_SLABS, LANES), lambda i: (i, 0)),
                  pl.BlockSpec((N_EXPERTS, D_MODEL), const),
                  pl.BlockSpec((N_EXPERTS, 1), const),
                  pl.BlockSpec((tm, tm), const)],
        out_specs=[pl.BlockSpec((TOP_K, tm), col)] * 3 + [pl.BlockSpec((N_EXPERTS, LANES), const)],
        out_shape=[jax.ShapeDtypeStruct((TOP_K, r), jnp.int32),
                   jax.ShapeDtypeStruct((TOP_K, r), F32),
                   jax.ShapeDtypeStruct((TOP_K, r), jnp.int32),
                   jax.ShapeDtypeStruct((N_EXPERTS, LANES), jnp.int32)],
        scratch_shapes=[pltpu.VMEM((N_EXPERTS, 1), F32)],
        compiler_params=_cparams(("arbitrary",)),
        name="router",
    )(xn2d, wr_t, br, tri)


def _row_copy(src, dst, sem):
    return pltpu.make_async_copy(src, dst, sem)


def _slab(ref, row, n, lead=()):
    return ref.at[lead + (pl.ds(pl.multiple_of(row * n, n), n), slice(None))]


def _scatter_kernel(pos_ref, xn_ref, xs_ref, sem):
    tm = xn_ref.shape[0] // X_SLABS

    def start(t, c):
        for k in range(TOP_K):
            _row_copy(_slab(xn_ref, t, X_SLABS), _slab(xs_ref, pos_ref[0, k, t], X_SLABS),
                      sem).start(priority=k % 2)
        return c

    lax.fori_loop(0, tm, start, 0, unroll=2)

    def wait(t, c):
        for k in range(TOP_K):
            _row_copy(_slab(xn_ref, 0, X_SLABS), _slab(xs_ref, 0, X_SLABS), sem).wait()
        return c

    lax.fori_loop(0, tm, wait, 0, unroll=4)


def _scatter(pos, xn2d, *, tm):
    r = xn2d.shape[0] // X_SLABS
    return pl.pallas_call(
        _scatter_kernel,
        grid=(r // tm,),
        in_specs=[pl.BlockSpec((1, TOP_K, tm), lambda i: (i, 0, 0), memory_space=pltpu.SMEM),
                  pl.BlockSpec((tm * X_SLABS, LANES), lambda i: (i, 0))],
        out_specs=pl.BlockSpec(memory_space=pl.ANY),
        out_shape=jax.ShapeDtypeStruct((TOP_K * r * X_SLABS, LANES), jnp.uint32),
        scratch_shapes=[pltpu.SemaphoreType.DMA],
        compiler_params=_cparams(("arbitrary",)),
        name="moe_scatter",
    )(pos, xn2d)


def _experts_kernel(blk_ref, exp_ref, lo_ref, hi_ref, first_ref, newexp_ref,
                    x_ref, wgu_ref, bgu_ref, wd_ref, bd_ref, y_ref, wgu_bf, wd_bf):
    i = pl.program_id(0)

    @pl.when(newexp_ref[i] == 1)
    def _():
        wgu_bf[...] = wgu_ref[0].astype(BF16)
        wd_bf[...] = wd_ref[0].astype(BF16)

    x = _unpack_bf16_pairs(_load_row_slabs(x_ref, MOE_ROWS, X_SLABS))
    gu = jnp.dot(x, wgu_bf[...], preferred_element_type=F32) + bgu_ref[0]
    gate = jnp.minimum(gu[:, :EXPERT_FF], SWIGLU_LIMIT)
    up = jnp.clip(gu[:, EXPERT_FF:], -SWIGLU_LIMIT, SWIGLU_LIMIT)
    act = (up + 1.0) * (gate * jax.nn.sigmoid(SWIGLU_ALPHA * gate))
    y = jnp.dot(act.astype(BF16), wd_bf[...], preferred_element_type=F32) + bd_ref[0]
    row0 = blk_ref[i] * MOE_ROWS
    whole = (lo_ref[i] <= row0) & (hi_ref[i] >= row0 + MOE_ROWS)

    @pl.when(whole)
    def _():
        _store_row_slabs(y_ref, y)

    @pl.when(jnp.logical_not(whole))
    def _():
        @pl.when(first_ref[i] == 1)
        def _():
            y_ref[...] = jnp.zeros(y_ref.shape, F32)

        rows = row0 + lax.broadcasted_iota(jnp.int32, (MOE_ROWS, 1), 0)
        mine = (rows >= lo_ref[i]) & (rows < hi_ref[i])
        _store_row_slabs(y_ref, jnp.where(mine, y, _load_row_slabs(y_ref, MOE_ROWS, Y_SLABS)))


def _experts(items, xs, wgu, bgu, wd, bd):
    blk, exp, lo, hi, first = items
    newexp = jnp.concatenate([jnp.ones((1,), jnp.int32), (exp[1:] != exp[:-1]).astype(jnp.int32)])
    n_items = blk.shape[0]
    n_rows = xs.shape[0] // X_SLABS
    return pl.pallas_call(
        _experts_kernel,
        grid_spec=pltpu.PrefetchScalarGridSpec(
            num_scalar_prefetch=6,
            grid=(n_items,),
            in_specs=[pl.BlockSpec((MOE_ROWS * X_SLABS, LANES), lambda i, b, e, *_: (b[i], 0)),
                      pl.BlockSpec((1, D_MODEL, 2 * EXPERT_FF), lambda i, b, e, *_: (e[i], 0, 0)),
                      pl.BlockSpec((1, 1, 2 * EXPERT_FF), lambda i, b, e, *_: (e[i], 0, 0)),
                      pl.BlockSpec((1, EXPERT_FF, D_MODEL), lambda i, b, e, *_: (e[i], 0, 0)),
                      pl.BlockSpec((1, 1, D_MODEL), lambda i, b, e, *_: (e[i], 0, 0))],
            out_specs=pl.BlockSpec((MOE_ROWS * Y_SLABS, LANES), lambda i, b, e, *_: (b[i], 0)),
            scratch_shapes=[pltpu.VMEM((D_MODEL, 2 * EXPERT_FF), BF16), pltpu.VMEM((EXPERT_FF, D_MODEL), BF16)]),
        out_shape=jax.ShapeDtypeStruct((n_rows * Y_SLABS, LANES), F32),
        compiler_params=pltpu.CompilerParams(dimension_semantics=("arbitrary",),
                                             vmem_limit_bytes=EXPERTS_VMEM_LIMIT),
        name="moe_experts",
    )(blk, exp, lo, hi, first, newexp, xs, wgu, bgu, wd, bd)


def _work_items(counts, n_rows):
    n_blocks = n_rows // MOE_ROWS
    n_items = n_blocks + N_EXPERTS - 1
    ends = jnp.cumsum(counts)
    starts = ends - counts
    nb = jnp.where(counts > 0, (ends - 1) // MOE_ROWS - starts // MOE_ROWS + 1, 0)
    item_end = jnp.cumsum(nb)
    item_start = item_end - nb
    total = item_end[-1]
    i = jnp.arange(n_items, dtype=jnp.int32)
    e = jnp.minimum(jnp.sum(item_end[None, :] <= i[:, None], axis=1), N_EXPERTS - 1).astype(jnp.int32)
    blk = (starts[e] // MOE_ROWS + (i - item_start[e])).astype(jnp.int32)
    valid = i < total
    lo = jnp.maximum(starts[e], blk * MOE_ROWS)
    hi = jnp.minimum(ends[e], (blk + 1) * MOE_ROWS)
    last = jnp.maximum(total - 1, 0)
    blk = jnp.where(valid, blk, blk[last])
    e = jnp.where(valid, e, e[last])
    lo = jnp.where(valid, lo, 0).astype(jnp.int32)
    hi = jnp.where(valid, hi, 0).astype(jnp.int32)
    prev = jnp.concatenate([jnp.full((1,), -1, jnp.int32), blk[:-1]])
    first = (blk != prev).astype(jnp.int32)
    return blk, e, lo, hi, first, starts


def _combine_kernel(pos_ref, pos_next_ref, h_ref, gate_ref, g_ref, ys_ref, o_first_ref, o_rest_ref, buf, sem,
                    *, n_first):
    tm = h_ref.shape[0]
    i = pl.program_id(0)
    n = pl.num_programs(0)
    cur = i % 2

    def fetch(p_ref, slot):
        def start(t, c):
            for k in range(TOP_K):
                _row_copy(_slab(ys_ref, p_ref[0, k, t], Y_SLABS), _slab(buf, t, Y_SLABS, (slot, k)),
                          sem.at[slot]).start(priority=k % 2)
            return c

        lax.fori_loop(0, tm, start, 0, unroll=2)

    @pl.when(i == 0)
    def _():
        fetch(pos_ref, 0)

    for slot in range(2):
        @pl.when((i + 1 < n) & (cur != slot))
        def _():
            fetch(pos_next_ref, slot)

    def wait(t, c):
        for k in range(TOP_K):
            _row_copy(_slab(ys_ref, 0, Y_SLABS), _slab(buf, 0, Y_SLABS, (cur, 0)), sem.at[cur]).wait()
        return c

    lax.fori_loop(0, tm, wait, 0, unroll=4)
    acc = h_ref[...]
    for k in range(TOP_K):
        acc = acc + gate_ref[:, k:k + 1] * _load_row_slabs(buf, tm, Y_SLABS, (cur, k))
    res = _rms(acc, g_ref[...])

    @pl.when(i < n_first)
    def _():
        o_first_ref[...] = res

    @pl.when(i >= n_first)
    def _():
        o_rest_ref[...] = res


def _combine(pos, h2d, gates_t, g_final, ys, *, tm, rows_first):
    r = h2d.shape[0]
    n_tiles = r // tm
    n_first = rows_first // tm
    row = lambda i: (i, 0)
    return pl.pallas_call(
        functools.partial(_combine_kernel, n_first=n_first),
        grid=(n_tiles,),
        in_specs=[pl.BlockSpec((1, TOP_K, tm), lambda i: (i, 0, 0), memory_space=pltpu.SMEM),
                  pl.BlockSpec((1, TOP_K, tm), lambda i: (jnp.minimum(i + 1, n_tiles - 1), 0, 0),
                               memory_space=pltpu.SMEM),
                  pl.BlockSpec((tm, D_MODEL), row),
                  pl.BlockSpec((tm, TOP_K), row),
                  pl.BlockSpec((1, D_MODEL), lambda i: (0, 0)),
                  pl.BlockSpec(memory_space=pl.ANY)],
        out_specs=[pl.BlockSpec((tm, D_MODEL), lambda i: (jnp.minimum(i, n_first - 1), 0)),
                   pl.BlockSpec((tm, D_MODEL), lambda i: (jnp.maximum(i - n_first, 0), 0))],
        out_shape=[jax.ShapeDtypeStruct((rows_first, D_MODEL), F32),
                   jax.ShapeDtypeStruct((r - rows_first, D_MODEL), F32)],
        scratch_shapes=[pltpu.VMEM((2, TOP_K, tm * Y_SLABS, LANES), F32), pltpu.SemaphoreType.DMA((2,))],
        compiler_params=_cparams(("arbitrary",)),
        name="moe_combine",
    )(pos, pos, h2d, gates_t, g_final, ys)


def _moe(h2d, xn2d, w, rows_first):
    r = h2d.shape[0]
    tm_router = _pick_tile(r, 512)
    tm_scatter = _pick_tile(r, 512)
    tm_combine = _pick_tile(r, 256)
    idx, gates, rank, cnt = _router(xn2d, w["wr_t"], w["br"], w["tri"][:tm_router, :tm_router], tm=tm_router)
    counts = cnt[:, 0]
    blk, e, lo, hi, first, starts = _work_items(counts, TOP_K * r)
    hot = idx[:, :, None] == jnp.arange(N_EXPERTS, dtype=jnp.int32)
    pos = jnp.sum(jnp.where(hot, starts.astype(jnp.int32), 0), axis=-1) + rank
    tiles = lambda tm: pos.reshape(TOP_K, -1, tm).transpose(1, 0, 2)
    xs = _scatter(tiles(tm_scatter), xn2d, tm=tm_scatter)
    ys = _experts((blk, e, lo, hi, first), xs, w["wgu"], w["bgu"], w["wd"], w["bd"])
    return _combine(tiles(tm_combine), h2d, gates.T, w["g_final"], ys, tm=tm_combine, rows_first=rows_first)


def _pick_tile(n, cap):
    t = cap
    while n % t:
        t //= 2
    return t


def kernel(x_prompt, x_sample, mem_prompt, cache_a_k, cache_a_v, cache_b_k, cache_b_v, cache_mem_k, cache_mem_v, g_mix, w_in, rel_bias, lam_q1, lam_k1, lam_q2, lam_k2, g_subln, w_out, g_xq, g_xmem, w_xq, w_xk, w_xv, w_xo, g_moe, w_router, b_router, w_gate_up, b_gate_up, w_down, b_down, g_final):
    assert g_mix.shape[0] == 1, "single-layer kernel"
    b, s, d = x_prompt.shape
    db, dl, _ = x_sample.shape
    past = cache_b_k.shape[2]
    a_len = cache_a_k.shape[2]
    mlen = mem_prompt.shape[1]
    keep_a = min(BAND, s)
    assert d == D_MODEL and s % B_QBLK == 0 and s >= A_WIN and dl == CHUNK and past % CHUNK == 0
    assert B_QBLK == 2 * B_KBLK

    row1 = lambda v: v.reshape(1, -1).astype(F32)
    w_in_bf = w_in[0].astype(BF16)
    w_out_bf = w_out[0].astype(BF16)
    weights = dict(
        wr_t=w_router[0].T.astype(BF16), br=b_router[0].reshape(N_EXPERTS, 1).astype(F32),
        tri=jnp.asarray(np.triu(np.ones((512, 512), np.float32), 1), BF16),
        wgu=w_gate_up[0].astype(F32), bgu=b_gate_up[0].reshape(N_EXPERTS, 1, -1).astype(F32),
        wd=w_down[0].astype(F32), bd=b_down[0].reshape(N_EXPERTS, 1, -1).astype(F32),
        g_final=row1(g_final))
    lam = (jnp.exp(jnp.sum(lam_q1[0].astype(F32) * lam_k1[0].astype(F32)))
           - jnp.exp(jnp.sum(lam_q2[0].astype(F32) * lam_k2[0].astype(F32))) + LAM_INIT).reshape(1)
    table = _band_table(rel_bias[0])
    g_sub = row1(g_subln[0])

    tm_p = _pick_tile(s, 512)
    tabs_p = _rope_tables(jnp.arange(s, dtype=jnp.int32))
    p_in = _inproj(x_prompt.reshape(b * s, d), row1(g_mix[0]), w_in_bf, *tabs_p,
                   tm=tm_p, n_pos_blocks=s // tm_p, tail_div=s // tm_p if keep_a == tm_p else 1, emit_vt=True)
    tabs_s = _rope_tables(jnp.tile(past + jnp.arange(dl, dtype=jnp.int32), db))
    s_in = _inproj(x_sample.reshape(db * dl, d), row1(g_mix[0]), w_in_bf, *tabs_s,
                   tm=db * dl, n_pos_blocks=1, tail_div=1, emit_vt=False)

    def split(outs, nb, nl):
        return [o.reshape(nb, nl, A_WIDTH) for o in outs[:6]]

    aq, ak, _, bq, bk, _ = split(p_in, b, s)
    avt = p_in[10].reshape((b, s // A_QBLK) + p_in[10].shape[1:])
    bvt = p_in[11].reshape((b, s // B_KBLK) + p_in[11].shape[1:])
    saq, sak, sav, sbq, sbk, sbv = split(s_in, db, dl)

    oa = _attn_a_prompt(aq, ak, avt, table.transpose(0, 2, 1))
    ob = _attn_b_prompt(lam, bq, bk, bvt, g_sub)
    soa = _attn_a_sample(saq, cache_a_k[0].reshape(db, a_len, A_WIDTH), cache_a_v[0].reshape(db, a_len, A_WIDTH),
                         sak, sav, table[:, :dl, :BAND + dl])
    sob = _attn_b_sample(lam, sbq, cache_b_k[0].reshape(db, past, B_WIDTH),
                         cache_b_v[0].reshape(db, past * N_HEADS_B, 2 * HEAD_DIM), sbk, sbv, g_sub)

    mkf, mvf, mkb, mvb = _memkv(mem_prompt.reshape(b * mlen, d), row1(g_xmem[0]),
                                w_xk[0].astype(BF16), w_xv[0].astype(BF16), tm=mlen)
    post_w = (w_out_bf[:A_WIDTH], w_out_bf[A_WIDTH:], row1(g_xq[0]), w_xq[0].astype(BF16))
    tm_post = _pick_tile(math.gcd(s, b * s + db * dl), 256)
    rows_p, rows_s = b * s, db * dl
    sob_t = sob.transpose(0, 2, 1).reshape(db, N_HEADS_B, 2 * HEAD_DIM, dl)
    h_xn = _post(x_prompt.reshape(rows_p, d), oa.reshape(rows_p, A_WIDTH), ob,
                 *post_w, mkb.reshape(b, mlen, d), mvb.reshape(b, mlen, d),
                 w_xo[0].astype(BF16), row1(g_moe[0]), tm=tm_post, tiles_per_batch=s // tm_post,
                 total_rows=rows_p + rows_s)
    h_all, xn_all = _post(x_sample.reshape(rows_s, d), soa.reshape(rows_s, A_WIDTH), sob_t,
                          *post_w, cache_mem_k[0].reshape(db, mlen, d).astype(BF16),
                          cache_mem_v[0].reshape(db, mlen, d).astype(BF16),
                          w_xo[0].astype(BF16), row1(g_moe[0]), tm=dl, tiles_per_batch=1,
                          total_rows=rows_p + rows_s, row_offset=rows_p, into=h_xn)

    y_prompt, y_sample = _moe(h_all, xn_all, weights, rows_p)
    y_prompt = y_prompt.reshape(b, s, d)
    y_sample = y_sample.reshape(db, dl, d)

    def heads(x2d, nb, nl, nh):
        return x2d.reshape(1, nb, nl, nh, -1)

    akf, avf, bkf, bvf = p_in[6:10]
    if keep_a != tm_p:
        akf = akf.reshape(b, s, A_WIDTH)[:, s - keep_a:]
        avf = avf.reshape(b, s, A_WIDTH)[:, s - keep_a:]
    sakf, savf, sbkf, sbvf = s_in[6:]
    return (y_prompt, y_sample,
            heads(akf, b, keep_a, N_HEADS_A), heads(avf, b, keep_a, N_HEADS_A),
            heads(bkf, b, s, 2 * N_HEADS_B), heads(bvf, b, s, N_HEADS_B),
            heads(mkf, b, mlen, N_HEADS_M), heads(mvf, b, mlen, N_HEADS_M),
            heads(sakf, db, dl, N_HEADS_A), heads(savf, db, dl, N_HEADS_A),
            heads(sbkf, db, dl, 2 * N_HEADS_B), heads(sbvf, db, dl, N_HEADS_B))
```

```python
import functools
import math

import numpy as np
import jax
import jax.numpy as jnp
from jax import lax
from jax.experimental import pallas as pl
from jax.experimental.pallas import tpu as pltpu

F32 = jnp.float32
BF16 = jnp.bfloat16

D_MODEL = 1024
CHUNK = 64
LEFT_CHUNKS = 8
BAND = LEFT_CHUNKS * CHUNK
HEAD_DIM = 64
A_WIDTH = 512
B_WIDTH = 512
N_HEADS_A = 8
N_HEADS_B = 4
REL_CLIP = 128
ROT_DIM = 16
ROPE_THETA = 500000.0
N_HEADS_M = 4
HEAD_DIM_M = 256
N_EXPERTS = 32
TOP_K = 4
EXPERT_FF = 1024
SWIGLU_LIMIT = 7.0
SWIGLU_ALPHA = 1.702
RMS_EPS = 1e-5
NEG_INF = -1e30
LAM_INIT = 0.8 - 0.6 * math.exp(-0.3 * 0)
LOG2E = math.log2(math.e)

LANES = 128
A_QBLK = 4 * CHUNK
A_WIN = BAND + A_QBLK
B_QBLK = 512
B_KBLK = 256
VT_ONES = 16
MOE_ROWS = 256
X_SLABS = D_MODEL // 2 // LANES
Y_SLABS = D_MODEL // LANES
VMEM_LIMIT = 48 * 1024 * 1024
EXPERTS_VMEM_LIMIT = 56 * 1024 * 1024


def _cparams(sem):
    return pltpu.CompilerParams(dimension_semantics=sem, vmem_limit_bytes=VMEM_LIMIT)


def _rms(x, g):
    return x * lax.rsqrt(jnp.mean(x * x, axis=-1, keepdims=True) + RMS_EPS) * g


HI16 = 0xFFFF0000


def _pack_bf16_pairs(x):
    n = x.shape[1] // 2
    bits = lambda v: lax.bitcast_convert_type(v.astype(BF16).astype(F32), jnp.uint32)
    return (bits(x[:, :n]) >> 16) | (bits(x[:, n:]) & jnp.uint32(HI16))


def _unpack_bf16_pairs(w):
    lo = lax.bitcast_convert_type(w << 16, F32)
    hi = lax.bitcast_convert_type(w & jnp.uint32(HI16), F32)
    return jnp.concatenate([lo, hi], axis=1).astype(BF16)


def _store_row_slabs(ref, x, lead=()):
    m, n = x.shape[0], x.shape[1] // LANES
    for c in range(n):
        ref[lead + (pl.ds(c, m, stride=n), slice(None))] = x[:, c * LANES:(c + 1) * LANES]


def _load_row_slabs(ref, m, n, lead=()):
    return jnp.concatenate([ref[lead + (pl.ds(c, m, stride=n), slice(None))] for c in range(n)], axis=1)


def _dot_nt(a, b):
    return lax.dot_general(a, b, (((1,), (1,)), ((), ())), preferred_element_type=F32)


def _inproj_kernel(x_ref, g_ref, w_ref, cos_ref, sa_ref, sb_ref,
                   aq_ref, ak_ref, av_ref, bq_ref, bk_ref, bv_ref,
                   akf_ref, avf_ref, bkf_ref, bvf_ref, *maybe_vt_refs):
    xn = _rms(x_ref[...], g_ref[...]).astype(BF16)

    def seg(s):
        return jnp.dot(xn, w_ref[:, s * A_WIDTH:(s + 1) * A_WIDTH], preferred_element_type=F32)

    def rope(z):
        cols = []
        for c in range(z.shape[1] // LANES):
            zc = z[:, c * LANES:(c + 1) * LANES]
            cols.append(zc * cos_ref[...] + pltpu.roll(zc, 8, 1) * sa_ref[...]
                        + pltpu.roll(zc, LANES - 8, 1) * sb_ref[...])
        return jnp.concatenate(cols, axis=1)

    scale = HEAD_DIM ** -0.5
    aq_ref[...] = (seg(0) * (scale * LOG2E)).astype(BF16)
    ak = seg(1)
    ak_ref[...] = ak.astype(BF16)
    akf_ref[...] = ak
    av = seg(2)
    av_ref[...] = av.astype(BF16)
    avf_ref[...] = av
    bq_ref[...] = (rope(seg(3)) * (scale * LOG2E)).astype(BF16)
    bk = rope(seg(4))
    bk_ref[...] = bk.astype(BF16)
    bkf_ref[...] = bk
    bv = seg(5)
    bv_ref[...] = bv.astype(BF16)
    _store_row_slabs(bvf_ref, bv)
    for vt_ref, v in zip(maybe_vt_refs, (av, bv)):
        vt = v.T.astype(BF16)
        blk = vt_ref.shape[3]
        for c in range(vt_ref.shape[0]):
            for g in range(vt_ref.shape[1]):
                vt_ref[c, g, 0:LANES, :] = vt[g * LANES:(g + 1) * LANES, c * blk:(c + 1) * blk]
                vt_ref[c, g, LANES:LANES + VT_ONES, :] = jnp.ones((VT_ONES, blk), BF16)


def _inproj(x2d, g, w_bf, cos_t, sa_t, sb_t, *, tm, n_pos_blocks, tail_div, emit_vt):
    r = x2d.shape[0]
    n_tiles = r // tm
    row = lambda i: (i, 0)
    const = lambda i: (0, 0)
    pos = lambda i: (i % n_pos_blocks, 0)
    tail = lambda i: (i // tail_div, 0)
    bf = jax.ShapeDtypeStruct((r, A_WIDTH), BF16)
    f_all = jax.ShapeDtypeStruct((r, A_WIDTH), F32)
    f_tail = jax.ShapeDtypeStruct((r // tail_div, A_WIDTH), F32)
    blk = lambda im: pl.BlockSpec((tm, A_WIDTH), im)
    out_specs = [blk(row)] * 6 + [blk(tail), blk(tail), blk(row),
                                  pl.BlockSpec((tm * N_HEADS_B, 2 * HEAD_DIM), row)]
    out_shape = [bf] * 6 + [f_tail, f_tail, f_all,
                            jax.ShapeDtypeStruct((r * N_HEADS_B, 2 * HEAD_DIM), F32)]
    if emit_vt:
        for kb in (A_QBLK, B_KBLK):
            grp = (A_WIDTH // LANES, LANES + VT_ONES, kb)
            out_specs.append(pl.BlockSpec((tm // kb,) + grp, lambda i: (i, 0, 0, 0)))
            out_shape.append(jax.ShapeDtypeStruct((r // kb,) + grp, BF16))
    return pl.pallas_call(
        _inproj_kernel,
        grid=(n_tiles,),
        in_specs=[pl.BlockSpec((tm, D_MODEL), row), pl.BlockSpec((1, D_MODEL), const),
                  pl.BlockSpec((D_MODEL, 3 * D_MODEL), const),
                  pl.BlockSpec((tm, LANES), pos), pl.BlockSpec((tm, LANES), pos),
                  pl.BlockSpec((tm, LANES), pos)],
        out_specs=out_specs,
        out_shape=out_shape,
        compiler_params=_cparams(("arbitrary",)),
        name="inproj",
    )(x2d, g, w_bf, cos_t, sa_t, sb_t)


def _rope_tables(positions):
    half = ROT_DIM // 2
    lane = np.arange(LANES) % HEAD_DIM
    inv_freq = np.power(ROPE_THETA, -np.arange(half, dtype=np.float64) / half)
    ang = np.asarray(positions, np.float64)[:, None] * inv_freq[None, :]
    cos = np.cos(ang)[:, lane % half]
    sin = np.sin(ang)[:, lane % half]
    in_rot = (lane < ROT_DIM)[None, :]
    first = (lane < half)[None, :]
    cos_t = np.where(in_rot, cos, 1.0)
    sa_t = np.where(in_rot & ~first, sin, 0.0)
    sb_t = np.where(first, -sin, 0.0)
    return tuple(jnp.asarray(t, F32) for t in (cos_t, sa_t, sb_t))


def _band_table(rel_bias):
    r = np.arange(A_QBLK)[:, None]
    j = np.arange(BAND + A_WIN)[None, :]
    dchunk = LEFT_CHUNKS + r // CHUNK - j // CHUNK
    allowed = (dchunk >= 0) & (dchunk <= LEFT_CHUNKS)
    bias = rel_bias.astype(F32) * LOG2E
    n_seq = BAND + A_WIN + A_QBLK - 1
    n_hi = BAND + A_QBLK - 1 - REL_CLIP
    seq = jnp.concatenate([jnp.broadcast_to(bias[:, -1:], (N_HEADS_A, n_hi)), bias[:, ::-1],
                           jnp.broadcast_to(bias[:, :1], (N_HEADS_A, n_seq - n_hi - bias.shape[1]))], axis=1)
    seq = jnp.roll(seq, -(A_QBLK - 1), axis=1)
    toe = jnp.tile(seq, (1, A_QBLK))[:, :A_QBLK * (n_seq - 1)].reshape(N_HEADS_A, A_QBLK, n_seq - 1)
    return jnp.where(jnp.asarray(allowed)[None], toe[:, :, :BAND + A_WIN], NEG_INF)


def _softmax_pv(s_list, v_list, exp_fn=jnp.exp):
    m = s_list[0].max(axis=-1, keepdims=True)
    for s in s_list[1:]:
        m = jnp.maximum(m, s.max(axis=-1, keepdims=True))
    l = None
    o = None
    for s, v in zip(s_list, v_list):
        p = exp_fn(s - m)
        ls = p.sum(axis=-1, keepdims=True)
        os_ = jnp.dot(p.astype(BF16), v, preferred_element_type=F32)
        l = ls if l is None else l + ls
        o = os_ if o is None else o + os_
    return o / l


def _attn_a_prompt_kernel(q_ref, k_ref, vt_ref, tab_ref, o_ref):
    i = pl.program_id(1)
    n_left = BAND // A_QBLK
    n_sub = A_WIN // A_QBLK
    blk0 = jnp.maximum(i - n_left, 0)
    off = pl.multiple_of(jnp.maximum(n_left - i, 0) * A_QBLK, A_QBLK)
    q = q_ref[0]
    lane = lax.broadcasted_iota(jnp.int32, (A_QBLK, LANES), 1)

    def scores(h):
        cols = slice(h // 2 * LANES, (h // 2 + 1) * LANES)
        qp = q[:, cols]
        qh = jnp.where((lane >= HEAD_DIM) == bool(h % 2), qp, jnp.zeros_like(qp))
        return [_dot_nt(k_ref[0, pl.ds(pl.multiple_of((blk0 + c) * A_QBLK, A_QBLK), A_QBLK), cols], qh)
                + tab_ref[h, pl.ds(off + c * A_QBLK, A_QBLK), :] for c in range(n_sub)]

    def attend(h, ss):
        m = _col_reduce(ss[0], jnp.maximum, jnp.max)
        for s in ss[1:]:
            m = jnp.maximum(m, _col_reduce(s, jnp.maximum, jnp.max))
        o_t = None
        for c in range(n_sub):
            p = jnp.exp2(ss[c] - m).astype(BF16)
            os_ = jnp.dot(vt_ref[0, blk0 + c, h // 2], p, preferred_element_type=F32)
            o_t = os_ if o_t is None else o_t + os_
        return o_t[h % 2 * HEAD_DIM:(h % 2 + 1) * HEAD_DIM] / o_t[LANES:LANES + 1]

    halves = []
    ss = scores(0)
    for h in range(N_HEADS_A):
        ss_next = scores(h + 1) if h + 1 < N_HEADS_A else None
        halves.append(attend(h, ss))
        ss = ss_next
    o_ref[0] = jnp.concatenate(halves, axis=0).astype(BF16)


def _attn_a_prompt(aq, ak, avt, table_t):
    b, l, _ = aq.shape
    n_kb = avt.shape[1]
    return pl.pallas_call(
        _attn_a_prompt_kernel,
        grid=(b, l // A_QBLK),
        in_specs=[pl.BlockSpec((1, A_QBLK, A_WIDTH), lambda bb, i: (bb, i, 0)),
                  pl.BlockSpec((1, l, A_WIDTH), lambda bb, i: (bb, 0, 0)),
                  pl.BlockSpec((1,) + avt.shape[1:], lambda bb, i: (bb, 0, 0, 0, 0)),
                  pl.BlockSpec(table_t.shape, lambda bb, i: (0, 0, 0))],
        out_specs=pl.BlockSpec((1, A_WIDTH, A_QBLK), lambda bb, i: (bb, 0, i)),
        out_shape=jax.ShapeDtypeStruct((b, A_WIDTH, l), BF16),
        compiler_params=_cparams(("arbitrary", "arbitrary")),
        name="attn_a_prompt",
    )(aq, ak, avt, table_t)


def _attn_a_sample_kernel(q_ref, kc_ref, vc_ref, k_ref, v_ref, tab_ref, o_ref):
    q = q_ref[0]
    kc = kc_ref[0].astype(BF16)
    vc = vc_ref[0].astype(BF16)
    ko = k_ref[0]
    vo = v_ref[0]
    a_len = kc.shape[0]
    lq = q.shape[0]
    c0 = BAND - a_len
    outs = []
    for h in range(N_HEADS_A):
        sl = slice(h * HEAD_DIM, (h + 1) * HEAD_DIM)
        s_c = _dot_nt(q[:, sl], kc[:, sl]) + tab_ref[h, :, c0:BAND]
        s_o = _dot_nt(q[:, sl], ko[:, sl]) + tab_ref[h, :, BAND:BAND + lq]
        outs.append(_softmax_pv([s_c, s_o], [vc[:, sl], vo[:, sl]], exp_fn=jnp.exp2))
    o_ref[0] = jnp.concatenate(outs, axis=1).astype(BF16)


def _attn_a_sample(aq, cache_k, cache_v, ak, av, table):
    b, l, _ = aq.shape
    a_len = cache_k.shape[1]
    bs = lambda n: pl.BlockSpec((1, n, A_WIDTH), lambda bb: (bb, 0, 0))
    return pl.pallas_call(
        _attn_a_sample_kernel,
        grid=(b,),
        in_specs=[bs(l), bs(a_len), bs(a_len), bs(l), bs(l),
                  pl.BlockSpec(table.shape, lambda bb: (0, 0, 0))],
        out_specs=bs(l),
        out_shape=jax.ShapeDtypeStruct((b, l, A_WIDTH), BF16),
        compiler_params=_cparams(("arbitrary",)),
        name="attn_a_sample",
    )(aq, cache_k, cache_v, ak, av, table)


def _subnorm(o, g):
    return _rms(o, g) * (1.0 - LAM_INIT)


def _col_reduce(x, op, reduce_fn):
    n = x.shape[0]
    while n > 8:
        n //= 2
        x = op(x[:n], x[n:])
    return reduce_fn(x, axis=0, keepdims=True)


def _attn_b_prompt_kernel(lam_ref, q_ref, qn_ref, k_ref, vt_ref, mask_ref, g_ref, o_ref,
                          sa_sc, sb_sc, m_sc, acc_sc):
    i = pl.program_id(2)
    tk = vt_ref.shape[4]
    lane = lax.broadcasted_iota(jnp.int32, q_ref.shape[1:], 1)

    def split_maps(q):
        zero = jnp.zeros_like(q)
        return jnp.where(lane < HEAD_DIM, q, zero), jnp.where(lane >= HEAD_DIM, q, zero)

    qs = split_maps(q_ref[0])
    m_sc[...] = jnp.full(m_sc.shape, NEG_INF, F32)
    acc_sc[...] = jnp.zeros(acc_sc.shape, F32)

    def scores(j, s_sc, qmaps=qs):
        kb = k_ref[0, pl.ds(pl.multiple_of(j * tk, tk), tk), :]
        for a in range(2):
            s_sc[a] = _dot_nt(kb, qmaps[a])

    def consume(j, s_sc, mask):
        vt = vt_ref[0, j, 0]
        for a in range(2):
            s = s_sc[a]
            if mask is not None:
                s = s + mask
            m_old = m_sc[a]
            m_new = jnp.maximum(m_old, _col_reduce(s, jnp.maximum, jnp.max))
            alpha = jnp.exp2(m_old - m_new)
            p = jnp.exp2(s - m_new).astype(BF16)
            m_sc[a] = m_new
            acc_sc[a] = alpha * acc_sc[a] + jnp.dot(vt, p, preferred_element_type=F32)

    @pl.when(i == 0)
    def _():
        scores(0, sa_sc)

    def body(jj, c):
        j = 2 * jj
        scores(j + 1, sb_sc)
        consume(j, sa_sc, None)
        scores(j + 2, sa_sc)
        consume(j + 1, sb_sc, None)
        return c

    lax.fori_loop(0, i, body, 0)
    scores(2 * i + 1, sb_sc)
    consume(2 * i, sa_sc, mask_ref[0])
    scores(0, sa_sc, split_maps(qn_ref[0]))
    consume(2 * i + 1, sb_sc, mask_ref[1])
    o_t = (acc_sc[0, 0:LANES] / acc_sc[0, LANES:LANES + 1]
           - lam_ref[0] * (acc_sc[1, 0:LANES] / acc_sc[1, LANES:LANES + 1]))
    inv = lax.rsqrt(jnp.mean(o_t * o_t, axis=0, keepdims=True) + RMS_EPS)
    o_ref[0, 0] = (o_t * inv * (g_ref[...] * (1.0 - LAM_INIT))).astype(BF16)


def _chunk_causal_masks_t(tk, tq):
    kc = (np.arange(tq // tk)[:, None, None] * tk + np.arange(tk)[None, :, None]) // CHUNK
    qc = np.arange(tq)[None, None, :] // CHUNK
    return jnp.asarray(np.where(kc <= qc, 0.0, NEG_INF), F32)


def _attn_b_prompt(lam, bq, bk, bvt, g_sub):
    b, l, _ = bq.shape
    tq = B_QBLK
    tk = B_KBLK
    hw = 2 * HEAD_DIM
    n_kb = bvt.shape[1]
    return pl.pallas_call(
        _attn_b_prompt_kernel,
        grid_spec=pltpu.PrefetchScalarGridSpec(
            num_scalar_prefetch=1,
            grid=(b, N_HEADS_B, l // tq),
            in_specs=[pl.BlockSpec((1, tq, hw), lambda bb, h, i, lam_: (bb, i, h)),
                      pl.BlockSpec((1, tq, hw), lambda bb, h, i, lam_: (bb, jnp.minimum(i + 1, l // tq - 1), h)),
                      pl.BlockSpec((1, l, hw), lambda bb, h, i, lam_: (bb, 0, h)),
                      pl.BlockSpec((1, n_kb, 1, hw + VT_ONES, tk), lambda bb, h, i, lam_: (bb, 0, h, 0, 0)),
                      pl.BlockSpec((tq // tk, tk, tq), lambda bb, h, i, lam_: (0, 0, 0)),
                      pl.BlockSpec((hw, 1), lambda bb, h, i, lam_: (0, 0))],
            out_specs=pl.BlockSpec((1, 1, hw, tq), lambda bb, h, i, lam_: (bb, h, 0, i)),
            scratch_shapes=[pltpu.VMEM((2, tk, tq), F32), pltpu.VMEM((2, tk, tq), F32),
                            pltpu.VMEM((2, 1, tq), F32), pltpu.VMEM((2, hw + VT_ONES, tq), F32)]),
        out_shape=jax.ShapeDtypeStruct((b, N_HEADS_B, hw, l), BF16),
        compiler_params=_cparams(("arbitrary", "arbitrary", "arbitrary")),
        name="attn_b_prompt",
    )(lam, bq, bq, bk, bvt, _chunk_causal_masks_t(tk, tq), g_sub.reshape(hw, 1))


def _attn_b_sample_kernel(lam_ref, q_ref, kc_ref, vc_ref, k_ref, v_ref, g_ref, o_ref):
    q = q_ref[0]
    kc = kc_ref[0].astype(BF16)
    n_past = vc_ref.shape[1] // N_HEADS_B
    vc = vc_ref[0, pl.ds(pl.program_id(1), n_past, stride=N_HEADS_B), :].astype(BF16)
    ko = k_ref[0]
    vo = v_ref[0]
    outs = []
    for a in range(2):
        sl = slice(a * HEAD_DIM, (a + 1) * HEAD_DIM)
        s_c = _dot_nt(q[:, sl], kc[:, sl])
        s_o = _dot_nt(q[:, sl], ko[:, sl])
        outs.append(_softmax_pv([s_c, s_o], [vc, vo], exp_fn=jnp.exp2))
    o = outs[0] - lam_ref[0] * outs[1]
    o_ref[0] = _subnorm(o, g_ref[...]).astype(BF16)


def _attn_b_sample(lam, bq, cache_k, cache_v, bk, bv, g_sub):
    b, l, _ = bq.shape
    p = cache_k.shape[1]
    hw = 2 * HEAD_DIM
    bs = lambda n: pl.BlockSpec((1, n, hw), lambda bb, h, lam_: (bb, 0, h))
    return pl.pallas_call(
        _attn_b_sample_kernel,
        grid_spec=pltpu.PrefetchScalarGridSpec(
            num_scalar_prefetch=1,
            grid=(b, N_HEADS_B),
            in_specs=[bs(l), bs(p),
                      pl.BlockSpec((1, p * N_HEADS_B, hw), lambda bb, h, lam_: (bb, 0, 0)),
                      bs(l), bs(l),
                      pl.BlockSpec((1, hw), lambda bb, h, lam_: (0, 0))],
            out_specs=bs(l)),
        out_shape=jax.ShapeDtypeStruct((b, l, B_WIDTH), BF16),
        compiler_params=_cparams(("arbitrary", "arbitrary")),
        name="attn_b_sample",
    )(lam, bq, cache_k, cache_v, bk, bv, g_sub)


def _memkv_kernel(m_ref, g_ref, wk_ref, wv_ref, kf_ref, vf_ref, kb_ref, vb_ref):
    mn = _rms(m_ref[...], g_ref[...]).astype(BF16)
    k = jnp.dot(mn, wk_ref[...], preferred_element_type=F32)
    v = jnp.dot(mn, wv_ref[...], preferred_element_type=F32)
    kf_ref[...] = k
    vf_ref[...] = v
    kb_ref[...] = k.astype(BF16)
    vb_ref[...] = v.astype(BF16)


def _memkv(mem2d, g, wk, wv, *, tm):
    r = mem2d.shape[0]
    row = lambda i: (i, 0)
    const = lambda i: (0, 0)
    blk = pl.BlockSpec((tm, D_MODEL), row)
    wspec = pl.BlockSpec((D_MODEL, D_MODEL), const)
    f = jax.ShapeDtypeStruct((r, D_MODEL), F32)
    bf = jax.ShapeDtypeStruct((r, D_MODEL), BF16)
    return pl.pallas_call(
        _memkv_kernel,
        grid=(r // tm,),
        in_specs=[blk, pl.BlockSpec((1, D_MODEL), const), wspec, wspec],
        out_specs=[blk] * 4,
        out_shape=[f, f, bf, bf],
        compiler_params=_cparams(("arbitrary",)),
        name="memkv",
    )(mem2d, g, wk, wv)


def _post_kernel(*refs, n_real):
    h_ref, xn_ref = refs[-2:]

    @pl.when(pl.program_id(0) < n_real)
    def _():
        _post_tile(*refs[:11], h_ref, xn_ref)

    @pl.when(pl.program_id(0) >= n_real)
    def _():
        h_ref[...] = jnp.zeros(h_ref.shape, h_ref.dtype)
        xn_ref[...] = jnp.zeros(xn_ref.shape, xn_ref.dtype)


def _post_tile(x_ref, oa_ref, ob_ref, woa_ref, wob_ref, gq_ref, wq_ref, mk_ref, mv_ref, wo_ref,
               gm_ref, h_ref, xn_ref):
    tn = lambda a_t, w: lax.dot_general(a_t, w, (((0,), (0,)), ((), ())), preferred_element_type=F32)
    ob_t = ob_ref[0].reshape(B_WIDTH, ob_ref.shape[3])
    h1 = x_ref[...] + tn(oa_ref[0], woa_ref[...]) + tn(ob_t, wob_ref[...])
    xq = _rms(h1, gq_ref[...]).astype(BF16)
    q = (jnp.dot(xq, wq_ref[...], preferred_element_type=F32) * HEAD_DIM_M ** -0.5).astype(BF16)
    outs = []
    for h in range(N_HEADS_M):
        sl = slice(h * HEAD_DIM_M, (h + 1) * HEAD_DIM_M)
        s = _dot_nt(q[:, sl], mk_ref[0, :, sl])
        outs.append(_softmax_pv([s], [mv_ref[0, :, sl]]).astype(BF16))
    o = jnp.concatenate(outs, axis=1)
    h2 = h1 + jnp.dot(o, wo_ref[...], preferred_element_type=F32)
    h_ref[...] = h2
    _store_row_slabs(xn_ref, _pack_bf16_pairs(_rms(h2, gm_ref[...])))


def _post(x2d, oa, ob, woa, wob, gq, wq, mk, mv, wo, gm, *, tm, tiles_per_batch, total_rows,
          row_offset=0, into=None):
    r = x2d.shape[0]
    n_real = r // tm
    n_steps = n_real if into is not None else total_rows // tm
    mlen = mk.shape[1]
    row = lambda i: (jnp.minimum(i, n_real - 1), 0)
    out_row = lambda i: (row_offset // tm + i, 0)
    const = lambda i: (0, 0)
    memb = lambda i: (jnp.minimum(i, n_real - 1) // tiles_per_batch, 0, 0)
    gspec = pl.BlockSpec((1, D_MODEL), const)
    wspec = pl.BlockSpec((D_MODEL, D_MODEL), const)
    hspec = pl.BlockSpec((A_WIDTH, D_MODEL), const)
    obt_blk = lambda i: (jnp.minimum(i, n_real - 1) // tiles_per_batch, 0, 0,
                         jnp.minimum(i, n_real - 1) % tiles_per_batch)
    oat_blk = lambda i: (jnp.minimum(i, n_real - 1) // tiles_per_batch, 0,
                         jnp.minimum(i, n_real - 1) % tiles_per_batch)
    in_specs = [pl.BlockSpec((tm, D_MODEL), row),
                pl.BlockSpec((1, A_WIDTH, tm), oat_blk),
                pl.BlockSpec((1, N_HEADS_B, 2 * HEAD_DIM, tm), obt_blk),
                hspec, hspec, gspec, wspec,
                pl.BlockSpec((1, mlen, D_MODEL), memb), pl.BlockSpec((1, mlen, D_MODEL), memb),
                wspec, gspec]
    args = [x2d, oa, ob, woa, wob, gq, wq, mk, mv, wo, gm]
    aliases = {}
    if into is not None:
        aliases = {len(args): 0, len(args) + 1: 1}
        in_specs += [pl.BlockSpec(memory_space=pl.ANY)] * 2
        args += list(into)
    return pl.pallas_call(
        functools.partial(_post_kernel, n_real=n_real),
        grid=(n_steps,),
        in_specs=in_specs,
        out_specs=[pl.BlockSpec((tm, D_MODEL), out_row), pl.BlockSpec((tm * X_SLABS, LANES), out_row)],
        out_shape=[jax.ShapeDtypeStruct((total_rows, D_MODEL), F32),
                   jax.ShapeDtypeStruct((total_rows * X_SLABS, LANES), jnp.uint32)],
        input_output_aliases=aliases,
        compiler_params=_cparams(("arbitrary",)),
        name="post_attn",
    )(*args)


def _router_kernel(xn_ref, wr_ref, br_ref, tri_ref, idx_ref, gate_ref, rank_ref, cnt_ref, base_sc):
    @pl.when(pl.program_id(0) == 0)
    def _():
        base_sc[...] = jnp.zeros(base_sc.shape, F32)

    tm = xn_ref.shape[0] // X_SLABS
    xn = _unpack_bf16_pairs(_load_row_slabs(xn_ref, tm, X_SLABS))
    logits = _dot_nt(wr_ref[...], xn) + br_ref[...]
    iota_e = lax.broadcasted_iota(jnp.int32, (N_EXPERTS, tm), 0).astype(F32)
    work = logits
    vals, idxs, hots = [], [], []
    for _k in range(TOP_K):
        mx = work.max(axis=0, keepdims=True)
        ix = jnp.where(work == mx, iota_e, float(N_EXPERTS)).min(axis=0, keepdims=True)
        hot = iota_e == ix
        vals.append(mx)
        idxs.append(ix)
        hots.append(hot)
        work = jnp.where(hot, -jnp.inf, work)
    es = [jnp.exp(v - vals[0]) for v in vals]
    den = es[0] + es[1] + es[2] + es[3]
    cnt = (hots[0] | hots[1] | hots[2] | hots[3]).astype(F32)
    before = jnp.dot(cnt.astype(BF16), tri_ref[...], preferred_element_type=F32) + base_sc[...]
    for k in range(TOP_K):
        idx_ref[k:k + 1, :] = idxs[k].astype(jnp.int32)
        gate_ref[k:k + 1, :] = es[k] / den
        rank_ref[k:k + 1, :] = jnp.where(hots[k], before, 0.0).sum(axis=0, keepdims=True).astype(jnp.int32)
    base_sc[...] = base_sc[...] + cnt.sum(axis=1, keepdims=True)
    cnt_ref[...] = jnp.broadcast_to(base_sc[...], cnt_ref.shape).astype(jnp.int32)


def _router(xn2d, wr_t, br, tri, *, tm):
    r = xn2d.shape[0] // X_SLABS
    col = lambda i: (0, i)
    const = lambda i: (0, 0)
    return pl.pallas_call(
        _router_kernel,
        grid=(r // tm,),
        in_specs=[pl.BlockSpec((tm * X_SLABS, LANES), lambda i: (i, 0)),
                  pl.BlockSpec((N_EXPERTS, D_MODEL), const),
                  pl.BlockSpec((N_EXPERTS, 1), const),
                  pl.BlockSpec((tm, tm), const)],
        out_specs=[pl.BlockSpec((TOP_K, tm), col)] * 3 + [pl.BlockSpec((N_EXPERTS, LANES), const)],
        out_shape=[jax.ShapeDtypeStruct((TOP_K, r), jnp.int32),
                   jax.ShapeDtypeStruct((TOP_K, r), F32),
                   jax.ShapeDtypeStruct((TOP_K, r), jnp.int32),
                   jax.ShapeDtypeStruct((N_EXPERTS, LANES), jnp.int32)],
        scratch_shapes=[pltpu.VMEM((N_EXPERTS, 1), F32)],
        compiler_params=_cparams(("arbitrary",)),
        name="router",
    )(xn2d, wr_t, br, tri)


def _row_copy(src, dst, sem):
    return pltpu.make_async_copy(src, dst, sem)


def _slab(ref, row, n, lead=()):
    return ref.at[lead + (pl.ds(pl.multiple_of(row * n, n), n), slice(None))]


def _scatter_kernel(pos_ref, xn_ref, xs_ref, sem):
    tm = xn_ref.shape[0] // X_SLABS

    def start(t, c):
        for k in range(TOP_K):
            _row_copy(_slab(xn_ref, t, X_SLABS), _slab(xs_ref, pos_ref[0, k, t], X_SLABS),
                      sem).start(priority=k % 2)
        return c

    lax.fori_loop(0, tm, start, 0, unroll=2)

    def wait(t, c):
        for k in range(TOP_K):
            _row_copy(_slab(xn_ref, 0, X_SLABS), _slab(xs_ref, 0, X_SLABS), sem).wait()
        return c

    lax.fori_loop(0, tm, wait, 0, unroll=4)


def _scatter(pos, xn2d, *, tm):
    r = xn2d.shape[0] // X_SLABS
    return pl.pallas_call(
        _scatter_kernel,
        grid=(r // tm,),
        in_specs=[pl.BlockSpec((1, TOP_K, tm), lambda i: (i, 0, 0), memory_space=pltpu.SMEM),
                  pl.BlockSpec((tm * X_SLABS, LANES), lambda i: (i, 0))],
        out_specs=pl.BlockSpec(memory_space=pl.ANY),
        out_shape=jax.ShapeDtypeStruct((TOP_K * r * X_SLABS, LANES), jnp.uint32),
        scratch_shapes=[pltpu.SemaphoreType.DMA],
        compiler_params=_cparams(("arbitrary",)),
        name="moe_scatter",
    )(pos, xn2d)


def _experts_kernel(blk_ref, exp_ref, lo_ref, hi_ref, first_ref, newexp_ref,
                    x_ref, wgu_ref, bgu_ref, wd_ref, bd_ref, y_ref, wgu_bf, wd_bf):
    i = pl.program_id(0)

    @pl.when(newexp_ref[i] == 1)
    def _():
        wgu_bf[...] = wgu_ref[0].astype(BF16)
        wd_bf[...] = wd_ref[0].astype(BF16)

    x = _unpack_bf16_pairs(_load_row_slabs(x_ref, MOE_ROWS, X_SLABS))
    gu = jnp.dot(x, wgu_bf[...], preferred_element_type=F32) + bgu_ref[0]
    gate = jnp.minimum(gu[:, :EXPERT_FF], SWIGLU_LIMIT)
    up = jnp.clip(gu[:, EXPERT_FF:], -SWIGLU_LIMIT, SWIGLU_LIMIT)
    act = (up + 1.0) * (gate * jax.nn.sigmoid(SWIGLU_ALPHA * gate))
    y = jnp.dot(act.astype(BF16), wd_bf[...], preferred_element_type=F32) + bd_ref[0]
    row0 = blk_ref[i] * MOE_ROWS
    whole = (lo_ref[i] <= row0) & (hi_ref[i] >= row0 + MOE_ROWS)

    @pl.when(whole)
    def _():
        _store_row_slabs(y_ref, y)

    @pl.when(jnp.logical_not(whole))
    def _():
        @pl.when(first_ref[i] == 1)
        def _():
            y_ref[...] = jnp.zeros(y_ref.shape, F32)

        rows = row0 + lax.broadcasted_iota(jnp.int32, (MOE_ROWS, 1), 0)
        mine = (rows >= lo_ref[i]) & (rows < hi_ref[i])
        _store_row_slabs(y_ref, jnp.where(mine, y, _load_row_slabs(y_ref, MOE_ROWS, Y_SLABS)))


def _experts(items, xs, wgu, bgu, wd, bd):
    blk, exp, lo, hi, first = items
    newexp = jnp.concatenate([jnp.ones((1,), jnp.int32), (exp[1:] != exp[:-1]).astype(jnp.int32)])
    n_items = blk.shape[0]
    n_rows = xs.shape[0] // X_SLABS
    return pl.pallas_call(
        _experts_kernel,
        grid_spec=pltpu.PrefetchScalarGridSpec(
            num_scalar_prefetch=6,
            grid=(n_items,),
            in_specs=[pl.BlockSpec((MOE_ROWS * X_SLABS, LANES), lambda i, b, e, *_: (b[i], 0)),
                      pl.BlockSpec((1, D_MODEL, 2 * EXPERT_FF), lambda i, b, e, *_: (e[i], 0, 0)),
                      pl.BlockSpec((1, 1, 2 * EXPERT_FF), lambda i, b, e, *_: (e[i], 0, 0)),
                      pl.BlockSpec((1, EXPERT_FF, D_MODEL), lambda i, b, e, *_: (e[i], 0, 0)),
                      pl.BlockSpec((1, 1, D_MODEL), lambda i, b, e, *_: (e[i], 0, 0))],
            out_specs=pl.BlockSpec((MOE_ROWS * Y_SLABS, LANES), lambda i, b, e, *_: (b[i], 0)),
            scratch_shapes=[pltpu.VMEM((D_MODEL, 2 * EXPERT_FF), BF16), pltpu.VMEM((EXPERT_FF, D_MODEL), BF16)]),
        out_shape=jax.ShapeDtypeStruct((n_rows * Y_SLABS, LANES), F32),
        compiler_params=pltpu.CompilerParams(dimension_semantics=("arbitrary",),
                                             vmem_limit_bytes=EXPERTS_VMEM_LIMIT),
        name="moe_experts",
    )(blk, exp, lo, hi, first, newexp, xs, wgu, bgu, wd, bd)


def _work_items(counts, n_rows):
    n_blocks = n_rows // MOE_ROWS
    n_items = n_blocks + N_EXPERTS - 1
    ends = jnp.cumsum(counts)
    starts = ends - counts
    nb = jnp.where(counts > 0, (ends - 1) // MOE_ROWS - starts // MOE_ROWS + 1, 0)
    item_end = jnp.cumsum(nb)
    item_start = item_end - nb
    total = item_end[-1]
    i = jnp.arange(n_items, dtype=jnp.int32)
    e = jnp.minimum(jnp.sum(item_end[None, :] <= i[:, None], axis=1), N_EXPERTS - 1).astype(jnp.int32)
    blk = (starts[e] // MOE_ROWS + (i - item_start[e])).astype(jnp.int32)
    valid = i < total
    lo = jnp.maximum(starts[e], blk * MOE_ROWS)
    hi = jnp.minimum(ends[e], (blk + 1) * MOE_ROWS)
    last = jnp.maximum(total - 1, 0)
    blk = jnp.where(valid, blk, blk[last])
    e = jnp.where(valid, e, e[last])
    lo = jnp.where(valid, lo, 0).astype(jnp.int32)
    hi = jnp.where(valid, hi, 0).astype(jnp.int32)
    prev = jnp.concatenate([jnp.full((1,), -1, jnp.int32), blk[:-1]])
    first = (blk != prev).astype(jnp.int32)
    return blk, e, lo, hi, first, starts


def _combine_kernel(pos_ref, pos_next_ref, h_ref, gate_ref, g_ref, ys_ref, o_first_ref, o_rest_ref, buf, sem,
                    *, n_first):
    tm = h_ref.shape[0]
    i = pl.program_id(0)
    n = pl.num_programs(0)
    cur = i % 2

    def fetch(p_ref, slot):
        def start(t, c):
            for k in range(TOP_K):
                _row_copy(_slab(ys_ref, p_ref[0, k, t], Y_SLABS), _slab(buf, t, Y_SLABS, (slot, k)),
                          sem.at[slot]).start(priority=k % 2)
            return c

        lax.fori_loop(0, tm, start, 0, unroll=2)

    @pl.when(i == 0)
    def _():
        fetch(pos_ref, 0)

    for slot in range(2):
        @pl.when((i + 1 < n) & (cur != slot))
        def _():
            fetch(pos_next_ref, slot)

    def wait(t, c):
        for k in range(TOP_K):
            _row_copy(_slab(ys_ref, 0, Y_SLABS), _slab(buf, 0, Y_SLABS, (cur, 0)), sem.at[cur]).wait()
        return c

    lax.fori_loop(0, tm, wait, 0, unroll=4)
    acc = h_ref[...]
    for k in range(TOP_K):
        acc = acc + gate_ref[:, k:k + 1] * _load_row_slabs(buf, tm, Y_SLABS, (cur, k))
    res = _rms(acc, g_ref[...])

    @pl.when(i < n_first)
    def _():
        o_first_ref[...] = res

    @pl.when(i >= n_first)
    def _():
        o_rest_ref[...] = res


def _combine(pos, h2d, gates_t, g_final, ys, *, tm, rows_first):
    r = h2d.shape[0]
    n_tiles = r // tm
    n_first = rows_first // tm
    row = lambda i: (i, 0)
    return pl.pallas_call(
        functools.partial(_combine_kernel, n_first=n_first),
        grid=(n_tiles,),
        in_specs=[pl.BlockSpec((1, TOP_K, tm), lambda i: (i, 0, 0), memory_space=pltpu.SMEM),
                  pl.BlockSpec((1, TOP_K, tm), lambda i: (jnp.minimum(i + 1, n_tiles - 1), 0, 0),
                               memory_space=pltpu.SMEM),
                  pl.BlockSpec((tm, D_MODEL), row),
                  pl.BlockSpec((tm, TOP_K), row),
                  pl.BlockSpec((1, D_MODEL), lambda i: (0, 0)),
                  pl.BlockSpec(memory_space=pl.ANY)],
        out_specs=[pl.BlockSpec((tm, D_MODEL), lambda i: (jnp.minimum(i, n_first - 1), 0)),
                   pl.BlockSpec((tm, D_MODEL), lambda i: (jnp.maximum(i - n_first, 0), 0))],
        out_shape=[jax.ShapeDtypeStruct((rows_first, D_MODEL), F32),
                   jax.ShapeDtypeStruct((r - rows_first, D_MODEL), F32)],
        scratch_shapes=[pltpu.VMEM((2, TOP_K, tm * Y_SLABS, LANES), F32), pltpu.SemaphoreType.DMA((2,))],
        compiler_params=_cparams(("arbitrary",)),
        name="moe_combine",
    )(pos, pos, h2d, gates_t, g_final, ys)


def _moe(h2d, xn2d, w, rows_first):
    r = h2d.shape[0]
    tm_router = _pick_tile(r, 512)
    tm_scatter = _pick_tile(r, 512)
    tm_combine = _pick_tile(r, 256)
    idx, gates, rank, cnt = _router(xn2d, w["wr_t"], w["br"], w["tri"][:tm_router, :tm_router], tm=tm_router)
    counts = cnt[:, 0]
    blk, e, lo, hi, first, starts = _work_items(counts, TOP_K * r)
    hot = idx[:, :, None] == jnp.arange(N_EXPERTS, dtype=jnp.int32)
    pos = jnp.sum(jnp.where(hot, starts.astype(jnp.int32), 0), axis=-1) + rank
    tiles = lambda tm: pos.reshape(TOP_K, -1, tm).transpose(1, 0, 2)
    xs = _scatter(tiles(tm_scatter), xn2d, tm=tm_scatter)
    ys = _experts((blk, e, lo, hi, first), xs, w["wgu"], w["bgu"], w["wd"], w["bd"])
    return _combine(tiles(tm_combine), h2d, gates.T, w["g_final"], ys, tm=tm_combine, rows_first=rows_first)


def _pick_tile(n, cap):
    t = cap
    while n % t:
        t //= 2
    return t


def kernel(x_prompt, x_sample, mem_prompt, cache_a_k, cache_a_v, cache_b_k, cache_b_v, cache_mem_k, cache_mem_v, g_mix, w_in, rel_bias, lam_q1, lam_k1, lam_q2, lam_k2, g_subln, w_out, g_xq, g_xmem, w_xq, w_xk, w_xv, w_xo, g_moe, w_router, b_router, w_gate_up, b_gate_up, w_down, b_down, g_final):
    assert g_mix.shape[0] == 1, "single-layer kernel"
    b, s, d = x_prompt.shape
    db, dl, _ = x_sample.shape
    past = cache_b_k.shape[2]
    a_len = cache_a_k.shape[2]
    mlen = mem_prompt.shape[1]
    keep_a = min(BAND, s)
    assert d == D_MODEL and s % B_QBLK == 0 and s >= A_WIN and dl == CHUNK and past % CHUNK == 0
    assert B_QBLK == 2 * B_KBLK

    row1 = lambda v: v.reshape(1, -1).astype(F32)
    w_in_bf = w_in[0].astype(BF16)
    w_out_bf = w_out[0].astype(BF16)
    weights = dict(
        wr_t=w_router[0].T.astype(BF16), br=b_router[0].reshape(N_EXPERTS, 1).astype(F32),
        tri=jnp.asarray(np.triu(np.ones((512, 512), np.float32), 1), BF16),
        wgu=w_gate_up[0].astype(F32), bgu=b_gate_up[0].reshape(N_EXPERTS, 1, -1).astype(F32),
        wd=w_down[0].astype(F32), bd=b_down[0].reshape(N_EXPERTS, 1, -1).astype(F32),
        g_final=row1(g_final))
    lam = (jnp.exp(jnp.sum(lam_q1[0].astype(F32) * lam_k1[0].astype(F32)))
           - jnp.exp(jnp.sum(lam_q2[0].astype(F32) * lam_k2[0].astype(F32))) + LAM_INIT).reshape(1)
    table = _band_table(rel_bias[0])
    g_sub = row1(g_subln[0])

    tm_p = _pick_tile(s, 512)
    tabs_p = _rope_tables(np.arange(s))
    p_in = _inproj(x_prompt.reshape(b * s, d), row1(g_mix[0]), w_in_bf, *tabs_p,
                   tm=tm_p, n_pos_blocks=s // tm_p, tail_div=s // tm_p if keep_a == tm_p else 1, emit_vt=True)
    tabs_s = _rope_tables(np.tile(past + np.arange(dl), db))
    s_in = _inproj(x_sample.reshape(db * dl, d), row1(g_mix[0]), w_in_bf, *tabs_s,
                   tm=db * dl, n_pos_blocks=1, tail_div=1, emit_vt=False)

    def split(outs, nb, nl):
        return [o.reshape(nb, nl, A_WIDTH) for o in outs[:6]]

    aq, ak, _, bq, bk, _ = split(p_in, b, s)
    avt = p_in[10].reshape((b, s // A_QBLK) + p_in[10].shape[1:])
    bvt = p_in[11].reshape((b, s // B_KBLK) + p_in[11].shape[1:])
    saq, sak, sav, sbq, sbk, sbv = split(s_in, db, dl)

    oa = _attn_a_prompt(aq, ak, avt, table.transpose(0, 2, 1))
    ob = _attn_b_prompt(lam, bq, bk, bvt, g_sub)
    soa = _attn_a_sample(saq, cache_a_k[0].reshape(db, a_len, A_WIDTH), cache_a_v[0].reshape(db, a_len, A_WIDTH),
                         sak, sav, table[:, :dl, :BAND + dl])
    sob = _attn_b_sample(lam, sbq, cache_b_k[0].reshape(db, past, B_WIDTH),
                         cache_b_v[0].reshape(db, past * N_HEADS_B, 2 * HEAD_DIM), sbk, sbv, g_sub)

    mkf, mvf, mkb, mvb = _memkv(mem_prompt.reshape(b * mlen, d), row1(g_xmem[0]),
                                w_xk[0].astype(BF16), w_xv[0].astype(BF16), tm=mlen)
    post_w = (w_out_bf[:A_WIDTH], w_out_bf[A_WIDTH:], row1(g_xq[0]), w_xq[0].astype(BF16))
    tm_post = _pick_tile(math.gcd(s, b * s + db * dl), 256)
    rows_p, rows_s = b * s, db * dl
    sob_t = sob.transpose(0, 2, 1).reshape(db, N_HEADS_B, 2 * HEAD_DIM, dl)
    h_xn = _post(x_prompt.reshape(rows_p, d), oa, ob,
                 *post_w, mkb.reshape(b, mlen, d), mvb.reshape(b, mlen, d),
                 w_xo[0].astype(BF16), row1(g_moe[0]), tm=tm_post, tiles_per_batch=s // tm_post,
                 total_rows=rows_p + rows_s)
    h_all, xn_all = _post(x_sample.reshape(rows_s, d), soa.transpose(0, 2, 1), sob_t,
                          *post_w, cache_mem_k[0].reshape(db, mlen, d).astype(BF16),
                          cache_mem_v[0].reshape(db, mlen, d).astype(BF16),
                          w_xo[0].astype(BF16), row1(g_moe[0]), tm=dl, tiles_per_batch=1,
                          total_rows=rows_p + rows_s, row_offset=rows_p, into=h_xn)

    y_prompt, y_sample = _moe(h_all, xn_all, weights, rows_p)
    y_prompt = y_prompt.reshape(b, s, d)
    y_sample = y_sample.reshape(db, dl, d)

    def heads(x2d, nb, nl, nh):
        return x2d.reshape(1, nb, nl, nh, -1)

    akf, avf, bkf, bvf = p_in[6:10]
    if keep_a != tm_p:
        akf = akf.reshape(b, s, A_WIDTH)[:, s - keep_a:]
        avf = avf.reshape(b, s, A_WIDTH)[:, s - keep_a:]
    sakf, savf, sbkf, sbvf = s_in[6:]
    return (y_prompt, y_sample,
            heads(akf, b, keep_a, N_HEADS_A), heads(avf, b, keep_a, N_HEADS_A),
            heads(bkf, b, s, 2 * N_HEADS_B), heads(bvf, b, s, N_HEADS_B),
            heads(mkf, b, mlen, N_HEADS_M), heads(mvf, b, mlen, N_HEADS_M),
            heads(sakf, db, dl, N_HEADS_A), heads(savf, db, dl, N_HEADS_A),
            heads(sbkf, db, dl, 2 * N_HEADS_B), heads(sbvf, db, dl, N_HEADS_B))
```

```python
import functools
import math

import numpy as np
import jax
import jax.numpy as jnp
from jax import lax
from jax.experimental import pallas as pl
from jax.experimental.pallas import tpu as pltpu

F32 = jnp.float32
BF16 = jnp.bfloat16

D_MODEL = 1024
CHUNK = 64
LEFT_CHUNKS = 8
BAND = LEFT_CHUNKS * CHUNK
HEAD_DIM = 64
A_WIDTH = 512
B_WIDTH = 512
N_HEADS_A = 8
N_HEADS_B = 4
REL_CLIP = 128
ROT_DIM = 16
ROPE_THETA = 500000.0
N_HEADS_M = 4
HEAD_DIM_M = 256
N_EXPERTS = 32
TOP_K = 4
EXPERT_FF = 1024
SWIGLU_LIMIT = 7.0
SWIGLU_ALPHA = 1.702
RMS_EPS = 1e-5
NEG_INF = -1e30
LAM_INIT = 0.8 - 0.6 * math.exp(-0.3 * 0)
LOG2E = math.log2(math.e)

LANES = 128
A_QBLK = 4 * CHUNK
A_WIN = BAND + A_QBLK
B_QBLK = 512
B_KBLK = 256
VT_ONES = 16
MOE_ROWS = 256
X_SLABS = D_MODEL // 2 // LANES
Y_SLABS = D_MODEL // LANES
VMEM_LIMIT = 48 * 1024 * 1024
EXPERTS_VMEM_LIMIT = 56 * 1024 * 1024


def _cparams(sem):
    return pltpu.CompilerParams(dimension_semantics=sem, vmem_limit_bytes=VMEM_LIMIT)


def _rms(x, g):
    return x * lax.rsqrt(jnp.mean(x * x, axis=-1, keepdims=True) + RMS_EPS) * g


HI16 = 0xFFFF0000


def _pack_bf16_pairs(x):
    n = x.shape[1] // 2
    bits = lambda v: lax.bitcast_convert_type(v.astype(BF16).astype(F32), jnp.uint32)
    return (bits(x[:, :n]) >> 16) | (bits(x[:, n:]) & jnp.uint32(HI16))


def _unpack_bf16_pairs(w):
    lo = lax.bitcast_convert_type(w << 16, F32)
    hi = lax.bitcast_convert_type(w & jnp.uint32(HI16), F32)
    return jnp.concatenate([lo, hi], axis=1).astype(BF16)


def _store_row_slabs(ref, x, lead=()):
    m, n = x.shape[0], x.shape[1] // LANES
    for c in range(n):
        ref[lead + (pl.ds(c, m, stride=n), slice(None))] = x[:, c * LANES:(c + 1) * LANES]


def _load_row_slabs(ref, m, n, lead=()):
    return jnp.concatenate([ref[lead + (pl.ds(c, m, stride=n), slice(None))] for c in range(n)], axis=1)


def _dot_nt(a, b):
    return lax.dot_general(a, b, (((1,), (1,)), ((), ())), preferred_element_type=F32)


def _store_heads64(ref, x):
    m, n_heads = x.shape[0], x.shape[1] // HEAD_DIM
    for c in range(x.shape[1] // LANES):
        grp = x[:, c * LANES:(c + 1) * LANES]
        ref[pl.ds(2 * c, m, stride=n_heads), :] = grp[:, :HEAD_DIM]
        ref[pl.ds(2 * c + 1, m, stride=n_heads), :] = pltpu.roll(grp, HEAD_DIM, 1)[:, :HEAD_DIM]


def _inproj_kernel(x_ref, g_ref, w_ref, cos_ref, sa_ref, sb_ref,
                   aq_ref, ak_ref, av_ref, bq_ref, bk_ref, bv_ref,
                   akf_ref, avf_ref, bkf_ref, bvf_ref, *maybe_vt_refs, tail_div):
    xn = _rms(x_ref[...], g_ref[...]).astype(BF16)
    is_tail = pl.program_id(0) % tail_div == tail_div - 1

    def seg(s):
        return jnp.dot(xn, w_ref[:, s * A_WIDTH:(s + 1) * A_WIDTH], preferred_element_type=F32)

    def rope(z):
        cols = []
        for c in range(z.shape[1] // LANES):
            zc = z[:, c * LANES:(c + 1) * LANES]
            cols.append(zc * cos_ref[...] + pltpu.roll(zc, 8, 1) * sa_ref[...]
                        + pltpu.roll(zc, LANES - 8, 1) * sb_ref[...])
        return jnp.concatenate(cols, axis=1)

    scale = HEAD_DIM ** -0.5
    aq_ref[...] = (seg(0) * (scale * LOG2E)).astype(BF16)
    ak = seg(1)
    ak_ref[...] = ak.astype(BF16)
    av = seg(2)
    av_ref[...] = av.astype(BF16)
    bq_ref[...] = (rope(seg(3)) * (scale * LOG2E)).astype(BF16)
    bk = rope(seg(4))
    bk_ref[...] = bk.astype(BF16)
    _store_heads64(bkf_ref, bk)
    bv = seg(5)
    bv_ref[...] = bv.astype(BF16)
    _store_row_slabs(bvf_ref, bv)
    for vt_ref, v in zip(maybe_vt_refs, (av, bv)):
        vt = v.T.astype(BF16)
        blk = vt_ref.shape[3]
        for c in range(vt_ref.shape[0]):
            for g in range(vt_ref.shape[1]):
                vt_ref[c, g, 0:LANES, :] = vt[g * LANES:(g + 1) * LANES, c * blk:(c + 1) * blk]
                vt_ref[c, g, LANES:LANES + VT_ONES, :] = jnp.ones((VT_ONES, blk), BF16)

    @pl.when(is_tail)
    def _():
        _store_heads64(akf_ref, ak)
        _store_heads64(avf_ref, av)


def _inproj(x2d, g, w_bf, cos_t, sa_t, sb_t, *, tm, n_pos_blocks, tail_div, emit_vt):
    r = x2d.shape[0]
    n_tiles = r // tm
    row = lambda i: (i, 0)
    const = lambda i: (0, 0)
    pos = lambda i: (i % n_pos_blocks, 0)
    tail = lambda i: (i // tail_div, 0)
    bf = jax.ShapeDtypeStruct((r, A_WIDTH), BF16)
    blk = lambda im: pl.BlockSpec((tm, A_WIDTH), im)
    h64 = lambda im: pl.BlockSpec((tm * N_HEADS_A, HEAD_DIM), im)
    f_tail = jax.ShapeDtypeStruct((r // tail_div * N_HEADS_A, HEAD_DIM), F32)
    out_specs = [blk(row)] * 6 + [h64(tail), h64(tail), h64(row),
                                  pl.BlockSpec((tm * N_HEADS_B, 2 * HEAD_DIM), row)]
    out_shape = [bf] * 6 + [f_tail, f_tail,
                            jax.ShapeDtypeStruct((r * 2 * N_HEADS_B, HEAD_DIM), F32),
                            jax.ShapeDtypeStruct((r * N_HEADS_B, 2 * HEAD_DIM), F32)]
    if emit_vt:
        for kb in (A_QBLK, B_KBLK):
            grp = (A_WIDTH // LANES, LANES + VT_ONES, kb)
            out_specs.append(pl.BlockSpec((tm // kb,) + grp, lambda i: (i, 0, 0, 0)))
            out_shape.append(jax.ShapeDtypeStruct((r // kb,) + grp, BF16))
    return pl.pallas_call(
        functools.partial(_inproj_kernel, tail_div=tail_div),
        grid=(n_tiles,),
        in_specs=[pl.BlockSpec((tm, D_MODEL), row), pl.BlockSpec((1, D_MODEL), const),
                  pl.BlockSpec((D_MODEL, 3 * D_MODEL), const),
                  pl.BlockSpec((tm, LANES), pos), pl.BlockSpec((tm, LANES), pos),
                  pl.BlockSpec((tm, LANES), pos)],
        out_specs=out_specs,
        out_shape=out_shape,
        compiler_params=_cparams(("arbitrary",)),
        name="inproj",
    )(x2d, g, w_bf, cos_t, sa_t, sb_t)


def _rope_tables(positions):
    half = ROT_DIM // 2
    lane = np.arange(LANES) % HEAD_DIM
    inv_freq = np.power(ROPE_THETA, -np.arange(half, dtype=np.float64) / half)
    ang = np.asarray(positions, np.float64)[:, None] * inv_freq[None, :]
    cos = np.cos(ang)[:, lane % half]
    sin = np.sin(ang)[:, lane % half]
    in_rot = (lane < ROT_DIM)[None, :]
    first = (lane < half)[None, :]
    cos_t = np.where(in_rot, cos, 1.0)
    sa_t = np.where(in_rot & ~first, sin, 0.0)
    sb_t = np.where(first, -sin, 0.0)
    return tuple(jnp.asarray(t, F32) for t in (cos_t, sa_t, sb_t))


def _band_table(rel_bias):
    r = np.arange(A_QBLK)[:, None]
    j = np.arange(BAND + A_WIN)[None, :]
    dchunk = LEFT_CHUNKS + r // CHUNK - j // CHUNK
    allowed = (dchunk >= 0) & (dchunk <= LEFT_CHUNKS)
    bias = rel_bias.astype(F32) * LOG2E
    n_seq = BAND + A_WIN + A_QBLK - 1
    n_hi = BAND + A_QBLK - 1 - REL_CLIP
    seq = jnp.concatenate([jnp.broadcast_to(bias[:, -1:], (N_HEADS_A, n_hi)), bias[:, ::-1],
                           jnp.broadcast_to(bias[:, :1], (N_HEADS_A, n_seq - n_hi - bias.shape[1]))], axis=1)
    seq = jnp.roll(seq, -(A_QBLK - 1), axis=1)
    toe = jnp.tile(seq, (1, A_QBLK))[:, :A_QBLK * (n_seq - 1)].reshape(N_HEADS_A, A_QBLK, n_seq - 1)
    return jnp.where(jnp.asarray(allowed)[None], toe[:, :, :BAND + A_WIN], NEG_INF)


def _softmax_pv(s_list, v_list, exp_fn=jnp.exp):
    m = s_list[0].max(axis=-1, keepdims=True)
    for s in s_list[1:]:
        m = jnp.maximum(m, s.max(axis=-1, keepdims=True))
    l = None
    o = None
    for s, v in zip(s_list, v_list):
        p = exp_fn(s - m)
        ls = p.sum(axis=-1, keepdims=True)
        os_ = jnp.dot(p.astype(BF16), v, preferred_element_type=F32)
        l = ls if l is None else l + ls
        o = os_ if o is None else o + os_
    return o / l


def _attn_a_prompt_kernel(q_ref, k_ref, vt_ref, tab_ref, o_ref):
    i = pl.program_id(1)
    n_left = BAND // A_QBLK
    n_sub = A_WIN // A_QBLK
    blk0 = jnp.maximum(i - n_left, 0)
    off = pl.multiple_of(jnp.maximum(n_left - i, 0) * A_QBLK, A_QBLK)
    q = q_ref[0]
    lane = lax.broadcasted_iota(jnp.int32, (A_QBLK, LANES), 1)

    def scores(h):
        cols = slice(h // 2 * LANES, (h // 2 + 1) * LANES)
        qp = q[:, cols]
        qh = jnp.where((lane >= HEAD_DIM) == bool(h % 2), qp, jnp.zeros_like(qp))
        return [_dot_nt(k_ref[0, pl.ds(pl.multiple_of((blk0 + c) * A_QBLK, A_QBLK), A_QBLK), cols], qh)
                + tab_ref[h, pl.ds(off + c * A_QBLK, A_QBLK), :] for c in range(n_sub)]

    def attend(h, ss):
        m = _col_reduce(ss[0], jnp.maximum, jnp.max)
        for s in ss[1:]:
            m = jnp.maximum(m, _col_reduce(s, jnp.maximum, jnp.max))
        o_t = None
        for c in range(n_sub):
            p = jnp.exp2(ss[c] - m).astype(BF16)
            os_ = jnp.dot(vt_ref[0, blk0 + c, h // 2], p, preferred_element_type=F32)
            o_t = os_ if o_t is None else o_t + os_
        return o_t[h % 2 * HEAD_DIM:(h % 2 + 1) * HEAD_DIM] / o_t[LANES:LANES + 1]

    halves = []
    ss = scores(0)
    for h in range(N_HEADS_A):
        ss_next = scores(h + 1) if h + 1 < N_HEADS_A else None
        halves.append(attend(h, ss))
        ss = ss_next
    o_ref[0] = jnp.concatenate(halves, axis=0).astype(BF16)


def _attn_a_prompt(aq, ak, avt, table_t):
    b, l, _ = aq.shape
    n_kb = avt.shape[1]
    return pl.pallas_call(
        _attn_a_prompt_kernel,
        grid=(b, l // A_QBLK),
        in_specs=[pl.BlockSpec((1, A_QBLK, A_WIDTH), lambda bb, i: (bb, i, 0)),
                  pl.BlockSpec((1, l, A_WIDTH), lambda bb, i: (bb, 0, 0)),
                  pl.BlockSpec((1,) + avt.shape[1:], lambda bb, i: (bb, 0, 0, 0, 0)),
                  pl.BlockSpec(table_t.shape, lambda bb, i: (0, 0, 0))],
        out_specs=pl.BlockSpec((1, A_WIDTH, A_QBLK), lambda bb, i: (bb, 0, i)),
        out_shape=jax.ShapeDtypeStruct((b, A_WIDTH, l), BF16),
        compiler_params=_cparams(("arbitrary", "arbitrary")),
        name="attn_a_prompt",
    )(aq, ak, avt, table_t)


def _attn_a_sample_kernel(q_ref, kc_ref, vc_ref, k_ref, v_ref, tab_ref, o_ref):
    q = q_ref[0]
    kc = kc_ref[0].astype(BF16)
    vc = vc_ref[0].astype(BF16)
    ko = k_ref[0]
    vo = v_ref[0]
    a_len = kc.shape[0]
    lq = q.shape[0]
    c0 = BAND - a_len
    outs = []
    for h in range(N_HEADS_A):
        sl = slice(h * HEAD_DIM, (h + 1) * HEAD_DIM)
        s_c = _dot_nt(q[:, sl], kc[:, sl]) + tab_ref[h, :, c0:BAND]
        s_o = _dot_nt(q[:, sl], ko[:, sl]) + tab_ref[h, :, BAND:BAND + lq]
        outs.append(_softmax_pv([s_c, s_o], [vc[:, sl], vo[:, sl]], exp_fn=jnp.exp2))
    o_ref[0] = jnp.concatenate(outs, axis=1).astype(BF16)


def _attn_a_sample(aq, cache_k, cache_v, ak, av, table):
    b, l, _ = aq.shape
    a_len = cache_k.shape[1]
    bs = lambda n: pl.BlockSpec((1, n, A_WIDTH), lambda bb: (bb, 0, 0))
    return pl.pallas_call(
        _attn_a_sample_kernel,
        grid=(b,),
        in_specs=[bs(l), bs(a_len), bs(a_len), bs(l), bs(l),
                  pl.BlockSpec(table.shape, lambda bb: (0, 0, 0))],
        out_specs=bs(l),
        out_shape=jax.ShapeDtypeStruct((b, l, A_WIDTH), BF16),
        compiler_params=_cparams(("arbitrary",)),
        name="attn_a_sample",
    )(aq, cache_k, cache_v, ak, av, table)


def _subnorm(o, g):
    return _rms(o, g) * (1.0 - LAM_INIT)


def _col_reduce(x, op, reduce_fn):
    n = x.shape[0]
    while n > 8:
        n //= 2
        x = op(x[:n], x[n:])
    return reduce_fn(x, axis=0, keepdims=True)


def _attn_b_prompt_kernel(lam_ref, q_ref, qn_ref, k_ref, vt_ref, mask_ref, g_ref, o_ref,
                          sa_sc, sb_sc, m_sc, acc_sc):
    i = pl.program_id(2)
    tk = vt_ref.shape[4]
    lane = lax.broadcasted_iota(jnp.int32, q_ref.shape[1:], 1)

    def split_maps(q):
        zero = jnp.zeros_like(q)
        return jnp.where(lane < HEAD_DIM, q, zero), jnp.where(lane >= HEAD_DIM, q, zero)

    qs = split_maps(q_ref[0])
    m_sc[...] = jnp.full(m_sc.shape, NEG_INF, F32)
    acc_sc[...] = jnp.zeros(acc_sc.shape, F32)

    def scores(j, s_sc, qmaps=qs):
        kb = k_ref[0, pl.ds(pl.multiple_of(j * tk, tk), tk), :]
        for a in range(2):
            s_sc[a] = _dot_nt(kb, qmaps[a])

    def consume(j, s_sc, mask):
        vt = vt_ref[0, j, 0]
        for a in range(2):
            s = s_sc[a]
            if mask is not None:
                s = s + mask
            m_old = m_sc[a]
            m_new = jnp.maximum(m_old, _col_reduce(s, jnp.maximum, jnp.max))
            alpha = jnp.exp2(m_old - m_new)
            p = jnp.exp2(s - m_new).astype(BF16)
            m_sc[a] = m_new
            acc_sc[a] = alpha * acc_sc[a] + jnp.dot(vt, p, preferred_element_type=F32)

    @pl.when(i == 0)
    def _():
        scores(0, sa_sc)

    def body(jj, c):
        j = 2 * jj
        scores(j + 1, sb_sc)
        consume(j, sa_sc, None)
        scores(j + 2, sa_sc)
        consume(j + 1, sb_sc, None)
        return c

    lax.fori_loop(0, i, body, 0)
    scores(2 * i + 1, sb_sc)
    consume(2 * i, sa_sc, mask_ref[0])
    scores(0, sa_sc, split_maps(qn_ref[0]))
    consume(2 * i + 1, sb_sc, mask_ref[1])
    o_t = (acc_sc[0, 0:LANES] / acc_sc[0, LANES:LANES + 1]
           - lam_ref[0] * (acc_sc[1, 0:LANES] / acc_sc[1, LANES:LANES + 1]))
    inv = lax.rsqrt(jnp.mean(o_t * o_t, axis=0, keepdims=True) + RMS_EPS)
    o_ref[0, 0] = (o_t * inv * (g_ref[...] * (1.0 - LAM_INIT))).astype(BF16)


def _chunk_causal_masks_t(tk, tq):
    kc = (np.arange(tq // tk)[:, None, None] * tk + np.arange(tk)[None, :, None]) // CHUNK
    qc = np.arange(tq)[None, None, :] // CHUNK
    return jnp.asarray(np.where(kc <= qc, 0.0, NEG_INF), F32)


def _attn_b_prompt(lam, bq, bk, bvt, g_sub):
    b, l, _ = bq.shape
    tq = B_QBLK
    tk = B_KBLK
    hw = 2 * HEAD_DIM
    n_kb = bvt.shape[1]
    return pl.pallas_call(
        _attn_b_prompt_kernel,
        grid_spec=pltpu.PrefetchScalarGridSpec(
            num_scalar_prefetch=1,
            grid=(b, N_HEADS_B, l // tq),
            in_specs=[pl.BlockSpec((1, tq, hw), lambda bb, h, i, lam_: (bb, i, h)),
                      pl.BlockSpec((1, tq, hw), lambda bb, h, i, lam_: (bb, jnp.minimum(i + 1, l // tq - 1), h)),
                      pl.BlockSpec((1, l, hw), lambda bb, h, i, lam_: (bb, 0, h)),
                      pl.BlockSpec((1, n_kb, 1, hw + VT_ONES, tk), lambda bb, h, i, lam_: (bb, 0, h, 0, 0)),
                      pl.BlockSpec((tq // tk, tk, tq), lambda bb, h, i, lam_: (0, 0, 0)),
                      pl.BlockSpec((hw, 1), lambda bb, h, i, lam_: (0, 0))],
            out_specs=pl.BlockSpec((1, 1, hw, tq), lambda bb, h, i, lam_: (bb, h, 0, i)),
            scratch_shapes=[pltpu.VMEM((2, tk, tq), F32), pltpu.VMEM((2, tk, tq), F32),
                            pltpu.VMEM((2, 1, tq), F32), pltpu.VMEM((2, hw + VT_ONES, tq), F32)]),
        out_shape=jax.ShapeDtypeStruct((b, N_HEADS_B, hw, l), BF16),
        compiler_params=_cparams(("arbitrary", "arbitrary", "arbitrary")),
        name="attn_b_prompt",
    )(lam, bq, bq, bk, bvt, _chunk_causal_masks_t(tk, tq), g_sub.reshape(hw, 1))


def _attn_b_sample_kernel(lam_ref, q_ref, kc_ref, vc_ref, k_ref, v_ref, g_ref, o_ref):
    q = q_ref[0]
    kc = kc_ref[0].astype(BF16)
    n_past = vc_ref.shape[1] // N_HEADS_B
    vc = vc_ref[0, pl.ds(pl.program_id(1), n_past, stride=N_HEADS_B), :].astype(BF16)
    ko = k_ref[0]
    vo = v_ref[0]
    outs = []
    for a in range(2):
        sl = slice(a * HEAD_DIM, (a + 1) * HEAD_DIM)
        s_c = _dot_nt(q[:, sl], kc[:, sl])
        s_o = _dot_nt(q[:, sl], ko[:, sl])
        outs.append(_softmax_pv([s_c, s_o], [vc, vo], exp_fn=jnp.exp2))
    o = outs[0] - lam_ref[0] * outs[1]
    o_ref[0] = _subnorm(o, g_ref[...]).astype(BF16)


def _attn_b_sample(lam, bq, cache_k, cache_v, bk, bv, g_sub):
    b, l, _ = bq.shape
    p = cache_k.shape[1]
    hw = 2 * HEAD_DIM
    bs = lambda n: pl.BlockSpec((1, n, hw), lambda bb, h, lam_: (bb, 0, h))
    return pl.pallas_call(
        _attn_b_sample_kernel,
        grid_spec=pltpu.PrefetchScalarGridSpec(
            num_scalar_prefetch=1,
            grid=(b, N_HEADS_B),
            in_specs=[bs(l), bs(p),
                      pl.BlockSpec((1, p * N_HEADS_B, hw), lambda bb, h, lam_: (bb, 0, 0)),
                      bs(l), bs(l),
                      pl.BlockSpec((1, hw), lambda bb, h, lam_: (0, 0))],
            out_specs=bs(l)),
        out_shape=jax.ShapeDtypeStruct((b, l, B_WIDTH), BF16),
        compiler_params=_cparams(("arbitrary", "arbitrary")),
        name="attn_b_sample",
    )(lam, bq, cache_k, cache_v, bk, bv, g_sub)


def _memkv_kernel(m_ref, g_ref, wk_ref, wv_ref, kf_ref, vf_ref, kb_ref, vb_ref):
    mn = _rms(m_ref[...], g_ref[...]).astype(BF16)
    k = jnp.dot(mn, wk_ref[...], preferred_element_type=F32)
    v = jnp.dot(mn, wv_ref[...], preferred_element_type=F32)
    kf_ref[...] = k
    vf_ref[...] = v
    kb_ref[...] = k.astype(BF16)
    vb_ref[...] = v.astype(BF16)


def _memkv(mem2d, g, wk, wv, *, tm):
    r = mem2d.shape[0]
    row = lambda i: (i, 0)
    const = lambda i: (0, 0)
    blk = pl.BlockSpec((tm, D_MODEL), row)
    wspec = pl.BlockSpec((D_MODEL, D_MODEL), const)
    f = jax.ShapeDtypeStruct((r, D_MODEL), F32)
    bf = jax.ShapeDtypeStruct((r, D_MODEL), BF16)
    return pl.pallas_call(
        _memkv_kernel,
        grid=(r // tm,),
        in_specs=[blk, pl.BlockSpec((1, D_MODEL), const), wspec, wspec],
        out_specs=[blk] * 4,
        out_shape=[f, f, bf, bf],
        compiler_params=_cparams(("arbitrary",)),
        name="memkv",
    )(mem2d, g, wk, wv)


def _post_kernel(*refs, n_real):
    h_ref, xn_ref = refs[-2:]

    @pl.when(pl.program_id(0) < n_real)
    def _():
        _post_tile(*refs[:11], h_ref, xn_ref)

    @pl.when(pl.program_id(0) >= n_real)
    def _():
        h_ref[...] = jnp.zeros(h_ref.shape, h_ref.dtype)
        xn_ref[...] = jnp.zeros(xn_ref.shape, xn_ref.dtype)


def _post_tile(x_ref, oa_ref, ob_ref, woa_ref, wob_ref, gq_ref, wq_ref, mk_ref, mv_ref, wo_ref,
               gm_ref, h_ref, xn_ref):
    tn = lambda a_t, w: lax.dot_general(a_t, w, (((0,), (0,)), ((), ())), preferred_element_type=F32)
    ob_t = ob_ref[0].reshape(B_WIDTH, ob_ref.shape[3])
    h1 = x_ref[...] + tn(oa_ref[0], woa_ref[...]) + tn(ob_t, wob_ref[...])
    xq = _rms(h1, gq_ref[...]).astype(BF16)
    q = (jnp.dot(xq, wq_ref[...], preferred_element_type=F32) * HEAD_DIM_M ** -0.5).astype(BF16)
    outs = []
    for h in range(N_HEADS_M):
        sl = slice(h * HEAD_DIM_M, (h + 1) * HEAD_DIM_M)
        s = _dot_nt(q[:, sl], mk_ref[0, :, sl])
        outs.append(_softmax_pv([s], [mv_ref[0, :, sl]]).astype(BF16))
    o = jnp.concatenate(outs, axis=1)
    h2 = h1 + jnp.dot(o, wo_ref[...], preferred_element_type=F32)
    h_ref[...] = h2
    _store_row_slabs(xn_ref, _pack_bf16_pairs(_rms(h2, gm_ref[...])))


def _post(x2d, oa, ob, woa, wob, gq, wq, mk, mv, wo, gm, *, tm, tiles_per_batch, total_rows,
          row_offset=0, into=None):
    r = x2d.shape[0]
    n_real = r // tm
    n_steps = n_real if into is not None else total_rows // tm
    mlen = mk.shape[1]
    row = lambda i: (jnp.minimum(i, n_real - 1), 0)
    out_row = lambda i: (row_offset // tm + i, 0)
    const = lambda i: (0, 0)
    memb = lambda i: (jnp.minimum(i, n_real - 1) // tiles_per_batch, 0, 0)
    gspec = pl.BlockSpec((1, D_MODEL), const)
    wspec = pl.BlockSpec((D_MODEL, D_MODEL), const)
    hspec = pl.BlockSpec((A_WIDTH, D_MODEL), const)
    obt_blk = lambda i: (jnp.minimum(i, n_real - 1) // tiles_per_batch, 0, 0,
                         jnp.minimum(i, n_real - 1) % tiles_per_batch)
    oat_blk = lambda i: (jnp.minimum(i, n_real - 1) // tiles_per_batch, 0,
                         jnp.minimum(i, n_real - 1) % tiles_per_batch)
    in_specs = [pl.BlockSpec((tm, D_MODEL), row),
                pl.BlockSpec((1, A_WIDTH, tm), oat_blk),
                pl.BlockSpec((1, N_HEADS_B, 2 * HEAD_DIM, tm), obt_blk),
                hspec, hspec, gspec, wspec,
                pl.BlockSpec((1, mlen, D_MODEL), memb), pl.BlockSpec((1, mlen, D_MODEL), memb),
                wspec, gspec]
    args = [x2d, oa, ob, woa, wob, gq, wq, mk, mv, wo, gm]
    aliases = {}
    if into is not None:
        aliases = {len(args): 0, len(args) + 1: 1}
        in_specs += [pl.BlockSpec(memory_space=pl.ANY)] * 2
        args += list(into)
    return pl.pallas_call(
        functools.partial(_post_kernel, n_real=n_real),
        grid=(n_steps,),
        in_specs=in_specs,
        out_specs=[pl.BlockSpec((tm, D_MODEL), out_row), pl.BlockSpec((tm * X_SLABS, LANES), out_row)],
        out_shape=[jax.ShapeDtypeStruct((total_rows, D_MODEL), F32),
                   jax.ShapeDtypeStruct((total_rows * X_SLABS, LANES), jnp.uint32)],
        input_output_aliases=aliases,
        compiler_params=_cparams(("arbitrary",)),
        name="post_attn",
    )(*args)


def _router_kernel(xn_ref, wr_ref, br_ref, tri_ref, idx_ref, gate_ref, rank_ref, cnt_ref, base_sc):
    @pl.when(pl.program_id(0) == 0)
    def _():
        base_sc[...] = jnp.zeros(base_sc.shape, F32)

    tm = xn_ref.shape[0] // X_SLABS
    xn = _unpack_bf16_pairs(_load_row_slabs(xn_ref, tm, X_SLABS))
    logits = _dot_nt(wr_ref[...], xn) + br_ref[...]
    iota_e = lax.broadcasted_iota(jnp.int32, (N_EXPERTS, tm), 0).astype(F32)
    work = logits
    vals, idxs, hots = [], [], []
    for _k in range(TOP_K):
        mx = work.max(axis=0, keepdims=True)
        ix = jnp.where(work == mx, iota_e, float(N_EXPERTS)).min(axis=0, keepdims=True)
        hot = iota_e == ix
        vals.append(mx)
        idxs.append(ix)
        hots.append(hot)
        work = jnp.where(hot, -jnp.inf, work)
    es = [jnp.exp(v - vals[0]) for v in vals]
    den = es[0] + es[1] + es[2] + es[3]
    cnt = (hots[0] | hots[1] | hots[2] | hots[3]).astype(F32)
    before = jnp.dot(cnt.astype(BF16), tri_ref[...], preferred_element_type=F32) + base_sc[...]
    for k in range(TOP_K):
        idx_ref[k:k + 1, :] = idxs[k].astype(jnp.int32)
        gate_ref[k:k + 1, :] = es[k] / den
        rank_ref[k:k + 1, :] = jnp.where(hots[k], before, 0.0).sum(axis=0, keepdims=True).astype(jnp.int32)
    base_sc[...] = base_sc[...] + cnt.sum(axis=1, keepdims=True)
    cnt_ref[...] = jnp.broadcast_to(base_sc[...], cnt_ref.shape).astype(jnp.int32)


def _router(xn2d, wr_t, br, tri, *, tm):
    r = xn2d.shape[0] // X_SLABS
    col = lambda i: (0, i)
    const = lambda i: (0, 0)
    return pl.pallas_call(
        _router_kernel,
        grid=(r // tm,),
        in_specs=[pl.BlockSpec((tm * X_SLABS, LANES), lambda i: (i, 0)),
                  pl.BlockSpec((N_EXPERTS, D_MODEL), const),
                  pl.BlockSpec((N_EXPERTS, 1), const),
                  pl.BlockSpec((tm, tm), const)],
        out_specs=[pl.BlockSpec((TOP_K, tm), col)] * 3 + [pl.BlockSpec((N_EXPERTS, LANES), const)],
        out_shape=[jax.ShapeDtypeStruct((TOP_K, r), jnp.int32),
                   jax.ShapeDtypeStruct((TOP_K, r), F32),
                   jax.ShapeDtypeStruct((TOP_K, r), jnp.int32),
                   jax.ShapeDtypeStruct((N_EXPERTS, LANES), jnp.int32)],
        scratch_shapes=[pltpu.VMEM((N_EXPERTS, 1), F32)],
        compiler_params=_cparams(("arbitrary",)),
        name="router",
    )(xn2d, wr_t, br, tri)


def _row_copy(src, dst, sem):
    return pltpu.make_async_copy(src, dst, sem)


def _slab(ref, row, n, lead=()):
    return ref.at[lead + (pl.ds(pl.multiple_of(row * n, n), n), slice(None))]


def _scatter_kernel(pos_ref, xn_ref, xs_ref, sem):
    tm = xn_ref.shape[0] // X_SLABS

    def start(t, c):
        for k in range(TOP_K):
            _row_copy(_slab(xn_ref, t, X_SLABS), _slab(xs_ref, pos_ref[0, k, t], X_SLABS),
                      sem).start(priority=k % 2)
        return c

    lax.fori_loop(0, tm, start, 0, unroll=2)

    def wait(t, c):
        for k in range(TOP_K):
            _row_copy(_slab(xn_ref, 0, X_SLABS), _slab(xs_ref, 0, X_SLABS), sem).wait()
        return c

    lax.fori_loop(0, tm, wait, 0, unroll=4)


def _scatter(pos, xn2d, *, tm):
    r = xn2d.shape[0] // X_SLABS
    return pl.pallas_call(
        _scatter_kernel,
        grid=(r // tm,),
        in_specs=[pl.BlockSpec((1, TOP_K, tm), lambda i: (i, 0, 0), memory_space=pltpu.SMEM),
                  pl.BlockSpec((tm * X_SLABS, LANES), lambda i: (i, 0))],
        out_specs=pl.BlockSpec(memory_space=pl.ANY),
        out_shape=jax.ShapeDtypeStruct((TOP_K * r * X_SLABS, LANES), jnp.uint32),
        scratch_shapes=[pltpu.SemaphoreType.DMA],
        compiler_params=_cparams(("arbitrary",)),
        name="moe_scatter",
    )(pos, xn2d)


def _experts_kernel(blk_ref, exp_ref, lo_ref, hi_ref, first_ref, newexp_ref,
                    x_ref, wgu_ref, bgu_ref, wd_ref, bd_ref, y_ref, wgu_bf, wd_bf):
    i = pl.program_id(0)

    @pl.when(newexp_ref[i] == 1)
    def _():
        wgu_bf[...] = wgu_ref[0].astype(BF16)
        wd_bf[...] = wd_ref[0].astype(BF16)

    x = _unpack_bf16_pairs(_load_row_slabs(x_ref, MOE_ROWS, X_SLABS))
    gu = jnp.dot(x, wgu_bf[...], preferred_element_type=F32) + bgu_ref[0]
    gate = jnp.minimum(gu[:, :EXPERT_FF], SWIGLU_LIMIT)
    up = jnp.clip(gu[:, EXPERT_FF:], -SWIGLU_LIMIT, SWIGLU_LIMIT)
    act = (up + 1.0) * (gate * jax.nn.sigmoid(SWIGLU_ALPHA * gate))
    y = jnp.dot(act.astype(BF16), wd_bf[...], preferred_element_type=F32) + bd_ref[0]
    row0 = blk_ref[i] * MOE_ROWS
    whole = (lo_ref[i] <= row0) & (hi_ref[i] >= row0 + MOE_ROWS)

    @pl.when(whole)
    def _():
        _store_row_slabs(y_ref, y)

    @pl.when(jnp.logical_not(whole))
    def _():
        @pl.when(first_ref[i] == 1)
        def _():
            y_ref[...] = jnp.zeros(y_ref.shape, F32)

        rows = row0 + lax.broadcasted_iota(jnp.int32, (MOE_ROWS, 1), 0)
        mine = (rows >= lo_ref[i]) & (rows < hi_ref[i])
        _store_row_slabs(y_ref, jnp.where(mine, y, _load_row_slabs(y_ref, MOE_ROWS, Y_SLABS)))


def _experts(items, xs, wgu, bgu, wd, bd):
    blk, exp, lo, hi, first = items
    newexp = jnp.concatenate([jnp.ones((1,), jnp.int32), (exp[1:] != exp[:-1]).astype(jnp.int32)])
    n_items = blk.shape[0]
    n_rows = xs.shape[0] // X_SLABS
    return pl.pallas_call(
        _experts_kernel,
        grid_spec=pltpu.PrefetchScalarGridSpec(
            num_scalar_prefetch=6,
            grid=(n_items,),
            in_specs=[pl.BlockSpec((MOE_ROWS * X_SLABS, LANES), lambda i, b, e, *_: (b[i], 0)),
                      pl.BlockSpec((1, D_MODEL, 2 * EXPERT_FF), lambda i, b, e, *_: (e[i], 0, 0)),
                      pl.BlockSpec((1, 1, 2 * EXPERT_FF), lambda i, b, e, *_: (e[i], 0, 0)),
                      pl.BlockSpec((1, EXPERT_FF, D_MODEL), lambda i, b, e, *_: (e[i], 0, 0)),
                      pl.BlockSpec((1, 1, D_MODEL), lambda i, b, e, *_: (e[i], 0, 0))],
            out_specs=pl.BlockSpec((MOE_ROWS * Y_SLABS, LANES), lambda i, b, e, *_: (b[i], 0)),
            scratch_shapes=[pltpu.VMEM((D_MODEL, 2 * EXPERT_FF), BF16), pltpu.VMEM((EXPERT_FF, D_MODEL), BF16)]),
        out_shape=jax.ShapeDtypeStruct((n_rows * Y_SLABS, LANES), F32),
        compiler_params=pltpu.CompilerParams(dimension_semantics=("arbitrary",),
                                             vmem_limit_bytes=EXPERTS_VMEM_LIMIT),
        name="moe_experts",
    )(blk, exp, lo, hi, first, newexp, xs, wgu, bgu, wd, bd)


def _work_items(counts, n_rows):
    n_blocks = n_rows // MOE_ROWS
    n_items = n_blocks + N_EXPERTS - 1
    ends = jnp.cumsum(counts)
    starts = ends - counts
    nb = jnp.where(counts > 0, (ends - 1) // MOE_ROWS - starts // MOE_ROWS + 1, 0)
    item_end = jnp.cumsum(nb)
    item_start = item_end - nb
    total = item_end[-1]
    i = jnp.arange(n_items, dtype=jnp.int32)
    e = jnp.minimum(jnp.sum(item_end[None, :] <= i[:, None], axis=1), N_EXPERTS - 1).astype(jnp.int32)
    blk = (starts[e] // MOE_ROWS + (i - item_start[e])).astype(jnp.int32)
    valid = i < total
    lo = jnp.maximum(starts[e], blk * MOE_ROWS)
    hi = jnp.minimum(ends[e], (blk + 1) * MOE_ROWS)
    last = jnp.maximum(total - 1, 0)
    blk = jnp.where(valid, blk, blk[last])
    e = jnp.where(valid, e, e[last])
    lo = jnp.where(valid, lo, 0).astype(jnp.int32)
    hi = jnp.where(valid, hi, 0).astype(jnp.int32)
    prev = jnp.concatenate([jnp.full((1,), -1, jnp.int32), blk[:-1]])
    first = (blk != prev).astype(jnp.int32)
    return blk, e, lo, hi, first, starts


def _combine_kernel(pos_ref, pos_next_ref, h_ref, gate_ref, g_ref, ys_ref, o_first_ref, o_rest_ref, buf, sem,
                    *, n_first):
    tm = h_ref.shape[0]
    i = pl.program_id(0)
    n = pl.num_programs(0)
    cur = i % 2

    def fetch(p_ref, slot):
        def start(t, c):
            for k in range(TOP_K):
                _row_copy(_slab(ys_ref, p_ref[0, k, t], Y_SLABS), _slab(buf, t, Y_SLABS, (slot, k)),
                          sem.at[slot]).start(priority=k % 2)
            return c

        lax.fori_loop(0, tm, start, 0, unroll=2)

    @pl.when(i == 0)
    def _():
        fetch(pos_ref, 0)

    for slot in range(2):
        @pl.when((i + 1 < n) & (cur != slot))
        def _():
            fetch(pos_next_ref, slot)

    def wait(t, c):
        for k in range(TOP_K):
            _row_copy(_slab(ys_ref, 0, Y_SLABS), _slab(buf, 0, Y_SLABS, (cur, 0)), sem.at[cur]).wait()
        return c

    lax.fori_loop(0, tm, wait, 0, unroll=4)
    acc = h_ref[...]
    for k in range(TOP_K):
        acc = acc + gate_ref[:, k:k + 1] * _load_row_slabs(buf, tm, Y_SLABS, (cur, k))
    res = _rms(acc, g_ref[...])

    @pl.when(i < n_first)
    def _():
        o_first_ref[...] = res

    @pl.when(i >= n_first)
    def _():
        o_rest_ref[...] = res


def _combine(pos, h2d, gates_t, g_final, ys, *, tm, rows_first):
    r = h2d.shape[0]
    n_tiles = r // tm
    n_first = rows_first // tm
    row = lambda i: (i, 0)
    return pl.pallas_call(
        functools.partial(_combine_kernel, n_first=n_first),
        grid=(n_tiles,),
        in_specs=[pl.BlockSpec((1, TOP_K, tm), lambda i: (i, 0, 0), memory_space=pltpu.SMEM),
                  pl.BlockSpec((1, TOP_K, tm), lambda i: (jnp.minimum(i + 1, n_tiles - 1), 0, 0),
                               memory_space=pltpu.SMEM),
                  pl.BlockSpec((tm, D_MODEL), row),
                  pl.BlockSpec((tm, TOP_K), row),
                  pl.BlockSpec((1, D_MODEL), lambda i: (0, 0)),
                  pl.BlockSpec(memory_space=pl.ANY)],
        out_specs=[pl.BlockSpec((tm, D_MODEL), lambda i: (jnp.minimum(i, n_first - 1), 0)),
                   pl.BlockSpec((tm, D_MODEL), lambda i: (jnp.maximum(i - n_first, 0), 0))],
        out_shape=[jax.ShapeDtypeStruct((rows_first, D_MODEL), F32),
                   jax.ShapeDtypeStruct((r - rows_first, D_MODEL), F32)],
        scratch_shapes=[pltpu.VMEM((2, TOP_K, tm * Y_SLABS, LANES), F32), pltpu.SemaphoreType.DMA((2,))],
        compiler_params=_cparams(("arbitrary",)),
        name="moe_combine",
    )(pos, pos, h2d, gates_t, g_final, ys)


def _moe(h2d, xn2d, w, rows_first):
    r = h2d.shape[0]
    tm_router = _pick_tile(r, 512)
    tm_scatter = _pick_tile(r, 512)
    tm_combine = _pick_tile(r, 256)
    idx, gates, rank, cnt = _router(xn2d, w["wr_t"], w["br"], w["tri"][:tm_router, :tm_router], tm=tm_router)
    counts = cnt[:, 0]
    blk, e, lo, hi, first, starts = _work_items(counts, TOP_K * r)
    hot = idx[:, :, None] == jnp.arange(N_EXPERTS, dtype=jnp.int32)
    pos = jnp.sum(jnp.where(hot, starts.astype(jnp.int32), 0), axis=-1) + rank
    tiles = lambda tm: pos.reshape(TOP_K, -1, tm).transpose(1, 0, 2)
    xs = _scatter(tiles(tm_scatter), xn2d, tm=tm_scatter)
    ys = _experts((blk, e, lo, hi, first), xs, w["wgu"], w["bgu"], w["wd"], w["bd"])
    return _combine(tiles(tm_combine), h2d, gates.T, w["g_final"], ys, tm=tm_combine, rows_first=rows_first)


def _pick_tile(n, cap):
    t = cap
    while n % t:
        t //= 2
    return t


def kernel(x_prompt, x_sample, mem_prompt, cache_a_k, cache_a_v, cache_b_k, cache_b_v, cache_mem_k, cache_mem_v, g_mix, w_in, rel_bias, lam_q1, lam_k1, lam_q2, lam_k2, g_subln, w_out, g_xq, g_xmem, w_xq, w_xk, w_xv, w_xo, g_moe, w_router, b_router, w_gate_up, b_gate_up, w_down, b_down, g_final):
    assert g_mix.shape[0] == 1, "single-layer kernel"
    b, s, d = x_prompt.shape
    db, dl, _ = x_sample.shape
    past = cache_b_k.shape[2]
    a_len = cache_a_k.shape[2]
    mlen = mem_prompt.shape[1]
    keep_a = min(BAND, s)
    assert d == D_MODEL and s % B_QBLK == 0 and s >= A_WIN and dl == CHUNK and past % CHUNK == 0
    assert B_QBLK == 2 * B_KBLK

    row1 = lambda v: v.reshape(1, -1).astype(F32)
    w_in_bf = w_in[0].astype(BF16)
    w_out_bf = w_out[0].astype(BF16)
    weights = dict(
        wr_t=w_router[0].T.astype(BF16), br=b_router[0].reshape(N_EXPERTS, 1).astype(F32),
        tri=jnp.asarray(np.triu(np.ones((512, 512), np.float32), 1), BF16),
        wgu=w_gate_up[0].astype(F32), bgu=b_gate_up[0].reshape(N_EXPERTS, 1, -1).astype(F32),
        wd=w_down[0].astype(F32), bd=b_down[0].reshape(N_EXPERTS, 1, -1).astype(F32),
        g_final=row1(g_final))
    lam = (jnp.exp(jnp.sum(lam_q1[0].astype(F32) * lam_k1[0].astype(F32)))
           - jnp.exp(jnp.sum(lam_q2[0].astype(F32) * lam_k2[0].astype(F32))) + LAM_INIT).reshape(1)
    table = _band_table(rel_bias[0])
    g_sub = row1(g_subln[0])

    tm_p = _pick_tile(s, 512)
    tabs_p = _rope_tables(np.arange(s))
    p_in = _inproj(x_prompt.reshape(b * s, d), row1(g_mix[0]), w_in_bf, *tabs_p,
                   tm=tm_p, n_pos_blocks=s // tm_p, tail_div=s // tm_p if keep_a == tm_p else 1, emit_vt=True)
    tabs_s = _rope_tables(np.tile(past + np.arange(dl), db))
    s_in = _inproj(x_sample.reshape(db * dl, d), row1(g_mix[0]), w_in_bf, *tabs_s,
                   tm=db * dl, n_pos_blocks=1, tail_div=1, emit_vt=False)

    def split(outs, nb, nl):
        return [o.reshape(nb, nl, A_WIDTH) for o in outs[:6]]

    aq, ak, _, bq, bk, _ = split(p_in, b, s)
    avt = p_in[10].reshape((b, s // A_QBLK) + p_in[10].shape[1:])
    bvt = p_in[11].reshape((b, s // B_KBLK) + p_in[11].shape[1:])
    saq, sak, sav, sbq, sbk, sbv = split(s_in, db, dl)

    oa = _attn_a_prompt(aq, ak, avt, table.transpose(0, 2, 1))
    ob = _attn_b_prompt(lam, bq, bk, bvt, g_sub)
    soa = _attn_a_sample(saq, cache_a_k[0].reshape(db, a_len, A_WIDTH), cache_a_v[0].reshape(db, a_len, A_WIDTH),
                         sak, sav, table[:, :dl, :BAND + dl])
    sob = _attn_b_sample(lam, sbq, cache_b_k[0].reshape(db, past, B_WIDTH),
                         cache_b_v[0].reshape(db, past * N_HEADS_B, 2 * HEAD_DIM), sbk, sbv, g_sub)

    mkf, mvf, mkb, mvb = _memkv(mem_prompt.reshape(b * mlen, d), row1(g_xmem[0]),
                                w_xk[0].astype(BF16), w_xv[0].astype(BF16), tm=mlen)
    post_w = (w_out_bf[:A_WIDTH], w_out_bf[A_WIDTH:], row1(g_xq[0]), w_xq[0].astype(BF16))
    tm_post = _pick_tile(math.gcd(s, b * s + db * dl), 256)
    rows_p, rows_s = b * s, db * dl
    sob_t = sob.transpose(0, 2, 1).reshape(db, N_HEADS_B, 2 * HEAD_DIM, dl)
    h_xn = _post(x_prompt.reshape(rows_p, d), oa, ob,
                 *post_w, mkb.reshape(b, mlen, d), mvb.reshape(b, mlen, d),
                 w_xo[0].astype(BF16), row1(g_moe[0]), tm=tm_post, tiles_per_batch=s // tm_post,
                 total_rows=rows_p + rows_s)
    h_all, xn_all = _post(x_sample.reshape(rows_s, d), soa.transpose(0, 2, 1), sob_t,
                          *post_w, cache_mem_k[0].reshape(db, mlen, d).astype(BF16),
                          cache_mem_v[0].reshape(db, mlen, d).astype(BF16),
                          w_xo[0].astype(BF16), row1(g_moe[0]), tm=dl, tiles_per_batch=1,
                          total_rows=rows_p + rows_s, row_offset=rows_p, into=h_xn)

    y_prompt, y_sample = _moe(h_all, xn_all, weights, rows_p)
    y_prompt = y_prompt.reshape(b, s, d)
    y_sample = y_sample.reshape(db, dl, d)

    def heads(x2d, nb, nl, nh):
        return x2d.reshape(1, nb, nl, nh, -1)

    akf, avf, bkf, bvf = p_in[6:10]
    if keep_a != tm_p:
        akf = akf.reshape(b, s, A_WIDTH)[:, s - keep_a:]
        avf = avf.reshape(b, s, A_WIDTH)[:, s - keep_a:]
    sakf, savf, sbkf, sbvf = s_in[6:]
    return (y_prompt, y_sample,
            heads(akf, b, keep_a, N_HEADS_A), heads(avf, b, keep_a, N_HEADS_A),
            heads(bkf, b, s, 2 * N_HEADS_B), heads(bvf, b, s, N_HEADS_B),
            heads(mkf, b, mlen, N_HEADS_M), heads(mvf, b, mlen, N_HEADS_M),
            heads(sakf, db, dl, N_HEADS_A), heads(savf, db, dl, N_HEADS_A),
            heads(sbkf, db, dl, 2 * N_HEADS_B), heads(sbvf, db, dl, N_HEADS_B))
```

```python
import functools
import math

import numpy as np
import jax
import jax.numpy as jnp
from jax import lax
from jax.experimental import pallas as pl
from jax.experimental.pallas import tpu as pltpu

F32 = jnp.float32
BF16 = jnp.bfloat16

D_MODEL = 1024
CHUNK = 64
LEFT_CHUNKS = 8
BAND = LEFT_CHUNKS * CHUNK
HEAD_DIM = 64
A_WIDTH = 512
B_WIDTH = 512
N_HEADS_A = 8
N_HEADS_B = 4
REL_CLIP = 128
ROT_DIM = 16
ROPE_THETA = 500000.0
N_HEADS_M = 4
HEAD_DIM_M = 256
N_EXPERTS = 32
TOP_K = 4
EXPERT_FF = 1024
SWIGLU_LIMIT = 7.0
SWIGLU_ALPHA = 1.702
RMS_EPS = 1e-5
NEG_INF = -1e30
LAM_INIT = 0.8 - 0.6 * math.exp(-0.3 * 0)
LOG2E = math.log2(math.e)

LANES = 128
A_QBLK = 4 * CHUNK
A_WIN = BAND + A_QBLK
B_QBLK = 512
B_KBLK = 256
VT_ONES = 16
POS_GROUP = 16
MOE_ROWS = 256
X_SLABS = D_MODEL // 2 // LANES
Y_SLABS = D_MODEL // LANES
VMEM_LIMIT = 48 * 1024 * 1024
EXPERTS_VMEM_LIMIT = 56 * 1024 * 1024


def _cparams(sem):
    return pltpu.CompilerParams(dimension_semantics=sem, vmem_limit_bytes=VMEM_LIMIT)


def _rms(x, g):
    return x * lax.rsqrt(jnp.mean(x * x, axis=-1, keepdims=True) + RMS_EPS) * g


HI16 = 0xFFFF0000


def _pack_bf16_pairs(x):
    n = x.shape[1] // 2
    bits = lambda v: lax.bitcast_convert_type(v.astype(BF16).astype(F32), jnp.uint32)
    return (bits(x[:, :n]) >> 16) | (bits(x[:, n:]) & jnp.uint32(HI16))


def _unpack_bf16_pairs(w):
    lo = lax.bitcast_convert_type(w << 16, F32)
    hi = lax.bitcast_convert_type(w & jnp.uint32(HI16), F32)
    return jnp.concatenate([lo, hi], axis=1).astype(BF16)


def _store_row_slabs(ref, x, lead=()):
    m, n = x.shape[0], x.shape[1] // LANES
    for c in range(n):
        ref[lead + (pl.ds(c, m, stride=n), slice(None))] = x[:, c * LANES:(c + 1) * LANES]


def _load_row_slabs(ref, m, n, lead=()):
    return jnp.concatenate([ref[lead + (pl.ds(c, m, stride=n), slice(None))] for c in range(n)], axis=1)


def _dot_nt(a, b):
    return lax.dot_general(a, b, (((1,), (1,)), ((), ())), preferred_element_type=F32)


def _store_heads64(ref, x):
    m, n_heads = x.shape[0], x.shape[1] // HEAD_DIM
    for c in range(x.shape[1] // LANES):
        grp = x[:, c * LANES:(c + 1) * LANES]
        ref[pl.ds(2 * c, m, stride=n_heads), :] = grp[:, :HEAD_DIM]
        ref[pl.ds(2 * c + 1, m, stride=n_heads), :] = pltpu.roll(grp, HEAD_DIM, 1)[:, :HEAD_DIM]


def _inproj_kernel(x_ref, g_ref, w_ref, cos_ref, sa_ref, sb_ref,
                   aq_ref, ak_ref, av_ref, bq_ref, bk_ref, bv_ref,
                   akf_ref, avf_ref, bkf_ref, bvf_ref, *maybe_vt_refs, tail_div):
    xn = _rms(x_ref[...], g_ref[...]).astype(BF16)
    is_tail = pl.program_id(0) % tail_div == tail_div - 1

    def seg(s):
        return jnp.dot(xn, w_ref[:, s * A_WIDTH:(s + 1) * A_WIDTH], preferred_element_type=F32)

    def rope(z):
        cols = []
        for c in range(z.shape[1] // LANES):
            zc = z[:, c * LANES:(c + 1) * LANES]
            cols.append(zc * cos_ref[...] + pltpu.roll(zc, 8, 1) * sa_ref[...]
                        + pltpu.roll(zc, LANES - 8, 1) * sb_ref[...])
        return jnp.concatenate(cols, axis=1)

    scale = HEAD_DIM ** -0.5
    aq_ref[...] = (seg(0) * (scale * LOG2E)).astype(BF16)
    ak = seg(1)
    ak_ref[...] = ak.astype(BF16)
    av = seg(2)
    av_ref[...] = av.astype(BF16)
    bq_ref[...] = (rope(seg(3)) * (scale * LOG2E)).astype(BF16)
    bk = rope(seg(4))
    bk_ref[...] = bk.astype(BF16)
    _store_heads64(bkf_ref, bk)
    bv = seg(5)
    bv_ref[...] = bv.astype(BF16)
    _store_row_slabs(bvf_ref, bv)
    for vt_ref, v in zip(maybe_vt_refs, (av, bv)):
        vt = v.T.astype(BF16)
        blk = vt_ref.shape[3]
        for c in range(vt_ref.shape[0]):
            for g in range(vt_ref.shape[1]):
                vt_ref[c, g, 0:LANES, :] = vt[g * LANES:(g + 1) * LANES, c * blk:(c + 1) * blk]
                vt_ref[c, g, LANES:LANES + VT_ONES, :] = jnp.ones((VT_ONES, blk), BF16)

    @pl.when(is_tail)
    def _():
        _store_heads64(akf_ref, ak)
        _store_heads64(avf_ref, av)


def _inproj(x2d, g, w_bf, cos_t, sa_t, sb_t, *, tm, n_pos_blocks, tail_div, emit_vt):
    r = x2d.shape[0]
    n_tiles = r // tm
    row = lambda i: (i, 0)
    const = lambda i: (0, 0)
    pos = lambda i: (i % n_pos_blocks, 0)
    tail = lambda i: (i // tail_div, 0)
    bf = jax.ShapeDtypeStruct((r, A_WIDTH), BF16)
    blk = lambda im: pl.BlockSpec((tm, A_WIDTH), im)
    h64 = lambda im: pl.BlockSpec((tm * N_HEADS_A, HEAD_DIM), im)
    f_tail = jax.ShapeDtypeStruct((r // tail_div * N_HEADS_A, HEAD_DIM), F32)
    out_specs = [blk(row)] * 6 + [h64(tail), h64(tail), h64(row),
                                  pl.BlockSpec((tm * N_HEADS_B, 2 * HEAD_DIM), row)]
    out_shape = [bf] * 6 + [f_tail, f_tail,
                            jax.ShapeDtypeStruct((r * 2 * N_HEADS_B, HEAD_DIM), F32),
                            jax.ShapeDtypeStruct((r * N_HEADS_B, 2 * HEAD_DIM), F32)]
    if emit_vt:
        for kb in (A_QBLK, B_KBLK):
            grp = (A_WIDTH // LANES, LANES + VT_ONES, kb)
            out_specs.append(pl.BlockSpec((tm // kb,) + grp, lambda i: (i, 0, 0, 0)))
            out_shape.append(jax.ShapeDtypeStruct((r // kb,) + grp, BF16))
    return pl.pallas_call(
        functools.partial(_inproj_kernel, tail_div=tail_div),
        grid=(n_tiles,),
        in_specs=[pl.BlockSpec((tm, D_MODEL), row), pl.BlockSpec((1, D_MODEL), const),
                  pl.BlockSpec((D_MODEL, 3 * D_MODEL), const),
                  pl.BlockSpec((tm, LANES), pos), pl.BlockSpec((tm, LANES), pos),
                  pl.BlockSpec((tm, LANES), pos)],
        out_specs=out_specs,
        out_shape=out_shape,
        compiler_params=_cparams(("arbitrary",)),
        name="inproj",
    )(x2d, g, w_bf, cos_t, sa_t, sb_t)


def _rope_tables(positions):
    half = ROT_DIM // 2
    lane = np.arange(LANES) % HEAD_DIM
    inv_freq = np.power(ROPE_THETA, -np.arange(half, dtype=np.float64) / half)
    ang = np.asarray(positions, np.float64)[:, None] * inv_freq[None, :]
    cos = np.cos(ang)[:, lane % half]
    sin = np.sin(ang)[:, lane % half]
    in_rot = (lane < ROT_DIM)[None, :]
    first = (lane < half)[None, :]
    cos_t = np.where(in_rot, cos, 1.0)
    sa_t = np.where(in_rot & ~first, sin, 0.0)
    sb_t = np.where(first, -sin, 0.0)
    return tuple(jnp.asarray(t, F32) for t in (cos_t, sa_t, sb_t))


def _band_table(rel_bias):
    r = np.arange(A_QBLK)[:, None]
    j = np.arange(BAND + A_WIN)[None, :]
    dchunk = LEFT_CHUNKS + r // CHUNK - j // CHUNK
    allowed = (dchunk >= 0) & (dchunk <= LEFT_CHUNKS)
    bias = rel_bias.astype(F32) * LOG2E
    n_seq = BAND + A_WIN + A_QBLK - 1
    n_hi = BAND + A_QBLK - 1 - REL_CLIP
    seq = jnp.concatenate([jnp.broadcast_to(bias[:, -1:], (N_HEADS_A, n_hi)), bias[:, ::-1],
                           jnp.broadcast_to(bias[:, :1], (N_HEADS_A, n_seq - n_hi - bias.shape[1]))], axis=1)
    seq = jnp.roll(seq, -(A_QBLK - 1), axis=1)
    toe = jnp.tile(seq, (1, A_QBLK))[:, :A_QBLK * (n_seq - 1)].reshape(N_HEADS_A, A_QBLK, n_seq - 1)
    return jnp.where(jnp.asarray(allowed)[None], toe[:, :, :BAND + A_WIN], NEG_INF)


def _softmax_pv(s_list, v_list, exp_fn=jnp.exp):
    m = s_list[0].max(axis=-1, keepdims=True)
    for s in s_list[1:]:
        m = jnp.maximum(m, s.max(axis=-1, keepdims=True))
    l = None
    o = None
    for s, v in zip(s_list, v_list):
        p = exp_fn(s - m)
        ls = p.sum(axis=-1, keepdims=True)
        os_ = jnp.dot(p.astype(BF16), v, preferred_element_type=F32)
        l = ls if l is None else l + ls
        o = os_ if o is None else o + os_
    return o / l


def _attn_a_prompt_kernel(q_ref, k_ref, vt_ref, tab_ref, o_ref):
    i = pl.program_id(1)
    n_left = BAND // A_QBLK
    n_sub = A_WIN // A_QBLK
    blk0 = jnp.maximum(i - n_left, 0)
    off = pl.multiple_of(jnp.maximum(n_left - i, 0) * A_QBLK, A_QBLK)
    q = q_ref[0]
    lane = lax.broadcasted_iota(jnp.int32, (A_QBLK, LANES), 1)

    def scores(h):
        cols = slice(h // 2 * LANES, (h // 2 + 1) * LANES)
        qp = q[:, cols]
        qh = jnp.where((lane >= HEAD_DIM) == bool(h % 2), qp, jnp.zeros_like(qp))
        return [_dot_nt(k_ref[0, pl.ds(pl.multiple_of((blk0 + c) * A_QBLK, A_QBLK), A_QBLK), cols], qh)
                + tab_ref[h, pl.ds(off + c * A_QBLK, A_QBLK), :] for c in range(n_sub)]

    def attend(h, ss):
        m = _col_reduce(ss[0], jnp.maximum, jnp.max)
        for s in ss[1:]:
            m = jnp.maximum(m, _col_reduce(s, jnp.maximum, jnp.max))
        o_t = None
        for c in range(n_sub):
            p = jnp.exp2(ss[c] - m).astype(BF16)
            os_ = jnp.dot(vt_ref[0, blk0 + c, h // 2], p, preferred_element_type=F32)
            o_t = os_ if o_t is None else o_t + os_
        return o_t[h % 2 * HEAD_DIM:(h % 2 + 1) * HEAD_DIM] / o_t[LANES:LANES + 1]

    halves = []
    ss = scores(0)
    for h in range(N_HEADS_A):
        ss_next = scores(h + 1) if h + 1 < N_HEADS_A else None
        halves.append(attend(h, ss))
        ss = ss_next
    o_ref[0] = jnp.concatenate(halves, axis=0).astype(BF16)


def _attn_a_prompt(aq, ak, avt, table_t):
    b, l, _ = aq.shape
    n_kb = avt.shape[1]
    return pl.pallas_call(
        _attn_a_prompt_kernel,
        grid=(b, l // A_QBLK),
        in_specs=[pl.BlockSpec((1, A_QBLK, A_WIDTH), lambda bb, i: (bb, i, 0)),
                  pl.BlockSpec((1, l, A_WIDTH), lambda bb, i: (bb, 0, 0)),
                  pl.BlockSpec((1,) + avt.shape[1:], lambda bb, i: (bb, 0, 0, 0, 0)),
                  pl.BlockSpec(table_t.shape, lambda bb, i: (0, 0, 0))],
        out_specs=pl.BlockSpec((1, A_WIDTH, A_QBLK), lambda bb, i: (bb, 0, i)),
        out_shape=jax.ShapeDtypeStruct((b, A_WIDTH, l), BF16),
        compiler_params=_cparams(("arbitrary", "arbitrary")),
        name="attn_a_prompt",
    )(aq, ak, avt, table_t)


def _attn_a_sample_kernel(q_ref, kc_ref, vc_ref, k_ref, v_ref, tab_ref, o_ref):
    q = q_ref[0]
    ko = k_ref[0]
    vo = v_ref[0]
    a_len = kc_ref.shape[1] // N_HEADS_A
    lq = q.shape[0]
    c0 = BAND - a_len
    outs = []
    for h in range(N_HEADS_A):
        sl = slice(h * HEAD_DIM, (h + 1) * HEAD_DIM)
        kc = kc_ref[0, pl.ds(h, a_len, stride=N_HEADS_A), :].astype(BF16)
        vc = vc_ref[0, pl.ds(h, a_len, stride=N_HEADS_A), :].astype(BF16)
        s_c = _dot_nt(q[:, sl], kc) + tab_ref[h, :, c0:BAND]
        s_o = _dot_nt(q[:, sl], ko[:, sl]) + tab_ref[h, :, BAND:BAND + lq]
        outs.append(_softmax_pv([s_c, s_o], [vc, vo[:, sl]], exp_fn=jnp.exp2))
    o_ref[0] = jnp.concatenate(outs, axis=1).astype(BF16)


def _attn_a_sample(aq, cache_k, cache_v, ak, av, table):
    b, l, _ = aq.shape
    bs = lambda n: pl.BlockSpec((1, n, A_WIDTH), lambda bb: (bb, 0, 0))
    cache = pl.BlockSpec((1,) + cache_k.shape[1:], lambda bb: (bb, 0, 0))
    return pl.pallas_call(
        _attn_a_sample_kernel,
        grid=(b,),
        in_specs=[bs(l), cache, cache, bs(l), bs(l),
                  pl.BlockSpec(table.shape, lambda bb: (0, 0, 0))],
        out_specs=bs(l),
        out_shape=jax.ShapeDtypeStruct((b, l, A_WIDTH), BF16),
        compiler_params=_cparams(("arbitrary",)),
        name="attn_a_sample",
    )(aq, cache_k, cache_v, ak, av, table)


def _subnorm(o, g):
    return _rms(o, g) * (1.0 - LAM_INIT)


def _col_reduce(x, op, reduce_fn):
    n = x.shape[0]
    while n > 8:
        n //= 2
        x = op(x[:n], x[n:])
    return reduce_fn(x, axis=0, keepdims=True)


def _attn_b_prompt_kernel(lam_ref, q_ref, qn_ref, k_ref, vt_ref, mask_ref, g_ref, o_ref,
                          sa_sc, sb_sc, m_sc, acc_sc):
    i = pl.program_id(2)
    tk = vt_ref.shape[4]
    lane = lax.broadcasted_iota(jnp.int32, q_ref.shape[1:], 1)

    def split_maps(q):
        zero = jnp.zeros_like(q)
        return jnp.where(lane < HEAD_DIM, q, zero), jnp.where(lane >= HEAD_DIM, q, zero)

    qs = split_maps(q_ref[0])
    m_sc[...] = jnp.full(m_sc.shape, NEG_INF, F32)
    acc_sc[...] = jnp.zeros(acc_sc.shape, F32)

    def scores(j, s_sc, qmaps=qs):
        kb = k_ref[0, pl.ds(pl.multiple_of(j * tk, tk), tk), :]
        for a in range(2):
            s_sc[a] = _dot_nt(kb, qmaps[a])

    def consume(j, s_sc, mask):
        vt = vt_ref[0, j, 0]
        for a in range(2):
            s = s_sc[a]
            if mask is not None:
                s = s + mask
            m_old = m_sc[a]
            m_new = jnp.maximum(m_old, _col_reduce(s, jnp.maximum, jnp.max))
            alpha = jnp.exp2(m_old - m_new)
            p = jnp.exp2(s - m_new).astype(BF16)
            m_sc[a] = m_new
            acc_sc[a] = alpha * acc_sc[a] + jnp.dot(vt, p, preferred_element_type=F32)

    @pl.when(i == 0)
    def _():
        scores(0, sa_sc)

    def body(jj, c):
        j = 2 * jj
        scores(j + 1, sb_sc)
        consume(j, sa_sc, None)
        scores(j + 2, sa_sc)
        consume(j + 1, sb_sc, None)
        return c

    lax.fori_loop(0, i, body, 0)
    scores(2 * i + 1, sb_sc)
    consume(2 * i, sa_sc, mask_ref[0])
    scores(0, sa_sc, split_maps(qn_ref[0]))
    consume(2 * i + 1, sb_sc, mask_ref[1])
    o_t = (acc_sc[0, 0:LANES] / acc_sc[0, LANES:LANES + 1]
           - lam_ref[0] * (acc_sc[1, 0:LANES] / acc_sc[1, LANES:LANES + 1]))
    inv = lax.rsqrt(jnp.mean(o_t * o_t, axis=0, keepdims=True) + RMS_EPS)
    o_ref[0, 0] = (o_t * inv * (g_ref[...] * (1.0 - LAM_INIT))).astype(BF16)


def _chunk_causal_masks_t(tk, tq):
    kc = (np.arange(tq // tk)[:, None, None] * tk + np.arange(tk)[None, :, None]) // CHUNK
    qc = np.arange(tq)[None, None, :] // CHUNK
    return jnp.asarray(np.where(kc <= qc, 0.0, NEG_INF), F32)


def _attn_b_prompt(lam, bq, bk, bvt, g_sub):
    b, l, _ = bq.shape
    tq = B_QBLK
    tk = B_KBLK
    hw = 2 * HEAD_DIM
    n_kb = bvt.shape[1]
    return pl.pallas_call(
        _attn_b_prompt_kernel,
        grid_spec=pltpu.PrefetchScalarGridSpec(
            num_scalar_prefetch=1,
            grid=(b, N_HEADS_B, l // tq),
            in_specs=[pl.BlockSpec((1, tq, hw), lambda bb, h, i, lam_: (bb, i, h)),
                      pl.BlockSpec((1, tq, hw), lambda bb, h, i, lam_: (bb, jnp.minimum(i + 1, l // tq - 1), h)),
                      pl.BlockSpec((1, l, hw), lambda bb, h, i, lam_: (bb, 0, h)),
                      pl.BlockSpec((1, n_kb, 1, hw + VT_ONES, tk), lambda bb, h, i, lam_: (bb, 0, h, 0, 0)),
                      pl.BlockSpec((tq // tk, tk, tq), lambda bb, h, i, lam_: (0, 0, 0)),
                      pl.BlockSpec((hw, 1), lambda bb, h, i, lam_: (0, 0))],
            out_specs=pl.BlockSpec((1, 1, hw, tq), lambda bb, h, i, lam_: (bb, h, 0, i)),
            scratch_shapes=[pltpu.VMEM((2, tk, tq), F32), pltpu.VMEM((2, tk, tq), F32),
                            pltpu.VMEM((2, 1, tq), F32), pltpu.VMEM((2, hw + VT_ONES, tq), F32)]),
        out_shape=jax.ShapeDtypeStruct((b, N_HEADS_B, hw, l), BF16),
        compiler_params=_cparams(("arbitrary", "arbitrary", "arbitrary")),
        name="attn_b_prompt",
    )(lam, bq, bq, bk, bvt, _chunk_causal_masks_t(tk, tq), g_sub.reshape(hw, 1))


def _attn_b_sample_kernel(lam_ref, q_ref, kc_ref, vc_ref, k_ref, v_ref, g_ref, o_ref):
    q = q_ref[0]
    kc = kc_ref[0].astype(BF16)
    n_past = vc_ref.shape[1] // N_HEADS_B
    vc = vc_ref[0, pl.ds(pl.program_id(1), n_past, stride=N_HEADS_B), :].astype(BF16)
    ko = k_ref[0]
    vo = v_ref[0]
    outs = []
    for a in range(2):
        sl = slice(a * HEAD_DIM, (a + 1) * HEAD_DIM)
        s_c = _dot_nt(q[:, sl], kc[:, sl])
        s_o = _dot_nt(q[:, sl], ko[:, sl])
        outs.append(_softmax_pv([s_c, s_o], [vc, vo], exp_fn=jnp.exp2))
    o = outs[0] - lam_ref[0] * outs[1]
    o_ref[0] = _subnorm(o, g_ref[...]).astype(BF16)


def _attn_b_sample(lam, bq, cache_k, cache_v, bk, bv, g_sub):
    b, l, _ = bq.shape
    p = cache_k.shape[1]
    hw = 2 * HEAD_DIM
    bs = lambda n: pl.BlockSpec((1, n, hw), lambda bb, h, lam_: (bb, 0, h))
    return pl.pallas_call(
        _attn_b_sample_kernel,
        grid_spec=pltpu.PrefetchScalarGridSpec(
            num_scalar_prefetch=1,
            grid=(b, N_HEADS_B),
            in_specs=[bs(l), bs(p),
                      pl.BlockSpec((1, p * N_HEADS_B, hw), lambda bb, h, lam_: (bb, 0, 0)),
                      bs(l), bs(l),
                      pl.BlockSpec((1, hw), lambda bb, h, lam_: (0, 0))],
            out_specs=bs(l)),
        out_shape=jax.ShapeDtypeStruct((b, l, B_WIDTH), BF16),
        compiler_params=_cparams(("arbitrary", "arbitrary")),
        name="attn_b_sample",
    )(lam, bq, cache_k, cache_v, bk, bv, g_sub)


def _memkv_kernel(m_ref, g_ref, wk_ref, wv_ref, kf_ref, vf_ref, kb_ref, vb_ref):
    mn = _rms(m_ref[...], g_ref[...]).astype(BF16)
    k = jnp.dot(mn, wk_ref[...], preferred_element_type=F32)
    v = jnp.dot(mn, wv_ref[...], preferred_element_type=F32)
    kf_ref[...] = k
    vf_ref[...] = v
    kb_ref[...] = k.astype(BF16)
    vb_ref[...] = v.astype(BF16)


def _memkv(mem2d, g, wk, wv, *, tm):
    r = mem2d.shape[0]
    row = lambda i: (i, 0)
    const = lambda i: (0, 0)
    blk = pl.BlockSpec((tm, D_MODEL), row)
    wspec = pl.BlockSpec((D_MODEL, D_MODEL), const)
    f = jax.ShapeDtypeStruct((r, D_MODEL), F32)
    bf = jax.ShapeDtypeStruct((r, D_MODEL), BF16)
    return pl.pallas_call(
        _memkv_kernel,
        grid=(r // tm,),
        in_specs=[blk, pl.BlockSpec((1, D_MODEL), const), wspec, wspec],
        out_specs=[blk] * 4,
        out_shape=[f, f, bf, bf],
        compiler_params=_cparams(("arbitrary",)),
        name="memkv",
    )(mem2d, g, wk, wv)


def _post_kernel(*refs, n_real):
    h_ref, xn_ref = refs[-2:]

    @pl.when(pl.program_id(0) < n_real)
    def _():
        _post_tile(*refs[:11], h_ref, xn_ref)

    @pl.when(pl.program_id(0) >= n_real)
    def _():
        h_ref[...] = jnp.zeros(h_ref.shape, h_ref.dtype)
        xn_ref[...] = jnp.zeros(xn_ref.shape, xn_ref.dtype)


def _post_tile(x_ref, oa_ref, ob_ref, woa_ref, wob_ref, gq_ref, wq_ref, mk_ref, mv_ref, wo_ref,
               gm_ref, h_ref, xn_ref):
    tn = lambda a_t, w: lax.dot_general(a_t, w, (((0,), (0,)), ((), ())), preferred_element_type=F32)
    ob_t = ob_ref[0].reshape(B_WIDTH, ob_ref.shape[3])
    h1 = x_ref[...] + tn(oa_ref[0], woa_ref[...]) + tn(ob_t, wob_ref[...])
    xq = _rms(h1, gq_ref[...]).astype(BF16)
    q = (jnp.dot(xq, wq_ref[...], preferred_element_type=F32) * HEAD_DIM_M ** -0.5).astype(BF16)
    outs = []
    for h in range(N_HEADS_M):
        sl = slice(h * HEAD_DIM_M, (h + 1) * HEAD_DIM_M)
        s = _dot_nt(q[:, sl], mk_ref[0, :, sl])
        outs.append(_softmax_pv([s], [mv_ref[0, :, sl]]).astype(BF16))
    o = jnp.concatenate(outs, axis=1)
    h2 = h1 + jnp.dot(o, wo_ref[...], preferred_element_type=F32)
    h_ref[...] = h2
    _store_row_slabs(xn_ref, _pack_bf16_pairs(_rms(h2, gm_ref[...])))


def _post(x2d, oa, ob, woa, wob, gq, wq, mk, mv, wo, gm, *, tm, tiles_per_batch, total_rows,
          row_offset=0, into=None):
    r = x2d.shape[0]
    n_real = r // tm
    n_steps = n_real if into is not None else total_rows // tm
    mlen = mk.shape[1]
    row = lambda i: (jnp.minimum(i, n_real - 1), 0)
    out_row = lambda i: (row_offset // tm + i, 0)
    const = lambda i: (0, 0)
    memb = lambda i: (jnp.minimum(i, n_real - 1) // tiles_per_batch, 0, 0)
    gspec = pl.BlockSpec((1, D_MODEL), const)
    wspec = pl.BlockSpec((D_MODEL, D_MODEL), const)
    hspec = pl.BlockSpec((A_WIDTH, D_MODEL), const)
    obt_blk = lambda i: (jnp.minimum(i, n_real - 1) // tiles_per_batch, 0, 0,
                         jnp.minimum(i, n_real - 1) % tiles_per_batch)
    oat_blk = lambda i: (jnp.minimum(i, n_real - 1) // tiles_per_batch, 0,
                         jnp.minimum(i, n_real - 1) % tiles_per_batch)
    in_specs = [pl.BlockSpec((tm, D_MODEL), row),
                pl.BlockSpec((1, A_WIDTH, tm), oat_blk),
                pl.BlockSpec((1, N_HEADS_B, 2 * HEAD_DIM, tm), obt_blk),
                hspec, hspec, gspec, wspec,
                pl.BlockSpec((1, mlen, D_MODEL), memb), pl.BlockSpec((1, mlen, D_MODEL), memb),
                wspec, gspec]
    args = [x2d, oa, ob, woa, wob, gq, wq, mk, mv, wo, gm]
    aliases = {}
    if into is not None:
        aliases = {len(args): 0, len(args) + 1: 1}
        in_specs += [pl.BlockSpec(memory_space=pl.ANY)] * 2
        args += list(into)
    return pl.pallas_call(
        functools.partial(_post_kernel, n_real=n_real),
        grid=(n_steps,),
        in_specs=in_specs,
        out_specs=[pl.BlockSpec((tm, D_MODEL), out_row), pl.BlockSpec((tm * X_SLABS, LANES), out_row)],
        out_shape=[jax.ShapeDtypeStruct((total_rows, D_MODEL), F32),
                   jax.ShapeDtypeStruct((total_rows * X_SLABS, LANES), jnp.uint32)],
        input_output_aliases=aliases,
        compiler_params=_cparams(("arbitrary",)),
        name="post_attn",
    )(*args)


def _router_kernel(xn_ref, wr_ref, br_ref, tri_ref, idx_ref, gate_ref, rank_ref, cnt_ref, base_sc):
    @pl.when(pl.program_id(0) == 0)
    def _():
        base_sc[...] = jnp.zeros(base_sc.shape, F32)

    tm = xn_ref.shape[0] // X_SLABS
    xn = _unpack_bf16_pairs(_load_row_slabs(xn_ref, tm, X_SLABS))
    logits = _dot_nt(wr_ref[...], xn) + br_ref[...]
    iota_e = lax.broadcasted_iota(jnp.int32, (N_EXPERTS, tm), 0).astype(F32)
    work = logits
    vals, idxs, hots = [], [], []
    for _k in range(TOP_K):
        mx = work.max(axis=0, keepdims=True)
        ix = jnp.where(work == mx, iota_e, float(N_EXPERTS)).min(axis=0, keepdims=True)
        hot = iota_e == ix
        vals.append(mx)
        idxs.append(ix)
        hots.append(hot)
        work = jnp.where(hot, -jnp.inf, work)
    es = [jnp.exp(v - vals[0]) for v in vals]
    den = es[0] + es[1] + es[2] + es[3]
    cnt = (hots[0] | hots[1] | hots[2] | hots[3]).astype(F32)
    before = jnp.dot(cnt.astype(BF16), tri_ref[...], preferred_element_type=F32) + base_sc[...]
    for k in range(TOP_K):
        idx_ref[k:k + 1, :] = idxs[k].astype(jnp.int32)
        gate_ref[k:k + 1, :] = es[k] / den
        rank_ref[k:k + 1, :] = jnp.where(hots[k], before, 0.0).sum(axis=0, keepdims=True).astype(jnp.int32)
    base_sc[...] = base_sc[...] + cnt.sum(axis=1, keepdims=True)
    cnt_ref[...] = jnp.broadcast_to(base_sc[...], cnt_ref.shape).astype(jnp.int32)


def _router(xn2d, wr_t, br, tri, *, tm):
    r = xn2d.shape[0] // X_SLABS
    col = lambda i: (0, i)
    const = lambda i: (0, 0)
    return pl.pallas_call(
        _router_kernel,
        grid=(r // tm,),
        in_specs=[pl.BlockSpec((tm * X_SLABS, LANES), lambda i: (i, 0)),
                  pl.BlockSpec((N_EXPERTS, D_MODEL), const),
                  pl.BlockSpec((N_EXPERTS, 1), const),
                  pl.BlockSpec((tm, tm), const)],
        out_specs=[pl.BlockSpec((TOP_K, tm), col)] * 3 + [pl.BlockSpec((N_EXPERTS, LANES), const)],
        out_shape=[jax.ShapeDtypeStruct((TOP_K, r), jnp.int32),
                   jax.ShapeDtypeStruct((TOP_K, r), F32),
                   jax.ShapeDtypeStruct((TOP_K, r), jnp.int32),
                   jax.ShapeDtypeStruct((N_EXPERTS, LANES), jnp.int32)],
        scratch_shapes=[pltpu.VMEM((N_EXPERTS, 1), F32)],
        compiler_params=_cparams(("arbitrary",)),
        name="router",
    )(xn2d, wr_t, br, tri)


def _row_copy(src, dst, sem):
    return pltpu.make_async_copy(src, dst, sem)


def _slab(ref, row, n, lead=()):
    return ref.at[lead + (pl.ds(pl.multiple_of(row * n, n), n), slice(None))]


def _scatter_kernel(pos_ref, xn_ref, xs_ref, sem):
    tm = xn_ref.shape[0] // X_SLABS

    def start(tt, c):
        for u in range(POS_GROUP):
            for k in range(TOP_K):
                _row_copy(_slab(xn_ref, tt * POS_GROUP + u, X_SLABS),
                          _slab(xs_ref, pos_ref[0, tt, k * POS_GROUP + u], X_SLABS),
                          sem).start(priority=k % 2)
        return c

    lax.fori_loop(0, tm // POS_GROUP, start, 0)

    def wait(t, c):
        for k in range(TOP_K):
            _row_copy(_slab(xn_ref, 0, X_SLABS), _slab(xs_ref, 0, X_SLABS), sem).wait()
        return c

    lax.fori_loop(0, tm, wait, 0, unroll=4)


def _scatter(pos, xn2d, *, tm):
    r = xn2d.shape[0] // X_SLABS
    return pl.pallas_call(
        _scatter_kernel,
        grid=(r // tm,),
        in_specs=[pl.BlockSpec((1, tm // POS_GROUP, TOP_K * POS_GROUP), lambda i: (i, 0, 0),
                               memory_space=pltpu.SMEM),
                  pl.BlockSpec((tm * X_SLABS, LANES), lambda i: (i, 0))],
        out_specs=pl.BlockSpec(memory_space=pl.ANY),
        out_shape=jax.ShapeDtypeStruct((TOP_K * r * X_SLABS, LANES), jnp.uint32),
        scratch_shapes=[pltpu.SemaphoreType.DMA],
        compiler_params=_cparams(("arbitrary",)),
        name="moe_scatter",
    )(pos, xn2d)


def _experts_kernel(blk_ref, exp_ref, lo_ref, hi_ref, first_ref, newexp_ref,
                    x_ref, wgu_ref, bgu_ref, wd_ref, bd_ref, y_ref, wgu_bf, wd_bf):
    i = pl.program_id(0)

    @pl.when(newexp_ref[i] == 1)
    def _():
        wgu_bf[...] = wgu_ref[0].astype(BF16)
        wd_bf[...] = wd_ref[0].astype(BF16)

    x = _unpack_bf16_pairs(_load_row_slabs(x_ref, MOE_ROWS, X_SLABS))
    gu = jnp.dot(x, wgu_bf[...], preferred_element_type=F32) + bgu_ref[0]
    gate = jnp.minimum(gu[:, :EXPERT_FF], SWIGLU_LIMIT)
    up = jnp.clip(gu[:, EXPERT_FF:], -SWIGLU_LIMIT, SWIGLU_LIMIT)
    act = (up + 1.0) * (gate * jax.nn.sigmoid(SWIGLU_ALPHA * gate))
    y = jnp.dot(act.astype(BF16), wd_bf[...], preferred_element_type=F32) + bd_ref[0]
    row0 = blk_ref[i] * MOE_ROWS
    whole = (lo_ref[i] <= row0) & (hi_ref[i] >= row0 + MOE_ROWS)

    @pl.when(whole)
    def _():
        _store_row_slabs(y_ref, y)

    @pl.when(jnp.logical_not(whole))
    def _():
        @pl.when(first_ref[i] == 1)
        def _():
            y_ref[...] = jnp.zeros(y_ref.shape, F32)

        rows = row0 + lax.broadcasted_iota(jnp.int32, (MOE_ROWS, 1), 0)
        mine = (rows >= lo_ref[i]) & (rows < hi_ref[i])
        _store_row_slabs(y_ref, jnp.where(mine, y, _load_row_slabs(y_ref, MOE_ROWS, Y_SLABS)))


def _experts(items, xs, wgu, bgu, wd, bd):
    blk, exp, lo, hi, first = items
    newexp = jnp.concatenate([jnp.ones((1,), jnp.int32), (exp[1:] != exp[:-1]).astype(jnp.int32)])
    n_items = blk.shape[0]
    n_rows = xs.shape[0] // X_SLABS
    return pl.pallas_call(
        _experts_kernel,
        grid_spec=pltpu.PrefetchScalarGridSpec(
            num_scalar_prefetch=6,
            grid=(n_items,),
            in_specs=[pl.BlockSpec((MOE_ROWS * X_SLABS, LANES), lambda i, b, e, *_: (b[i], 0)),
                      pl.BlockSpec((1, D_MODEL, 2 * EXPERT_FF), lambda i, b, e, *_: (e[i], 0, 0)),
                      pl.BlockSpec((1, 1, 2 * EXPERT_FF), lambda i, b, e, *_: (e[i], 0, 0)),
                      pl.BlockSpec((1, EXPERT_FF, D_MODEL), lambda i, b, e, *_: (e[i], 0, 0)),
                      pl.BlockSpec((1, 1, D_MODEL), lambda i, b, e, *_: (e[i], 0, 0))],
            out_specs=pl.BlockSpec((MOE_ROWS * Y_SLABS, LANES), lambda i, b, e, *_: (b[i], 0)),
            scratch_shapes=[pltpu.VMEM((D_MODEL, 2 * EXPERT_FF), BF16), pltpu.VMEM((EXPERT_FF, D_MODEL), BF16)]),
        out_shape=jax.ShapeDtypeStruct((n_rows * Y_SLABS, LANES), F32),
        compiler_params=pltpu.CompilerParams(dimension_semantics=("arbitrary",),
                                             vmem_limit_bytes=EXPERTS_VMEM_LIMIT),
        name="moe_experts",
    )(blk, exp, lo, hi, first, newexp, xs, wgu, bgu, wd, bd)


def _work_items(counts, n_rows):
    n_blocks = n_rows // MOE_ROWS
    n_items = n_blocks + N_EXPERTS - 1
    ends = jnp.cumsum(counts)
    starts = ends - counts
    nb = jnp.where(counts > 0, (ends - 1) // MOE_ROWS - starts // MOE_ROWS + 1, 0)
    item_end = jnp.cumsum(nb)
    item_start = item_end - nb
    total = item_end[-1]
    i = jnp.arange(n_items, dtype=jnp.int32)
    e = jnp.minimum(jnp.sum(item_end[None, :] <= i[:, None], axis=1), N_EXPERTS - 1).astype(jnp.int32)
    blk = (starts[e] // MOE_ROWS + (i - item_start[e])).astype(jnp.int32)
    valid = i < total
    lo = jnp.maximum(starts[e], blk * MOE_ROWS)
    hi = jnp.minimum(ends[e], (blk + 1) * MOE_ROWS)
    last = jnp.maximum(total - 1, 0)
    blk = jnp.where(valid, blk, blk[last])
    e = jnp.where(valid, e, e[last])
    lo = jnp.where(valid, lo, 0).astype(jnp.int32)
    hi = jnp.where(valid, hi, 0).astype(jnp.int32)
    prev = jnp.concatenate([jnp.full((1,), -1, jnp.int32), blk[:-1]])
    first = (blk != prev).astype(jnp.int32)
    return blk, e, lo, hi, first, starts


def _combine_kernel(pos_ref, pos_next_ref, h_ref, gate_ref, g_ref, ys_ref, o_first_ref, o_rest_ref, buf, sem,
                    *, n_first):
    tm = h_ref.shape[0]
    i = pl.program_id(0)
    n = pl.num_programs(0)
    cur = i % 2

    def fetch(p_ref, slot):
        def start(tt, c):
            for u in range(POS_GROUP):
                for k in range(TOP_K):
                    _row_copy(_slab(ys_ref, p_ref[0, tt, k * POS_GROUP + u], Y_SLABS),
                              _slab(buf, tt * POS_GROUP + u, Y_SLABS, (slot, k)),
                              sem.at[slot]).start(priority=k % 2)
            return c

        lax.fori_loop(0, tm // POS_GROUP, start, 0)

    @pl.when(i == 0)
    def _():
        fetch(pos_ref, 0)

    for slot in range(2):
        @pl.when((i + 1 < n) & (cur != slot))
        def _():
            fetch(pos_next_ref, slot)

    def wait(t, c):
        for k in range(TOP_K):
            _row_copy(_slab(ys_ref, 0, Y_SLABS), _slab(buf, 0, Y_SLABS, (cur, 0)), sem.at[cur]).wait()
        return c

    lax.fori_loop(0, tm, wait, 0, unroll=4)
    acc = h_ref[...]
    for k in range(TOP_K):
        acc = acc + gate_ref[:, k:k + 1] * _load_row_slabs(buf, tm, Y_SLABS, (cur, k))
    res = _rms(acc, g_ref[...])

    @pl.when(i < n_first)
    def _():
        o_first_ref[...] = res

    @pl.when(i >= n_first)
    def _():
        o_rest_ref[...] = res


def _combine(pos, h2d, gates_t, g_final, ys, *, tm, rows_first):
    r = h2d.shape[0]
    n_tiles = r // tm
    n_first = rows_first // tm
    row = lambda i: (i, 0)
    return pl.pallas_call(
        functools.partial(_combine_kernel, n_first=n_first),
        grid=(n_tiles,),
        in_specs=[pl.BlockSpec((1, tm // POS_GROUP, TOP_K * POS_GROUP), lambda i: (i, 0, 0),
                               memory_space=pltpu.SMEM),
                  pl.BlockSpec((1, tm // POS_GROUP, TOP_K * POS_GROUP),
                               lambda i: (jnp.minimum(i + 1, n_tiles - 1), 0, 0), memory_space=pltpu.SMEM),
                  pl.BlockSpec((tm, D_MODEL), row),
                  pl.BlockSpec((tm, TOP_K), row),
                  pl.BlockSpec((1, D_MODEL), lambda i: (0, 0)),
                  pl.BlockSpec(memory_space=pl.ANY)],
        out_specs=[pl.BlockSpec((tm, D_MODEL), lambda i: (jnp.minimum(i, n_first - 1), 0)),
                   pl.BlockSpec((tm, D_MODEL), lambda i: (jnp.maximum(i - n_first, 0), 0))],
        out_shape=[jax.ShapeDtypeStruct((rows_first, D_MODEL), F32),
                   jax.ShapeDtypeStruct((r - rows_first, D_MODEL), F32)],
        scratch_shapes=[pltpu.VMEM((2, TOP_K, tm * Y_SLABS, LANES), F32), pltpu.SemaphoreType.DMA((2,))],
        compiler_params=_cparams(("arbitrary",)),
        name="moe_combine",
    )(pos, pos, h2d, gates_t, g_final, ys)


def _moe(h2d, xn2d, w, rows_first):
    r = h2d.shape[0]
    tm_router = _pick_tile(r, 512)
    tm_scatter = _pick_tile(r, 512)
    tm_combine = _pick_tile(r, 256)
    idx, gates, rank, cnt = _router(xn2d, w["wr_t"], w["br"], w["tri"][:tm_router, :tm_router], tm=tm_router)
    counts = cnt[:, 0]
    blk, e, lo, hi, first, starts = _work_items(counts, TOP_K * r)
    hot = idx[:, :, None] == jnp.arange(N_EXPERTS, dtype=jnp.int32)
    pos = jnp.sum(jnp.where(hot, starts.astype(jnp.int32), 0), axis=-1) + rank
    tiles = lambda tm: (pos.reshape(TOP_K, -1, tm // POS_GROUP, POS_GROUP).transpose(1, 2, 0, 3)
                        .reshape(-1, tm // POS_GROUP, TOP_K * POS_GROUP))
    xs = _scatter(tiles(tm_scatter), xn2d, tm=tm_scatter)
    ys = _experts((blk, e, lo, hi, first), xs, w["wgu"], w["bgu"], w["wd"], w["bd"])
    return _combine(tiles(tm_combine), h2d, gates.T, w["g_final"], ys, tm=tm_combine, rows_first=rows_first)


def _pick_tile(n, cap):
    t = cap
    while n % t:
        t //= 2
    return t


def kernel(x_prompt, x_sample, mem_prompt, cache_a_k, cache_a_v, cache_b_k, cache_b_v, cache_mem_k, cache_mem_v, g_mix, w_in, rel_bias, lam_q1, lam_k1, lam_q2, lam_k2, g_subln, w_out, g_xq, g_xmem, w_xq, w_xk, w_xv, w_xo, g_moe, w_router, b_router, w_gate_up, b_gate_up, w_down, b_down, g_final):
    assert g_mix.shape[0] == 1, "single-layer kernel"
    b, s, d = x_prompt.shape
    db, dl, _ = x_sample.shape
    past = cache_b_k.shape[2]
    a_len = cache_a_k.shape[2]
    mlen = mem_prompt.shape[1]
    keep_a = min(BAND, s)
    assert d == D_MODEL and s % B_QBLK == 0 and s >= A_WIN and dl == CHUNK and past % CHUNK == 0
    assert B_QBLK == 2 * B_KBLK

    row1 = lambda v: v.reshape(1, -1).astype(F32)
    w_in_bf = w_in[0].astype(BF16)
    w_out_bf = w_out[0].astype(BF16)
    weights = dict(
        wr_t=w_router[0].T.astype(BF16), br=b_router[0].reshape(N_EXPERTS, 1).astype(F32),
        tri=jnp.asarray(np.triu(np.ones((512, 512), np.float32), 1), BF16),
        wgu=w_gate_up[0].astype(F32), bgu=b_gate_up[0].reshape(N_EXPERTS, 1, -1).astype(F32),
        wd=w_down[0].astype(F32), bd=b_down[0].reshape(N_EXPERTS, 1, -1).astype(F32),
        g_final=row1(g_final))
    lam = (jnp.exp(jnp.sum(lam_q1[0].astype(F32) * lam_k1[0].astype(F32)))
           - jnp.exp(jnp.sum(lam_q2[0].astype(F32) * lam_k2[0].astype(F32))) + LAM_INIT).reshape(1)
    table = _band_table(rel_bias[0])
    g_sub = row1(g_subln[0])

    tm_p = _pick_tile(s, 512)
    tabs_p = _rope_tables(np.arange(s))
    p_in = _inproj(x_prompt.reshape(b * s, d), row1(g_mix[0]), w_in_bf, *tabs_p,
                   tm=tm_p, n_pos_blocks=s // tm_p, tail_div=s // tm_p if keep_a == tm_p else 1, emit_vt=True)
    tabs_s = _rope_tables(np.tile(past + np.arange(dl), db))
    s_in = _inproj(x_sample.reshape(db * dl, d), row1(g_mix[0]), w_in_bf, *tabs_s,
                   tm=db * dl, n_pos_blocks=1, tail_div=1, emit_vt=False)

    def split(outs, nb, nl):
        return [o.reshape(nb, nl, A_WIDTH) for o in outs[:6]]

    aq, ak, _, bq, bk, _ = split(p_in, b, s)
    avt = p_in[10].reshape((b, s // A_QBLK) + p_in[10].shape[1:])
    bvt = p_in[11].reshape((b, s // B_KBLK) + p_in[11].shape[1:])
    saq, sak, sav, sbq, sbk, sbv = split(s_in, db, dl)

    oa = _attn_a_prompt(aq, ak, avt, table.transpose(0, 2, 1))
    ob = _attn_b_prompt(lam, bq, bk, bvt, g_sub)
    soa = _attn_a_sample(saq, cache_a_k[0].reshape(db, a_len * N_HEADS_A, HEAD_DIM),
                         cache_a_v[0].reshape(db, a_len * N_HEADS_A, HEAD_DIM), sak, sav, table[:, :dl, :BAND + dl])
    sob = _attn_b_sample(lam, sbq, cache_b_k[0].reshape(db, past, B_WIDTH),
                         cache_b_v[0].reshape(db, past * N_HEADS_B, 2 * HEAD_DIM), sbk, sbv, g_sub)

    mkf, mvf, mkb, mvb = _memkv(mem_prompt.reshape(b * mlen, d), row1(g_xmem[0]),
                                w_xk[0].astype(BF16), w_xv[0].astype(BF16), tm=mlen)
    post_w = (w_out_bf[:A_WIDTH], w_out_bf[A_WIDTH:], row1(g_xq[0]), w_xq[0].astype(BF16))
    tm_post = _pick_tile(math.gcd(s, b * s + db * dl), 256)
    rows_p, rows_s = b * s, db * dl
    sob_t = sob.transpose(0, 2, 1).reshape(db, N_HEADS_B, 2 * HEAD_DIM, dl)
    h_xn = _post(x_prompt.reshape(rows_p, d), oa, ob,
                 *post_w, mkb.reshape(b, mlen, d), mvb.reshape(b, mlen, d),
                 w_xo[0].astype(BF16), row1(g_moe[0]), tm=tm_post, tiles_per_batch=s // tm_post,
                 total_rows=rows_p + rows_s)
    h_all, xn_all = _post(x_sample.reshape(rows_s, d), soa.transpose(0, 2, 1), sob_t,
                          *post_w, cache_mem_k[0].reshape(db, mlen, d).astype(BF16),
                          cache_mem_v[0].reshape(db, mlen, d).astype(BF16),
                          w_xo[0].astype(BF16), row1(g_moe[0]), tm=dl, tiles_per_batch=1,
                          total_rows=rows_p + rows_s, row_offset=rows_p, into=h_xn)

    y_prompt, y_sample = _moe(h_all, xn_all, weights, rows_p)
    y_prompt = y_prompt.reshape(b, s, d)
    y_sample = y_sample.reshape(db, dl, d)

    def heads(x2d, nb, nl, nh):
        return x2d.reshape(1, nb, nl, nh, -1)

    akf, avf, bkf, bvf = p_in[6:10]
    if keep_a != tm_p:
        akf = akf.reshape(b, s, A_WIDTH)[:, s - keep_a:]
        avf = avf.reshape(b, s, A_WIDTH)[:, s - keep_a:]
    sakf, savf, sbkf, sbvf = s_in[6:]
    return (y_prompt, y_sample,
            heads(akf, b, keep_a, N_HEADS_A), heads(avf, b, keep_a, N_HEADS_A),
            heads(bkf, b, s, 2 * N_HEADS_B), heads(bvf, b, s, N_HEADS_B),
            heads(mkf, b, mlen, N_HEADS_M), heads(mvf, b, mlen, N_HEADS_M),
            heads(sakf, db, dl, N_HEADS_A), heads(savf, db, dl, N_HEADS_A),
            heads(sbkf, db, dl, 2 * N_HEADS_B), heads(sbvf, db, dl, N_HEADS_B))
```

```python
import functools
import math

import numpy as np
import jax
import jax.numpy as jnp
from jax import lax
from jax.experimental import pallas as pl
from jax.experimental.pallas import tpu as pltpu

F32 = jnp.float32
BF16 = jnp.bfloat16

D_MODEL = 1024
CHUNK = 64
LEFT_CHUNKS = 8
BAND = LEFT_CHUNKS * CHUNK
HEAD_DIM = 64
A_WIDTH = 512
B_WIDTH = 512
N_HEADS_A = 8
N_HEADS_B = 4
REL_CLIP = 128
ROT_DIM = 16
ROPE_THETA = 500000.0
N_HEADS_M = 4
HEAD_DIM_M = 256
N_EXPERTS = 32
TOP_K = 4
EXPERT_FF = 1024
SWIGLU_LIMIT = 7.0
SWIGLU_ALPHA = 1.702
RMS_EPS = 1e-5
NEG_INF = -1e30
LAM_INIT = 0.8 - 0.6 * math.exp(-0.3 * 0)
LOG2E = math.log2(math.e)

LANES = 128
A_QBLK = 4 * CHUNK
A_WIN = BAND + A_QBLK
B_QBLK = 512
B_KBLK = 256
VT_ONES = 16
POS_GROUP = 16
MOE_ROWS = 256
X_SLABS = D_MODEL // 2 // LANES
Y_SLABS = D_MODEL // LANES
VMEM_LIMIT = 48 * 1024 * 1024
EXPERTS_VMEM_LIMIT = 56 * 1024 * 1024


def _cparams(sem):
    return pltpu.CompilerParams(dimension_semantics=sem, vmem_limit_bytes=VMEM_LIMIT)


def _rms(x, g):
    return x * lax.rsqrt(jnp.mean(x * x, axis=-1, keepdims=True) + RMS_EPS) * g


HI16 = 0xFFFF0000


def _pack_bf16_pairs(x):
    n = x.shape[1] // 2
    bits = lambda v: lax.bitcast_convert_type(v.astype(BF16).astype(F32), jnp.uint32)
    return (bits(x[:, :n]) >> 16) | (bits(x[:, n:]) & jnp.uint32(HI16))


def _unpack_bf16_pairs(w):
    lo = lax.bitcast_convert_type(w << 16, F32)
    hi = lax.bitcast_convert_type(w & jnp.uint32(HI16), F32)
    return jnp.concatenate([lo, hi], axis=1).astype(BF16)


def _store_row_slabs(ref, x, lead=()):
    m, n = x.shape[0], x.shape[1] // LANES
    for c in range(n):
        ref[lead + (pl.ds(c, m, stride=n), slice(None))] = x[:, c * LANES:(c + 1) * LANES]


def _load_row_slabs(ref, m, n, lead=()):
    return jnp.concatenate([ref[lead + (pl.ds(c, m, stride=n), slice(None))] for c in range(n)], axis=1)


def _dot_nt(a, b):
    return lax.dot_general(a, b, (((1,), (1,)), ((), ())), preferred_element_type=F32)


def _store_heads64(ref, x):
    m, n_heads = x.shape[0], x.shape[1] // HEAD_DIM
    for c in range(x.shape[1] // LANES):
        grp = x[:, c * LANES:(c + 1) * LANES]
        ref[pl.ds(2 * c, m, stride=n_heads), :] = grp[:, :HEAD_DIM]
        ref[pl.ds(2 * c + 1, m, stride=n_heads), :] = pltpu.roll(grp, HEAD_DIM, 1)[:, :HEAD_DIM]


def _inproj_kernel(x_ref, g_ref, w_ref, cos_ref, sa_ref, sb_ref,
                   aq_ref, ak_ref, av_ref, bq_ref, bk_ref, bv_ref,
                   akf_ref, avf_ref, bkf_ref, bvf_ref, *maybe_vt_refs, tail_div):
    xn = _rms(x_ref[...], g_ref[...]).astype(BF16)
    is_tail = pl.program_id(0) % tail_div == tail_div - 1

    def seg(s):
        return jnp.dot(xn, w_ref[:, s * A_WIDTH:(s + 1) * A_WIDTH], preferred_element_type=F32)

    def rope(z):
        cols = []
        for c in range(z.shape[1] // LANES):
            zc = z[:, c * LANES:(c + 1) * LANES]
            cols.append(zc * cos_ref[...] + pltpu.roll(zc, 8, 1) * sa_ref[...]
                        + pltpu.roll(zc, LANES - 8, 1) * sb_ref[...])
        return jnp.concatenate(cols, axis=1)

    scale = HEAD_DIM ** -0.5
    aq_ref[...] = (seg(0) * (scale * LOG2E)).astype(BF16)
    ak = seg(1)
    ak_ref[...] = ak.astype(BF16)
    av = seg(2)
    av_ref[...] = av.astype(BF16)
    bq_ref[...] = (rope(seg(3)) * (scale * LOG2E)).astype(BF16)
    bk = rope(seg(4))
    bk_ref[...] = bk.astype(BF16)
    _store_heads64(bkf_ref, bk)
    bv = seg(5)
    bv_ref[...] = bv.astype(BF16)
    _store_row_slabs(bvf_ref, bv)
    for vt_ref, v in zip(maybe_vt_refs, (av, bv)):
        vt = v.T.astype(BF16)
        blk = vt_ref.shape[3]
        for c in range(vt_ref.shape[0]):
            for g in range(vt_ref.shape[1]):
                vt_ref[c, g, 0:LANES, :] = vt[g * LANES:(g + 1) * LANES, c * blk:(c + 1) * blk]
                vt_ref[c, g, LANES:LANES + VT_ONES, :] = jnp.ones((VT_ONES, blk), BF16)

    @pl.when(is_tail)
    def _():
        _store_heads64(akf_ref, ak)
        _store_heads64(avf_ref, av)


def _inproj(x2d, g, w_bf, cos_t, sa_t, sb_t, *, tm, n_pos_blocks, tail_div, emit_vt):
    r = x2d.shape[0]
    n_tiles = r // tm
    row = lambda i: (i, 0)
    const = lambda i: (0, 0)
    pos = lambda i: (i % n_pos_blocks, 0)
    tail = lambda i: (i // tail_div, 0)
    bf = jax.ShapeDtypeStruct((r, A_WIDTH), BF16)
    blk = lambda im: pl.BlockSpec((tm, A_WIDTH), im)
    h64 = lambda im: pl.BlockSpec((tm * N_HEADS_A, HEAD_DIM), im)
    f_tail = jax.ShapeDtypeStruct((r // tail_div * N_HEADS_A, HEAD_DIM), F32)
    out_specs = [blk(row)] * 6 + [h64(tail), h64(tail), h64(row),
                                  pl.BlockSpec((tm * N_HEADS_B, 2 * HEAD_DIM), row)]
    out_shape = [bf] * 6 + [f_tail, f_tail,
                            jax.ShapeDtypeStruct((r * 2 * N_HEADS_B, HEAD_DIM), F32),
                            jax.ShapeDtypeStruct((r * N_HEADS_B, 2 * HEAD_DIM), F32)]
    if emit_vt:
        for kb in (A_QBLK, B_KBLK):
            grp = (A_WIDTH // LANES, LANES + VT_ONES, kb)
            out_specs.append(pl.BlockSpec((tm // kb,) + grp, lambda i: (i, 0, 0, 0)))
            out_shape.append(jax.ShapeDtypeStruct((r // kb,) + grp, BF16))
    return pl.pallas_call(
        functools.partial(_inproj_kernel, tail_div=tail_div),
        grid=(n_tiles,),
        in_specs=[pl.BlockSpec((tm, D_MODEL), row), pl.BlockSpec((1, D_MODEL), const),
                  pl.BlockSpec((D_MODEL, 3 * D_MODEL), const),
                  pl.BlockSpec((tm, LANES), pos), pl.BlockSpec((tm, LANES), pos),
                  pl.BlockSpec((tm, LANES), pos)],
        out_specs=out_specs,
        out_shape=out_shape,
        compiler_params=_cparams(("arbitrary",)),
        name="inproj",
    )(x2d, g, w_bf, cos_t, sa_t, sb_t)


def _rope_tables(positions):
    half = ROT_DIM // 2
    lane = np.arange(LANES) % HEAD_DIM
    inv_freq = np.power(ROPE_THETA, -np.arange(half, dtype=np.float64) / half)
    ang = np.asarray(positions, np.float64)[:, None] * inv_freq[None, :]
    cos = np.cos(ang)[:, lane % half]
    sin = np.sin(ang)[:, lane % half]
    in_rot = (lane < ROT_DIM)[None, :]
    first = (lane < half)[None, :]
    cos_t = np.where(in_rot, cos, 1.0)
    sa_t = np.where(in_rot & ~first, sin, 0.0)
    sb_t = np.where(first, -sin, 0.0)
    return tuple(jnp.asarray(t, F32) for t in (cos_t, sa_t, sb_t))


def _band_table(rel_bias):
    r = np.arange(A_QBLK)[:, None]
    j = np.arange(BAND + A_WIN)[None, :]
    dchunk = LEFT_CHUNKS + r // CHUNK - j // CHUNK
    allowed = (dchunk >= 0) & (dchunk <= LEFT_CHUNKS)
    bias = rel_bias.astype(F32) * LOG2E
    n_seq = BAND + A_WIN + A_QBLK - 1
    n_hi = BAND + A_QBLK - 1 - REL_CLIP
    seq = jnp.concatenate([jnp.broadcast_to(bias[:, -1:], (N_HEADS_A, n_hi)), bias[:, ::-1],
                           jnp.broadcast_to(bias[:, :1], (N_HEADS_A, n_seq - n_hi - bias.shape[1]))], axis=1)
    seq = jnp.roll(seq, -(A_QBLK - 1), axis=1)
    toe = jnp.tile(seq, (1, A_QBLK))[:, :A_QBLK * (n_seq - 1)].reshape(N_HEADS_A, A_QBLK, n_seq - 1)
    return jnp.where(jnp.asarray(allowed)[None], toe[:, :, :BAND + A_WIN], NEG_INF)


def _softmax_pv(s_list, v_list, exp_fn=jnp.exp):
    m = s_list[0].max(axis=-1, keepdims=True)
    for s in s_list[1:]:
        m = jnp.maximum(m, s.max(axis=-1, keepdims=True))
    l = None
    o = None
    for s, v in zip(s_list, v_list):
        p = exp_fn(s - m)
        ls = p.sum(axis=-1, keepdims=True)
        os_ = jnp.dot(p.astype(BF16), v, preferred_element_type=F32)
        l = ls if l is None else l + ls
        o = os_ if o is None else o + os_
    return o / l


def _attn_a_prompt_kernel(q_ref, k_ref, vt_ref, tab_ref, o_ref):
    i = pl.program_id(1)
    n_left = BAND // A_QBLK
    n_sub = A_WIN // A_QBLK
    blk0 = jnp.maximum(i - n_left, 0)
    off = pl.multiple_of(jnp.maximum(n_left - i, 0) * A_QBLK, A_QBLK)
    q = q_ref[0]
    lane = lax.broadcasted_iota(jnp.int32, (A_QBLK, LANES), 1)

    def scores(h):
        cols = slice(h // 2 * LANES, (h // 2 + 1) * LANES)
        qp = q[:, cols]
        qh = jnp.where((lane >= HEAD_DIM) == bool(h % 2), qp, jnp.zeros_like(qp))
        return [_dot_nt(k_ref[0, pl.ds(pl.multiple_of((blk0 + c) * A_QBLK, A_QBLK), A_QBLK), cols], qh)
                + tab_ref[h, pl.ds(off + c * A_QBLK, A_QBLK), :] for c in range(n_sub)]

    def attend(h, ss):
        m = _col_reduce(ss[0], jnp.maximum, jnp.max)
        for s in ss[1:]:
            m = jnp.maximum(m, _col_reduce(s, jnp.maximum, jnp.max))
        o_t = None
        for c in range(n_sub):
            p = jnp.exp2(ss[c] - m).astype(BF16)
            os_ = jnp.dot(vt_ref[0, blk0 + c, h // 2], p, preferred_element_type=F32)
            o_t = os_ if o_t is None else o_t + os_
        return o_t[h % 2 * HEAD_DIM:(h % 2 + 1) * HEAD_DIM] / o_t[LANES:LANES + 1]

    halves = []
    ss = scores(0)
    for h in range(N_HEADS_A):
        ss_next = scores(h + 1) if h + 1 < N_HEADS_A else None
        halves.append(attend(h, ss))
        ss = ss_next
    o_ref[0] = jnp.concatenate(halves, axis=0).astype(BF16)


def _attn_a_prompt(aq, ak, avt, table_t):
    b, l, _ = aq.shape
    n_kb = avt.shape[1]
    return pl.pallas_call(
        _attn_a_prompt_kernel,
        grid=(b, l // A_QBLK),
        in_specs=[pl.BlockSpec((1, A_QBLK, A_WIDTH), lambda bb, i: (bb, i, 0)),
                  pl.BlockSpec((1, l, A_WIDTH), lambda bb, i: (bb, 0, 0)),
                  pl.BlockSpec((1,) + avt.shape[1:], lambda bb, i: (bb, 0, 0, 0, 0)),
                  pl.BlockSpec(table_t.shape, lambda bb, i: (0, 0, 0))],
        out_specs=pl.BlockSpec((1, A_WIDTH, A_QBLK), lambda bb, i: (bb, 0, i)),
        out_shape=jax.ShapeDtypeStruct((b, A_WIDTH, l), BF16),
        compiler_params=_cparams(("arbitrary", "arbitrary")),
        name="attn_a_prompt",
    )(aq, ak, avt, table_t)


def _attn_a_sample_kernel(q_ref, kc_ref, vc_ref, k_ref, v_ref, tab_ref, o_ref):
    q = q_ref[0]
    ko = k_ref[0]
    vo = v_ref[0]
    a_len = kc_ref.shape[1] // N_HEADS_A
    lq = q.shape[0]
    c0 = BAND - a_len
    outs = []
    for h in range(N_HEADS_A):
        sl = slice(h * HEAD_DIM, (h + 1) * HEAD_DIM)
        kc = kc_ref[0, pl.ds(h, a_len, stride=N_HEADS_A), :].astype(BF16)
        vc = vc_ref[0, pl.ds(h, a_len, stride=N_HEADS_A), :].astype(BF16)
        s_c = _dot_nt(q[:, sl], kc) + tab_ref[h, :, c0:BAND]
        s_o = _dot_nt(q[:, sl], ko[:, sl]) + tab_ref[h, :, BAND:BAND + lq]
        outs.append(_softmax_pv([s_c, s_o], [vc, vo[:, sl]], exp_fn=jnp.exp2))
    o_ref[0] = jnp.concatenate(outs, axis=1).astype(BF16)


def _attn_a_sample(aq, cache_k, cache_v, ak, av, table):
    b, l, _ = aq.shape
    bs = lambda n: pl.BlockSpec((1, n, A_WIDTH), lambda bb: (bb, 0, 0))
    cache = pl.BlockSpec((1,) + cache_k.shape[1:], lambda bb: (bb, 0, 0))
    return pl.pallas_call(
        _attn_a_sample_kernel,
        grid=(b,),
        in_specs=[bs(l), cache, cache, bs(l), bs(l),
                  pl.BlockSpec(table.shape, lambda bb: (0, 0, 0))],
        out_specs=bs(l),
        out_shape=jax.ShapeDtypeStruct((b, l, A_WIDTH), BF16),
        compiler_params=_cparams(("arbitrary",)),
        name="attn_a_sample",
    )(aq, cache_k, cache_v, ak, av, table)


def _subnorm(o, g):
    return _rms(o, g) * (1.0 - LAM_INIT)


def _col_reduce(x, op, reduce_fn):
    n = x.shape[0]
    while n > 8:
        n //= 2
        x = op(x[:n], x[n:])
    return reduce_fn(x, axis=0, keepdims=True)


def _attn_b_prompt_kernel(lam_ref, q_ref, qn_ref, k_ref, vt_ref, mask_ref, g_ref, o_ref,
                          sa_sc, sb_sc, m_sc, acc_sc):
    i = pl.program_id(2)
    tk = vt_ref.shape[4]
    lane = lax.broadcasted_iota(jnp.int32, q_ref.shape[1:], 1)

    def split_maps(q):
        zero = jnp.zeros_like(q)
        return jnp.where(lane < HEAD_DIM, q, zero), jnp.where(lane >= HEAD_DIM, q, zero)

    qs = split_maps(q_ref[0])
    m_sc[...] = jnp.full(m_sc.shape, NEG_INF, F32)
    acc_sc[...] = jnp.zeros(acc_sc.shape, F32)

    def scores(j, s_sc, qmaps=qs):
        kb = k_ref[0, pl.ds(pl.multiple_of(j * tk, tk), tk), :]
        for a in range(2):
            s_sc[a] = _dot_nt(kb, qmaps[a])

    def consume(j, s_sc, mask):
        vt = vt_ref[0, j, 0]
        for a in range(2):
            s = s_sc[a]
            if mask is not None:
                s = s + mask
            m_old = m_sc[a]
            m_new = jnp.maximum(m_old, _col_reduce(s, jnp.maximum, jnp.max))
            alpha = jnp.exp2(m_old - m_new)
            p = jnp.exp2(s - m_new).astype(BF16)
            m_sc[a] = m_new
            acc_sc[a] = alpha * acc_sc[a] + jnp.dot(vt, p, preferred_element_type=F32)

    @pl.when(i == 0)
    def _():
        scores(0, sa_sc)

    def body(jj, c):
        j = 2 * jj
        scores(j + 1, sb_sc)
        consume(j, sa_sc, None)
        scores(j + 2, sa_sc)
        consume(j + 1, sb_sc, None)
        return c

    lax.fori_loop(0, i, body, 0)
    scores(2 * i + 1, sb_sc)
    consume(2 * i, sa_sc, mask_ref[0])
    scores(0, sa_sc, split_maps(qn_ref[0]))
    consume(2 * i + 1, sb_sc, mask_ref[1])
    o_t = (acc_sc[0, 0:LANES] / acc_sc[0, LANES:LANES + 1]
           - lam_ref[0] * (acc_sc[1, 0:LANES] / acc_sc[1, LANES:LANES + 1]))
    inv = lax.rsqrt(jnp.mean(o_t * o_t, axis=0, keepdims=True) + RMS_EPS)
    o_ref[0, 0] = (o_t * inv * (g_ref[...] * (1.0 - LAM_INIT))).astype(BF16)


def _chunk_causal_masks_t(tk, tq):
    kc = (np.arange(tq // tk)[:, None, None] * tk + np.arange(tk)[None, :, None]) // CHUNK
    qc = np.arange(tq)[None, None, :] // CHUNK
    return jnp.asarray(np.where(kc <= qc, 0.0, NEG_INF), F32)


def _attn_b_prompt(lam, bq, bk, bvt, g_sub):
    b, l, _ = bq.shape
    tq = B_QBLK
    tk = B_KBLK
    hw = 2 * HEAD_DIM
    n_kb = bvt.shape[1]
    return pl.pallas_call(
        _attn_b_prompt_kernel,
        grid_spec=pltpu.PrefetchScalarGridSpec(
            num_scalar_prefetch=1,
            grid=(b, N_HEADS_B, l // tq),
            in_specs=[pl.BlockSpec((1, tq, hw), lambda bb, h, i, lam_: (bb, i, h)),
                      pl.BlockSpec((1, tq, hw), lambda bb, h, i, lam_: (bb, jnp.minimum(i + 1, l // tq - 1), h)),
                      pl.BlockSpec((1, l, hw), lambda bb, h, i, lam_: (bb, 0, h)),
                      pl.BlockSpec((1, n_kb, 1, hw + VT_ONES, tk), lambda bb, h, i, lam_: (bb, 0, h, 0, 0)),
                      pl.BlockSpec((tq // tk, tk, tq), lambda bb, h, i, lam_: (0, 0, 0)),
                      pl.BlockSpec((hw, 1), lambda bb, h, i, lam_: (0, 0))],
            out_specs=pl.BlockSpec((1, 1, hw, tq), lambda bb, h, i, lam_: (bb, h, 0, i)),
            scratch_shapes=[pltpu.VMEM((2, tk, tq), F32), pltpu.VMEM((2, tk, tq), F32),
                            pltpu.VMEM((2, 1, tq), F32), pltpu.VMEM((2, hw + VT_ONES, tq), F32)]),
        out_shape=jax.ShapeDtypeStruct((b, N_HEADS_B, hw, l), BF16),
        compiler_params=_cparams(("arbitrary", "arbitrary", "arbitrary")),
        name="attn_b_prompt",
    )(lam, bq, bq, bk, bvt, _chunk_causal_masks_t(tk, tq), g_sub.reshape(hw, 1))


def _attn_b_sample_kernel(lam_ref, q_ref, kc_ref, vc_ref, k_ref, v_ref, g_ref, o_ref):
    q = q_ref[0]
    kc = kc_ref[0].astype(BF16)
    n_past = vc_ref.shape[1] // N_HEADS_B
    vc = vc_ref[0, pl.ds(pl.program_id(1), n_past, stride=N_HEADS_B), :].astype(BF16)
    ko = k_ref[0]
    vo = v_ref[0]
    outs = []
    for a in range(2):
        sl = slice(a * HEAD_DIM, (a + 1) * HEAD_DIM)
        s_c = _dot_nt(q[:, sl], kc[:, sl])
        s_o = _dot_nt(q[:, sl], ko[:, sl])
        outs.append(_softmax_pv([s_c, s_o], [vc, vo], exp_fn=jnp.exp2))
    o = outs[0] - lam_ref[0] * outs[1]
    o_ref[0] = _subnorm(o, g_ref[...]).astype(BF16)


def _attn_b_sample(lam, bq, cache_k, cache_v, bk, bv, g_sub):
    b, l, _ = bq.shape
    p = cache_k.shape[1]
    hw = 2 * HEAD_DIM
    bs = lambda n: pl.BlockSpec((1, n, hw), lambda bb, h, lam_: (bb, 0, h))
    return pl.pallas_call(
        _attn_b_sample_kernel,
        grid_spec=pltpu.PrefetchScalarGridSpec(
            num_scalar_prefetch=1,
            grid=(b, N_HEADS_B),
            in_specs=[bs(l), bs(p),
                      pl.BlockSpec((1, p * N_HEADS_B, hw), lambda bb, h, lam_: (bb, 0, 0)),
                      bs(l), bs(l),
                      pl.BlockSpec((1, hw), lambda bb, h, lam_: (0, 0))],
            out_specs=bs(l)),
        out_shape=jax.ShapeDtypeStruct((b, l, B_WIDTH), BF16),
        compiler_params=_cparams(("arbitrary", "arbitrary")),
        name="attn_b_sample",
    )(lam, bq, cache_k, cache_v, bk, bv, g_sub)


def _memkv_kernel(m_ref, g_ref, wk_ref, wv_ref, kf_ref, vf_ref, kb_ref, vb_ref):
    mn = _rms(m_ref[...], g_ref[...]).astype(BF16)
    k = jnp.dot(mn, wk_ref[...], preferred_element_type=F32)
    v = jnp.dot(mn, wv_ref[...], preferred_element_type=F32)
    kf_ref[...] = k
    vf_ref[...] = v
    kb_ref[...] = k.astype(BF16)
    vb_ref[...] = v.astype(BF16)


def _memkv(mem2d, g, wk, wv, *, tm):
    r = mem2d.shape[0]
    row = lambda i: (i, 0)
    const = lambda i: (0, 0)
    blk = pl.BlockSpec((tm, D_MODEL), row)
    wspec = pl.BlockSpec((D_MODEL, D_MODEL), const)
    f = jax.ShapeDtypeStruct((r, D_MODEL), F32)
    bf = jax.ShapeDtypeStruct((r, D_MODEL), BF16)
    return pl.pallas_call(
        _memkv_kernel,
        grid=(r // tm,),
        in_specs=[blk, pl.BlockSpec((1, D_MODEL), const), wspec, wspec],
        out_specs=[blk] * 4,
        out_shape=[f, f, bf, bf],
        compiler_params=_cparams(("arbitrary",)),
        name="memkv",
    )(mem2d, g, wk, wv)


def _post_kernel(*refs, n_real):
    h_ref, xn_ref = refs[-2:]

    @pl.when(pl.program_id(0) < n_real)
    def _():
        _post_tile(*refs[:11], h_ref, xn_ref)

    @pl.when(pl.program_id(0) >= n_real)
    def _():
        h_ref[...] = jnp.zeros(h_ref.shape, h_ref.dtype)
        xn_ref[...] = jnp.zeros(xn_ref.shape, xn_ref.dtype)


def _post_tile(x_ref, oa_ref, ob_ref, woa_ref, wob_ref, gq_ref, wq_ref, mk_ref, mv_ref, wo_ref,
               gm_ref, h_ref, xn_ref):
    tn = lambda a_t, w: lax.dot_general(a_t, w, (((0,), (0,)), ((), ())), preferred_element_type=F32)
    ob_t = ob_ref[0].reshape(B_WIDTH, ob_ref.shape[3])
    h1 = x_ref[...] + tn(oa_ref[0], woa_ref[...]) + tn(ob_t, wob_ref[...])
    xq = _rms(h1, gq_ref[...]).astype(BF16)
    q = (jnp.dot(xq, wq_ref[...], preferred_element_type=F32) * HEAD_DIM_M ** -0.5).astype(BF16)
    outs = []
    for h in range(N_HEADS_M):
        sl = slice(h * HEAD_DIM_M, (h + 1) * HEAD_DIM_M)
        s = _dot_nt(q[:, sl], mk_ref[0, :, sl])
        outs.append(_softmax_pv([s], [mv_ref[0, :, sl]]).astype(BF16))
    o = jnp.concatenate(outs, axis=1)
    h2 = h1 + jnp.dot(o, wo_ref[...], preferred_element_type=F32)
    h_ref[...] = h2
    _store_row_slabs(xn_ref, _pack_bf16_pairs(_rms(h2, gm_ref[...])))


def _post(x2d, oa, ob, woa, wob, gq, wq, mk, mv, wo, gm, *, tm, tiles_per_batch, total_rows,
          row_offset=0, into=None):
    r = x2d.shape[0]
    n_real = r // tm
    n_steps = n_real if into is not None else total_rows // tm
    mlen = mk.shape[1]
    row = lambda i: (jnp.minimum(i, n_real - 1), 0)
    out_row = lambda i: (row_offset // tm + i, 0)
    const = lambda i: (0, 0)
    memb = lambda i: (jnp.minimum(i, n_real - 1) // tiles_per_batch, 0, 0)
    gspec = pl.BlockSpec((1, D_MODEL), const)
    wspec = pl.BlockSpec((D_MODEL, D_MODEL), const)
    hspec = pl.BlockSpec((A_WIDTH, D_MODEL), const)
    obt_blk = lambda i: (jnp.minimum(i, n_real - 1) // tiles_per_batch, 0, 0,
                         jnp.minimum(i, n_real - 1) % tiles_per_batch)
    oat_blk = lambda i: (jnp.minimum(i, n_real - 1) // tiles_per_batch, 0,
                         jnp.minimum(i, n_real - 1) % tiles_per_batch)
    in_specs = [pl.BlockSpec((tm, D_MODEL), row),
                pl.BlockSpec((1, A_WIDTH, tm), oat_blk),
                pl.BlockSpec((1, N_HEADS_B, 2 * HEAD_DIM, tm), obt_blk),
                hspec, hspec, gspec, wspec,
                pl.BlockSpec((1, mlen, D_MODEL), memb), pl.BlockSpec((1, mlen, D_MODEL), memb),
                wspec, gspec]
    args = [x2d, oa, ob, woa, wob, gq, wq, mk, mv, wo, gm]
    aliases = {}
    if into is not None:
        aliases = {len(args): 0, len(args) + 1: 1}
        in_specs += [pl.BlockSpec(memory_space=pl.ANY)] * 2
        args += list(into)
    return pl.pallas_call(
        functools.partial(_post_kernel, n_real=n_real),
        grid=(n_steps,),
        in_specs=in_specs,
        out_specs=[pl.BlockSpec((tm, D_MODEL), out_row), pl.BlockSpec((tm * X_SLABS, LANES), out_row)],
        out_shape=[jax.ShapeDtypeStruct((total_rows, D_MODEL), F32),
                   jax.ShapeDtypeStruct((total_rows * X_SLABS, LANES), jnp.uint32)],
        input_output_aliases=aliases,
        compiler_params=_cparams(("arbitrary",)),
        name="post_attn",
    )(*args)


def _router_kernel(xn_ref, wr_ref, br_ref, tri_ref, idx_ref, gate_ref, rank_ref, cnt_ref, base_sc):
    @pl.when(pl.program_id(0) == 0)
    def _():
        base_sc[...] = jnp.zeros(base_sc.shape, F32)

    tm = xn_ref.shape[0] // X_SLABS
    xn = _unpack_bf16_pairs(_load_row_slabs(xn_ref, tm, X_SLABS))
    logits = _dot_nt(wr_ref[...], xn) + br_ref[...]
    iota_e = lax.broadcasted_iota(jnp.int32, (N_EXPERTS, tm), 0).astype(F32)
    work = logits
    vals, idxs, hots = [], [], []
    for _k in range(TOP_K):
        mx = work.max(axis=0, keepdims=True)
        ix = jnp.where(work == mx, iota_e, float(N_EXPERTS)).min(axis=0, keepdims=True)
        hot = iota_e == ix
        vals.append(mx)
        idxs.append(ix)
        hots.append(hot)
        work = jnp.where(hot, -jnp.inf, work)
    es = [jnp.exp(v - vals[0]) for v in vals]
    den = es[0] + es[1] + es[2] + es[3]
    cnt = (hots[0] | hots[1] | hots[2] | hots[3]).astype(F32)
    before = jnp.dot(cnt.astype(BF16), tri_ref[...], preferred_element_type=F32) + base_sc[...]
    for k in range(TOP_K):
        idx_ref[k:k + 1, :] = idxs[k].astype(jnp.int32)
        gate_ref[k:k + 1, :] = es[k] / den
        rank_ref[k:k + 1, :] = jnp.where(hots[k], before, 0.0).sum(axis=0, keepdims=True).astype(jnp.int32)
    base_sc[...] = base_sc[...] + cnt.sum(axis=1, keepdims=True)
    cnt_ref[...] = jnp.broadcast_to(base_sc[...], cnt_ref.shape).astype(jnp.int32)


def _router(xn2d, wr_t, br, tri, *, tm):
    r = xn2d.shape[0] // X_SLABS
    col = lambda i: (0, i)
    const = lambda i: (0, 0)
    return pl.pallas_call(
        _router_kernel,
        grid=(r // tm,),
        in_specs=[pl.BlockSpec((tm * X_SLABS, LANES), lambda i: (i, 0)),
                  pl.BlockSpec((N_EXPERTS, D_MODEL), const),
                  pl.BlockSpec((N_EXPERTS, 1), const),
                  pl.BlockSpec((tm, tm), const)],
        out_specs=[pl.BlockSpec((TOP_K, tm), col)] * 3 + [pl.BlockSpec((N_EXPERTS, LANES), const)],
        out_shape=[jax.ShapeDtypeStruct((TOP_K, r), jnp.int32),
                   jax.ShapeDtypeStruct((TOP_K, r), F32),
                   jax.ShapeDtypeStruct((TOP_K, r), jnp.int32),
                   jax.ShapeDtypeStruct((N_EXPERTS, LANES), jnp.int32)],
        scratch_shapes=[pltpu.VMEM((N_EXPERTS, 1), F32)],
        compiler_params=_cparams(("arbitrary",)),
        name="router",
    )(xn2d, wr_t, br, tri)


def _row_copy(src, dst, sem):
    return pltpu.make_async_copy(src, dst, sem)


def _slab(ref, row, n, lead=()):
    return ref.at[lead + (pl.ds(pl.multiple_of(row * n, n), n), slice(None))]


def _scatter_kernel(pos_ref, xn_ref, xs_ref, tile, tile_sem, sem, *, tm):
    i = pl.program_id(0)
    n = pl.num_programs(0)
    cur = i % 2
    rows = tm * X_SLABS

    def tile_copy(step, slot):
        return _row_copy(xn_ref.at[pl.ds(pl.multiple_of(step * rows, rows), rows), :], tile.at[slot],
                         tile_sem.at[slot])

    def drain(slot):
        def wait(t, c):
            for k in range(TOP_K):
                _row_copy(_slab(tile, 0, X_SLABS, (slot,)), _slab(xs_ref, 0, X_SLABS), sem.at[slot]).wait()
            return c

        lax.fori_loop(0, tm, wait, 0, unroll=4)

    @pl.when(i == 0)
    def _():
        tile_copy(0, 0).start()

    tile_copy(i, cur).wait()

    def start(tt, c):
        for u in range(POS_GROUP):
            for k in range(TOP_K):
                _row_copy(_slab(tile, tt * POS_GROUP + u, X_SLABS, (cur,)),
                          _slab(xs_ref, pos_ref[0, tt, k * POS_GROUP + u], X_SLABS),
                          sem.at[cur]).start(priority=k % 2)
        return c

    lax.fori_loop(0, tm // POS_GROUP, start, 0)

    @pl.when(i > 0)
    def _():
        drain(1 - cur)

    @pl.when(i + 1 < n)
    def _():
        tile_copy(i + 1, 1 - cur).start()

    @pl.when(i == n - 1)
    def _():
        drain(cur)


def _scatter(pos, xn2d, *, tm):
    r = xn2d.shape[0] // X_SLABS
    return pl.pallas_call(
        functools.partial(_scatter_kernel, tm=tm),
        grid=(r // tm,),
        in_specs=[pl.BlockSpec((1, tm // POS_GROUP, TOP_K * POS_GROUP), lambda i: (i, 0, 0),
                               memory_space=pltpu.SMEM),
                  pl.BlockSpec(memory_space=pl.ANY)],
        out_specs=pl.BlockSpec(memory_space=pl.ANY),
        out_shape=jax.ShapeDtypeStruct((TOP_K * r * X_SLABS, LANES), jnp.uint32),
        scratch_shapes=[pltpu.VMEM((2, tm * X_SLABS, LANES), jnp.uint32),
                        pltpu.SemaphoreType.DMA((2,)), pltpu.SemaphoreType.DMA((2,))],
        compiler_params=_cparams(("arbitrary",)),
        name="moe_scatter",
    )(pos, xn2d)


def _experts_kernel(blk_ref, exp_ref, lo_ref, hi_ref, first_ref, newexp_ref,
                    x_ref, wgu_ref, bgu_ref, wd_ref, bd_ref, y_ref, wgu_bf, wd_bf):
    i = pl.program_id(0)

    @pl.when(newexp_ref[i] == 1)
    def _():
        wgu_bf[...] = wgu_ref[0].astype(BF16)
        wd_bf[...] = wd_ref[0].astype(BF16)

    x = _unpack_bf16_pairs(_load_row_slabs(x_ref, MOE_ROWS, X_SLABS))
    gu = jnp.dot(x, wgu_bf[...], preferred_element_type=F32) + bgu_ref[0]
    gate = jnp.minimum(gu[:, :EXPERT_FF], SWIGLU_LIMIT)
    up = jnp.clip(gu[:, EXPERT_FF:], -SWIGLU_LIMIT, SWIGLU_LIMIT)
    act = (up + 1.0) * (gate * jax.nn.sigmoid(SWIGLU_ALPHA * gate))
    y = jnp.dot(act.astype(BF16), wd_bf[...], preferred_element_type=F32) + bd_ref[0]
    row0 = blk_ref[i] * MOE_ROWS
    whole = (lo_ref[i] <= row0) & (hi_ref[i] >= row0 + MOE_ROWS)

    @pl.when(whole)
    def _():
        _store_row_slabs(y_ref, y)

    @pl.when(jnp.logical_not(whole))
    def _():
        @pl.when(first_ref[i] == 1)
        def _():
            y_ref[...] = jnp.zeros(y_ref.shape, F32)

        rows = row0 + lax.broadcasted_iota(jnp.int32, (MOE_ROWS, 1), 0)
        mine = (rows >= lo_ref[i]) & (rows < hi_ref[i])
        _store_row_slabs(y_ref, jnp.where(mine, y, _load_row_slabs(y_ref, MOE_ROWS, Y_SLABS)))


def _experts(items, xs, wgu, bgu, wd, bd):
    blk, exp, lo, hi, first = items
    newexp = jnp.concatenate([jnp.ones((1,), jnp.int32), (exp[1:] != exp[:-1]).astype(jnp.int32)])
    n_items = blk.shape[0]
    n_rows = xs.shape[0] // X_SLABS
    return pl.pallas_call(
        _experts_kernel,
        grid_spec=pltpu.PrefetchScalarGridSpec(
            num_scalar_prefetch=6,
            grid=(n_items,),
            in_specs=[pl.BlockSpec((MOE_ROWS * X_SLABS, LANES), lambda i, b, e, *_: (b[i], 0)),
                      pl.BlockSpec((1, D_MODEL, 2 * EXPERT_FF), lambda i, b, e, *_: (e[i], 0, 0)),
                      pl.BlockSpec((1, 1, 2 * EXPERT_FF), lambda i, b, e, *_: (e[i], 0, 0)),
                      pl.BlockSpec((1, EXPERT_FF, D_MODEL), lambda i, b, e, *_: (e[i], 0, 0)),
                      pl.BlockSpec((1, 1, D_MODEL), lambda i, b, e, *_: (e[i], 0, 0))],
            out_specs=pl.BlockSpec((MOE_ROWS * Y_SLABS, LANES), lambda i, b, e, *_: (b[i], 0)),
            scratch_shapes=[pltpu.VMEM((D_MODEL, 2 * EXPERT_FF), BF16), pltpu.VMEM((EXPERT_FF, D_MODEL), BF16)]),
        out_shape=jax.ShapeDtypeStruct((n_rows * Y_SLABS, LANES), F32),
        compiler_params=pltpu.CompilerParams(dimension_semantics=("arbitrary",),
                                             vmem_limit_bytes=EXPERTS_VMEM_LIMIT),
        name="moe_experts",
    )(blk, exp, lo, hi, first, newexp, xs, wgu, bgu, wd, bd)


def _work_items(counts, n_rows):
    n_blocks = n_rows // MOE_ROWS
    n_items = n_blocks + N_EXPERTS - 1
    ends = jnp.cumsum(counts)
    starts = ends - counts
    nb = jnp.where(counts > 0, (ends - 1) // MOE_ROWS - starts // MOE_ROWS + 1, 0)
    item_end = jnp.cumsum(nb)
    item_start = item_end - nb
    total = item_end[-1]
    i = jnp.arange(n_items, dtype=jnp.int32)
    e = jnp.minimum(jnp.sum(item_end[None, :] <= i[:, None], axis=1), N_EXPERTS - 1).astype(jnp.int32)
    blk = (starts[e] // MOE_ROWS + (i - item_start[e])).astype(jnp.int32)
    valid = i < total
    lo = jnp.maximum(starts[e], blk * MOE_ROWS)
    hi = jnp.minimum(ends[e], (blk + 1) * MOE_ROWS)
    last = jnp.maximum(total - 1, 0)
    blk = jnp.where(valid, blk, blk[last])
    e = jnp.where(valid, e, e[last])
    lo = jnp.where(valid, lo, 0).astype(jnp.int32)
    hi = jnp.where(valid, hi, 0).astype(jnp.int32)
    prev = jnp.concatenate([jnp.full((1,), -1, jnp.int32), blk[:-1]])
    first = (blk != prev).astype(jnp.int32)
    return blk, e, lo, hi, first, starts


def _combine_kernel(pos_ref, pos_next_ref, h_ref, gate_ref, g_ref, ys_ref, o_first_ref, o_rest_ref, buf, sem,
                    *, n_first):
    tm = h_ref.shape[0]
    i = pl.program_id(0)
    n = pl.num_programs(0)
    cur = i % 2

    def fetch(p_ref, slot):
        def start(tt, c):
            for u in range(POS_GROUP):
                for k in range(TOP_K):
                    _row_copy(_slab(ys_ref, p_ref[0, tt, k * POS_GROUP + u], Y_SLABS),
                              _slab(buf, tt * POS_GROUP + u, Y_SLABS, (slot, k)),
                              sem.at[slot]).start(priority=k % 2)
            return c

        lax.fori_loop(0, tm // POS_GROUP, start, 0)

    @pl.when(i == 0)
    def _():
        fetch(pos_ref, 0)

    for slot in range(2):
        @pl.when((i + 1 < n) & (cur != slot))
        def _():
            fetch(pos_next_ref, slot)

    def wait(t, c):
        for k in range(TOP_K):
            _row_copy(_slab(ys_ref, 0, Y_SLABS), _slab(buf, 0, Y_SLABS, (cur, 0)), sem.at[cur]).wait()
        return c

    lax.fori_loop(0, tm, wait, 0, unroll=4)
    acc = h_ref[...]
    for k in range(TOP_K):
        acc = acc + gate_ref[:, k:k + 1] * _load_row_slabs(buf, tm, Y_SLABS, (cur, k))
    res = _rms(acc, g_ref[...])

    @pl.when(i < n_first)
    def _():
        o_first_ref[...] = res

    @pl.when(i >= n_first)
    def _():
        o_rest_ref[...] = res


def _combine(pos, h2d, gates_t, g_final, ys, *, tm, rows_first):
    r = h2d.shape[0]
    n_tiles = r // tm
    n_first = rows_first // tm
    row = lambda i: (i, 0)
    return pl.pallas_call(
        functools.partial(_combine_kernel, n_first=n_first),
        grid=(n_tiles,),
        in_specs=[pl.BlockSpec((1, tm // POS_GROUP, TOP_K * POS_GROUP), lambda i: (i, 0, 0),
                               memory_space=pltpu.SMEM),
                  pl.BlockSpec((1, tm // POS_GROUP, TOP_K * POS_GROUP),
                               lambda i: (jnp.minimum(i + 1, n_tiles - 1), 0, 0), memory_space=pltpu.SMEM),
                  pl.BlockSpec((tm, D_MODEL), row),
                  pl.BlockSpec((tm, TOP_K), row),
                  pl.BlockSpec((1, D_MODEL), lambda i: (0, 0)),
                  pl.BlockSpec(memory_space=pl.ANY)],
        out_specs=[pl.BlockSpec((tm, D_MODEL), lambda i: (jnp.minimum(i, n_first - 1), 0)),
                   pl.BlockSpec((tm, D_MODEL), lambda i: (jnp.maximum(i - n_first, 0), 0))],
        out_shape=[jax.ShapeDtypeStruct((rows_first, D_MODEL), F32),
                   jax.ShapeDtypeStruct((r - rows_first, D_MODEL), F32)],
        scratch_shapes=[pltpu.VMEM((2, TOP_K, tm * Y_SLABS, LANES), F32), pltpu.SemaphoreType.DMA((2,))],
        compiler_params=_cparams(("arbitrary",)),
        name="moe_combine",
    )(pos, pos, h2d, gates_t, g_final, ys)


def _moe(h2d, xn2d, w, rows_first):
    r = h2d.shape[0]
    tm_router = _pick_tile(r, 512)
    tm_scatter = _pick_tile(r, 512)
    tm_combine = _pick_tile(r, 256)
    idx, gates, rank, cnt = _router(xn2d, w["wr_t"], w["br"], w["tri"][:tm_router, :tm_router], tm=tm_router)
    counts = cnt[:, 0]
    blk, e, lo, hi, first, starts = _work_items(counts, TOP_K * r)
    hot = idx[:, :, None] == jnp.arange(N_EXPERTS, dtype=jnp.int32)
    pos = jnp.sum(jnp.where(hot, starts.astype(jnp.int32), 0), axis=-1) + rank
    tiles = lambda tm: (pos.reshape(TOP_K, -1, tm // POS_GROUP, POS_GROUP).transpose(1, 2, 0, 3)
                        .reshape(-1, tm // POS_GROUP, TOP_K * POS_GROUP))
    xs = _scatter(tiles(tm_scatter), xn2d, tm=tm_scatter)
    ys = _experts((blk, e, lo, hi, first), xs, w["wgu"], w["bgu"], w["wd"], w["bd"])
    return _combine(tiles(tm_combine), h2d, gates.T, w["g_final"], ys, tm=tm_combine, rows_first=rows_first)


def _pick_tile(n, cap):
    t = cap
    while n % t:
        t //= 2
    return t


def kernel(x_prompt, x_sample, mem_prompt, cache_a_k, cache_a_v, cache_b_k, cache_b_v, cache_mem_k, cache_mem_v, g_mix, w_in, rel_bias, lam_q1, lam_k1, lam_q2, lam_k2, g_subln, w_out, g_xq, g_xmem, w_xq, w_xk, w_xv, w_xo, g_moe, w_router, b_router, w_gate_up, b_gate_up, w_down, b_down, g_final):
    assert g_mix.shape[0] == 1, "single-layer kernel"
    b, s, d = x_prompt.shape
    db, dl, _ = x_sample.shape
    past = cache_b_k.shape[2]
    a_len = cache_a_k.shape[2]
    mlen = mem_prompt.shape[1]
    keep_a = min(BAND, s)
    assert d == D_MODEL and s % B_QBLK == 0 and s >= A_WIN and dl == CHUNK and past % CHUNK == 0
    assert B_QBLK == 2 * B_KBLK

    row1 = lambda v: v.reshape(1, -1).astype(F32)
    w_in_bf = w_in[0].astype(BF16)
    w_out_bf = w_out[0].astype(BF16)
    weights = dict(
        wr_t=w_router[0].T.astype(BF16), br=b_router[0].reshape(N_EXPERTS, 1).astype(F32),
        tri=jnp.asarray(np.triu(np.ones((512, 512), np.float32), 1), BF16),
        wgu=w_gate_up[0].astype(F32), bgu=b_gate_up[0].reshape(N_EXPERTS, 1, -1).astype(F32),
        wd=w_down[0].astype(F32), bd=b_down[0].reshape(N_EXPERTS, 1, -1).astype(F32),
        g_final=row1(g_final))
    lam = (jnp.exp(jnp.sum(lam_q1[0].astype(F32) * lam_k1[0].astype(F32)))
           - jnp.exp(jnp.sum(lam_q2[0].astype(F32) * lam_k2[0].astype(F32))) + LAM_INIT).reshape(1)
    table = _band_table(rel_bias[0])
    g_sub = row1(g_subln[0])

    tm_p = _pick_tile(s, 512)
    tabs_p = _rope_tables(np.arange(s))
    p_in = _inproj(x_prompt.reshape(b * s, d), row1(g_mix[0]), w_in_bf, *tabs_p,
                   tm=tm_p, n_pos_blocks=s // tm_p, tail_div=s // tm_p if keep_a == tm_p else 1, emit_vt=True)
    tabs_s = _rope_tables(np.tile(past + np.arange(dl), db))
    s_in = _inproj(x_sample.reshape(db * dl, d), row1(g_mix[0]), w_in_bf, *tabs_s,
                   tm=db * dl, n_pos_blocks=1, tail_div=1, emit_vt=False)

    def split(outs, nb, nl):
        return [o.reshape(nb, nl, A_WIDTH) for o in outs[:6]]

    aq, ak, _, bq, bk, _ = split(p_in, b, s)
    avt = p_in[10].reshape((b, s // A_QBLK) + p_in[10].shape[1:])
    bvt = p_in[11].reshape((b, s // B_KBLK) + p_in[11].shape[1:])
    saq, sak, sav, sbq, sbk, sbv = split(s_in, db, dl)

    oa = _attn_a_prompt(aq, ak, avt, table.transpose(0, 2, 1))
    ob = _attn_b_prompt(lam, bq, bk, bvt, g_sub)
    soa = _attn_a_sample(saq, cache_a_k[0].reshape(db, a_len * N_HEADS_A, HEAD_DIM),
                         cache_a_v[0].reshape(db, a_len * N_HEADS_A, HEAD_DIM), sak, sav, table[:, :dl, :BAND + dl])
    sob = _attn_b_sample(lam, sbq, cache_b_k[0].reshape(db, past, B_WIDTH),
                         cache_b_v[0].reshape(db, past * N_HEADS_B, 2 * HEAD_DIM), sbk, sbv, g_sub)

    mkf, mvf, mkb, mvb = _memkv(mem_prompt.reshape(b * mlen, d), row1(g_xmem[0]),
                                w_xk[0].astype(BF16), w_xv[0].astype(BF16), tm=mlen)
    post_w = (w_out_bf[:A_WIDTH], w_out_bf[A_WIDTH:], row1(g_xq[0]), w_xq[0].astype(BF16))
    tm_post = _pick_tile(math.gcd(s, b * s + db * dl), 256)
    rows_p, rows_s = b * s, db * dl
    sob_t = sob.transpose(0, 2, 1).reshape(db, N_HEADS_B, 2 * HEAD_DIM, dl)
    h_xn = _post(x_prompt.reshape(rows_p, d), oa, ob,
                 *post_w, mkb.reshape(b, mlen, d), mvb.reshape(b, mlen, d),
                 w_xo[0].astype(BF16), row1(g_moe[0]), tm=tm_post, tiles_per_batch=s // tm_post,
                 total_rows=rows_p + rows_s)
    h_all, xn_all = _post(x_sample.reshape(rows_s, d), soa.transpose(0, 2, 1), sob_t,
                          *post_w, cache_mem_k[0].reshape(db, mlen, d).astype(BF16),
                          cache_mem_v[0].reshape(db, mlen, d).astype(BF16),
                          w_xo[0].astype(BF16), row1(g_moe[0]), tm=dl, tiles_per_batch=1,
                          total_rows=rows_p + rows_s, row_offset=rows_p, into=h_xn)

    y_prompt, y_sample = _moe(h_all, xn_all, weights, rows_p)
    y_prompt = y_prompt.reshape(b, s, d)
    y_sample = y_sample.reshape(db, dl, d)

    def heads(x2d, nb, nl, nh):
        return x2d.reshape(1, nb, nl, nh, -1)

    akf, avf, bkf, bvf = p_in[6:10]
    if keep_a != tm_p:
        akf = akf.reshape(b, s, A_WIDTH)[:, s - keep_a:]
        avf = avf.reshape(b, s, A_WIDTH)[:, s - keep_a:]
    sakf, savf, sbkf, sbvf = s_in[6:]
    return (y_prompt, y_sample,
            heads(akf, b, keep_a, N_HEADS_A), heads(avf, b, keep_a, N_HEADS_A),
            heads(bkf, b, s, 2 * N_HEADS_B), heads(bvf, b, s, N_HEADS_B),
            heads(mkf, b, mlen, N_HEADS_M), heads(mvf, b, mlen, N_HEADS_M),
            heads(sakf, db, dl, N_HEADS_A), heads(savf, db, dl, N_HEADS_A),
            heads(sbkf, db, dl, 2 * N_HEADS_B), heads(sbvf, db, dl, N_HEADS_B))
```

```python
import functools
import math

import numpy as np
import jax
import jax.numpy as jnp
from jax import lax
from jax.experimental import pallas as pl
from jax.experimental.pallas import tpu as pltpu

F32 = jnp.float32
BF16 = jnp.bfloat16

D_MODEL = 1024
CHUNK = 64
LEFT_CHUNKS = 8
BAND = LEFT_CHUNKS * CHUNK
HEAD_DIM = 64
A_WIDTH = 512
B_WIDTH = 512
N_HEADS_A = 8
N_HEADS_B = 4
REL_CLIP = 128
ROT_DIM = 16
ROPE_THETA = 500000.0
N_HEADS_M = 4
HEAD_DIM_M = 256
N_EXPERTS = 32
TOP_K = 4
EXPERT_FF = 1024
SWIGLU_LIMIT = 7.0
SWIGLU_ALPHA = 1.702
RMS_EPS = 1e-5
NEG_INF = -1e30
LAM_INIT = 0.8 - 0.6 * math.exp(-0.3 * 0)
LOG2E = math.log2(math.e)

LANES = 128
A_QBLK = 4 * CHUNK
A_WIN = BAND + A_QBLK
B_QBLK = 512
B_KBLK = 256
VT_ONES = 16
BUF_PITCH = 9
POS_GROUP = 16
MOE_ROWS = 256
X_SLABS = D_MODEL // 2 // LANES
Y_SLABS = D_MODEL // LANES
VMEM_LIMIT = 48 * 1024 * 1024
EXPERTS_VMEM_LIMIT = 56 * 1024 * 1024


def _cparams(sem):
    return pltpu.CompilerParams(dimension_semantics=sem, vmem_limit_bytes=VMEM_LIMIT)


def _rms(x, g):
    return x * lax.rsqrt(jnp.mean(x * x, axis=-1, keepdims=True) + RMS_EPS) * g


HI16 = 0xFFFF0000


def _pack_bf16_pairs(x):
    n = x.shape[1] // 2
    bits = lambda v: lax.bitcast_convert_type(v.astype(BF16).astype(F32), jnp.uint32)
    return (bits(x[:, :n]) >> 16) | (bits(x[:, n:]) & jnp.uint32(HI16))


def _unpack_bf16_pairs(w):
    lo = lax.bitcast_convert_type(w << 16, F32)
    hi = lax.bitcast_convert_type(w & jnp.uint32(HI16), F32)
    return jnp.concatenate([lo, hi], axis=1).astype(BF16)


def _store_row_slabs(ref, x, lead=()):
    m, n = x.shape[0], x.shape[1] // LANES
    for c in range(n):
        ref[lead + (pl.ds(c, m, stride=n), slice(None))] = x[:, c * LANES:(c + 1) * LANES]


def _load_row_slabs(ref, m, n, lead=()):
    return jnp.concatenate([ref[lead + (pl.ds(c, m, stride=n), slice(None))] for c in range(n)], axis=1)


def _dot_nt(a, b):
    return lax.dot_general(a, b, (((1,), (1,)), ((), ())), preferred_element_type=F32)


def _store_heads64(ref, x):
    m, n_heads = x.shape[0], x.shape[1] // HEAD_DIM
    for c in range(x.shape[1] // LANES):
        grp = x[:, c * LANES:(c + 1) * LANES]
        ref[pl.ds(2 * c, m, stride=n_heads), :] = grp[:, :HEAD_DIM]
        ref[pl.ds(2 * c + 1, m, stride=n_heads), :] = pltpu.roll(grp, HEAD_DIM, 1)[:, :HEAD_DIM]


def _inproj_kernel(x_ref, g_ref, w_ref, cos_ref, sa_ref, sb_ref,
                   aq_ref, ak_ref, av_ref, bq_ref, bk_ref, bv_ref,
                   akf_ref, avf_ref, bkf_ref, bvf_ref, *maybe_vt_refs, tail_div):
    xn = _rms(x_ref[...], g_ref[...]).astype(BF16)
    is_tail = pl.program_id(0) % tail_div == tail_div - 1

    def seg(s):
        return jnp.dot(xn, w_ref[:, s * A_WIDTH:(s + 1) * A_WIDTH], preferred_element_type=F32)

    def rope(z):
        cols = []
        for c in range(z.shape[1] // LANES):
            zc = z[:, c * LANES:(c + 1) * LANES]
            cols.append(zc * cos_ref[...] + pltpu.roll(zc, 8, 1) * sa_ref[...]
                        + pltpu.roll(zc, LANES - 8, 1) * sb_ref[...])
        return jnp.concatenate(cols, axis=1)

    scale = HEAD_DIM ** -0.5
    aq_ref[...] = (seg(0) * (scale * LOG2E)).astype(BF16)
    ak = seg(1)
    ak_ref[...] = ak.astype(BF16)
    av = seg(2)
    av_ref[...] = av.astype(BF16)
    bq_ref[...] = (rope(seg(3)) * (scale * LOG2E)).astype(BF16)
    bk = rope(seg(4))
    bk_ref[...] = bk.astype(BF16)
    _store_heads64(bkf_ref, bk)
    bv = seg(5)
    bv_ref[...] = bv.astype(BF16)
    _store_row_slabs(bvf_ref, bv)
    for vt_ref, v in zip(maybe_vt_refs, (av, bv)):
        vt = v.T.astype(BF16)
        blk = vt_ref.shape[3]
        for c in range(vt_ref.shape[0]):
            for g in range(vt_ref.shape[1]):
                vt_ref[c, g, 0:LANES, :] = vt[g * LANES:(g + 1) * LANES, c * blk:(c + 1) * blk]
                vt_ref[c, g, LANES:LANES + VT_ONES, :] = jnp.ones((VT_ONES, blk), BF16)

    @pl.when(is_tail)
    def _():
        _store_heads64(akf_ref, ak)
        _store_heads64(avf_ref, av)


def _inproj(x2d, g, w_bf, cos_t, sa_t, sb_t, *, tm, n_pos_blocks, tail_div, emit_vt):
    r = x2d.shape[0]
    n_tiles = r // tm
    row = lambda i: (i, 0)
    const = lambda i: (0, 0)
    pos = lambda i: (i % n_pos_blocks, 0)
    tail = lambda i: (i // tail_div, 0)
    bf = jax.ShapeDtypeStruct((r, A_WIDTH), BF16)
    blk = lambda im: pl.BlockSpec((tm, A_WIDTH), im)
    h64 = lambda im: pl.BlockSpec((tm * N_HEADS_A, HEAD_DIM), im)
    f_tail = jax.ShapeDtypeStruct((r // tail_div * N_HEADS_A, HEAD_DIM), F32)
    out_specs = [blk(row)] * 6 + [h64(tail), h64(tail), h64(row),
                                  pl.BlockSpec((tm * N_HEADS_B, 2 * HEAD_DIM), row)]
    out_shape = [bf] * 6 + [f_tail, f_tail,
                            jax.ShapeDtypeStruct((r * 2 * N_HEADS_B, HEAD_DIM), F32),
                            jax.ShapeDtypeStruct((r * N_HEADS_B, 2 * HEAD_DIM), F32)]
    if emit_vt:
        for kb in (A_QBLK, B_KBLK):
            grp = (A_WIDTH // LANES, LANES + VT_ONES, kb)
            out_specs.append(pl.BlockSpec((tm // kb,) + grp, lambda i: (i, 0, 0, 0)))
            out_shape.append(jax.ShapeDtypeStruct((r // kb,) + grp, BF16))
    return pl.pallas_call(
        functools.partial(_inproj_kernel, tail_div=tail_div),
        grid=(n_tiles,),
        in_specs=[pl.BlockSpec((tm, D_MODEL), row), pl.BlockSpec((1, D_MODEL), const),
                  pl.BlockSpec((D_MODEL, 3 * D_MODEL), const),
                  pl.BlockSpec((tm, LANES), pos), pl.BlockSpec((tm, LANES), pos),
                  pl.BlockSpec((tm, LANES), pos)],
        out_specs=out_specs,
        out_shape=out_shape,
        compiler_params=_cparams(("arbitrary",)),
        name="inproj",
    )(x2d, g, w_bf, cos_t, sa_t, sb_t)


def _rope_tables(positions):
    half = ROT_DIM // 2
    lane = np.arange(LANES) % HEAD_DIM
    inv_freq = np.power(ROPE_THETA, -np.arange(half, dtype=np.float64) / half)
    ang = np.asarray(positions, np.float64)[:, None] * inv_freq[None, :]
    cos = np.cos(ang)[:, lane % half]
    sin = np.sin(ang)[:, lane % half]
    in_rot = (lane < ROT_DIM)[None, :]
    first = (lane < half)[None, :]
    cos_t = np.where(in_rot, cos, 1.0)
    sa_t = np.where(in_rot & ~first, sin, 0.0)
    sb_t = np.where(first, -sin, 0.0)
    return tuple(jnp.asarray(t, F32) for t in (cos_t, sa_t, sb_t))


def _band_table(rel_bias):
    r = np.arange(A_QBLK)[:, None]
    j = np.arange(BAND + A_WIN)[None, :]
    dchunk = LEFT_CHUNKS + r // CHUNK - j // CHUNK
    allowed = (dchunk >= 0) & (dchunk <= LEFT_CHUNKS)
    bias = rel_bias.astype(F32) * LOG2E
    n_seq = BAND + A_WIN + A_QBLK - 1
    n_hi = BAND + A_QBLK - 1 - REL_CLIP
    seq = jnp.concatenate([jnp.broadcast_to(bias[:, -1:], (N_HEADS_A, n_hi)), bias[:, ::-1],
                           jnp.broadcast_to(bias[:, :1], (N_HEADS_A, n_seq - n_hi - bias.shape[1]))], axis=1)
    seq = jnp.roll(seq, -(A_QBLK - 1), axis=1)
    toe = jnp.tile(seq, (1, A_QBLK))[:, :A_QBLK * (n_seq - 1)].reshape(N_HEADS_A, A_QBLK, n_seq - 1)
    return jnp.where(jnp.asarray(allowed)[None], toe[:, :, :BAND + A_WIN], NEG_INF)


def _softmax_pv(s_list, v_list, exp_fn=jnp.exp):
    m = s_list[0].max(axis=-1, keepdims=True)
    for s in s_list[1:]:
        m = jnp.maximum(m, s.max(axis=-1, keepdims=True))
    l = None
    o = None
    for s, v in zip(s_list, v_list):
        p = exp_fn(s - m)
        ls = p.sum(axis=-1, keepdims=True)
        os_ = jnp.dot(p.astype(BF16), v, preferred_element_type=F32)
        l = ls if l is None else l + ls
        o = os_ if o is None else o + os_
    return o / l


def _attn_a_prompt_kernel(q_ref, k_ref, vt_ref, tab_ref, o_ref):
    i = pl.program_id(1)
    n_left = BAND // A_QBLK
    n_sub = A_WIN // A_QBLK
    blk0 = jnp.maximum(i - n_left, 0)
    off = pl.multiple_of(jnp.maximum(n_left - i, 0) * A_QBLK, A_QBLK)
    q = q_ref[0]
    lane = lax.broadcasted_iota(jnp.int32, (A_QBLK, LANES), 1)

    def scores(h):
        cols = slice(h // 2 * LANES, (h // 2 + 1) * LANES)
        qp = q[:, cols]
        qh = jnp.where((lane >= HEAD_DIM) == bool(h % 2), qp, jnp.zeros_like(qp))
        return [_dot_nt(k_ref[0, pl.ds(pl.multiple_of((blk0 + c) * A_QBLK, A_QBLK), A_QBLK), cols], qh)
                + tab_ref[h, pl.ds(off + c * A_QBLK, A_QBLK), :] for c in range(n_sub)]

    def attend(h, ss):
        m = _col_reduce(ss[0], jnp.maximum, jnp.max)
        for s in ss[1:]:
            m = jnp.maximum(m, _col_reduce(s, jnp.maximum, jnp.max))
        o_t = None
        for c in range(n_sub):
            p = jnp.exp2(ss[c] - m).astype(BF16)
            os_ = jnp.dot(vt_ref[0, blk0 + c, h // 2], p, preferred_element_type=F32)
            o_t = os_ if o_t is None else o_t + os_
        return o_t[h % 2 * HEAD_DIM:(h % 2 + 1) * HEAD_DIM] / o_t[LANES:LANES + 1]

    halves = []
    ss = scores(0)
    for h in range(N_HEADS_A):
        ss_next = scores(h + 1) if h + 1 < N_HEADS_A else None
        halves.append(attend(h, ss))
        ss = ss_next
    o_ref[0] = jnp.concatenate(halves, axis=0).astype(BF16)


def _attn_a_prompt(aq, ak, avt, table_t):
    b, l, _ = aq.shape
    n_kb = avt.shape[1]
    return pl.pallas_call(
        _attn_a_prompt_kernel,
        grid=(b, l // A_QBLK),
        in_specs=[pl.BlockSpec((1, A_QBLK, A_WIDTH), lambda bb, i: (bb, i, 0)),
                  pl.BlockSpec((1, l, A_WIDTH), lambda bb, i: (bb, 0, 0)),
                  pl.BlockSpec((1,) + avt.shape[1:], lambda bb, i: (bb, 0, 0, 0, 0)),
                  pl.BlockSpec(table_t.shape, lambda bb, i: (0, 0, 0))],
        out_specs=pl.BlockSpec((1, A_WIDTH, A_QBLK), lambda bb, i: (bb, 0, i)),
        out_shape=jax.ShapeDtypeStruct((b, A_WIDTH, l), BF16),
        compiler_params=_cparams(("arbitrary", "arbitrary")),
        name="attn_a_prompt",
    )(aq, ak, avt, table_t)


def _attn_a_sample_kernel(q_ref, kc_ref, vc_ref, k_ref, v_ref, tab_ref, o_ref):
    q = q_ref[0]
    ko = k_ref[0]
    vo = v_ref[0]
    a_len = kc_ref.shape[1] // N_HEADS_A
    lq = q.shape[0]
    c0 = BAND - a_len
    outs = []
    for h in range(N_HEADS_A):
        sl = slice(h * HEAD_DIM, (h + 1) * HEAD_DIM)
        kc = kc_ref[0, pl.ds(h, a_len, stride=N_HEADS_A), :].astype(BF16)
        vc = vc_ref[0, pl.ds(h, a_len, stride=N_HEADS_A), :].astype(BF16)
        s_c = _dot_nt(q[:, sl], kc) + tab_ref[h, :, c0:BAND]
        s_o = _dot_nt(q[:, sl], ko[:, sl]) + tab_ref[h, :, BAND:BAND + lq]
        outs.append(_softmax_pv([s_c, s_o], [vc, vo[:, sl]], exp_fn=jnp.exp2))
    o_ref[0] = jnp.concatenate(outs, axis=1).astype(BF16)


def _attn_a_sample(aq, cache_k, cache_v, ak, av, table):
    b, l, _ = aq.shape
    bs = lambda n: pl.BlockSpec((1, n, A_WIDTH), lambda bb: (bb, 0, 0))
    cache = pl.BlockSpec((1,) + cache_k.shape[1:], lambda bb: (bb, 0, 0))
    return pl.pallas_call(
        _attn_a_sample_kernel,
        grid=(b,),
        in_specs=[bs(l), cache, cache, bs(l), bs(l),
                  pl.BlockSpec(table.shape, lambda bb: (0, 0, 0))],
        out_specs=bs(l),
        out_shape=jax.ShapeDtypeStruct((b, l, A_WIDTH), BF16),
        compiler_params=_cparams(("arbitrary",)),
        name="attn_a_sample",
    )(aq, cache_k, cache_v, ak, av, table)


def _subnorm(o, g):
    return _rms(o, g) * (1.0 - LAM_INIT)


def _col_reduce(x, op, reduce_fn):
    n = x.shape[0]
    while n > 8:
        n //= 2
        x = op(x[:n], x[n:])
    return reduce_fn(x, axis=0, keepdims=True)


def _attn_b_prompt_kernel(lam_ref, q_ref, qn_ref, k_ref, vt_ref, mask_ref, g_ref, o_ref,
                          sa_sc, sb_sc, m_sc, acc_sc):
    i = pl.program_id(2)
    tk = vt_ref.shape[4]
    lane = lax.broadcasted_iota(jnp.int32, q_ref.shape[1:], 1)

    def split_maps(q):
        zero = jnp.zeros_like(q)
        return jnp.where(lane < HEAD_DIM, q, zero), jnp.where(lane >= HEAD_DIM, q, zero)

    qs = split_maps(q_ref[0])
    m_sc[...] = jnp.full(m_sc.shape, NEG_INF, F32)
    acc_sc[...] = jnp.zeros(acc_sc.shape, F32)

    def scores(j, s_sc, qmaps=qs):
        kb = k_ref[0, pl.ds(pl.multiple_of(j * tk, tk), tk), :]
        for a in range(2):
            s_sc[a] = _dot_nt(kb, qmaps[a])

    def consume(j, s_sc, mask):
        vt = vt_ref[0, j, 0]
        for a in range(2):
            s = s_sc[a]
            if mask is not None:
                s = s + mask
            m_old = m_sc[a]
            m_new = jnp.maximum(m_old, _col_reduce(s, jnp.maximum, jnp.max))
            alpha = jnp.exp2(m_old - m_new)
            p = jnp.exp2(s - m_new).astype(BF16)
            m_sc[a] = m_new
            acc_sc[a] = alpha * acc_sc[a] + jnp.dot(vt, p, preferred_element_type=F32)

    @pl.when(i == 0)
    def _():
        scores(0, sa_sc)

    def body(jj, c):
        j = 2 * jj
        scores(j + 1, sb_sc)
        consume(j, sa_sc, None)
        scores(j + 2, sa_sc)
        consume(j + 1, sb_sc, None)
        return c

    lax.fori_loop(0, i, body, 0)
    scores(2 * i + 1, sb_sc)
    consume(2 * i, sa_sc, mask_ref[0])
    scores(0, sa_sc, split_maps(qn_ref[0]))
    consume(2 * i + 1, sb_sc, mask_ref[1])
    o_t = (acc_sc[0, 0:LANES] / acc_sc[0, LANES:LANES + 1]
           - lam_ref[0] * (acc_sc[1, 0:LANES] / acc_sc[1, LANES:LANES + 1]))
    inv = lax.rsqrt(jnp.mean(o_t * o_t, axis=0, keepdims=True) + RMS_EPS)
    o_ref[0, 0] = (o_t * inv * (g_ref[...] * (1.0 - LAM_INIT))).astype(BF16)


def _chunk_causal_masks_t(tk, tq):
    kc = (np.arange(tq // tk)[:, None, None] * tk + np.arange(tk)[None, :, None]) // CHUNK
    qc = np.arange(tq)[None, None, :] // CHUNK
    return jnp.asarray(np.where(kc <= qc, 0.0, NEG_INF), F32)


def _attn_b_prompt(lam, bq, bk, bvt, g_sub):
    b, l, _ = bq.shape
    tq = B_QBLK
    tk = B_KBLK
    hw = 2 * HEAD_DIM
    n_kb = bvt.shape[1]
    return pl.pallas_call(
        _attn_b_prompt_kernel,
        grid_spec=pltpu.PrefetchScalarGridSpec(
            num_scalar_prefetch=1,
            grid=(b, N_HEADS_B, l // tq),
            in_specs=[pl.BlockSpec((1, tq, hw), lambda bb, h, i, lam_: (bb, i, h)),
                      pl.BlockSpec((1, tq, hw), lambda bb, h, i, lam_: (bb, jnp.minimum(i + 1, l // tq - 1), h)),
                      pl.BlockSpec((1, l, hw), lambda bb, h, i, lam_: (bb, 0, h)),
                      pl.BlockSpec((1, n_kb, 1, hw + VT_ONES, tk), lambda bb, h, i, lam_: (bb, 0, h, 0, 0)),
                      pl.BlockSpec((tq // tk, tk, tq), lambda bb, h, i, lam_: (0, 0, 0)),
                      pl.BlockSpec((hw, 1), lambda bb, h, i, lam_: (0, 0))],
            out_specs=pl.BlockSpec((1, 1, hw, tq), lambda bb, h, i, lam_: (bb, h, 0, i)),
            scratch_shapes=[pltpu.VMEM((2, tk, tq), F32), pltpu.VMEM((2, tk, tq), F32),
                            pltpu.VMEM((2, 1, tq), F32), pltpu.VMEM((2, hw + VT_ONES, tq), F32)]),
        out_shape=jax.ShapeDtypeStruct((b, N_HEADS_B, hw, l), BF16),
        compiler_params=_cparams(("arbitrary", "arbitrary", "arbitrary")),
        name="attn_b_prompt",
    )(lam, bq, bq, bk, bvt, _chunk_causal_masks_t(tk, tq), g_sub.reshape(hw, 1))


def _attn_b_sample_kernel(lam_ref, q_ref, kc_ref, vc_ref, k_ref, v_ref, g_ref, o_ref):
    q = q_ref[0]
    kc = kc_ref[0].astype(BF16)
    n_past = vc_ref.shape[1] // N_HEADS_B
    vc = vc_ref[0, pl.ds(pl.program_id(1), n_past, stride=N_HEADS_B), :].astype(BF16)
    ko = k_ref[0]
    vo = v_ref[0]
    outs = []
    for a in range(2):
        sl = slice(a * HEAD_DIM, (a + 1) * HEAD_DIM)
        s_c = _dot_nt(q[:, sl], kc[:, sl])
        s_o = _dot_nt(q[:, sl], ko[:, sl])
        outs.append(_softmax_pv([s_c, s_o], [vc, vo], exp_fn=jnp.exp2))
    o = outs[0] - lam_ref[0] * outs[1]
    o_ref[0] = _subnorm(o, g_ref[...]).astype(BF16)


def _attn_b_sample(lam, bq, cache_k, cache_v, bk, bv, g_sub):
    b, l, _ = bq.shape
    p = cache_k.shape[1]
    hw = 2 * HEAD_DIM
    bs = lambda n: pl.BlockSpec((1, n, hw), lambda bb, h, lam_: (bb, 0, h))
    return pl.pallas_call(
        _attn_b_sample_kernel,
        grid_spec=pltpu.PrefetchScalarGridSpec(
            num_scalar_prefetch=1,
            grid=(b, N_HEADS_B),
            in_specs=[bs(l), bs(p),
                      pl.BlockSpec((1, p * N_HEADS_B, hw), lambda bb, h, lam_: (bb, 0, 0)),
                      bs(l), bs(l),
                      pl.BlockSpec((1, hw), lambda bb, h, lam_: (0, 0))],
            out_specs=bs(l)),
        out_shape=jax.ShapeDtypeStruct((b, l, B_WIDTH), BF16),
        compiler_params=_cparams(("arbitrary", "arbitrary")),
        name="attn_b_sample",
    )(lam, bq, cache_k, cache_v, bk, bv, g_sub)


def _memkv_kernel(m_ref, g_ref, wk_ref, wv_ref, kf_ref, vf_ref, kb_ref, vb_ref):
    mn = _rms(m_ref[...], g_ref[...]).astype(BF16)
    k = jnp.dot(mn, wk_ref[...], preferred_element_type=F32)
    v = jnp.dot(mn, wv_ref[...], preferred_element_type=F32)
    kf_ref[...] = k
    vf_ref[...] = v
    kb_ref[...] = k.astype(BF16)
    vb_ref[...] = v.astype(BF16)


def _memkv(mem2d, g, wk, wv, *, tm):
    r = mem2d.shape[0]
    row = lambda i: (i, 0)
    const = lambda i: (0, 0)
    blk = pl.BlockSpec((tm, D_MODEL), row)
    wspec = pl.BlockSpec((D_MODEL, D_MODEL), const)
    f = jax.ShapeDtypeStruct((r, D_MODEL), F32)
    bf = jax.ShapeDtypeStruct((r, D_MODEL), BF16)
    return pl.pallas_call(
        _memkv_kernel,
        grid=(r // tm,),
        in_specs=[blk, pl.BlockSpec((1, D_MODEL), const), wspec, wspec],
        out_specs=[blk] * 4,
        out_shape=[f, f, bf, bf],
        compiler_params=_cparams(("arbitrary",)),
        name="memkv",
    )(mem2d, g, wk, wv)


def _post_kernel(*refs, n_real):
    h_ref, xn_ref = refs[-2:]

    @pl.when(pl.program_id(0) < n_real)
    def _():
        _post_tile(*refs[:11], h_ref, xn_ref)

    @pl.when(pl.program_id(0) >= n_real)
    def _():
        h_ref[...] = jnp.zeros(h_ref.shape, h_ref.dtype)
        xn_ref[...] = jnp.zeros(xn_ref.shape, xn_ref.dtype)


def _post_tile(x_ref, oa_ref, ob_ref, woa_ref, wob_ref, gq_ref, wq_ref, mk_ref, mv_ref, wo_ref,
               gm_ref, h_ref, xn_ref):
    tn = lambda a_t, w: lax.dot_general(a_t, w, (((0,), (0,)), ((), ())), preferred_element_type=F32)
    ob_t = ob_ref[0].reshape(B_WIDTH, ob_ref.shape[3])
    h1 = x_ref[...] + tn(oa_ref[0], woa_ref[...]) + tn(ob_t, wob_ref[...])
    xq = _rms(h1, gq_ref[...]).astype(BF16)
    q = (jnp.dot(xq, wq_ref[...], preferred_element_type=F32) * HEAD_DIM_M ** -0.5).astype(BF16)
    outs = []
    for h in range(N_HEADS_M):
        sl = slice(h * HEAD_DIM_M, (h + 1) * HEAD_DIM_M)
        s = _dot_nt(q[:, sl], mk_ref[0, :, sl])
        outs.append(_softmax_pv([s], [mv_ref[0, :, sl]]).astype(BF16))
    o = jnp.concatenate(outs, axis=1)
    h2 = h1 + jnp.dot(o, wo_ref[...], preferred_element_type=F32)
    h_ref[...] = h2
    _store_row_slabs(xn_ref, _pack_bf16_pairs(_rms(h2, gm_ref[...])))


def _post(x2d, oa, ob, woa, wob, gq, wq, mk, mv, wo, gm, *, tm, tiles_per_batch, total_rows,
          row_offset=0, into=None):
    r = x2d.shape[0]
    n_real = r // tm
    n_steps = n_real if into is not None else total_rows // tm
    mlen = mk.shape[1]
    row = lambda i: (jnp.minimum(i, n_real - 1), 0)
    out_row = lambda i: (row_offset // tm + i, 0)
    const = lambda i: (0, 0)
    memb = lambda i: (jnp.minimum(i, n_real - 1) // tiles_per_batch, 0, 0)
    gspec = pl.BlockSpec((1, D_MODEL), const)
    wspec = pl.BlockSpec((D_MODEL, D_MODEL), const)
    hspec = pl.BlockSpec((A_WIDTH, D_MODEL), const)
    obt_blk = lambda i: (jnp.minimum(i, n_real - 1) // tiles_per_batch, 0, 0,
                         jnp.minimum(i, n_real - 1) % tiles_per_batch)
    oat_blk = lambda i: (jnp.minimum(i, n_real - 1) // tiles_per_batch, 0,
                         jnp.minimum(i, n_real - 1) % tiles_per_batch)
    in_specs = [pl.BlockSpec((tm, D_MODEL), row),
                pl.BlockSpec((1, A_WIDTH, tm), oat_blk),
                pl.BlockSpec((1, N_HEADS_B, 2 * HEAD_DIM, tm), obt_blk),
                hspec, hspec, gspec, wspec,
                pl.BlockSpec((1, mlen, D_MODEL), memb), pl.BlockSpec((1, mlen, D_MODEL), memb),
                wspec, gspec]
    args = [x2d, oa, ob, woa, wob, gq, wq, mk, mv, wo, gm]
    aliases = {}
    if into is not None:
        aliases = {len(args): 0, len(args) + 1: 1}
        in_specs += [pl.BlockSpec(memory_space=pl.ANY)] * 2
        args += list(into)
    return pl.pallas_call(
        functools.partial(_post_kernel, n_real=n_real),
        grid=(n_steps,),
        in_specs=in_specs,
        out_specs=[pl.BlockSpec((tm, D_MODEL), out_row), pl.BlockSpec((tm * X_SLABS, LANES), out_row)],
        out_shape=[jax.ShapeDtypeStruct((total_rows, D_MODEL), F32),
                   jax.ShapeDtypeStruct((total_rows * X_SLABS, LANES), jnp.uint32)],
        input_output_aliases=aliases,
        compiler_params=_cparams(("arbitrary",)),
        name="post_attn",
    )(*args)


def _router_kernel(xn_ref, wr_ref, br_ref, tri_ref, idx_ref, gate_ref, rank_ref, cnt_ref, base_sc):
    @pl.when(pl.program_id(0) == 0)
    def _():
        base_sc[...] = jnp.zeros(base_sc.shape, F32)

    tm = xn_ref.shape[0] // X_SLABS
    xn = _unpack_bf16_pairs(_load_row_slabs(xn_ref, tm, X_SLABS))
    logits = _dot_nt(wr_ref[...], xn) + br_ref[...]
    iota_e = lax.broadcasted_iota(jnp.int32, (N_EXPERTS, tm), 0).astype(F32)
    work = logits
    vals, idxs, hots = [], [], []
    for _k in range(TOP_K):
        mx = work.max(axis=0, keepdims=True)
        ix = jnp.where(work == mx, iota_e, float(N_EXPERTS)).min(axis=0, keepdims=True)
        hot = iota_e == ix
        vals.append(mx)
        idxs.append(ix)
        hots.append(hot)
        work = jnp.where(hot, -jnp.inf, work)
    es = [jnp.exp(v - vals[0]) for v in vals]
    den = es[0] + es[1] + es[2] + es[3]
    cnt = (hots[0] | hots[1] | hots[2] | hots[3]).astype(F32)
    before = jnp.dot(cnt.astype(BF16), tri_ref[...], preferred_element_type=F32) + base_sc[...]
    for k in range(TOP_K):
        idx_ref[k:k + 1, :] = idxs[k].astype(jnp.int32)
        gate_ref[k:k + 1, :] = es[k] / den
        rank_ref[k:k + 1, :] = jnp.where(hots[k], before, 0.0).sum(axis=0, keepdims=True).astype(jnp.int32)
    base_sc[...] = base_sc[...] + cnt.sum(axis=1, keepdims=True)
    cnt_ref[...] = jnp.broadcast_to(base_sc[...], cnt_ref.shape).astype(jnp.int32)


def _router(xn2d, wr_t, br, tri, *, tm):
    r = xn2d.shape[0] // X_SLABS
    col = lambda i: (0, i)
    const = lambda i: (0, 0)
    return pl.pallas_call(
        _router_kernel,
        grid=(r // tm,),
        in_specs=[pl.BlockSpec((tm * X_SLABS, LANES), lambda i: (i, 0)),
                  pl.BlockSpec((N_EXPERTS, D_MODEL), const),
                  pl.BlockSpec((N_EXPERTS, 1), const),
                  pl.BlockSpec((tm, tm), const)],
        out_specs=[pl.BlockSpec((TOP_K, tm), col)] * 3 + [pl.BlockSpec((N_EXPERTS, LANES), const)],
        out_shape=[jax.ShapeDtypeStruct((TOP_K, r), jnp.int32),
                   jax.ShapeDtypeStruct((TOP_K, r), F32),
                   jax.ShapeDtypeStruct((TOP_K, r), jnp.int32),
                   jax.ShapeDtypeStruct((N_EXPERTS, LANES), jnp.int32)],
        scratch_shapes=[pltpu.VMEM((N_EXPERTS, 1), F32)],
        compiler_params=_cparams(("arbitrary",)),
        name="router",
    )(xn2d, wr_t, br, tri)


def _row_copy(src, dst, sem):
    return pltpu.make_async_copy(src, dst, sem)


def _slab(ref, row, n, lead=()):
    return ref.at[lead + (pl.ds(pl.multiple_of(row * n, n), n), slice(None))]


def _scatter_kernel(pos_ref, xn_ref, xs_ref, sem):
    tm = xn_ref.shape[0] // X_SLABS

    def start(tt, c):
        for u in range(POS_GROUP):
            for k in range(TOP_K):
                _row_copy(_slab(xn_ref, tt * POS_GROUP + u, X_SLABS),
                          _slab(xs_ref, pos_ref[0, tt, k * POS_GROUP + u], X_SLABS),
                          sem).start(priority=k % 2)
        return c

    lax.fori_loop(0, tm // POS_GROUP, start, 0)

    def wait(t, c):
        for k in range(TOP_K):
            _row_copy(_slab(xn_ref, 0, X_SLABS), _slab(xs_ref, 0, X_SLABS), sem).wait()
        return c

    lax.fori_loop(0, tm, wait, 0, unroll=4)


def _scatter(pos, xn2d, *, tm):
    r = xn2d.shape[0] // X_SLABS
    return pl.pallas_call(
        _scatter_kernel,
        grid=(r // tm,),
        in_specs=[pl.BlockSpec((1, tm // POS_GROUP, TOP_K * POS_GROUP), lambda i: (i, 0, 0),
                               memory_space=pltpu.SMEM),
                  pl.BlockSpec((tm * X_SLABS, LANES), lambda i: (i, 0))],
        out_specs=pl.BlockSpec(memory_space=pl.ANY),
        out_shape=jax.ShapeDtypeStruct((TOP_K * r * X_SLABS, LANES), jnp.uint32),
        scratch_shapes=[pltpu.SemaphoreType.DMA],
        compiler_params=_cparams(("arbitrary",)),
        name="moe_scatter",
    )(pos, xn2d)


def _experts_kernel(blk_ref, exp_ref, lo_ref, hi_ref, first_ref, newexp_ref,
                    x_ref, wgu_ref, bgu_ref, wd_ref, bd_ref, y_ref, wgu_bf, wd_bf):
    i = pl.program_id(0)

    @pl.when(newexp_ref[i] == 1)
    def _():
        wgu_bf[...] = wgu_ref[0].astype(BF16)
        wd_bf[...] = wd_ref[0].astype(BF16)

    x = _unpack_bf16_pairs(_load_row_slabs(x_ref, MOE_ROWS, X_SLABS))
    gu = jnp.dot(x, wgu_bf[...], preferred_element_type=F32) + bgu_ref[0]
    gate = jnp.minimum(gu[:, :EXPERT_FF], SWIGLU_LIMIT)
    up = jnp.clip(gu[:, EXPERT_FF:], -SWIGLU_LIMIT, SWIGLU_LIMIT)
    act = (up + 1.0) * (gate * jax.nn.sigmoid(SWIGLU_ALPHA * gate))
    y = jnp.dot(act.astype(BF16), wd_bf[...], preferred_element_type=F32) + bd_ref[0]
    row0 = blk_ref[i] * MOE_ROWS
    whole = (lo_ref[i] <= row0) & (hi_ref[i] >= row0 + MOE_ROWS)

    @pl.when(whole)
    def _():
        _store_row_slabs(y_ref, y)

    @pl.when(jnp.logical_not(whole))
    def _():
        @pl.when(first_ref[i] == 1)
        def _():
            y_ref[...] = jnp.zeros(y_ref.shape, F32)

        rows = row0 + lax.broadcasted_iota(jnp.int32, (MOE_ROWS, 1), 0)
        mine = (rows >= lo_ref[i]) & (rows < hi_ref[i])
        _store_row_slabs(y_ref, jnp.where(mine, y, _load_row_slabs(y_ref, MOE_ROWS, Y_SLABS)))


def _experts(items, xs, wgu, bgu, wd, bd):
    blk, exp, lo, hi, first = items
    newexp = jnp.concatenate([jnp.ones((1,), jnp.int32), (exp[1:] != exp[:-1]).astype(jnp.int32)])
    n_items = blk.shape[0]
    n_rows = xs.shape[0] // X_SLABS
    return pl.pallas_call(
        _experts_kernel,
        grid_spec=pltpu.PrefetchScalarGridSpec(
            num_scalar_prefetch=6,
            grid=(n_items,),
            in_specs=[pl.BlockSpec((MOE_ROWS * X_SLABS, LANES), lambda i, b, e, *_: (b[i], 0)),
                      pl.BlockSpec((1, D_MODEL, 2 * EXPERT_FF), lambda i, b, e, *_: (e[i], 0, 0)),
                      pl.BlockSpec((1, 1, 2 * EXPERT_FF), lambda i, b, e, *_: (e[i], 0, 0)),
                      pl.BlockSpec((1, EXPERT_FF, D_MODEL), lambda i, b, e, *_: (e[i], 0, 0)),
                      pl.BlockSpec((1, 1, D_MODEL), lambda i, b, e, *_: (e[i], 0, 0))],
            out_specs=pl.BlockSpec((MOE_ROWS * Y_SLABS, LANES), lambda i, b, e, *_: (b[i], 0)),
            scratch_shapes=[pltpu.VMEM((D_MODEL, 2 * EXPERT_FF), BF16), pltpu.VMEM((EXPERT_FF, D_MODEL), BF16)]),
        out_shape=jax.ShapeDtypeStruct((n_rows * Y_SLABS, LANES), F32),
        compiler_params=pltpu.CompilerParams(dimension_semantics=("arbitrary",),
                                             vmem_limit_bytes=EXPERTS_VMEM_LIMIT),
        name="moe_experts",
    )(blk, exp, lo, hi, first, newexp, xs, wgu, bgu, wd, bd)


def _work_items(counts, n_rows):
    n_blocks = n_rows // MOE_ROWS
    n_items = n_blocks + N_EXPERTS - 1
    ends = jnp.cumsum(counts)
    starts = ends - counts
    nb = jnp.where(counts > 0, (ends - 1) // MOE_ROWS - starts // MOE_ROWS + 1, 0)
    item_end = jnp.cumsum(nb)
    item_start = item_end - nb
    total = item_end[-1]
    i = jnp.arange(n_items, dtype=jnp.int32)
    e = jnp.minimum(jnp.sum(item_end[None, :] <= i[:, None], axis=1), N_EXPERTS - 1).astype(jnp.int32)
    blk = (starts[e] // MOE_ROWS + (i - item_start[e])).astype(jnp.int32)
    valid = i < total
    lo = jnp.maximum(starts[e], blk * MOE_ROWS)
    hi = jnp.minimum(ends[e], (blk + 1) * MOE_ROWS)
    last = jnp.maximum(total - 1, 0)
    blk = jnp.where(valid, blk, blk[last])
    e = jnp.where(valid, e, e[last])
    lo = jnp.where(valid, lo, 0).astype(jnp.int32)
    hi = jnp.where(valid, hi, 0).astype(jnp.int32)
    prev = jnp.concatenate([jnp.full((1,), -1, jnp.int32), blk[:-1]])
    first = (blk != prev).astype(jnp.int32)
    return blk, e, lo, hi, first, starts


def _combine_kernel(pos_ref, pos_next_ref, h_ref, gate_ref, g_ref, ys_ref, o_first_ref, o_rest_ref, buf, sem,
                    *, n_first):
    tm = h_ref.shape[0]
    i = pl.program_id(0)
    n = pl.num_programs(0)
    cur = i % 2

    def fetch(p_ref, slot):
        def start(tt, c):
            for u in range(POS_GROUP):
                for k in range(TOP_K):
                    _row_copy(_slab(ys_ref, p_ref[0, tt, k * POS_GROUP + u], Y_SLABS),
                              buf.at[slot, k, pl.ds((tt * POS_GROUP + u) * BUF_PITCH, Y_SLABS), :],
                              sem.at[slot]).start(priority=k % 2)
            return c

        lax.fori_loop(0, tm // POS_GROUP, start, 0)

    @pl.when(i == 0)
    def _():
        fetch(pos_ref, 0)

    for slot in range(2):
        @pl.when((i + 1 < n) & (cur != slot))
        def _():
            fetch(pos_next_ref, slot)

    def wait(t, c):
        for k in range(TOP_K):
            _row_copy(_slab(ys_ref, 0, Y_SLABS), _slab(buf, 0, Y_SLABS, (cur, 0)), sem.at[cur]).wait()
        return c

    lax.fori_loop(0, tm, wait, 0, unroll=4)
    acc = h_ref[...]
    for k in range(TOP_K):
        rows = jnp.concatenate([buf[cur, k, pl.ds(c, tm, stride=BUF_PITCH), :] for c in range(Y_SLABS)], axis=1)
        acc = acc + gate_ref[:, k:k + 1] * rows
    res = _rms(acc, g_ref[...])

    @pl.when(i < n_first)
    def _():
        o_first_ref[...] = res

    @pl.when(i >= n_first)
    def _():
        o_rest_ref[...] = res


def _combine(pos, h2d, gates_t, g_final, ys, *, tm, rows_first):
    r = h2d.shape[0]
    n_tiles = r // tm
    n_first = rows_first // tm
    row = lambda i: (i, 0)
    return pl.pallas_call(
        functools.partial(_combine_kernel, n_first=n_first),
        grid=(n_tiles,),
        in_specs=[pl.BlockSpec((1, tm // POS_GROUP, TOP_K * POS_GROUP), lambda i: (i, 0, 0),
                               memory_space=pltpu.SMEM),
                  pl.BlockSpec((1, tm // POS_GROUP, TOP_K * POS_GROUP),
                               lambda i: (jnp.minimum(i + 1, n_tiles - 1), 0, 0), memory_space=pltpu.SMEM),
                  pl.BlockSpec((tm, D_MODEL), row),
                  pl.BlockSpec((tm, TOP_K), row),
                  pl.BlockSpec((1, D_MODEL), lambda i: (0, 0)),
                  pl.BlockSpec(memory_space=pl.ANY)],
        out_specs=[pl.BlockSpec((tm, D_MODEL), lambda i: (jnp.minimum(i, n_first - 1), 0)),
                   pl.BlockSpec((tm, D_MODEL), lambda i: (jnp.maximum(i - n_first, 0), 0))],
        out_shape=[jax.ShapeDtypeStruct((rows_first, D_MODEL), F32),
                   jax.ShapeDtypeStruct((r - rows_first, D_MODEL), F32)],
        scratch_shapes=[pltpu.VMEM((2, TOP_K, tm * BUF_PITCH, LANES), F32), pltpu.SemaphoreType.DMA((2,))],
        compiler_params=_cparams(("arbitrary",)),
        name="moe_combine",
    )(pos, pos, h2d, gates_t, g_final, ys)


def _moe(h2d, xn2d, w, rows_first):
    r = h2d.shape[0]
    tm_router = _pick_tile(r, 512)
    tm_scatter = _pick_tile(r, 512)
    tm_combine = _pick_tile(r, 256)
    idx, gates, rank, cnt = _router(xn2d, w["wr_t"], w["br"], w["tri"][:tm_router, :tm_router], tm=tm_router)
    counts = cnt[:, 0]
    blk, e, lo, hi, first, starts = _work_items(counts, TOP_K * r)
    hot = idx[:, :, None] == jnp.arange(N_EXPERTS, dtype=jnp.int32)
    pos = jnp.sum(jnp.where(hot, starts.astype(jnp.int32), 0), axis=-1) + rank
    tiles = lambda tm: (pos.reshape(TOP_K, -1, tm // POS_GROUP, POS_GROUP).transpose(1, 2, 0, 3)
                        .reshape(-1, tm // POS_GROUP, TOP_K * POS_GROUP))
    xs = _scatter(tiles(tm_scatter), xn2d, tm=tm_scatter)
    ys = _experts((blk, e, lo, hi, first), xs, w["wgu"], w["bgu"], w["wd"], w["bd"])
    return _combine(tiles(tm_combine), h2d, gates.T, w["g_final"], ys, tm=tm_combine, rows_first=rows_first)


def _pick_tile(n, cap):
    t = cap
    while n % t:
        t //= 2
    return t


def kernel(x_prompt, x_sample, mem_prompt, cache_a_k, cache_a_v, cache_b_k, cache_b_v, cache_mem_k, cache_mem_v, g_mix, w_in, rel_bias, lam_q1, lam_k1, lam_q2, lam_k2, g_subln, w_out, g_xq, g_xmem, w_xq, w_xk, w_xv, w_xo, g_moe, w_router, b_router, w_gate_up, b_gate_up, w_down, b_down, g_final):
    assert g_mix.shape[0] == 1, "single-layer kernel"
    b, s, d = x_prompt.shape
    db, dl, _ = x_sample.shape
    past = cache_b_k.shape[2]
    a_len = cache_a_k.shape[2]
    mlen = mem_prompt.shape[1]
    keep_a = min(BAND, s)
    assert d == D_MODEL and s % B_QBLK == 0 and s >= A_WIN and dl == CHUNK and past % CHUNK == 0
    assert B_QBLK == 2 * B_KBLK

    row1 = lambda v: v.reshape(1, -1).astype(F32)
    w_in_bf = w_in[0].astype(BF16)
    w_out_bf = w_out[0].astype(BF16)
    weights = dict(
        wr_t=w_router[0].T.astype(BF16), br=b_router[0].reshape(N_EXPERTS, 1).astype(F32),
        tri=jnp.asarray(np.triu(np.ones((512, 512), np.float32), 1), BF16),
        wgu=w_gate_up[0].astype(F32), bgu=b_gate_up[0].reshape(N_EXPERTS, 1, -1).astype(F32),
        wd=w_down[0].astype(F32), bd=b_down[0].reshape(N_EXPERTS, 1, -1).astype(F32),
        g_final=row1(g_final))
    lam = (jnp.exp(jnp.sum(lam_q1[0].astype(F32) * lam_k1[0].astype(F32)))
           - jnp.exp(jnp.sum(lam_q2[0].astype(F32) * lam_k2[0].astype(F32))) + LAM_INIT).reshape(1)
    table = _band_table(rel_bias[0])
    g_sub = row1(g_subln[0])

    tm_p = _pick_tile(s, 512)
    tabs_p = _rope_tables(np.arange(s))
    p_in = _inproj(x_prompt.reshape(b * s, d), row1(g_mix[0]), w_in_bf, *tabs_p,
                   tm=tm_p, n_pos_blocks=s // tm_p, tail_div=s // tm_p if keep_a == tm_p else 1, emit_vt=True)
    tabs_s = _rope_tables(np.tile(past + np.arange(dl), db))
    s_in = _inproj(x_sample.reshape(db * dl, d), row1(g_mix[0]), w_in_bf, *tabs_s,
                   tm=db * dl, n_pos_blocks=1, tail_div=1, emit_vt=False)

    def split(outs, nb, nl):
        return [o.reshape(nb, nl, A_WIDTH) for o in outs[:6]]

    aq, ak, _, bq, bk, _ = split(p_in, b, s)
    avt = p_in[10].reshape((b, s // A_QBLK) + p_in[10].shape[1:])
    bvt = p_in[11].reshape((b, s // B_KBLK) + p_in[11].shape[1:])
    saq, sak, sav, sbq, sbk, sbv = split(s_in, db, dl)

    oa = _attn_a_prompt(aq, ak, avt, table.transpose(0, 2, 1))
    ob = _attn_b_prompt(lam, bq, bk, bvt, g_sub)
    soa = _attn_a_sample(saq, cache_a_k[0].reshape(db, a_len * N_HEADS_A, HEAD_DIM),
                         cache_a_v[0].reshape(db, a_len * N_HEADS_A, HEAD_DIM), sak, sav, table[:, :dl, :BAND + dl])
    sob = _attn_b_sample(lam, sbq, cache_b_k[0].reshape(db, past, B_WIDTH),
                         cache_b_v[0].reshape(db, past * N_HEADS_B, 2 * HEAD_DIM), sbk, sbv, g_sub)

    mkf, mvf, mkb, mvb = _memkv(mem_prompt.reshape(b * mlen, d), row1(g_xmem[0]),
                                w_xk[0].astype(BF16), w_xv[0].astype(BF16), tm=mlen)
    post_w = (w_out_bf[:A_WIDTH], w_out_bf[A_WIDTH:], row1(g_xq[0]), w_xq[0].astype(BF16))
    tm_post = _pick_tile(math.gcd(s, b * s + db * dl), 256)
    rows_p, rows_s = b * s, db * dl
    sob_t = sob.transpose(0, 2, 1).reshape(db, N_HEADS_B, 2 * HEAD_DIM, dl)
    h_xn = _post(x_prompt.reshape(rows_p, d), oa, ob,
                 *post_w, mkb.reshape(b, mlen, d), mvb.reshape(b, mlen, d),
                 w_xo[0].astype(BF16), row1(g_moe[0]), tm=tm_post, tiles_per_batch=s // tm_post,
                 total_rows=rows_p + rows_s)
    h_all, xn_all = _post(x_sample.reshape(rows_s, d), soa.transpose(0, 2, 1), sob_t,
                          *post_w, cache_mem_k[0].reshape(db, mlen, d).astype(BF16),
                          cache_mem_v[0].reshape(db, mlen, d).astype(BF16),
                          w_xo[0].astype(BF16), row1(g_moe[0]), tm=dl, tiles_per_batch=1,
                          total_rows=rows_p + rows_s, row_offset=rows_p, into=h_xn)

    y_prompt, y_sample = _moe(h_all, xn_all, weights, rows_p)
    y_prompt = y_prompt.reshape(b, s, d)
    y_sample = y_sample.reshape(db, dl, d)

    def heads(x2d, nb, nl, nh):
        return x2d.reshape(1, nb, nl, nh, -1)

    akf, avf, bkf, bvf = p_in[6:10]
    if keep_a != tm_p:
        akf = akf.reshape(b, s, A_WIDTH)[:, s - keep_a:]
        avf = avf.reshape(b, s, A_WIDTH)[:, s - keep_a:]
    sakf, savf, sbkf, sbvf = s_in[6:]
    return (y_prompt, y_sample,
            heads(akf, b, keep_a, N_HEADS_A), heads(avf, b, keep_a, N_HEADS_A),
            heads(bkf, b, s, 2 * N_HEADS_B), heads(bvf, b, s, N_HEADS_B),
            heads(mkf, b, mlen, N_HEADS_M), heads(mvf, b, mlen, N_HEADS_M),
            heads(sakf, db, dl, N_HEADS_A), heads(savf, db, dl, N_HEADS_A),
            heads(sbkf, db, dl, 2 * N_HEADS_B), heads(sbvf, db, dl, N_HEADS_B))
```

```python
import functools
import math

import numpy as np
import jax
import jax.numpy as jnp
from jax import lax
from jax.experimental import pallas as pl
from jax.experimental.pallas import tpu as pltpu

F32 = jnp.float32
BF16 = jnp.bfloat16

D_MODEL = 1024
CHUNK = 64
LEFT_CHUNKS = 8
BAND = LEFT_CHUNKS * CHUNK
HEAD_DIM = 64
A_WIDTH = 512
B_WIDTH = 512
N_HEADS_A = 8
N_HEADS_B = 4
REL_CLIP = 128
ROT_DIM = 16
ROPE_THETA = 500000.0
N_HEADS_M = 4
HEAD_DIM_M = 256
N_EXPERTS = 32
TOP_K = 4
EXPERT_FF = 1024
SWIGLU_LIMIT = 7.0
SWIGLU_ALPHA = 1.702
RMS_EPS = 1e-5
NEG_INF = -1e30
LAM_INIT = 0.8 - 0.6 * math.exp(-0.3 * 0)
LOG2E = math.log2(math.e)

LANES = 128
A_QBLK = 4 * CHUNK
A_WIN = BAND + A_QBLK
B_QBLK = 512
B_KBLK = 256
VT_ONES = 16
BUF_PITCH = 9
POS_GROUP = 16
MOE_ROWS = 512
X_SLABS = D_MODEL // 2 // LANES
Y_SLABS = D_MODEL // LANES
VMEM_LIMIT = 48 * 1024 * 1024
EXPERTS_VMEM_LIMIT = 56 * 1024 * 1024


def _cparams(sem):
    return pltpu.CompilerParams(dimension_semantics=sem, vmem_limit_bytes=VMEM_LIMIT)


def _rms(x, g):
    return x * lax.rsqrt(jnp.mean(x * x, axis=-1, keepdims=True) + RMS_EPS) * g


HI16 = 0xFFFF0000


def _pack_bf16_pairs(x):
    n = x.shape[1] // 2
    bits = lambda v: lax.bitcast_convert_type(v.astype(BF16).astype(F32), jnp.uint32)
    return (bits(x[:, :n]) >> 16) | (bits(x[:, n:]) & jnp.uint32(HI16))


def _unpack_bf16_pairs(w):
    lo = lax.bitcast_convert_type(w << 16, F32)
    hi = lax.bitcast_convert_type(w & jnp.uint32(HI16), F32)
    return jnp.concatenate([lo, hi], axis=1).astype(BF16)


def _store_row_slabs(ref, x, lead=()):
    m, n = x.shape[0], x.shape[1] // LANES
    for c in range(n):
        ref[lead + (pl.ds(c, m, stride=n), slice(None))] = x[:, c * LANES:(c + 1) * LANES]


def _load_row_slabs(ref, m, n, lead=()):
    return jnp.concatenate([ref[lead + (pl.ds(c, m, stride=n), slice(None))] for c in range(n)], axis=1)


def _dot_nt(a, b):
    return lax.dot_general(a, b, (((1,), (1,)), ((), ())), preferred_element_type=F32)


def _store_heads64(ref, x):
    m, n_heads = x.shape[0], x.shape[1] // HEAD_DIM
    for c in range(x.shape[1] // LANES):
        grp = x[:, c * LANES:(c + 1) * LANES]
        ref[pl.ds(2 * c, m, stride=n_heads), :] = grp[:, :HEAD_DIM]
        ref[pl.ds(2 * c + 1, m, stride=n_heads), :] = pltpu.roll(grp, HEAD_DIM, 1)[:, :HEAD_DIM]


def _inproj_kernel(x_ref, g_ref, w_ref, cos_ref, sa_ref, sb_ref,
                   aq_ref, ak_ref, av_ref, bq_ref, bk_ref, bv_ref,
                   akf_ref, avf_ref, bkf_ref, bvf_ref, *maybe_vt_refs, tail_div):
    xn = _rms(x_ref[...], g_ref[...]).astype(BF16)
    is_tail = pl.program_id(0) % tail_div == tail_div - 1

    def seg(s):
        return jnp.dot(xn, w_ref[:, s * A_WIDTH:(s + 1) * A_WIDTH], preferred_element_type=F32)

    def rope(z):
        cols = []
        for c in range(z.shape[1] // LANES):
            zc = z[:, c * LANES:(c + 1) * LANES]
            cols.append(zc * cos_ref[...] + pltpu.roll(zc, 8, 1) * sa_ref[...]
                        + pltpu.roll(zc, LANES - 8, 1) * sb_ref[...])
        return jnp.concatenate(cols, axis=1)

    scale = HEAD_DIM ** -0.5
    aq_ref[...] = (seg(0) * (scale * LOG2E)).astype(BF16)
    ak = seg(1)
    ak_ref[...] = ak.astype(BF16)
    av = seg(2)
    av_ref[...] = av.astype(BF16)
    bq_ref[...] = (rope(seg(3)) * (scale * LOG2E)).astype(BF16)
    bk = rope(seg(4))
    bk_ref[...] = bk.astype(BF16)
    _store_heads64(bkf_ref, bk)
    bv = seg(5)
    bv_ref[...] = bv.astype(BF16)
    _store_row_slabs(bvf_ref, bv)
    for vt_ref, v in zip(maybe_vt_refs, (av, bv)):
        vt = v.T.astype(BF16)
        blk = vt_ref.shape[3]
        for c in range(vt_ref.shape[0]):
            for g in range(vt_ref.shape[1]):
                vt_ref[c, g, 0:LANES, :] = vt[g * LANES:(g + 1) * LANES, c * blk:(c + 1) * blk]
                vt_ref[c, g, LANES:LANES + VT_ONES, :] = jnp.ones((VT_ONES, blk), BF16)

    @pl.when(is_tail)
    def _():
        _store_heads64(akf_ref, ak)
        _store_heads64(avf_ref, av)


def _inproj(x2d, g, w_bf, cos_t, sa_t, sb_t, *, tm, n_pos_blocks, tail_div, emit_vt):
    r = x2d.shape[0]
    n_tiles = r // tm
    row = lambda i: (i, 0)
    const = lambda i: (0, 0)
    pos = lambda i: (i % n_pos_blocks, 0)
    tail = lambda i: (i // tail_div, 0)
    bf = jax.ShapeDtypeStruct((r, A_WIDTH), BF16)
    blk = lambda im: pl.BlockSpec((tm, A_WIDTH), im)
    h64 = lambda im: pl.BlockSpec((tm * N_HEADS_A, HEAD_DIM), im)
    f_tail = jax.ShapeDtypeStruct((r // tail_div * N_HEADS_A, HEAD_DIM), F32)
    out_specs = [blk(row)] * 6 + [h64(tail), h64(tail), h64(row),
                                  pl.BlockSpec((tm * N_HEADS_B, 2 * HEAD_DIM), row)]
    out_shape = [bf] * 6 + [f_tail, f_tail,
                            jax.ShapeDtypeStruct((r * 2 * N_HEADS_B, HEAD_DIM), F32),
                            jax.ShapeDtypeStruct((r * N_HEADS_B, 2 * HEAD_DIM), F32)]
    if emit_vt:
        for kb in (A_QBLK, B_KBLK):
            grp = (A_WIDTH // LANES, LANES + VT_ONES, kb)
            out_specs.append(pl.BlockSpec((tm // kb,) + grp, lambda i: (i, 0, 0, 0)))
            out_shape.append(jax.ShapeDtypeStruct((r // kb,) + grp, BF16))
    return pl.pallas_call(
        functools.partial(_inproj_kernel, tail_div=tail_div),
        grid=(n_tiles,),
        in_specs=[pl.BlockSpec((tm, D_MODEL), row), pl.BlockSpec((1, D_MODEL), const),
                  pl.BlockSpec((D_MODEL, 3 * D_MODEL), const),
                  pl.BlockSpec((tm, LANES), pos), pl.BlockSpec((tm, LANES), pos),
                  pl.BlockSpec((tm, LANES), pos)],
        out_specs=out_specs,
        out_shape=out_shape,
        compiler_params=_cparams(("arbitrary",)),
        name="inproj",
    )(x2d, g, w_bf, cos_t, sa_t, sb_t)


def _rope_tables(positions):
    half = ROT_DIM // 2
    lane = np.arange(LANES) % HEAD_DIM
    inv_freq = np.power(ROPE_THETA, -np.arange(half, dtype=np.float64) / half)
    ang = np.asarray(positions, np.float64)[:, None] * inv_freq[None, :]
    cos = np.cos(ang)[:, lane % half]
    sin = np.sin(ang)[:, lane % half]
    in_rot = (lane < ROT_DIM)[None, :]
    first = (lane < half)[None, :]
    cos_t = np.where(in_rot, cos, 1.0)
    sa_t = np.where(in_rot & ~first, sin, 0.0)
    sb_t = np.where(first, -sin, 0.0)
    return tuple(jnp.asarray(t, F32) for t in (cos_t, sa_t, sb_t))


def _band_table(rel_bias):
    r = np.arange(A_QBLK)[:, None]
    j = np.arange(BAND + A_WIN)[None, :]
    dchunk = LEFT_CHUNKS + r // CHUNK - j // CHUNK
    allowed = (dchunk >= 0) & (dchunk <= LEFT_CHUNKS)
    bias = rel_bias.astype(F32) * LOG2E
    n_seq = BAND + A_WIN + A_QBLK - 1
    n_hi = BAND + A_QBLK - 1 - REL_CLIP
    seq = jnp.concatenate([jnp.broadcast_to(bias[:, -1:], (N_HEADS_A, n_hi)), bias[:, ::-1],
                           jnp.broadcast_to(bias[:, :1], (N_HEADS_A, n_seq - n_hi - bias.shape[1]))], axis=1)
    seq = jnp.roll(seq, -(A_QBLK - 1), axis=1)
    toe = jnp.tile(seq, (1, A_QBLK))[:, :A_QBLK * (n_seq - 1)].reshape(N_HEADS_A, A_QBLK, n_seq - 1)
    return jnp.where(jnp.asarray(allowed)[None], toe[:, :, :BAND + A_WIN], NEG_INF)


def _softmax_pv(s_list, v_list, exp_fn=jnp.exp):
    m = s_list[0].max(axis=-1, keepdims=True)
    for s in s_list[1:]:
        m = jnp.maximum(m, s.max(axis=-1, keepdims=True))
    l = None
    o = None
    for s, v in zip(s_list, v_list):
        p = exp_fn(s - m)
        ls = p.sum(axis=-1, keepdims=True)
        os_ = jnp.dot(p.astype(BF16), v, preferred_element_type=F32)
        l = ls if l is None else l + ls
        o = os_ if o is None else o + os_
    return o / l


def _attn_a_prompt_kernel(q_ref, k_ref, vt_ref, tab_ref, o_ref):
    i = pl.program_id(1)
    n_left = BAND // A_QBLK
    n_sub = A_WIN // A_QBLK
    blk0 = jnp.maximum(i - n_left, 0)
    off = pl.multiple_of(jnp.maximum(n_left - i, 0) * A_QBLK, A_QBLK)
    q = q_ref[0]
    lane = lax.broadcasted_iota(jnp.int32, (A_QBLK, LANES), 1)

    def scores(h):
        cols = slice(h // 2 * LANES, (h // 2 + 1) * LANES)
        qp = q[:, cols]
        qh = jnp.where((lane >= HEAD_DIM) == bool(h % 2), qp, jnp.zeros_like(qp))
        return [_dot_nt(k_ref[0, pl.ds(pl.multiple_of((blk0 + c) * A_QBLK, A_QBLK), A_QBLK), cols], qh)
                + tab_ref[h, pl.ds(off + c * A_QBLK, A_QBLK), :] for c in range(n_sub)]

    def attend(h, ss):
        m = _col_reduce(ss[0], jnp.maximum, jnp.max)
        for s in ss[1:]:
            m = jnp.maximum(m, _col_reduce(s, jnp.maximum, jnp.max))
        o_t = None
        for c in range(n_sub):
            p = jnp.exp2(ss[c] - m).astype(BF16)
            os_ = jnp.dot(vt_ref[0, blk0 + c, h // 2], p, preferred_element_type=F32)
            o_t = os_ if o_t is None else o_t + os_
        return o_t[h % 2 * HEAD_DIM:(h % 2 + 1) * HEAD_DIM] / o_t[LANES:LANES + 1]

    halves = []
    ss = scores(0)
    for h in range(N_HEADS_A):
        ss_next = scores(h + 1) if h + 1 < N_HEADS_A else None
        halves.append(attend(h, ss))
        ss = ss_next
    o_ref[0] = jnp.concatenate(halves, axis=0).astype(BF16)


def _attn_a_prompt(aq, ak, avt, table_t):
    b, l, _ = aq.shape
    n_kb = avt.shape[1]
    return pl.pallas_call(
        _attn_a_prompt_kernel,
        grid=(b, l // A_QBLK),
        in_specs=[pl.BlockSpec((1, A_QBLK, A_WIDTH), lambda bb, i: (bb, i, 0)),
                  pl.BlockSpec((1, l, A_WIDTH), lambda bb, i: (bb, 0, 0)),
                  pl.BlockSpec((1,) + avt.shape[1:], lambda bb, i: (bb, 0, 0, 0, 0)),
                  pl.BlockSpec(table_t.shape, lambda bb, i: (0, 0, 0))],
        out_specs=pl.BlockSpec((1, A_WIDTH, A_QBLK), lambda bb, i: (bb, 0, i)),
        out_shape=jax.ShapeDtypeStruct((b, A_WIDTH, l), BF16),
        compiler_params=_cparams(("arbitrary", "arbitrary")),
        name="attn_a_prompt",
    )(aq, ak, avt, table_t)


def _attn_a_sample_kernel(q_ref, kc_ref, vc_ref, k_ref, v_ref, tab_ref, o_ref):
    q = q_ref[0]
    ko = k_ref[0]
    vo = v_ref[0]
    a_len = kc_ref.shape[1] // N_HEADS_A
    lq = q.shape[0]
    c0 = BAND - a_len
    outs = []
    for h in range(N_HEADS_A):
        sl = slice(h * HEAD_DIM, (h + 1) * HEAD_DIM)
        kc = kc_ref[0, pl.ds(h, a_len, stride=N_HEADS_A), :].astype(BF16)
        vc = vc_ref[0, pl.ds(h, a_len, stride=N_HEADS_A), :].astype(BF16)
        s_c = _dot_nt(q[:, sl], kc) + tab_ref[h, :, c0:BAND]
        s_o = _dot_nt(q[:, sl], ko[:, sl]) + tab_ref[h, :, BAND:BAND + lq]
        outs.append(_softmax_pv([s_c, s_o], [vc, vo[:, sl]], exp_fn=jnp.exp2))
    o_ref[0] = jnp.concatenate(outs, axis=1).astype(BF16)


def _attn_a_sample(aq, cache_k, cache_v, ak, av, table):
    b, l, _ = aq.shape
    bs = lambda n: pl.BlockSpec((1, n, A_WIDTH), lambda bb: (bb, 0, 0))
    cache = pl.BlockSpec((1,) + cache_k.shape[1:], lambda bb: (bb, 0, 0))
    return pl.pallas_call(
        _attn_a_sample_kernel,
        grid=(b,),
        in_specs=[bs(l), cache, cache, bs(l), bs(l),
                  pl.BlockSpec(table.shape, lambda bb: (0, 0, 0))],
        out_specs=bs(l),
        out_shape=jax.ShapeDtypeStruct((b, l, A_WIDTH), BF16),
        compiler_params=_cparams(("arbitrary",)),
        name="attn_a_sample",
    )(aq, cache_k, cache_v, ak, av, table)


def _subnorm(o, g):
    return _rms(o, g) * (1.0 - LAM_INIT)


def _col_reduce(x, op, reduce_fn):
    n = x.shape[0]
    while n > 8:
        n //= 2
        x = op(x[:n], x[n:])
    return reduce_fn(x, axis=0, keepdims=True)


def _attn_b_prompt_kernel(lam_ref, q_ref, qn_ref, k_ref, vt_ref, mask_ref, g_ref, o_ref,
                          sa_sc, sb_sc, m_sc, acc_sc):
    i = pl.program_id(2)
    tk = vt_ref.shape[4]
    lane = lax.broadcasted_iota(jnp.int32, q_ref.shape[1:], 1)

    def split_maps(q):
        zero = jnp.zeros_like(q)
        return jnp.where(lane < HEAD_DIM, q, zero), jnp.where(lane >= HEAD_DIM, q, zero)

    qs = split_maps(q_ref[0])
    m_sc[...] = jnp.full(m_sc.shape, NEG_INF, F32)
    acc_sc[...] = jnp.zeros(acc_sc.shape, F32)

    def scores(j, s_sc, qmaps=qs):
        kb = k_ref[0, pl.ds(pl.multiple_of(j * tk, tk), tk), :]
        for a in range(2):
            s_sc[a] = _dot_nt(kb, qmaps[a])

    def consume(j, s_sc, mask):
        vt = vt_ref[0, j, 0]
        for a in range(2):
            s = s_sc[a]
            if mask is not None:
                s = s + mask
            m_old = m_sc[a]
            m_new = jnp.maximum(m_old, _col_reduce(s, jnp.maximum, jnp.max))
            alpha = jnp.exp2(m_old - m_new)
            p = jnp.exp2(s - m_new).astype(BF16)
            m_sc[a] = m_new
            acc_sc[a] = alpha * acc_sc[a] + jnp.dot(vt, p, preferred_element_type=F32)

    @pl.when(i == 0)
    def _():
        scores(0, sa_sc)

    def body(jj, c):
        j = 2 * jj
        scores(j + 1, sb_sc)
        consume(j, sa_sc, None)
        scores(j + 2, sa_sc)
        consume(j + 1, sb_sc, None)
        return c

    lax.fori_loop(0, i, body, 0)
    scores(2 * i + 1, sb_sc)
    consume(2 * i, sa_sc, mask_ref[0])
    scores(0, sa_sc, split_maps(qn_ref[0]))
    consume(2 * i + 1, sb_sc, mask_ref[1])
    o_t = (acc_sc[0, 0:LANES] / acc_sc[0, LANES:LANES + 1]
           - lam_ref[0] * (acc_sc[1, 0:LANES] / acc_sc[1, LANES:LANES + 1]))
    inv = lax.rsqrt(jnp.mean(o_t * o_t, axis=0, keepdims=True) + RMS_EPS)
    o_ref[0, 0] = (o_t * inv * (g_ref[...] * (1.0 - LAM_INIT))).astype(BF16)


def _chunk_causal_masks_t(tk, tq):
    kc = (np.arange(tq // tk)[:, None, None] * tk + np.arange(tk)[None, :, None]) // CHUNK
    qc = np.arange(tq)[None, None, :] // CHUNK
    return jnp.asarray(np.where(kc <= qc, 0.0, NEG_INF), F32)


def _attn_b_prompt(lam, bq, bk, bvt, g_sub):
    b, l, _ = bq.shape
    tq = B_QBLK
    tk = B_KBLK
    hw = 2 * HEAD_DIM
    n_kb = bvt.shape[1]
    return pl.pallas_call(
        _attn_b_prompt_kernel,
        grid_spec=pltpu.PrefetchScalarGridSpec(
            num_scalar_prefetch=1,
            grid=(b, N_HEADS_B, l // tq),
            in_specs=[pl.BlockSpec((1, tq, hw), lambda bb, h, i, lam_: (bb, i, h)),
                      pl.BlockSpec((1, tq, hw), lambda bb, h, i, lam_: (bb, jnp.minimum(i + 1, l // tq - 1), h)),
                      pl.BlockSpec((1, l, hw), lambda bb, h, i, lam_: (bb, 0, h)),
                      pl.BlockSpec((1, n_kb, 1, hw + VT_ONES, tk), lambda bb, h, i, lam_: (bb, 0, h, 0, 0)),
                      pl.BlockSpec((tq // tk, tk, tq), lambda bb, h, i, lam_: (0, 0, 0)),
                      pl.BlockSpec((hw, 1), lambda bb, h, i, lam_: (0, 0))],
            out_specs=pl.BlockSpec((1, 1, hw, tq), lambda bb, h, i, lam_: (bb, h, 0, i)),
            scratch_shapes=[pltpu.VMEM((2, tk, tq), F32), pltpu.VMEM((2, tk, tq), F32),
                            pltpu.VMEM((2, 1, tq), F32), pltpu.VMEM((2, hw + VT_ONES, tq), F32)]),
        out_shape=jax.ShapeDtypeStruct((b, N_HEADS_B, hw, l), BF16),
        compiler_params=_cparams(("arbitrary", "arbitrary", "arbitrary")),
        name="attn_b_prompt",
    )(lam, bq, bq, bk, bvt, _chunk_causal_masks_t(tk, tq), g_sub.reshape(hw, 1))


def _attn_b_sample_kernel(lam_ref, q_ref, kc_ref, vc_ref, k_ref, v_ref, g_ref, o_ref):
    q = q_ref[0]
    kc = kc_ref[0].astype(BF16)
    n_past = vc_ref.shape[1] // N_HEADS_B
    vc = vc_ref[0, pl.ds(pl.program_id(1), n_past, stride=N_HEADS_B), :].astype(BF16)
    ko = k_ref[0]
    vo = v_ref[0]
    outs = []
    for a in range(2):
        sl = slice(a * HEAD_DIM, (a + 1) * HEAD_DIM)
        s_c = _dot_nt(q[:, sl], kc[:, sl])
        s_o = _dot_nt(q[:, sl], ko[:, sl])
        outs.append(_softmax_pv([s_c, s_o], [vc, vo], exp_fn=jnp.exp2))
    o = outs[0] - lam_ref[0] * outs[1]
    o_ref[0] = _subnorm(o, g_ref[...]).astype(BF16)


def _attn_b_sample(lam, bq, cache_k, cache_v, bk, bv, g_sub):
    b, l, _ = bq.shape
    p = cache_k.shape[1]
    hw = 2 * HEAD_DIM
    bs = lambda n: pl.BlockSpec((1, n, hw), lambda bb, h, lam_: (bb, 0, h))
    return pl.pallas_call(
        _attn_b_sample_kernel,
        grid_spec=pltpu.PrefetchScalarGridSpec(
            num_scalar_prefetch=1,
            grid=(b, N_HEADS_B),
            in_specs=[bs(l), bs(p),
                      pl.BlockSpec((1, p * N_HEADS_B, hw), lambda bb, h, lam_: (bb, 0, 0)),
                      bs(l), bs(l),
                      pl.BlockSpec((1, hw), lambda bb, h, lam_: (0, 0))],
            out_specs=bs(l)),
        out_shape=jax.ShapeDtypeStruct((b, l, B_WIDTH), BF16),
        compiler_params=_cparams(("arbitrary", "arbitrary")),
        name="attn_b_sample",
    )(lam, bq, cache_k, cache_v, bk, bv, g_sub)


def _memkv_kernel(m_ref, g_ref, wk_ref, wv_ref, kf_ref, vf_ref, kb_ref, vb_ref):
    mn = _rms(m_ref[...], g_ref[...]).astype(BF16)
    k = jnp.dot(mn, wk_ref[...], preferred_element_type=F32)
    v = jnp.dot(mn, wv_ref[...], preferred_element_type=F32)
    kf_ref[...] = k
    vf_ref[...] = v
    kb_ref[...] = k.astype(BF16)
    vb_ref[...] = v.astype(BF16)


def _memkv(mem2d, g, wk, wv, *, tm):
    r = mem2d.shape[0]
    row = lambda i: (i, 0)
    const = lambda i: (0, 0)
    blk = pl.BlockSpec((tm, D_MODEL), row)
    wspec = pl.BlockSpec((D_MODEL, D_MODEL), const)
    f = jax.ShapeDtypeStruct((r, D_MODEL), F32)
    bf = jax.ShapeDtypeStruct((r, D_MODEL), BF16)
    return pl.pallas_call(
        _memkv_kernel,
        grid=(r // tm,),
        in_specs=[blk, pl.BlockSpec((1, D_MODEL), const), wspec, wspec],
        out_specs=[blk] * 4,
        out_shape=[f, f, bf, bf],
        compiler_params=_cparams(("arbitrary",)),
        name="memkv",
    )(mem2d, g, wk, wv)


def _post_kernel(*refs, n_real):
    h_ref, xn_ref = refs[-2:]

    @pl.when(pl.program_id(0) < n_real)
    def _():
        _post_tile(*refs[:11], h_ref, xn_ref)

    @pl.when(pl.program_id(0) >= n_real)
    def _():
        h_ref[...] = jnp.zeros(h_ref.shape, h_ref.dtype)
        xn_ref[...] = jnp.zeros(xn_ref.shape, xn_ref.dtype)


def _post_tile(x_ref, oa_ref, ob_ref, woa_ref, wob_ref, gq_ref, wq_ref, mk_ref, mv_ref, wo_ref,
               gm_ref, h_ref, xn_ref):
    tn = lambda a_t, w: lax.dot_general(a_t, w, (((0,), (0,)), ((), ())), preferred_element_type=F32)
    ob_t = ob_ref[0].reshape(B_WIDTH, ob_ref.shape[3])
    h1 = x_ref[...] + tn(oa_ref[0], woa_ref[...]) + tn(ob_t, wob_ref[...])
    xq = _rms(h1, gq_ref[...]).astype(BF16)
    q = (jnp.dot(xq, wq_ref[...], preferred_element_type=F32) * HEAD_DIM_M ** -0.5).astype(BF16)
    outs = []
    for h in range(N_HEADS_M):
        sl = slice(h * HEAD_DIM_M, (h + 1) * HEAD_DIM_M)
        s = _dot_nt(q[:, sl], mk_ref[0, :, sl])
        outs.append(_softmax_pv([s], [mv_ref[0, :, sl]]).astype(BF16))
    o = jnp.concatenate(outs, axis=1)
    h2 = h1 + jnp.dot(o, wo_ref[...], preferred_element_type=F32)
    h_ref[...] = h2
    _store_row_slabs(xn_ref, _pack_bf16_pairs(_rms(h2, gm_ref[...])))


def _post(x2d, oa, ob, woa, wob, gq, wq, mk, mv, wo, gm, *, tm, tiles_per_batch, total_rows,
          row_offset=0, into=None):
    r = x2d.shape[0]
    n_real = r // tm
    n_steps = n_real if into is not None else total_rows // tm
    mlen = mk.shape[1]
    row = lambda i: (jnp.minimum(i, n_real - 1), 0)
    out_row = lambda i: (row_offset // tm + i, 0)
    const = lambda i: (0, 0)
    memb = lambda i: (jnp.minimum(i, n_real - 1) // tiles_per_batch, 0, 0)
    gspec = pl.BlockSpec((1, D_MODEL), const)
    wspec = pl.BlockSpec((D_MODEL, D_MODEL), const)
    hspec = pl.BlockSpec((A_WIDTH, D_MODEL), const)
    obt_blk = lambda i: (jnp.minimum(i, n_real - 1) // tiles_per_batch, 0, 0,
                         jnp.minimum(i, n_real - 1) % tiles_per_batch)
    oat_blk = lambda i: (jnp.minimum(i, n_real - 1) // tiles_per_batch, 0,
                         jnp.minimum(i, n_real - 1) % tiles_per_batch)
    in_specs = [pl.BlockSpec((tm, D_MODEL), row),
                pl.BlockSpec((1, A_WIDTH, tm), oat_blk),
                pl.BlockSpec((1, N_HEADS_B, 2 * HEAD_DIM, tm), obt_blk),
                hspec, hspec, gspec, wspec,
                pl.BlockSpec((1, mlen, D_MODEL), memb), pl.BlockSpec((1, mlen, D_MODEL), memb),
                wspec, gspec]
    args = [x2d, oa, ob, woa, wob, gq, wq, mk, mv, wo, gm]
    aliases = {}
    if into is not None:
        aliases = {len(args): 0, len(args) + 1: 1}
        in_specs += [pl.BlockSpec(memory_space=pl.ANY)] * 2
        args += list(into)
    return pl.pallas_call(
        functools.partial(_post_kernel, n_real=n_real),
        grid=(n_steps,),
        in_specs=in_specs,
        out_specs=[pl.BlockSpec((tm, D_MODEL), out_row), pl.BlockSpec((tm * X_SLABS, LANES), out_row)],
        out_shape=[jax.ShapeDtypeStruct((total_rows, D_MODEL), F32),
                   jax.ShapeDtypeStruct((total_rows * X_SLABS, LANES), jnp.uint32)],
        input_output_aliases=aliases,
        compiler_params=_cparams(("arbitrary",)),
        name="post_attn",
    )(*args)


def _router_kernel(xn_ref, wr_ref, br_ref, tri_ref, idx_ref, gate_ref, rank_ref, cnt_ref, base_sc):
    @pl.when(pl.program_id(0) == 0)
    def _():
        base_sc[...] = jnp.zeros(base_sc.shape, F32)

    tm = xn_ref.shape[0] // X_SLABS
    xn = _unpack_bf16_pairs(_load_row_slabs(xn_ref, tm, X_SLABS))
    logits = _dot_nt(wr_ref[...], xn) + br_ref[...]
    iota_e = lax.broadcasted_iota(jnp.int32, (N_EXPERTS, tm), 0).astype(F32)
    work = logits
    vals, idxs, hots = [], [], []
    for _k in range(TOP_K):
        mx = work.max(axis=0, keepdims=True)
        ix = jnp.where(work == mx, iota_e, float(N_EXPERTS)).min(axis=0, keepdims=True)
        hot = iota_e == ix
        vals.append(mx)
        idxs.append(ix)
        hots.append(hot)
        work = jnp.where(hot, -jnp.inf, work)
    es = [jnp.exp(v - vals[0]) for v in vals]
    den = es[0] + es[1] + es[2] + es[3]
    cnt = (hots[0] | hots[1] | hots[2] | hots[3]).astype(F32)
    before = jnp.dot(cnt.astype(BF16), tri_ref[...], preferred_element_type=F32) + base_sc[...]
    for k in range(TOP_K):
        idx_ref[k:k + 1, :] = idxs[k].astype(jnp.int32)
        gate_ref[k:k + 1, :] = es[k] / den
        rank_ref[k:k + 1, :] = jnp.where(hots[k], before, 0.0).sum(axis=0, keepdims=True).astype(jnp.int32)
    base_sc[...] = base_sc[...] + cnt.sum(axis=1, keepdims=True)
    cnt_ref[...] = jnp.broadcast_to(base_sc[...], cnt_ref.shape).astype(jnp.int32)


def _router(xn2d, wr_t, br, tri, *, tm):
    r = xn2d.shape[0] // X_SLABS
    col = lambda i: (0, i)
    const = lambda i: (0, 0)
    return pl.pallas_call(
        _router_kernel,
        grid=(r // tm,),
        in_specs=[pl.BlockSpec((tm * X_SLABS, LANES), lambda i: (i, 0)),
                  pl.BlockSpec((N_EXPERTS, D_MODEL), const),
                  pl.BlockSpec((N_EXPERTS, 1), const),
                  pl.BlockSpec((tm, tm), const)],
        out_specs=[pl.BlockSpec((TOP_K, tm), col)] * 3 + [pl.BlockSpec((N_EXPERTS, LANES), const)],
        out_shape=[jax.ShapeDtypeStruct((TOP_K, r), jnp.int32),
                   jax.ShapeDtypeStruct((TOP_K, r), F32),
                   jax.ShapeDtypeStruct((TOP_K, r), jnp.int32),
                   jax.ShapeDtypeStruct((N_EXPERTS, LANES), jnp.int32)],
        scratch_shapes=[pltpu.VMEM((N_EXPERTS, 1), F32)],
        compiler_params=_cparams(("arbitrary",)),
        name="router",
    )(xn2d, wr_t, br, tri)


def _row_copy(src, dst, sem):
    return pltpu.make_async_copy(src, dst, sem)


def _slab(ref, row, n, lead=()):
    return ref.at[lead + (pl.ds(pl.multiple_of(row * n, n), n), slice(None))]


def _scatter_kernel(pos_ref, xn_ref, xs_ref, sem):
    tm = xn_ref.shape[0] // X_SLABS

    def start(tt, c):
        for u in range(POS_GROUP):
            for k in range(TOP_K):
                _row_copy(_slab(xn_ref, tt * POS_GROUP + u, X_SLABS),
                          _slab(xs_ref, pos_ref[0, tt, k * POS_GROUP + u], X_SLABS),
                          sem).start(priority=k % 2)
        return c

    lax.fori_loop(0, tm // POS_GROUP, start, 0)

    def wait(t, c):
        for k in range(TOP_K):
            _row_copy(_slab(xn_ref, 0, X_SLABS), _slab(xs_ref, 0, X_SLABS), sem).wait()
        return c

    lax.fori_loop(0, tm, wait, 0, unroll=4)


def _scatter(pos, xn2d, *, tm):
    r = xn2d.shape[0] // X_SLABS
    return pl.pallas_call(
        _scatter_kernel,
        grid=(r // tm,),
        in_specs=[pl.BlockSpec((1, tm // POS_GROUP, TOP_K * POS_GROUP), lambda i: (i, 0, 0),
                               memory_space=pltpu.SMEM),
                  pl.BlockSpec((tm * X_SLABS, LANES), lambda i: (i, 0))],
        out_specs=pl.BlockSpec(memory_space=pl.ANY),
        out_shape=jax.ShapeDtypeStruct((TOP_K * r * X_SLABS, LANES), jnp.uint32),
        scratch_shapes=[pltpu.SemaphoreType.DMA],
        compiler_params=_cparams(("arbitrary",)),
        name="moe_scatter",
    )(pos, xn2d)


def _experts_kernel(blk_ref, exp_ref, lo_ref, hi_ref, first_ref, newexp_ref,
                    x_ref, wgu_ref, bgu_ref, wd_ref, bd_ref, y_ref, wgu_bf, wd_bf):
    i = pl.program_id(0)

    @pl.when(newexp_ref[i] == 1)
    def _():
        wgu_bf[...] = wgu_ref[0].astype(BF16)
        wd_bf[...] = wd_ref[0].astype(BF16)

    x = _unpack_bf16_pairs(_load_row_slabs(x_ref, MOE_ROWS, X_SLABS))
    gu = jnp.dot(x, wgu_bf[...], preferred_element_type=F32) + bgu_ref[0]
    gate = jnp.minimum(gu[:, :EXPERT_FF], SWIGLU_LIMIT)
    up = jnp.clip(gu[:, EXPERT_FF:], -SWIGLU_LIMIT, SWIGLU_LIMIT)
    act = (up + 1.0) * (gate * jax.nn.sigmoid(SWIGLU_ALPHA * gate))
    y = jnp.dot(act.astype(BF16), wd_bf[...], preferred_element_type=F32) + bd_ref[0]
    row0 = blk_ref[i] * MOE_ROWS
    whole = (lo_ref[i] <= row0) & (hi_ref[i] >= row0 + MOE_ROWS)

    @pl.when(whole)
    def _():
        _store_row_slabs(y_ref, y)

    @pl.when(jnp.logical_not(whole))
    def _():
        @pl.when(first_ref[i] == 1)
        def _():
            y_ref[...] = jnp.zeros(y_ref.shape, F32)

        rows = row0 + lax.broadcasted_iota(jnp.int32, (MOE_ROWS, 1), 0)
        mine = (rows >= lo_ref[i]) & (rows < hi_ref[i])
        _store_row_slabs(y_ref, jnp.where(mine, y, _load_row_slabs(y_ref, MOE_ROWS, Y_SLABS)))


def _experts(items, xs, wgu, bgu, wd, bd):
    blk, exp, lo, hi, first = items
    newexp = jnp.concatenate([jnp.ones((1,), jnp.int32), (exp[1:] != exp[:-1]).astype(jnp.int32)])
    n_items = blk.shape[0]
    n_rows = xs.shape[0] // X_SLABS
    return pl.pallas_call(
        _experts_kernel,
        grid_spec=pltpu.PrefetchScalarGridSpec(
            num_scalar_prefetch=6,
            grid=(n_items,),
            in_specs=[pl.BlockSpec((MOE_ROWS * X_SLABS, LANES), lambda i, b, e, *_: (b[i], 0)),
                      pl.BlockSpec((1, D_MODEL, 2 * EXPERT_FF), lambda i, b, e, *_: (e[i], 0, 0)),
                      pl.BlockSpec((1, 1, 2 * EXPERT_FF), lambda i, b, e, *_: (e[i], 0, 0)),
                      pl.BlockSpec((1, EXPERT_FF, D_MODEL), lambda i, b, e, *_: (e[i], 0, 0)),
                      pl.BlockSpec((1, 1, D_MODEL), lambda i, b, e, *_: (e[i], 0, 0))],
            out_specs=pl.BlockSpec((MOE_ROWS * Y_SLABS, LANES), lambda i, b, e, *_: (b[i], 0)),
            scratch_shapes=[pltpu.VMEM((D_MODEL, 2 * EXPERT_FF), BF16), pltpu.VMEM((EXPERT_FF, D_MODEL), BF16)]),
        out_shape=jax.ShapeDtypeStruct((n_rows * Y_SLABS, LANES), F32),
        compiler_params=pltpu.CompilerParams(dimension_semantics=("arbitrary",),
                                             vmem_limit_bytes=EXPERTS_VMEM_LIMIT),
        name="moe_experts",
    )(blk, exp, lo, hi, first, newexp, xs, wgu, bgu, wd, bd)


def _work_items(counts, n_rows):
    n_blocks = n_rows // MOE_ROWS
    n_items = n_blocks + N_EXPERTS - 1
    ends = jnp.cumsum(counts)
    starts = ends - counts
    nb = jnp.where(counts > 0, (ends - 1) // MOE_ROWS - starts // MOE_ROWS + 1, 0)
    item_end = jnp.cumsum(nb)
    item_start = item_end - nb
    total = item_end[-1]
    i = jnp.arange(n_items, dtype=jnp.int32)
    e = jnp.minimum(jnp.sum(item_end[None, :] <= i[:, None], axis=1), N_EXPERTS - 1).astype(jnp.int32)
    blk = (starts[e] // MOE_ROWS + (i - item_start[e])).astype(jnp.int32)
    valid = i < total
    lo = jnp.maximum(starts[e], blk * MOE_ROWS)
    hi = jnp.minimum(ends[e], (blk + 1) * MOE_ROWS)
    last = jnp.maximum(total - 1, 0)
    blk = jnp.where(valid, blk, blk[last])
    e = jnp.where(valid, e, e[last])
    lo = jnp.where(valid, lo, 0).astype(jnp.int32)
    hi = jnp.where(valid, hi, 0).astype(jnp.int32)
    prev = jnp.concatenate([jnp.full((1,), -1, jnp.int32), blk[:-1]])
    first = (blk != prev).astype(jnp.int32)
    return blk, e, lo, hi, first, starts


def _combine_kernel(pos_ref, pos_next_ref, h_ref, gate_ref, g_ref, ys_ref, o_first_ref, o_rest_ref, buf, sem,
                    *, n_first):
    tm = h_ref.shape[0]
    i = pl.program_id(0)
    n = pl.num_programs(0)
    cur = i % 2

    def fetch(p_ref, slot):
        def start(tt, c):
            for u in range(POS_GROUP):
                for k in range(TOP_K):
                    _row_copy(_slab(ys_ref, p_ref[0, tt, k * POS_GROUP + u], Y_SLABS),
                              buf.at[slot, k, pl.ds((tt * POS_GROUP + u) * BUF_PITCH, Y_SLABS), :],
                              sem.at[slot]).start(priority=k % 2)
            return c

        lax.fori_loop(0, tm // POS_GROUP, start, 0)

    @pl.when(i == 0)
    def _():
        fetch(pos_ref, 0)

    for slot in range(2):
        @pl.when((i + 1 < n) & (cur != slot))
        def _():
            fetch(pos_next_ref, slot)

    def wait(t, c):
        for k in range(TOP_K):
            _row_copy(_slab(ys_ref, 0, Y_SLABS), _slab(buf, 0, Y_SLABS, (cur, 0)), sem.at[cur]).wait()
        return c

    lax.fori_loop(0, tm, wait, 0, unroll=4)
    acc = h_ref[...]
    for k in range(TOP_K):
        rows = jnp.concatenate([buf[cur, k, pl.ds(c, tm, stride=BUF_PITCH), :] for c in range(Y_SLABS)], axis=1)
        acc = acc + gate_ref[:, k:k + 1] * rows
    res = _rms(acc, g_ref[...])

    @pl.when(i < n_first)
    def _():
        o_first_ref[...] = res

    @pl.when(i >= n_first)
    def _():
        o_rest_ref[...] = res


def _combine(pos, h2d, gates_t, g_final, ys, *, tm, rows_first):
    r = h2d.shape[0]
    n_tiles = r // tm
    n_first = rows_first // tm
    row = lambda i: (i, 0)
    return pl.pallas_call(
        functools.partial(_combine_kernel, n_first=n_first),
        grid=(n_tiles,),
        in_specs=[pl.BlockSpec((1, tm // POS_GROUP, TOP_K * POS_GROUP), lambda i: (i, 0, 0),
                               memory_space=pltpu.SMEM),
                  pl.BlockSpec((1, tm // POS_GROUP, TOP_K * POS_GROUP),
                               lambda i: (jnp.minimum(i + 1, n_tiles - 1), 0, 0), memory_space=pltpu.SMEM),
                  pl.BlockSpec((tm, D_MODEL), row),
                  pl.BlockSpec((tm, TOP_K), row),
                  pl.BlockSpec((1, D_MODEL), lambda i: (0, 0)),
                  pl.BlockSpec(memory_space=pl.ANY)],
        out_specs=[pl.BlockSpec((tm, D_MODEL), lambda i: (jnp.minimum(i, n_first - 1), 0)),
                   pl.BlockSpec((tm, D_MODEL), lambda i: (jnp.maximum(i - n_first, 0), 0))],
        out_shape=[jax.ShapeDtypeStruct((rows_first, D_MODEL), F32),
                   jax.ShapeDtypeStruct((r - rows_first, D_MODEL), F32)],
        scratch_shapes=[pltpu.VMEM((2, TOP_K, tm * BUF_PITCH, LANES), F32), pltpu.SemaphoreType.DMA((2,))],
        compiler_params=_cparams(("arbitrary",)),
        name="moe_combine",
    )(pos, pos, h2d, gates_t, g_final, ys)


def _moe(h2d, xn2d, w, rows_first):
    r = h2d.shape[0]
    tm_router = _pick_tile(r, 512)
    tm_scatter = _pick_tile(r, 512)
    tm_combine = _pick_tile(r, 256)
    idx, gates, rank, cnt = _router(xn2d, w["wr_t"], w["br"], w["tri"][:tm_router, :tm_router], tm=tm_router)
    counts = cnt[:, 0]
    blk, e, lo, hi, first, starts = _work_items(counts, TOP_K * r)
    hot = idx[:, :, None] == jnp.arange(N_EXPERTS, dtype=jnp.int32)
    pos = jnp.sum(jnp.where(hot, starts.astype(jnp.int32), 0), axis=-1) + rank
    tiles = lambda tm: (pos.reshape(TOP_K, -1, tm // POS_GROUP, POS_GROUP).transpose(1, 2, 0, 3)
                        .reshape(-1, tm // POS_GROUP, TOP_K * POS_GROUP))
    xs = _scatter(tiles(tm_scatter), xn2d, tm=tm_scatter)
    ys = _experts((blk, e, lo, hi, first), xs, w["wgu"], w["bgu"], w["wd"], w["bd"])
    return _combine(tiles(tm_combine), h2d, gates.T, w["g_final"], ys, tm=tm_combine, rows_first=rows_first)


def _pick_tile(n, cap):
    t = cap
    while n % t:
        t //= 2
    return t


def kernel(x_prompt, x_sample, mem_prompt, cache_a_k, cache_a_v, cache_b_k, cache_b_v, cache_mem_k, cache_mem_v, g_mix, w_in, rel_bias, lam_q1, lam_k1, lam_q2, lam_k2, g_subln, w_out, g_xq, g_xmem, w_xq, w_xk, w_xv, w_xo, g_moe, w_router, b_router, w_gate_up, b_gate_up, w_down, b_down, g_final):
    assert g_mix.shape[0] == 1, "single-layer kernel"
    b, s, d = x_prompt.shape
    db, dl, _ = x_sample.shape
    past = cache_b_k.shape[2]
    a_len = cache_a_k.shape[2]
    mlen = mem_prompt.shape[1]
    keep_a = min(BAND, s)
    assert d == D_MODEL and s % B_QBLK == 0 and s >= A_WIN and dl == CHUNK and past % CHUNK == 0
    assert B_QBLK == 2 * B_KBLK

    row1 = lambda v: v.reshape(1, -1).astype(F32)
    w_in_bf = w_in[0].astype(BF16)
    w_out_bf = w_out[0].astype(BF16)
    weights = dict(
        wr_t=w_router[0].T.astype(BF16), br=b_router[0].reshape(N_EXPERTS, 1).astype(F32),
        tri=jnp.asarray(np.triu(np.ones((512, 512), np.float32), 1), BF16),
        wgu=w_gate_up[0].astype(F32), bgu=b_gate_up[0].reshape(N_EXPERTS, 1, -1).astype(F32),
        wd=w_down[0].astype(F32), bd=b_down[0].reshape(N_EXPERTS, 1, -1).astype(F32),
        g_final=row1(g_final))
    lam = (jnp.exp(jnp.sum(lam_q1[0].astype(F32) * lam_k1[0].astype(F32)))
           - jnp.exp(jnp.sum(lam_q2[0].astype(F32) * lam_k2[0].astype(F32))) + LAM_INIT).reshape(1)
    table = _band_table(rel_bias[0])
    g_sub = row1(g_subln[0])

    tm_p = _pick_tile(s, 512)
    tabs_p = _rope_tables(np.arange(s))
    p_in = _inproj(x_prompt.reshape(b * s, d), row1(g_mix[0]), w_in_bf, *tabs_p,
                   tm=tm_p, n_pos_blocks=s // tm_p, tail_div=s // tm_p if keep_a == tm_p else 1, emit_vt=True)
    tabs_s = _rope_tables(np.tile(past + np.arange(dl), db))
    s_in = _inproj(x_sample.reshape(db * dl, d), row1(g_mix[0]), w_in_bf, *tabs_s,
                   tm=db * dl, n_pos_blocks=1, tail_div=1, emit_vt=False)

    def split(outs, nb, nl):
        return [o.reshape(nb, nl, A_WIDTH) for o in outs[:6]]

    aq, ak, _, bq, bk, _ = split(p_in, b, s)
    avt = p_in[10].reshape((b, s // A_QBLK) + p_in[10].shape[1:])
    bvt = p_in[11].reshape((b, s // B_KBLK) + p_in[11].shape[1:])
    saq, sak, sav, sbq, sbk, sbv = split(s_in, db, dl)

    oa = _attn_a_prompt(aq, ak, avt, table.transpose(0, 2, 1))
    ob = _attn_b_prompt(lam, bq, bk, bvt, g_sub)
    soa = _attn_a_sample(saq, cache_a_k[0].reshape(db, a_len * N_HEADS_A, HEAD_DIM),
                         cache_a_v[0].reshape(db, a_len * N_HEADS_A, HEAD_DIM), sak, sav, table[:, :dl, :BAND + dl])
    sob = _attn_b_sample(lam, sbq, cache_b_k[0].reshape(db, past, B_WIDTH),
                         cache_b_v[0].reshape(db, past * N_HEADS_B, 2 * HEAD_DIM), sbk, sbv, g_sub)

    mkf, mvf, mkb, mvb = _memkv(mem_prompt.reshape(b * mlen, d), row1(g_xmem[0]),
                                w_xk[0].astype(BF16), w_xv[0].astype(BF16), tm=mlen)
    post_w = (w_out_bf[:A_WIDTH], w_out_bf[A_WIDTH:], row1(g_xq[0]), w_xq[0].astype(BF16))
    tm_post = _pick_tile(math.gcd(s, b * s + db * dl), 256)
    rows_p, rows_s = b * s, db * dl
    sob_t = sob.transpose(0, 2, 1).reshape(db, N_HEADS_B, 2 * HEAD_DIM, dl)
    h_xn = _post(x_prompt.reshape(rows_p, d), oa, ob,
                 *post_w, mkb.reshape(b, mlen, d), mvb.reshape(b, mlen, d),
                 w_xo[0].astype(BF16), row1(g_moe[0]), tm=tm_post, tiles_per_batch=s // tm_post,
                 total_rows=rows_p + rows_s)
    h_all, xn_all = _post(x_sample.reshape(rows_s, d), soa.transpose(0, 2, 1), sob_t,
                          *post_w, cache_mem_k[0].reshape(db, mlen, d).astype(BF16),
                          cache_mem_v[0].reshape(db, mlen, d).astype(BF16),
                          w_xo[0].astype(BF16), row1(g_moe[0]), tm=dl, tiles_per_batch=1,
                          total_rows=rows_p + rows_s, row_offset=rows_p, into=h_xn)

    y_prompt, y_sample = _moe(h_all, xn_all, weights, rows_p)
    y_prompt = y_prompt.reshape(b, s, d)
    y_sample = y_sample.reshape(db, dl, d)

    def heads(x2d, nb, nl, nh):
        return x2d.reshape(1, nb, nl, nh, -1)

    akf, avf, bkf, bvf = p_in[6:10]
    if keep_a != tm_p:
        akf = akf.reshape(b, s, A_WIDTH)[:, s - keep_a:]
        avf = avf.reshape(b, s, A_WIDTH)[:, s - keep_a:]
    sakf, savf, sbkf, sbvf = s_in[6:]
    return (y_prompt, y_sample,
            heads(akf, b, keep_a, N_HEADS_A), heads(avf, b, keep_a, N_HEADS_A),
            heads(bkf, b, s, 2 * N_HEADS_B), heads(bvf, b, s, N_HEADS_B),
            heads(mkf, b, mlen, N_HEADS_M), heads(mvf, b, mlen, N_HEADS_M),
            heads(sakf, db, dl, N_HEADS_A), heads(savf, db, dl, N_HEADS_A),
            heads(sbkf, db, dl, 2 * N_HEADS_B), heads(sbvf, db, dl, N_HEADS_B))
```

```python
import functools
import math

import numpy as np
import jax
import jax.numpy as jnp
from jax import lax
from jax.experimental import pallas as pl
from jax.experimental.pallas import tpu as pltpu

F32 = jnp.float32
BF16 = jnp.bfloat16

D_MODEL = 1024
CHUNK = 64
LEFT_CHUNKS = 8
BAND = LEFT_CHUNKS * CHUNK
HEAD_DIM = 64
A_WIDTH = 512
B_WIDTH = 512
N_HEADS_A = 8
N_HEADS_B = 4
REL_CLIP = 128
ROT_DIM = 16
ROPE_THETA = 500000.0
N_HEADS_M = 4
HEAD_DIM_M = 256
N_EXPERTS = 32
TOP_K = 4
EXPERT_FF = 1024
SWIGLU_LIMIT = 7.0
SWIGLU_ALPHA = 1.702
RMS_EPS = 1e-5
NEG_INF = -1e30
LAM_INIT = 0.8 - 0.6 * math.exp(-0.3 * 0)
LOG2E = math.log2(math.e)

LANES = 128
A_QBLK = 4 * CHUNK
A_WIN = BAND + A_QBLK
B_QBLK = 512
B_KBLK = 256
VT_ONES = 16
BUF_PITCH = 9
POS_GROUP = 16
MOE_ROWS = 512
X_SLABS = D_MODEL // 2 // LANES
Y_SLABS = D_MODEL // LANES
VMEM_LIMIT = 48 * 1024 * 1024
EXPERTS_VMEM_LIMIT = 56 * 1024 * 1024


def _cparams(sem):
    return pltpu.CompilerParams(dimension_semantics=sem, vmem_limit_bytes=VMEM_LIMIT)


def _rms(x, g):
    return x * lax.rsqrt(jnp.mean(x * x, axis=-1, keepdims=True) + RMS_EPS) * g


HI16 = 0xFFFF0000


def _pack_bf16_pairs(x):
    n = x.shape[1] // 2
    bits = lambda v: lax.bitcast_convert_type(v.astype(BF16).astype(F32), jnp.uint32)
    return (bits(x[:, :n]) >> 16) | (bits(x[:, n:]) & jnp.uint32(HI16))


def _unpack_bf16_pairs(w):
    lo = lax.bitcast_convert_type(w << 16, F32)
    hi = lax.bitcast_convert_type(w & jnp.uint32(HI16), F32)
    return jnp.concatenate([lo, hi], axis=1).astype(BF16)


def _store_row_slabs(ref, x, lead=()):
    m, n = x.shape[0], x.shape[1] // LANES
    for c in range(n):
        ref[lead + (pl.ds(c, m, stride=n), slice(None))] = x[:, c * LANES:(c + 1) * LANES]


def _load_row_slabs(ref, m, n, lead=()):
    return jnp.concatenate([ref[lead + (pl.ds(c, m, stride=n), slice(None))] for c in range(n)], axis=1)


def _dot_nt(a, b):
    return lax.dot_general(a, b, (((1,), (1,)), ((), ())), preferred_element_type=F32)


def _store_heads64(ref, x):
    m, n_heads = x.shape[0], x.shape[1] // HEAD_DIM
    for c in range(x.shape[1] // LANES):
        grp = x[:, c * LANES:(c + 1) * LANES]
        ref[pl.ds(2 * c, m, stride=n_heads), :] = grp[:, :HEAD_DIM]
        ref[pl.ds(2 * c + 1, m, stride=n_heads), :] = pltpu.roll(grp, HEAD_DIM, 1)[:, :HEAD_DIM]


def _inproj_kernel(x_ref, g_ref, w_ref, cos_ref, sa_ref, sb_ref,
                   aq_ref, ak_ref, av_ref, bq_ref, bk_ref, bv_ref,
                   akf_ref, avf_ref, bkf_ref, bvf_ref, *maybe_vt_refs, tail_div):
    xn = _rms(x_ref[...], g_ref[...]).astype(BF16)
    is_tail = pl.program_id(0) % tail_div == tail_div - 1

    def seg(s):
        return jnp.dot(xn, w_ref[:, s * A_WIDTH:(s + 1) * A_WIDTH], preferred_element_type=F32)

    def rope(z):
        cols = []
        for c in range(z.shape[1] // LANES):
            zc = z[:, c * LANES:(c + 1) * LANES]
            cols.append(zc * cos_ref[...] + pltpu.roll(zc, 8, 1) * sa_ref[...]
                        + pltpu.roll(zc, LANES - 8, 1) * sb_ref[...])
        return jnp.concatenate(cols, axis=1)

    scale = HEAD_DIM ** -0.5
    aq_ref[...] = (seg(0) * (scale * LOG2E)).astype(BF16)
    ak = seg(1)
    ak_ref[...] = ak.astype(BF16)
    av = seg(2)
    av_ref[...] = av.astype(BF16)
    bq_ref[...] = (rope(seg(3)) * (scale * LOG2E)).astype(BF16)
    bk = rope(seg(4))
    bk_ref[...] = bk.astype(BF16)
    _store_heads64(bkf_ref, bk)
    bv = seg(5)
    bv_ref[...] = bv.astype(BF16)
    _store_row_slabs(bvf_ref, bv)
    for vt_ref, v in zip(maybe_vt_refs, (av, bv)):
        vt = v.T.astype(BF16)
        blk = vt_ref.shape[3]
        for c in range(vt_ref.shape[0]):
            for g in range(vt_ref.shape[1]):
                vt_ref[c, g, 0:LANES, :] = vt[g * LANES:(g + 1) * LANES, c * blk:(c + 1) * blk]
                vt_ref[c, g, LANES:LANES + VT_ONES, :] = jnp.ones((VT_ONES, blk), BF16)

    @pl.when(is_tail)
    def _():
        _store_heads64(akf_ref, ak)
        _store_heads64(avf_ref, av)


def _inproj(x2d, g, w_bf, cos_t, sa_t, sb_t, *, tm, n_pos_blocks, tail_div, emit_vt):
    r = x2d.shape[0]
    n_tiles = r // tm
    row = lambda i: (i, 0)
    const = lambda i: (0, 0)
    pos = lambda i: (i % n_pos_blocks, 0)
    tail = lambda i: (i // tail_div, 0)
    bf = jax.ShapeDtypeStruct((r, A_WIDTH), BF16)
    blk = lambda im: pl.BlockSpec((tm, A_WIDTH), im)
    h64 = lambda im: pl.BlockSpec((tm * N_HEADS_A, HEAD_DIM), im)
    f_tail = jax.ShapeDtypeStruct((r // tail_div * N_HEADS_A, HEAD_DIM), F32)
    out_specs = [blk(row)] * 6 + [h64(tail), h64(tail), h64(row),
                                  pl.BlockSpec((tm * N_HEADS_B, 2 * HEAD_DIM), row)]
    out_shape = [bf] * 6 + [f_tail, f_tail,
                            jax.ShapeDtypeStruct((r * 2 * N_HEADS_B, HEAD_DIM), F32),
                            jax.ShapeDtypeStruct((r * N_HEADS_B, 2 * HEAD_DIM), F32)]
    if emit_vt:
        for kb in (A_QBLK, B_KBLK):
            grp = (A_WIDTH // LANES, LANES + VT_ONES, kb)
            out_specs.append(pl.BlockSpec((tm // kb,) + grp, lambda i: (i, 0, 0, 0)))
            out_shape.append(jax.ShapeDtypeStruct((r // kb,) + grp, BF16))
    return pl.pallas_call(
        functools.partial(_inproj_kernel, tail_div=tail_div),
        grid=(n_tiles,),
        in_specs=[pl.BlockSpec((tm, D_MODEL), row), pl.BlockSpec((1, D_MODEL), const),
                  pl.BlockSpec((D_MODEL, 3 * D_MODEL), const),
                  pl.BlockSpec((tm, LANES), pos), pl.BlockSpec((tm, LANES), pos),
                  pl.BlockSpec((tm, LANES), pos)],
        out_specs=out_specs,
        out_shape=out_shape,
        compiler_params=_cparams(("arbitrary",)),
        name="inproj",
    )(x2d, g, w_bf, cos_t, sa_t, sb_t)


def _rope_tables(positions):
    half = ROT_DIM // 2
    lane = np.arange(LANES) % HEAD_DIM
    inv_freq = np.power(ROPE_THETA, -np.arange(half, dtype=np.float64) / half)
    ang = np.asarray(positions, np.float64)[:, None] * inv_freq[None, :]
    cos = np.cos(ang)[:, lane % half]
    sin = np.sin(ang)[:, lane % half]
    in_rot = (lane < ROT_DIM)[None, :]
    first = (lane < half)[None, :]
    cos_t = np.where(in_rot, cos, 1.0)
    sa_t = np.where(in_rot & ~first, sin, 0.0)
    sb_t = np.where(first, -sin, 0.0)
    return tuple(jnp.asarray(t, F32) for t in (cos_t, sa_t, sb_t))


def _band_table(rel_bias):
    r = np.arange(A_QBLK)[:, None]
    j = np.arange(BAND + A_WIN)[None, :]
    dchunk = LEFT_CHUNKS + r // CHUNK - j // CHUNK
    allowed = (dchunk >= 0) & (dchunk <= LEFT_CHUNKS)
    bias = rel_bias.astype(F32) * LOG2E
    n_seq = BAND + A_WIN + A_QBLK - 1
    n_hi = BAND + A_QBLK - 1 - REL_CLIP
    seq = jnp.concatenate([jnp.broadcast_to(bias[:, -1:], (N_HEADS_A, n_hi)), bias[:, ::-1],
                           jnp.broadcast_to(bias[:, :1], (N_HEADS_A, n_seq - n_hi - bias.shape[1]))], axis=1)
    seq = jnp.roll(seq, -(A_QBLK - 1), axis=1)
    toe = jnp.tile(seq, (1, A_QBLK))[:, :A_QBLK * (n_seq - 1)].reshape(N_HEADS_A, A_QBLK, n_seq - 1)
    return jnp.where(jnp.asarray(allowed)[None], toe[:, :, :BAND + A_WIN], NEG_INF)


def _softmax_pv(s_list, v_list, exp_fn=jnp.exp):
    m = s_list[0].max(axis=-1, keepdims=True)
    for s in s_list[1:]:
        m = jnp.maximum(m, s.max(axis=-1, keepdims=True))
    l = None
    o = None
    for s, v in zip(s_list, v_list):
        p = exp_fn(s - m)
        ls = p.sum(axis=-1, keepdims=True)
        os_ = jnp.dot(p.astype(BF16), v, preferred_element_type=F32)
        l = ls if l is None else l + ls
        o = os_ if o is None else o + os_
    return o / l


def _attn_a_prompt_kernel(q_ref, k_ref, vt_ref, tab_ref, o_ref):
    i = pl.program_id(1)
    n_left = BAND // A_QBLK
    n_sub = A_WIN // A_QBLK
    blk0 = jnp.maximum(i - n_left, 0)
    off = pl.multiple_of(jnp.maximum(n_left - i, 0) * A_QBLK, A_QBLK)
    q = q_ref[0]
    lane = lax.broadcasted_iota(jnp.int32, (A_QBLK, LANES), 1)

    def scores(h):
        cols = slice(h // 2 * LANES, (h // 2 + 1) * LANES)
        qp = q[:, cols]
        qh = jnp.where((lane >= HEAD_DIM) == bool(h % 2), qp, jnp.zeros_like(qp))
        return [_dot_nt(k_ref[0, pl.ds(pl.multiple_of((blk0 + c) * A_QBLK, A_QBLK), A_QBLK), cols], qh)
                + tab_ref[h, pl.ds(off + c * A_QBLK, A_QBLK), :] for c in range(n_sub)]

    def attend(h, ss):
        m = _col_reduce(ss[0], jnp.maximum, jnp.max)
        for s in ss[1:]:
            m = jnp.maximum(m, _col_reduce(s, jnp.maximum, jnp.max))
        o_t = None
        for c in range(n_sub):
            p = jnp.exp2(ss[c] - m).astype(BF16)
            os_ = jnp.dot(vt_ref[0, blk0 + c, h // 2], p, preferred_element_type=F32)
            o_t = os_ if o_t is None else o_t + os_
        return o_t[h % 2 * HEAD_DIM:(h % 2 + 1) * HEAD_DIM] / o_t[LANES:LANES + 1]

    halves = []
    ss = scores(0)
    for h in range(N_HEADS_A):
        ss_next = scores(h + 1) if h + 1 < N_HEADS_A else None
        halves.append(attend(h, ss))
        ss = ss_next
    o_ref[0] = jnp.concatenate(halves, axis=0).astype(BF16)


def _attn_a_prompt(aq, ak, avt, table_t):
    b, l, _ = aq.shape
    n_kb = avt.shape[1]
    return pl.pallas_call(
        _attn_a_prompt_kernel,
        grid=(b, l // A_QBLK),
        in_specs=[pl.BlockSpec((1, A_QBLK, A_WIDTH), lambda bb, i: (bb, i, 0)),
                  pl.BlockSpec((1, l, A_WIDTH), lambda bb, i: (bb, 0, 0)),
                  pl.BlockSpec((1,) + avt.shape[1:], lambda bb, i: (bb, 0, 0, 0, 0)),
                  pl.BlockSpec(table_t.shape, lambda bb, i: (0, 0, 0))],
        out_specs=pl.BlockSpec((1, A_WIDTH, A_QBLK), lambda bb, i: (bb, 0, i)),
        out_shape=jax.ShapeDtypeStruct((b, A_WIDTH, l), BF16),
        compiler_params=_cparams(("arbitrary", "arbitrary")),
        name="attn_a_prompt",
    )(aq, ak, avt, table_t)


def _attn_a_sample_kernel(q_ref, kc_ref, vc_ref, k_ref, v_ref, tab_ref, o_ref):
    q = q_ref[0]
    ko = k_ref[0]
    vo = v_ref[0]
    a_len = kc_ref.shape[1] // N_HEADS_A
    lq = q.shape[0]
    c0 = BAND - a_len
    outs = []
    for h in range(N_HEADS_A):
        sl = slice(h * HEAD_DIM, (h + 1) * HEAD_DIM)
        kc = kc_ref[0, pl.ds(h, a_len, stride=N_HEADS_A), :].astype(BF16)
        vc = vc_ref[0, pl.ds(h, a_len, stride=N_HEADS_A), :].astype(BF16)
        s_c = _dot_nt(q[:, sl], kc) + tab_ref[h, :, c0:BAND]
        s_o = _dot_nt(q[:, sl], ko[:, sl]) + tab_ref[h, :, BAND:BAND + lq]
        outs.append(_softmax_pv([s_c, s_o], [vc, vo[:, sl]], exp_fn=jnp.exp2))
    o_ref[0] = jnp.concatenate(outs, axis=1).astype(BF16)


def _attn_a_sample(aq, cache_k, cache_v, ak, av, table):
    b, l, _ = aq.shape
    bs = lambda n: pl.BlockSpec((1, n, A_WIDTH), lambda bb: (bb, 0, 0))
    cache = pl.BlockSpec((1,) + cache_k.shape[1:], lambda bb: (bb, 0, 0))
    return pl.pallas_call(
        _attn_a_sample_kernel,
        grid=(b,),
        in_specs=[bs(l), cache, cache, bs(l), bs(l),
                  pl.BlockSpec(table.shape, lambda bb: (0, 0, 0))],
        out_specs=bs(l),
        out_shape=jax.ShapeDtypeStruct((b, l, A_WIDTH), BF16),
        compiler_params=_cparams(("arbitrary",)),
        name="attn_a_sample",
    )(aq, cache_k, cache_v, ak, av, table)


def _subnorm(o, g):
    return _rms(o, g) * (1.0 - LAM_INIT)


def _col_reduce(x, op, reduce_fn):
    n = x.shape[0]
    while n > 8:
        n //= 2
        x = op(x[:n], x[n:])
    return reduce_fn(x, axis=0, keepdims=True)


def _attn_b_prompt_kernel(lam_ref, q_ref, qn_ref, k_ref, vt_ref, mask_ref, g_ref, o_ref,
                          sa_sc, sb_sc, m_sc, acc_sc):
    i = pl.program_id(2)
    tk = vt_ref.shape[4]
    lane = lax.broadcasted_iota(jnp.int32, q_ref.shape[1:], 1)

    def split_maps(q):
        zero = jnp.zeros_like(q)
        return jnp.where(lane < HEAD_DIM, q, zero), jnp.where(lane >= HEAD_DIM, q, zero)

    qs = split_maps(q_ref[0])
    m_sc[...] = jnp.full(m_sc.shape, NEG_INF, F32)
    acc_sc[...] = jnp.zeros(acc_sc.shape, F32)

    def scores(j, s_sc, qmaps=qs):
        kb = k_ref[0, pl.ds(pl.multiple_of(j * tk, tk), tk), :]
        for a in range(2):
            s_sc[a] = _dot_nt(kb, qmaps[a])

    def consume(j, s_sc, mask):
        vt = vt_ref[0, j, 0]
        for a in range(2):
            s = s_sc[a]
            if mask is not None:
                s = s + mask
            m_old = m_sc[a]
            m_new = jnp.maximum(m_old, _col_reduce(s, jnp.maximum, jnp.max))
            alpha = jnp.exp2(m_old - m_new)
            p = jnp.exp2(s - m_new).astype(BF16)
            m_sc[a] = m_new
            acc_sc[a] = alpha * acc_sc[a] + jnp.dot(vt, p, preferred_element_type=F32)

    @pl.when(i == 0)
    def _():
        scores(0, sa_sc)

    def body(jj, c):
        j = 2 * jj
        scores(j + 1, sb_sc)
        consume(j, sa_sc, None)
        scores(j + 2, sa_sc)
        consume(j + 1, sb_sc, None)
        return c

    lax.fori_loop(0, i, body, 0)
    scores(2 * i + 1, sb_sc)
    consume(2 * i, sa_sc, mask_ref[0])
    scores(0, sa_sc, split_maps(qn_ref[0]))
    consume(2 * i + 1, sb_sc, mask_ref[1])
    o_t = (acc_sc[0, 0:LANES] / acc_sc[0, LANES:LANES + 1]
           - lam_ref[0] * (acc_sc[1, 0:LANES] / acc_sc[1, LANES:LANES + 1]))
    inv = lax.rsqrt(jnp.mean(o_t * o_t, axis=0, keepdims=True) + RMS_EPS)
    o_ref[0, 0] = (o_t * inv * (g_ref[...] * (1.0 - LAM_INIT))).astype(BF16)


def _chunk_causal_masks_t(tk, tq):
    kc = (np.arange(tq // tk)[:, None, None] * tk + np.arange(tk)[None, :, None]) // CHUNK
    qc = np.arange(tq)[None, None, :] // CHUNK
    return jnp.asarray(np.where(kc <= qc, 0.0, NEG_INF), F32)


def _attn_b_prompt(lam, bq, bk, bvt, g_sub):
    b, l, _ = bq.shape
    tq = B_QBLK
    tk = B_KBLK
    hw = 2 * HEAD_DIM
    n_kb = bvt.shape[1]
    return pl.pallas_call(
        _attn_b_prompt_kernel,
        grid_spec=pltpu.PrefetchScalarGridSpec(
            num_scalar_prefetch=1,
            grid=(b, N_HEADS_B, l // tq),
            in_specs=[pl.BlockSpec((1, tq, hw), lambda bb, h, i, lam_: (bb, i, h)),
                      pl.BlockSpec((1, tq, hw), lambda bb, h, i, lam_: (bb, jnp.minimum(i + 1, l // tq - 1), h)),
                      pl.BlockSpec((1, l, hw), lambda bb, h, i, lam_: (bb, 0, h)),
                      pl.BlockSpec((1, n_kb, 1, hw + VT_ONES, tk), lambda bb, h, i, lam_: (bb, 0, h, 0, 0)),
                      pl.BlockSpec((tq // tk, tk, tq), lambda bb, h, i, lam_: (0, 0, 0)),
                      pl.BlockSpec((hw, 1), lambda bb, h, i, lam_: (0, 0))],
            out_specs=pl.BlockSpec((1, 1, hw, tq), lambda bb, h, i, lam_: (bb, h, 0, i)),
            scratch_shapes=[pltpu.VMEM((2, tk, tq), F32), pltpu.VMEM((2, tk, tq), F32),
                            pltpu.VMEM((2, 1, tq), F32), pltpu.VMEM((2, hw + VT_ONES, tq), F32)]),
        out_shape=jax.ShapeDtypeStruct((b, N_HEADS_B, hw, l), BF16),
        compiler_params=_cparams(("arbitrary", "arbitrary", "arbitrary")),
        name="attn_b_prompt",
    )(lam, bq, bq, bk, bvt, _chunk_causal_masks_t(tk, tq), g_sub.reshape(hw, 1))


def _attn_b_sample_kernel(lam_ref, q_ref, kc_ref, vc_ref, k_ref, v_ref, g_ref, o_ref):
    q = q_ref[0]
    kc = kc_ref[0].astype(BF16)
    n_past = vc_ref.shape[1] // N_HEADS_B
    vc = vc_ref[0, pl.ds(pl.program_id(1), n_past, stride=N_HEADS_B), :].astype(BF16)
    ko = k_ref[0]
    vo = v_ref[0]
    outs = []
    for a in range(2):
        sl = slice(a * HEAD_DIM, (a + 1) * HEAD_DIM)
        s_c = _dot_nt(q[:, sl], kc[:, sl])
        s_o = _dot_nt(q[:, sl], ko[:, sl])
        outs.append(_softmax_pv([s_c, s_o], [vc, vo], exp_fn=jnp.exp2))
    o = outs[0] - lam_ref[0] * outs[1]
    o_ref[0] = _subnorm(o, g_ref[...]).astype(BF16)


def _attn_b_sample(lam, bq, cache_k, cache_v, bk, bv, g_sub):
    b, l, _ = bq.shape
    p = cache_k.shape[1]
    hw = 2 * HEAD_DIM
    bs = lambda n: pl.BlockSpec((1, n, hw), lambda bb, h, lam_: (bb, 0, h))
    return pl.pallas_call(
        _attn_b_sample_kernel,
        grid_spec=pltpu.PrefetchScalarGridSpec(
            num_scalar_prefetch=1,
            grid=(b, N_HEADS_B),
            in_specs=[bs(l), bs(p),
                      pl.BlockSpec((1, p * N_HEADS_B, hw), lambda bb, h, lam_: (bb, 0, 0)),
                      bs(l), bs(l),
                      pl.BlockSpec((1, hw), lambda bb, h, lam_: (0, 0))],
            out_specs=bs(l)),
        out_shape=jax.ShapeDtypeStruct((b, l, B_WIDTH), BF16),
        compiler_params=_cparams(("arbitrary", "arbitrary")),
        name="attn_b_sample",
    )(lam, bq, cache_k, cache_v, bk, bv, g_sub)


def _memkv_kernel(m_ref, g_ref, wk_ref, wv_ref, kf_ref, vf_ref, kb_ref, vb_ref):
    mn = _rms(m_ref[...], g_ref[...]).astype(BF16)
    k = jnp.dot(mn, wk_ref[...], preferred_element_type=F32)
    v = jnp.dot(mn, wv_ref[...], preferred_element_type=F32)
    kf_ref[...] = k
    vf_ref[...] = v
    kb_ref[...] = k.astype(BF16)
    vb_ref[...] = v.astype(BF16)


def _memkv(mem2d, g, wk, wv, *, tm):
    r = mem2d.shape[0]
    row = lambda i: (i, 0)
    const = lambda i: (0, 0)
    blk = pl.BlockSpec((tm, D_MODEL), row)
    wspec = pl.BlockSpec((D_MODEL, D_MODEL), const)
    f = jax.ShapeDtypeStruct((r, D_MODEL), F32)
    bf = jax.ShapeDtypeStruct((r, D_MODEL), BF16)
    return pl.pallas_call(
        _memkv_kernel,
        grid=(r // tm,),
        in_specs=[blk, pl.BlockSpec((1, D_MODEL), const), wspec, wspec],
        out_specs=[blk] * 4,
        out_shape=[f, f, bf, bf],
        compiler_params=_cparams(("arbitrary",)),
        name="memkv",
    )(mem2d, g, wk, wv)


def _post_kernel(*refs, n_real):
    h_ref, xn_ref = refs[-2:]

    @pl.when(pl.program_id(0) < n_real)
    def _():
        _post_tile(*refs[:11], h_ref, xn_ref)

    @pl.when(pl.program_id(0) >= n_real)
    def _():
        h_ref[...] = jnp.zeros(h_ref.shape, h_ref.dtype)
        xn_ref[...] = jnp.zeros(xn_ref.shape, xn_ref.dtype)


def _post_tile(x_ref, oa_ref, ob_ref, woa_ref, wob_ref, gq_ref, wq_ref, mk_ref, mv_ref, wo_ref,
               gm_ref, h_ref, xn_ref):
    tn = lambda a_t, w: lax.dot_general(a_t, w, (((0,), (0,)), ((), ())), preferred_element_type=F32)
    ob_t = ob_ref[0].reshape(B_WIDTH, ob_ref.shape[3])
    h1 = x_ref[...] + tn(oa_ref[0], woa_ref[...]) + tn(ob_t, wob_ref[...])
    xq = _rms(h1, gq_ref[...]).astype(BF16)
    q = (jnp.dot(xq, wq_ref[...], preferred_element_type=F32) * HEAD_DIM_M ** -0.5).astype(BF16)
    outs = []
    for h in range(N_HEADS_M):
        sl = slice(h * HEAD_DIM_M, (h + 1) * HEAD_DIM_M)
        s = _dot_nt(q[:, sl], mk_ref[0, :, sl])
        outs.append(_softmax_pv([s], [mv_ref[0, :, sl]]).astype(BF16))
    o = jnp.concatenate(outs, axis=1)
    h2 = h1 + jnp.dot(o, wo_ref[...], preferred_element_type=F32)
    h_ref[...] = h2
    _store_row_slabs(xn_ref, _pack_bf16_pairs(_rms(h2, gm_ref[...])))


def _post(x2d, oa, ob, woa, wob, gq, wq, mk, mv, wo, gm, *, tm, tiles_per_batch, total_rows,
          row_offset=0, into=None):
    r = x2d.shape[0]
    n_real = r // tm
    n_steps = n_real if into is not None else total_rows // tm
    mlen = mk.shape[1]
    row = lambda i: (jnp.minimum(i, n_real - 1), 0)
    out_row = lambda i: (row_offset // tm + i, 0)
    const = lambda i: (0, 0)
    memb = lambda i: (jnp.minimum(i, n_real - 1) // tiles_per_batch, 0, 0)
    gspec = pl.BlockSpec((1, D_MODEL), const)
    wspec = pl.BlockSpec((D_MODEL, D_MODEL), const)
    hspec = pl.BlockSpec((A_WIDTH, D_MODEL), const)
    obt_blk = lambda i: (jnp.minimum(i, n_real - 1) // tiles_per_batch, 0, 0,
                         jnp.minimum(i, n_real - 1) % tiles_per_batch)
    oat_blk = lambda i: (jnp.minimum(i, n_real - 1) // tiles_per_batch, 0,
                         jnp.minimum(i, n_real - 1) % tiles_per_batch)
    in_specs = [pl.BlockSpec((tm, D_MODEL), row),
                pl.BlockSpec((1, A_WIDTH, tm), oat_blk),
                pl.BlockSpec((1, N_HEADS_B, 2 * HEAD_DIM, tm), obt_blk),
                hspec, hspec, gspec, wspec,
                pl.BlockSpec((1, mlen, D_MODEL), memb), pl.BlockSpec((1, mlen, D_MODEL), memb),
                wspec, gspec]
    args = [x2d, oa, ob, woa, wob, gq, wq, mk, mv, wo, gm]
    aliases = {}
    if into is not None:
        aliases = {len(args): 0, len(args) + 1: 1}
        in_specs += [pl.BlockSpec(memory_space=pl.ANY)] * 2
        args += list(into)
    return pl.pallas_call(
        functools.partial(_post_kernel, n_real=n_real),
        grid=(n_steps,),
        in_specs=in_specs,
        out_specs=[pl.BlockSpec((tm, D_MODEL), out_row), pl.BlockSpec((tm * X_SLABS, LANES), out_row)],
        out_shape=[jax.ShapeDtypeStruct((total_rows, D_MODEL), F32),
                   jax.ShapeDtypeStruct((total_rows * X_SLABS, LANES), jnp.uint32)],
        input_output_aliases=aliases,
        compiler_params=_cparams(("arbitrary",)),
        name="post_attn",
    )(*args)


def _router_kernel(xn_ref, wr_ref, br_ref, tri_ref, idx_ref, gate_ref, rank_ref, cnt_ref, base_sc):
    @pl.when(pl.program_id(0) == 0)
    def _():
        base_sc[...] = jnp.zeros(base_sc.shape, F32)

    tm = xn_ref.shape[0] // X_SLABS
    xn = _unpack_bf16_pairs(_load_row_slabs(xn_ref, tm, X_SLABS))
    logits = _dot_nt(wr_ref[...], xn) + br_ref[...]
    iota_e = lax.broadcasted_iota(jnp.int32, (N_EXPERTS, tm), 0).astype(F32)
    work = logits
    vals, idxs, hots = [], [], []
    for _k in range(TOP_K):
        mx = work.max(axis=0, keepdims=True)
        ix = jnp.where(work == mx, iota_e, float(N_EXPERTS)).min(axis=0, keepdims=True)
        hot = iota_e == ix
        vals.append(mx)
        idxs.append(ix)
        hots.append(hot)
        work = jnp.where(hot, -jnp.inf, work)
    es = [jnp.exp(v - vals[0]) for v in vals]
    den = es[0] + es[1] + es[2] + es[3]
    cnt = (hots[0] | hots[1] | hots[2] | hots[3]).astype(F32)
    before = jnp.dot(cnt.astype(BF16), tri_ref[...], preferred_element_type=F32) + base_sc[...]
    for k in range(TOP_K):
        idx_ref[k:k + 1, :] = idxs[k].astype(jnp.int32)
        gate_ref[k:k + 1, :] = es[k] / den
        rank_ref[k:k + 1, :] = jnp.where(hots[k], before, 0.0).sum(axis=0, keepdims=True).astype(jnp.int32)
    base_sc[...] = base_sc[...] + cnt.sum(axis=1, keepdims=True)
    cnt_ref[...] = jnp.broadcast_to(base_sc[...], cnt_ref.shape).astype(jnp.int32)


def _router(xn2d, wr_t, br, tri, *, tm):
    r = xn2d.shape[0] // X_SLABS
    col = lambda i: (0, i)
    const = lambda i: (0, 0)
    return pl.pallas_call(
        _router_kernel,
        grid=(r // tm,),
        in_specs=[pl.BlockSpec((tm * X_SLABS, LANES), lambda i: (i, 0)),
                  pl.BlockSpec((N_EXPERTS, D_MODEL), const),
                  pl.BlockSpec((N_EXPERTS, 1), const),
                  pl.BlockSpec((tm, tm), const)],
        out_specs=[pl.BlockSpec((TOP_K, tm), col)] * 3 + [pl.BlockSpec((N_EXPERTS, LANES), const)],
        out_shape=[jax.ShapeDtypeStruct((TOP_K, r), jnp.int32),
                   jax.ShapeDtypeStruct((TOP_K, r), F32),
                   jax.ShapeDtypeStruct((TOP_K, r), jnp.int32),
                   jax.ShapeDtypeStruct((N_EXPERTS, LANES), jnp.int32)],
        scratch_shapes=[pltpu.VMEM((N_EXPERTS, 1), F32)],
        compiler_params=_cparams(("arbitrary",)),
        name="router",
    )(xn2d, wr_t, br, tri)


def _row_copy(src, dst, sem):
    return pltpu.make_async_copy(src, dst, sem)


def _slab(ref, row, n, lead=()):
    return ref.at[lead + (pl.ds(pl.multiple_of(row * n, n), n), slice(None))]


def _scatter_kernel(pos_ref, xn_ref, xs_ref, sem):
    tm = xn_ref.shape[0] // X_SLABS

    def start(tt, c):
        for u in range(POS_GROUP):
            for k in range(TOP_K):
                _row_copy(_slab(xn_ref, tt * POS_GROUP + u, X_SLABS),
                          _slab(xs_ref, pos_ref[0, tt, k * POS_GROUP + u], X_SLABS),
                          sem).start(priority=k % 2)
        return c

    lax.fori_loop(0, tm // POS_GROUP, start, 0)

    def wait(t, c):
        for k in range(TOP_K):
            _row_copy(_slab(xn_ref, 0, X_SLABS), _slab(xs_ref, 0, X_SLABS), sem).wait()
        return c

    lax.fori_loop(0, tm, wait, 0, unroll=4)


def _scatter(pos, xn2d, *, tm):
    r = xn2d.shape[0] // X_SLABS
    return pl.pallas_call(
        _scatter_kernel,
        grid=(r // tm,),
        in_specs=[pl.BlockSpec((1, tm // POS_GROUP, TOP_K * POS_GROUP), lambda i: (i, 0, 0),
                               memory_space=pltpu.SMEM),
                  pl.BlockSpec((tm * X_SLABS, LANES), lambda i: (i, 0))],
        out_specs=pl.BlockSpec(memory_space=pl.ANY),
        out_shape=jax.ShapeDtypeStruct((TOP_K * r * X_SLABS, LANES), jnp.uint32),
        scratch_shapes=[pltpu.SemaphoreType.DMA],
        compiler_params=_cparams(("arbitrary",)),
        name="moe_scatter",
    )(pos, xn2d)


def _experts_kernel(blk_ref, exp_ref, lo_ref, hi_ref, first_ref, newexp_ref,
                    x_ref, wgu_ref, bgu_ref, wd_ref, bd_ref, y_ref, wgu_bf, wd_bf):
    i = pl.program_id(0)

    @pl.when(newexp_ref[i] == 1)
    def _():
        wgu_bf[...] = wgu_ref[0].astype(BF16)
        wd_bf[...] = wd_ref[0].astype(BF16)

    x = _unpack_bf16_pairs(_load_row_slabs(x_ref, MOE_ROWS, X_SLABS))
    gu = jnp.dot(x, wgu_bf[...], preferred_element_type=F32) + bgu_ref[0]
    gate = jnp.minimum(gu[:, :EXPERT_FF], SWIGLU_LIMIT)
    up = jnp.clip(gu[:, EXPERT_FF:], -SWIGLU_LIMIT, SWIGLU_LIMIT)
    act = (up + 1.0) * (gate * jax.nn.sigmoid(SWIGLU_ALPHA * gate))
    y = jnp.dot(act.astype(BF16), wd_bf[...], preferred_element_type=F32) + bd_ref[0]
    row0 = blk_ref[i] * MOE_ROWS
    whole = (lo_ref[i] <= row0) & (hi_ref[i] >= row0 + MOE_ROWS)

    @pl.when(whole)
    def _():
        _store_row_slabs(y_ref, y)

    @pl.when(jnp.logical_not(whole))
    def _():
        @pl.when(first_ref[i] == 1)
        def _():
            y_ref[...] = jnp.zeros(y_ref.shape, F32)

        rows = row0 + lax.broadcasted_iota(jnp.int32, (MOE_ROWS, 1), 0)
        mine = (rows >= lo_ref[i]) & (rows < hi_ref[i])
        _store_row_slabs(y_ref, jnp.where(mine, y, _load_row_slabs(y_ref, MOE_ROWS, Y_SLABS)))


def _experts(items, xs, wgu, bgu, wd, bd):
    blk, exp, lo, hi, first = items
    newexp = jnp.concatenate([jnp.ones((1,), jnp.int32), (exp[1:] != exp[:-1]).astype(jnp.int32)])
    n_items = blk.shape[0]
    n_rows = xs.shape[0] // X_SLABS
    return pl.pallas_call(
        _experts_kernel,
        grid_spec=pltpu.PrefetchScalarGridSpec(
            num_scalar_prefetch=6,
            grid=(n_items,),
            in_specs=[pl.BlockSpec((MOE_ROWS * X_SLABS, LANES), lambda i, b, e, *_: (b[i], 0)),
                      pl.BlockSpec((1, D_MODEL, 2 * EXPERT_FF), lambda i, b, e, *_: (e[i], 0, 0)),
                      pl.BlockSpec((1, 1, 2 * EXPERT_FF), lambda i, b, e, *_: (e[i], 0, 0)),
                      pl.BlockSpec((1, EXPERT_FF, D_MODEL), lambda i, b, e, *_: (e[i], 0, 0)),
                      pl.BlockSpec((1, 1, D_MODEL), lambda i, b, e, *_: (e[i], 0, 0))],
            out_specs=pl.BlockSpec((MOE_ROWS * Y_SLABS, LANES), lambda i, b, e, *_: (b[i], 0)),
            scratch_shapes=[pltpu.VMEM((D_MODEL, 2 * EXPERT_FF), BF16), pltpu.VMEM((EXPERT_FF, D_MODEL), BF16)]),
        out_shape=jax.ShapeDtypeStruct((n_rows * Y_SLABS, LANES), F32),
        compiler_params=pltpu.CompilerParams(dimension_semantics=("arbitrary",),
                                             vmem_limit_bytes=EXPERTS_VMEM_LIMIT),
        name="moe_experts",
    )(blk, exp, lo, hi, first, newexp, xs, wgu, bgu, wd, bd)


def _work_items(counts, n_rows):
    n_blocks = n_rows // MOE_ROWS
    n_items = n_blocks + N_EXPERTS - 1
    ends = jnp.cumsum(counts)
    starts = ends - counts
    nb = jnp.where(counts > 0, (ends - 1) // MOE_ROWS - starts // MOE_ROWS + 1, 0)
    item_end = jnp.cumsum(nb)
    item_start = item_end - nb
    total = item_end[-1]
    i = jnp.arange(n_items, dtype=jnp.int32)
    e = jnp.minimum(jnp.sum(item_end[None, :] <= i[:, None], axis=1), N_EXPERTS - 1).astype(jnp.int32)
    blk = (starts[e] // MOE_ROWS + (i - item_start[e])).astype(jnp.int32)
    valid = i < total
    lo = jnp.maximum(starts[e], blk * MOE_ROWS)
    hi = jnp.minimum(ends[e], (blk + 1) * MOE_ROWS)
    last = jnp.maximum(total - 1, 0)
    blk = jnp.where(valid, blk, blk[last])
    e = jnp.where(valid, e, e[last])
    lo = jnp.where(valid, lo, 0).astype(jnp.int32)
    hi = jnp.where(valid, hi, 0).astype(jnp.int32)
    prev = jnp.concatenate([jnp.full((1,), -1, jnp.int32), blk[:-1]])
    first = (blk != prev).astype(jnp.int32)
    return blk, e, lo, hi, first, starts


def _combine_kernel(pos_ref, pos_next_ref, h_ref, gate_ref, g_ref, ys_ref, o_first_ref, o_rest_ref, buf, sem,
                    *, n_first):
    tm = h_ref.shape[0]
    i = pl.program_id(0)
    n = pl.num_programs(0)
    cur = i % 2

    def fetch(p_ref, slot):
        def start(tt, c):
            for u in range(POS_GROUP):
                for k in range(TOP_K):
                    _row_copy(_slab(ys_ref, p_ref[0, tt, k * POS_GROUP + u], Y_SLABS),
                              buf.at[slot, k, pl.ds((tt * POS_GROUP + u) * BUF_PITCH, Y_SLABS), :],
                              sem.at[slot]).start(priority=k % 2)
            return c

        lax.fori_loop(0, tm // POS_GROUP, start, 0)

    @pl.when(i == 0)
    def _():
        fetch(pos_ref, 0)

    for slot in range(2):
        @pl.when((i + 1 < n) & (cur != slot))
        def _():
            fetch(pos_next_ref, slot)

    def wait(t, c):
        for k in range(TOP_K):
            _row_copy(_slab(ys_ref, 0, Y_SLABS), _slab(buf, 0, Y_SLABS, (cur, 0)), sem.at[cur]).wait()
        return c

    lax.fori_loop(0, tm, wait, 0, unroll=4)
    acc = h_ref[...]
    for k in range(TOP_K):
        rows = jnp.concatenate([buf[cur, k, pl.ds(c, tm, stride=BUF_PITCH), :] for c in range(Y_SLABS)], axis=1)
        acc = acc + gate_ref[:, k:k + 1] * rows
    res = _rms(acc, g_ref[...])

    @pl.when(i < n_first)
    def _():
        o_first_ref[...] = res

    @pl.when(i >= n_first)
    def _():
        o_rest_ref[...] = res


def _combine(pos, h2d, gates_t, g_final, ys, *, tm, rows_first):
    r = h2d.shape[0]
    n_tiles = r // tm
    n_first = rows_first // tm
    row = lambda i: (i, 0)
    return pl.pallas_call(
        functools.partial(_combine_kernel, n_first=n_first),
        grid=(n_tiles,),
        in_specs=[pl.BlockSpec((1, tm // POS_GROUP, TOP_K * POS_GROUP), lambda i: (i, 0, 0),
                               memory_space=pltpu.SMEM),
                  pl.BlockSpec((1, tm // POS_GROUP, TOP_K * POS_GROUP),
                               lambda i: (jnp.minimum(i + 1, n_tiles - 1), 0, 0), memory_space=pltpu.SMEM),
                  pl.BlockSpec((tm, D_MODEL), row),
                  pl.BlockSpec((tm, TOP_K), row),
                  pl.BlockSpec((1, D_MODEL), lambda i: (0, 0)),
                  pl.BlockSpec(memory_space=pl.ANY)],
        out_specs=[pl.BlockSpec((tm, D_MODEL), lambda i: (jnp.minimum(i, n_first - 1), 0)),
                   pl.BlockSpec((tm, D_MODEL), lambda i: (jnp.maximum(i - n_first, 0), 0))],
        out_shape=[jax.ShapeDtypeStruct((rows_first, D_MODEL), F32),
                   jax.ShapeDtypeStruct((r - rows_first, D_MODEL), F32)],
        scratch_shapes=[pltpu.VMEM((2, TOP_K, tm * BUF_PITCH, LANES), F32), pltpu.SemaphoreType.DMA((2,))],
        compiler_params=_cparams(("arbitrary",)),
        name="moe_combine",
    )(pos, pos, h2d, gates_t, g_final, ys)


def _moe(h2d, xn2d, w, rows_first):
    r = h2d.shape[0]
    tm_router = _pick_tile(r, 512)
    tm_scatter = _pick_tile(r, 512)
    tm_combine = _pick_tile(r, 256)
    idx, gates, rank, cnt = _router(xn2d, w["wr_t"], w["br"], w["tri"][:tm_router, :tm_router], tm=tm_router)
    counts = cnt[:, 0]
    blk, e, lo, hi, first, starts = _work_items(counts, TOP_K * r)
    hot = idx[:, :, None] == jnp.arange(N_EXPERTS, dtype=jnp.int32)
    pos = jnp.sum(jnp.where(hot, starts.astype(jnp.int32), 0), axis=-1) + rank
    tiles = lambda tm: (pos.reshape(TOP_K, -1, tm // POS_GROUP, POS_GROUP).transpose(1, 2, 0, 3)
                        .reshape(-1, tm // POS_GROUP, TOP_K * POS_GROUP))
    xs = _scatter(tiles(tm_scatter), xn2d, tm=tm_scatter)
    ys = _experts((blk, e, lo, hi, first), xs, w["wgu"], w["bgu"], w["wd"], w["bd"])
    return _combine(tiles(tm_combine), h2d, gates.T, w["g_final"], ys, tm=tm_combine, rows_first=rows_first)


def _pick_tile(n, cap):
    t = cap
    while n % t:
        t //= 2
    return t


def kernel(x_prompt, x_sample, mem_prompt, cache_a_k, cache_a_v, cache_b_k, cache_b_v, cache_mem_k, cache_mem_v, g_mix, w_in, rel_bias, lam_q1, lam_k1, lam_q2, lam_k2, g_subln, w_out, g_xq, g_xmem, w_xq, w_xk, w_xv, w_xo, g_moe, w_router, b_router, w_gate_up, b_gate_up, w_down, b_down, g_final):
    assert g_mix.shape[0] == 1, "single-layer kernel"
    b, s, d = x_prompt.shape
    db, dl, _ = x_sample.shape
    past = cache_b_k.shape[2]
    a_len = cache_a_k.shape[2]
    mlen = mem_prompt.shape[1]
    keep_a = min(BAND, s)
    assert d == D_MODEL and s % B_QBLK == 0 and s >= A_WIN and dl == CHUNK and past % CHUNK == 0
    assert B_QBLK == 2 * B_KBLK

    row1 = lambda v: v.reshape(1, -1).astype(F32)
    w_in_bf = w_in[0].astype(BF16)
    w_out_bf = w_out[0].astype(BF16)
    weights = dict(
        wr_t=w_router[0].T.astype(BF16), br=b_router[0].reshape(N_EXPERTS, 1).astype(F32),
        tri=jnp.asarray(np.triu(np.ones((512, 512), np.float32), 1), BF16),
        wgu=w_gate_up[0].astype(F32), bgu=b_gate_up[0].reshape(N_EXPERTS, 1, -1).astype(F32),
        wd=w_down[0].astype(F32), bd=b_down[0].reshape(N_EXPERTS, 1, -1).astype(F32),
        g_final=row1(g_final))
    lam = (jnp.exp(jnp.sum(lam_q1[0].astype(F32) * lam_k1[0].astype(F32)))
           - jnp.exp(jnp.sum(lam_q2[0].astype(F32) * lam_k2[0].astype(F32))) + LAM_INIT).reshape(1)
    table = _band_table(rel_bias[0])
    g_sub = row1(g_subln[0])

    tm_p = _pick_tile(s, 512)
    tabs_p = _rope_tables(np.arange(s))
    p_in = _inproj(x_prompt.reshape(b * s, d), row1(g_mix[0]), w_in_bf, *tabs_p,
                   tm=tm_p, n_pos_blocks=s // tm_p, tail_div=s // tm_p if keep_a == tm_p else 1, emit_vt=True)
    tabs_s = _rope_tables(np.tile(past + np.arange(dl), db))
    s_in = _inproj(x_sample.reshape(db * dl, d), row1(g_mix[0]), w_in_bf, *tabs_s,
                   tm=db * dl, n_pos_blocks=1, tail_div=1, emit_vt=False)

    def split(outs, nb, nl):
        return [o.reshape(nb, nl, A_WIDTH) for o in outs[:6]]

    aq, ak, _, bq, bk, _ = split(p_in, b, s)
    avt = p_in[10].reshape((b, s // A_QBLK) + p_in[10].shape[1:])
    bvt = p_in[11].reshape((b, s // B_KBLK) + p_in[11].shape[1:])
    saq, sak, sav, sbq, sbk, sbv = split(s_in, db, dl)

    oa = _attn_a_prompt(aq, ak, avt, table.transpose(0, 2, 1))
    ob = _attn_b_prompt(lam, bq, bk, bvt, g_sub)
    soa = _attn_a_sample(saq, cache_a_k[0].reshape(db, a_len * N_HEADS_A, HEAD_DIM),
                         cache_a_v[0].reshape(db, a_len * N_HEADS_A, HEAD_DIM), sak, sav, table[:, :dl, :BAND + dl])
    sob = _attn_b_sample(lam, sbq, cache_b_k[0].reshape(db, past, B_WIDTH),
                         cache_b_v[0].reshape(db, past * N_HEADS_B, 2 * HEAD_DIM), sbk, sbv, g_sub)

    mkf, mvf, mkb, mvb = _memkv(mem_prompt.reshape(b * mlen, d), row1(g_xmem[0]),
                                w_xk[0].astype(BF16), w_xv[0].astype(BF16), tm=mlen)
    post_w = (w_out_bf[:A_WIDTH], w_out_bf[A_WIDTH:], row1(g_xq[0]), w_xq[0].astype(BF16))
    tm_post = _pick_tile(math.gcd(s, b * s + db * dl), 512)
    rows_p, rows_s = b * s, db * dl
    sob_t = sob.transpose(0, 2, 1).reshape(db, N_HEADS_B, 2 * HEAD_DIM, dl)
    h_xn = _post(x_prompt.reshape(rows_p, d), oa, ob,
                 *post_w, mkb.reshape(b, mlen, d), mvb.reshape(b, mlen, d),
                 w_xo[0].astype(BF16), row1(g_moe[0]), tm=tm_post, tiles_per_batch=s // tm_post,
                 total_rows=rows_p + rows_s)
    h_all, xn_all = _post(x_sample.reshape(rows_s, d), soa.transpose(0, 2, 1), sob_t,
                          *post_w, cache_mem_k[0].reshape(db, mlen, d).astype(BF16),
                          cache_mem_v[0].reshape(db, mlen, d).astype(BF16),
                          w_xo[0].astype(BF16), row1(g_moe[0]), tm=dl, tiles_per_batch=1,
                          total_rows=rows_p + rows_s, row_offset=rows_p, into=h_xn)

    y_prompt, y_sample = _moe(h_all, xn_all, weights, rows_p)
    y_prompt = y_prompt.reshape(b, s, d)
    y_sample = y_sample.reshape(db, dl, d)

    def heads(x2d, nb, nl, nh):
        return x2d.reshape(1, nb, nl, nh, -1)

    akf, avf, bkf, bvf = p_in[6:10]
    if keep_a != tm_p:
        akf = akf.reshape(b, s, A_WIDTH)[:, s - keep_a:]
        avf = avf.reshape(b, s, A_WIDTH)[:, s - keep_a:]
    sakf, savf, sbkf, sbvf = s_in[6:]
    return (y_prompt, y_sample,
            heads(akf, b, keep_a, N_HEADS_A), heads(avf, b, keep_a, N_HEADS_A),
            heads(bkf, b, s, 2 * N_HEADS_B), heads(bvf, b, s, N_HEADS_B),
            heads(mkf, b, mlen, N_HEADS_M), heads(mvf, b, mlen, N_HEADS_M),
            heads(sakf, db, dl, N_HEADS_A), heads(savf, db, dl, N_HEADS_A),
            heads(sbkf, db, dl, 2 * N_HEADS_B), heads(sbvf, db, dl, N_HEADS_B))
```

```python
import functools
import math

import numpy as np
import jax
import jax.numpy as jnp
from jax import lax
from jax.experimental import pallas as pl
from jax.experimental.pallas import tpu as pltpu

F32 = jnp.float32
BF16 = jnp.bfloat16

D_MODEL = 1024
CHUNK = 64
LEFT_CHUNKS = 8
BAND = LEFT_CHUNKS * CHUNK
HEAD_DIM = 64
A_WIDTH = 512
B_WIDTH = 512
N_HEADS_A = 8
N_HEADS_B = 4
REL_CLIP = 128
ROT_DIM = 16
ROPE_THETA = 500000.0
N_HEADS_M = 4
HEAD_DIM_M = 256
N_EXPERTS = 32
TOP_K = 4
EXPERT_FF = 1024
SWIGLU_LIMIT = 7.0
SWIGLU_ALPHA = 1.702
RMS_EPS = 1e-5
NEG_INF = -1e30
LAM_INIT = 0.8 - 0.6 * math.exp(-0.3 * 0)
LOG2E = math.log2(math.e)

LANES = 128
A_QBLK = 4 * CHUNK
A_WIN = BAND + A_QBLK
B_QBLK = 512
B_KBLK = 256
VT_ONES = 16
BUF_PITCH = 9
POS_GROUP = 16
MOE_ROWS = 512
X_SLABS = D_MODEL // 2 // LANES
Y_SLABS = D_MODEL // LANES
VMEM_LIMIT = 48 * 1024 * 1024
EXPERTS_VMEM_LIMIT = 56 * 1024 * 1024


def _cparams(sem):
    return pltpu.CompilerParams(dimension_semantics=sem, vmem_limit_bytes=VMEM_LIMIT)


def _rms(x, g):
    return x * lax.rsqrt(jnp.mean(x * x, axis=-1, keepdims=True) + RMS_EPS) * g


HI16 = 0xFFFF0000


def _pack_bf16_pairs(x):
    n = x.shape[1] // 2
    bits = lambda v: lax.bitcast_convert_type(v.astype(BF16).astype(F32), jnp.uint32)
    return (bits(x[:, :n]) >> 16) | (bits(x[:, n:]) & jnp.uint32(HI16))


def _unpack_bf16_pairs(w):
    lo = lax.bitcast_convert_type(w << 16, F32)
    hi = lax.bitcast_convert_type(w & jnp.uint32(HI16), F32)
    return jnp.concatenate([lo, hi], axis=1).astype(BF16)


def _store_row_slabs(ref, x, lead=()):
    m, n = x.shape[0], x.shape[1] // LANES
    for c in range(n):
        ref[lead + (pl.ds(c, m, stride=n), slice(None))] = x[:, c * LANES:(c + 1) * LANES]


def _load_row_slabs(ref, m, n, lead=()):
    return jnp.concatenate([ref[lead + (pl.ds(c, m, stride=n), slice(None))] for c in range(n)], axis=1)


def _dot_nt(a, b):
    return lax.dot_general(a, b, (((1,), (1,)), ((), ())), preferred_element_type=F32)


def _store_heads64(ref, x):
    m, n_heads = x.shape[0], x.shape[1] // HEAD_DIM
    for c in range(x.shape[1] // LANES):
        grp = x[:, c * LANES:(c + 1) * LANES]
        ref[pl.ds(2 * c, m, stride=n_heads), :] = grp[:, :HEAD_DIM]
        ref[pl.ds(2 * c + 1, m, stride=n_heads), :] = pltpu.roll(grp, HEAD_DIM, 1)[:, :HEAD_DIM]


def _inproj_kernel(x_ref, g_ref, w_ref, cos_ref, sa_ref, sb_ref,
                   aq_ref, ak_ref, av_ref, bq_ref, bk_ref, bv_ref,
                   akf_ref, avf_ref, bkf_ref, bvf_ref, *maybe_vt_refs, tail_div):
    xn = _rms(x_ref[...], g_ref[...]).astype(BF16)
    is_tail = pl.program_id(0) % tail_div == tail_div - 1

    def seg(s):
        return jnp.dot(xn, w_ref[:, s * A_WIDTH:(s + 1) * A_WIDTH], preferred_element_type=F32)

    def rope(z):
        cols = []
        for c in range(z.shape[1] // LANES):
            zc = z[:, c * LANES:(c + 1) * LANES]
            cols.append(zc * cos_ref[...] + pltpu.roll(zc, 8, 1) * sa_ref[...]
                        + pltpu.roll(zc, LANES - 8, 1) * sb_ref[...])
        return jnp.concatenate(cols, axis=1)

    scale = HEAD_DIM ** -0.5
    aq_ref[...] = (seg(0) * (scale * LOG2E)).astype(BF16)
    ak = seg(1)
    ak_ref[...] = ak.astype(BF16)
    av = seg(2)
    av_ref[...] = av.astype(BF16)
    bq_ref[...] = (rope(seg(3)) * (scale * LOG2E)).astype(BF16)
    bk = rope(seg(4))
    bk_ref[...] = bk.astype(BF16)
    _store_heads64(bkf_ref, bk)
    bv = seg(5)
    bv_ref[...] = bv.astype(BF16)
    _store_row_slabs(bvf_ref, bv)
    for vt_ref, v in zip(maybe_vt_refs, (av, bv)):
        vt = v.T.astype(BF16)
        blk = vt_ref.shape[3]
        for c in range(vt_ref.shape[0]):
            for g in range(vt_ref.shape[1]):
                vt_ref[c, g, 0:LANES, :] = vt[g * LANES:(g + 1) * LANES, c * blk:(c + 1) * blk]
                vt_ref[c, g, LANES:LANES + VT_ONES, :] = jnp.ones((VT_ONES, blk), BF16)

    @pl.when(is_tail)
    def _():
        _store_heads64(akf_ref, ak)
        _store_heads64(avf_ref, av)


def _inproj(x2d, g, w_bf, cos_t, sa_t, sb_t, *, tm, n_pos_blocks, tail_div, emit_vt):
    r = x2d.shape[0]
    n_tiles = r // tm
    row = lambda i: (i, 0)
    const = lambda i: (0, 0)
    pos = lambda i: (i % n_pos_blocks, 0)
    tail = lambda i: (i // tail_div, 0)
    bf = jax.ShapeDtypeStruct((r, A_WIDTH), BF16)
    blk = lambda im: pl.BlockSpec((tm, A_WIDTH), im)
    h64 = lambda im: pl.BlockSpec((tm * N_HEADS_A, HEAD_DIM), im)
    f_tail = jax.ShapeDtypeStruct((r // tail_div * N_HEADS_A, HEAD_DIM), F32)
    out_specs = [blk(row)] * 6 + [h64(tail), h64(tail), h64(row),
                                  pl.BlockSpec((tm * N_HEADS_B, 2 * HEAD_DIM), row)]
    out_shape = [bf] * 6 + [f_tail, f_tail,
                            jax.ShapeDtypeStruct((r * 2 * N_HEADS_B, HEAD_DIM), F32),
                            jax.ShapeDtypeStruct((r * N_HEADS_B, 2 * HEAD_DIM), F32)]
    if emit_vt:
        for kb in (A_QBLK, B_KBLK):
            grp = (A_WIDTH // LANES, LANES + VT_ONES, kb)
            out_specs.append(pl.BlockSpec((tm // kb,) + grp, lambda i: (i, 0, 0, 0)))
            out_shape.append(jax.ShapeDtypeStruct((r // kb,) + grp, BF16))
    return pl.pallas_call(
        functools.partial(_inproj_kernel, tail_div=tail_div),
        grid=(n_tiles,),
        in_specs=[pl.BlockSpec((tm, D_MODEL), row), pl.BlockSpec((1, D_MODEL), const),
                  pl.BlockSpec((D_MODEL, 3 * D_MODEL), const),
                  pl.BlockSpec((tm, LANES), pos), pl.BlockSpec((tm, LANES), pos),
                  pl.BlockSpec((tm, LANES), pos)],
        out_specs=out_specs,
        out_shape=out_shape,
        compiler_params=_cparams(("arbitrary",)),
        name="inproj",
    )(x2d, g, w_bf, cos_t, sa_t, sb_t)


def _rope_tables(positions):
    half = ROT_DIM // 2
    lane = np.arange(LANES) % HEAD_DIM
    inv_freq = np.power(ROPE_THETA, -np.arange(half, dtype=np.float64) / half)
    ang = np.asarray(positions, np.float64)[:, None] * inv_freq[None, :]
    cos = np.cos(ang)[:, lane % half]
    sin = np.sin(ang)[:, lane % half]
    in_rot = (lane < ROT_DIM)[None, :]
    first = (lane < half)[None, :]
    cos_t = np.where(in_rot, cos, 1.0)
    sa_t = np.where(in_rot & ~first, sin, 0.0)
    sb_t = np.where(first, -sin, 0.0)
    return tuple(jnp.asarray(t, F32) for t in (cos_t, sa_t, sb_t))


def _band_table(rel_bias):
    r = np.arange(A_QBLK)[:, None]
    j = np.arange(BAND + A_WIN)[None, :]
    dchunk = LEFT_CHUNKS + r // CHUNK - j // CHUNK
    allowed = (dchunk >= 0) & (dchunk <= LEFT_CHUNKS)
    bias = rel_bias.astype(F32) * LOG2E
    n_seq = BAND + A_WIN + A_QBLK - 1
    n_hi = BAND + A_QBLK - 1 - REL_CLIP
    seq = jnp.concatenate([jnp.broadcast_to(bias[:, -1:], (N_HEADS_A, n_hi)), bias[:, ::-1],
                           jnp.broadcast_to(bias[:, :1], (N_HEADS_A, n_seq - n_hi - bias.shape[1]))], axis=1)
    seq = jnp.roll(seq, -(A_QBLK - 1), axis=1)
    toe = jnp.tile(seq, (1, A_QBLK))[:, :A_QBLK * (n_seq - 1)].reshape(N_HEADS_A, A_QBLK, n_seq - 1)
    return jnp.where(jnp.asarray(allowed)[None], toe[:, :, :BAND + A_WIN], NEG_INF)


def _softmax_pv(s_list, v_list, exp_fn=jnp.exp):
    m = s_list[0].max(axis=-1, keepdims=True)
    for s in s_list[1:]:
        m = jnp.maximum(m, s.max(axis=-1, keepdims=True))
    l = None
    o = None
    for s, v in zip(s_list, v_list):
        p = exp_fn(s - m)
        ls = p.sum(axis=-1, keepdims=True)
        os_ = jnp.dot(p.astype(BF16), v, preferred_element_type=F32)
        l = ls if l is None else l + ls
        o = os_ if o is None else o + os_
    return o / l


def _attn_a_prompt_kernel(q_ref, k_ref, vt_ref, tab_ref, o_ref):
    i = pl.program_id(1)
    n_left = BAND // A_QBLK
    n_sub = A_WIN // A_QBLK
    blk0 = jnp.maximum(i - n_left, 0)
    off = pl.multiple_of(jnp.maximum(n_left - i, 0) * A_QBLK, A_QBLK)
    q = q_ref[0]
    lane = lax.broadcasted_iota(jnp.int32, (A_QBLK, LANES), 1)

    def scores(h):
        cols = slice(h // 2 * LANES, (h // 2 + 1) * LANES)
        qp = q[:, cols]
        qh = jnp.where((lane >= HEAD_DIM) == bool(h % 2), qp, jnp.zeros_like(qp))
        return [_dot_nt(k_ref[0, pl.ds(pl.multiple_of((blk0 + c) * A_QBLK, A_QBLK), A_QBLK), cols], qh)
                + tab_ref[h, pl.ds(off + c * A_QBLK, A_QBLK), :] for c in range(n_sub)]

    def attend(h, ss):
        m = _col_reduce(ss[0], jnp.maximum, jnp.max)
        for s in ss[1:]:
            m = jnp.maximum(m, _col_reduce(s, jnp.maximum, jnp.max))
        o_t = None
        for c in range(n_sub):
            p = jnp.exp2(ss[c] - m).astype(BF16)
            os_ = jnp.dot(vt_ref[0, blk0 + c, h // 2], p, preferred_element_type=F32)
            o_t = os_ if o_t is None else o_t + os_
        return o_t[h % 2 * HEAD_DIM:(h % 2 + 1) * HEAD_DIM] / o_t[LANES:LANES + 1]

    halves = []
    ss = scores(0)
    for h in range(N_HEADS_A):
        ss_next = scores(h + 1) if h + 1 < N_HEADS_A else None
        halves.append(attend(h, ss))
        ss = ss_next
    o_ref[0] = jnp.concatenate(halves, axis=0).astype(BF16)


def _attn_a_prompt(aq, ak, avt, table_t):
    b, l, _ = aq.shape
    n_kb = avt.shape[1]
    return pl.pallas_call(
        _attn_a_prompt_kernel,
        grid=(b, l // A_QBLK),
        in_specs=[pl.BlockSpec((1, A_QBLK, A_WIDTH), lambda bb, i: (bb, i, 0)),
                  pl.BlockSpec((1, l, A_WIDTH), lambda bb, i: (bb, 0, 0)),
                  pl.BlockSpec((1,) + avt.shape[1:], lambda bb, i: (bb, 0, 0, 0, 0)),
                  pl.BlockSpec(table_t.shape, lambda bb, i: (0, 0, 0))],
        out_specs=pl.BlockSpec((1, A_WIDTH, A_QBLK), lambda bb, i: (bb, 0, i)),
        out_shape=jax.ShapeDtypeStruct((b, A_WIDTH, l), BF16),
        compiler_params=_cparams(("arbitrary", "arbitrary")),
        name="attn_a_prompt",
    )(aq, ak, avt, table_t)


def _attn_a_sample_kernel(q_ref, kc_ref, vc_ref, k_ref, v_ref, tab_ref, o_ref):
    q = q_ref[0]
    ko = k_ref[0]
    vo = v_ref[0]
    a_len = kc_ref.shape[1] // N_HEADS_A
    lq = q.shape[0]
    c0 = BAND - a_len
    outs = []
    for h in range(N_HEADS_A):
        sl = slice(h * HEAD_DIM, (h + 1) * HEAD_DIM)
        kc = kc_ref[0, pl.ds(h, a_len, stride=N_HEADS_A), :].astype(BF16)
        vc = vc_ref[0, pl.ds(h, a_len, stride=N_HEADS_A), :].astype(BF16)
        s_c = _dot_nt(q[:, sl], kc) + tab_ref[h, :, c0:BAND]
        s_o = _dot_nt(q[:, sl], ko[:, sl]) + tab_ref[h, :, BAND:BAND + lq]
        outs.append(_softmax_pv([s_c, s_o], [vc, vo[:, sl]], exp_fn=jnp.exp2))
    o_ref[0] = jnp.concatenate(outs, axis=1).astype(BF16)


def _attn_a_sample(aq, cache_k, cache_v, ak, av, table):
    b, l, _ = aq.shape
    bs = lambda n: pl.BlockSpec((1, n, A_WIDTH), lambda bb: (bb, 0, 0))
    cache = pl.BlockSpec((1,) + cache_k.shape[1:], lambda bb: (bb, 0, 0))
    return pl.pallas_call(
        _attn_a_sample_kernel,
        grid=(b,),
        in_specs=[bs(l), cache, cache, bs(l), bs(l),
                  pl.BlockSpec(table.shape, lambda bb: (0, 0, 0))],
        out_specs=bs(l),
        out_shape=jax.ShapeDtypeStruct((b, l, A_WIDTH), BF16),
        compiler_params=_cparams(("arbitrary",)),
        name="attn_a_sample",
    )(aq, cache_k, cache_v, ak, av, table)


def _subnorm(o, g):
    return _rms(o, g) * (1.0 - LAM_INIT)


def _col_reduce(x, op, reduce_fn):
    n = x.shape[0]
    while n > 8:
        n //= 2
        x = op(x[:n], x[n:])
    return reduce_fn(x, axis=0, keepdims=True)


def _attn_b_prompt_kernel(lam_ref, q_ref, qn_ref, k_ref, vt_ref, mask_ref, g_ref, o_ref,
                          sa_sc, sb_sc, m_sc, acc_sc):
    i = pl.program_id(2)
    tk = vt_ref.shape[4]
    lane = lax.broadcasted_iota(jnp.int32, q_ref.shape[1:], 1)

    def split_maps(q):
        zero = jnp.zeros_like(q)
        return jnp.where(lane < HEAD_DIM, q, zero), jnp.where(lane >= HEAD_DIM, q, zero)

    qs = split_maps(q_ref[0])
    m_sc[...] = jnp.full(m_sc.shape, NEG_INF, F32)
    acc_sc[...] = jnp.zeros(acc_sc.shape, F32)

    def scores(j, s_sc, qmaps=qs):
        kb = k_ref[0, pl.ds(pl.multiple_of(j * tk, tk), tk), :]
        for a in range(2):
            s_sc[a] = _dot_nt(kb, qmaps[a])

    def consume(j, s_sc, mask):
        vt = vt_ref[0, j, 0]
        for a in range(2):
            s = s_sc[a]
            if mask is not None:
                s = s + mask
            m_old = m_sc[a]
            m_new = jnp.maximum(m_old, _col_reduce(s, jnp.maximum, jnp.max))
            alpha = jnp.exp2(m_old - m_new)
            p = jnp.exp2(s - m_new).astype(BF16)
            m_sc[a] = m_new
            acc_sc[a] = alpha * acc_sc[a] + jnp.dot(vt, p, preferred_element_type=F32)

    @pl.when(i == 0)
    def _():
        scores(0, sa_sc)

    def pair(j):
        scores(j + 1, sb_sc)
        consume(j, sa_sc, None)
        scores(j + 2, sa_sc)
        consume(j + 1, sb_sc, None)

    def body(jj, c):
        pair(4 * jj)
        pair(4 * jj + 2)
        return c

    lax.fori_loop(0, i // 2, body, 0)

    @pl.when(i % 2 == 1)
    def _():
        pair(2 * i - 2)
    scores(2 * i + 1, sb_sc)
    consume(2 * i, sa_sc, mask_ref[0])
    scores(0, sa_sc, split_maps(qn_ref[0]))
    consume(2 * i + 1, sb_sc, mask_ref[1])
    o_t = (acc_sc[0, 0:LANES] / acc_sc[0, LANES:LANES + 1]
           - lam_ref[0] * (acc_sc[1, 0:LANES] / acc_sc[1, LANES:LANES + 1]))
    inv = lax.rsqrt(jnp.mean(o_t * o_t, axis=0, keepdims=True) + RMS_EPS)
    o_ref[0, 0] = (o_t * inv * (g_ref[...] * (1.0 - LAM_INIT))).astype(BF16)


def _chunk_causal_masks_t(tk, tq):
    kc = (np.arange(tq // tk)[:, None, None] * tk + np.arange(tk)[None, :, None]) // CHUNK
    qc = np.arange(tq)[None, None, :] // CHUNK
    return jnp.asarray(np.where(kc <= qc, 0.0, NEG_INF), F32)


def _attn_b_prompt(lam, bq, bk, bvt, g_sub):
    b, l, _ = bq.shape
    tq = B_QBLK
    tk = B_KBLK
    hw = 2 * HEAD_DIM
    n_kb = bvt.shape[1]
    return pl.pallas_call(
        _attn_b_prompt_kernel,
        grid_spec=pltpu.PrefetchScalarGridSpec(
            num_scalar_prefetch=1,
            grid=(b, N_HEADS_B, l // tq),
            in_specs=[pl.BlockSpec((1, tq, hw), lambda bb, h, i, lam_: (bb, i, h)),
                      pl.BlockSpec((1, tq, hw), lambda bb, h, i, lam_: (bb, jnp.minimum(i + 1, l // tq - 1), h)),
                      pl.BlockSpec((1, l, hw), lambda bb, h, i, lam_: (bb, 0, h)),
                      pl.BlockSpec((1, n_kb, 1, hw + VT_ONES, tk), lambda bb, h, i, lam_: (bb, 0, h, 0, 0)),
                      pl.BlockSpec((tq // tk, tk, tq), lambda bb, h, i, lam_: (0, 0, 0)),
                      pl.BlockSpec((hw, 1), lambda bb, h, i, lam_: (0, 0))],
            out_specs=pl.BlockSpec((1, 1, hw, tq), lambda bb, h, i, lam_: (bb, h, 0, i)),
            scratch_shapes=[pltpu.VMEM((2, tk, tq), F32), pltpu.VMEM((2, tk, tq), F32),
                            pltpu.VMEM((2, 1, tq), F32), pltpu.VMEM((2, hw + VT_ONES, tq), F32)]),
        out_shape=jax.ShapeDtypeStruct((b, N_HEADS_B, hw, l), BF16),
        compiler_params=_cparams(("arbitrary", "arbitrary", "arbitrary")),
        name="attn_b_prompt",
    )(lam, bq, bq, bk, bvt, _chunk_causal_masks_t(tk, tq), g_sub.reshape(hw, 1))


def _attn_b_sample_kernel(lam_ref, q_ref, kc_ref, vc_ref, k_ref, v_ref, g_ref, o_ref):
    q = q_ref[0]
    kc = kc_ref[0].astype(BF16)
    n_past = vc_ref.shape[1] // N_HEADS_B
    vc = vc_ref[0, pl.ds(pl.program_id(1), n_past, stride=N_HEADS_B), :].astype(BF16)
    ko = k_ref[0]
    vo = v_ref[0]
    outs = []
    for a in range(2):
        sl = slice(a * HEAD_DIM, (a + 1) * HEAD_DIM)
        s_c = _dot_nt(q[:, sl], kc[:, sl])
        s_o = _dot_nt(q[:, sl], ko[:, sl])
        outs.append(_softmax_pv([s_c, s_o], [vc, vo], exp_fn=jnp.exp2))
    o = outs[0] - lam_ref[0] * outs[1]
    o_ref[0] = _subnorm(o, g_ref[...]).astype(BF16)


def _attn_b_sample(lam, bq, cache_k, cache_v, bk, bv, g_sub):
    b, l, _ = bq.shape
    p = cache_k.shape[1]
    hw = 2 * HEAD_DIM
    bs = lambda n: pl.BlockSpec((1, n, hw), lambda bb, h, lam_: (bb, 0, h))
    return pl.pallas_call(
        _attn_b_sample_kernel,
        grid_spec=pltpu.PrefetchScalarGridSpec(
            num_scalar_prefetch=1,
            grid=(b, N_HEADS_B),
            in_specs=[bs(l), bs(p),
                      pl.BlockSpec((1, p * N_HEADS_B, hw), lambda bb, h, lam_: (bb, 0, 0)),
                      bs(l), bs(l),
                      pl.BlockSpec((1, hw), lambda bb, h, lam_: (0, 0))],
            out_specs=bs(l)),
        out_shape=jax.ShapeDtypeStruct((b, l, B_WIDTH), BF16),
        compiler_params=_cparams(("arbitrary", "arbitrary")),
        name="attn_b_sample",
    )(lam, bq, cache_k, cache_v, bk, bv, g_sub)


def _memkv_kernel(m_ref, g_ref, wk_ref, wv_ref, kf_ref, vf_ref, kb_ref, vb_ref):
    mn = _rms(m_ref[...], g_ref[...]).astype(BF16)
    k = jnp.dot(mn, wk_ref[...], preferred_element_type=F32)
    v = jnp.dot(mn, wv_ref[...], preferred_element_type=F32)
    kf_ref[...] = k
    vf_ref[...] = v
    kb_ref[...] = k.astype(BF16)
    vb_ref[...] = v.astype(BF16)


def _memkv(mem2d, g, wk, wv, *, tm):
    r = mem2d.shape[0]
    row = lambda i: (i, 0)
    const = lambda i: (0, 0)
    blk = pl.BlockSpec((tm, D_MODEL), row)
    wspec = pl.BlockSpec((D_MODEL, D_MODEL), const)
    f = jax.ShapeDtypeStruct((r, D_MODEL), F32)
    bf = jax.ShapeDtypeStruct((r, D_MODEL), BF16)
    return pl.pallas_call(
        _memkv_kernel,
        grid=(r // tm,),
        in_specs=[blk, pl.BlockSpec((1, D_MODEL), const), wspec, wspec],
        out_specs=[blk] * 4,
        out_shape=[f, f, bf, bf],
        compiler_params=_cparams(("arbitrary",)),
        name="memkv",
    )(mem2d, g, wk, wv)


def _post_kernel(*refs, n_real):
    h_ref, xn_ref = refs[-2:]

    @pl.when(pl.program_id(0) < n_real)
    def _():
        _post_tile(*refs[:11], h_ref, xn_ref)

    @pl.when(pl.program_id(0) >= n_real)
    def _():
        h_ref[...] = jnp.zeros(h_ref.shape, h_ref.dtype)
        xn_ref[...] = jnp.zeros(xn_ref.shape, xn_ref.dtype)


def _post_tile(x_ref, oa_ref, ob_ref, woa_ref, wob_ref, gq_ref, wq_ref, mk_ref, mv_ref, wo_ref,
               gm_ref, h_ref, xn_ref):
    tn = lambda a_t, w: lax.dot_general(a_t, w, (((0,), (0,)), ((), ())), preferred_element_type=F32)
    ob_t = ob_ref[0].reshape(B_WIDTH, ob_ref.shape[3])
    h1 = x_ref[...] + tn(oa_ref[0], woa_ref[...]) + tn(ob_t, wob_ref[...])
    xq = _rms(h1, gq_ref[...]).astype(BF16)
    q = (jnp.dot(xq, wq_ref[...], preferred_element_type=F32) * HEAD_DIM_M ** -0.5).astype(BF16)
    outs = []
    for h in range(N_HEADS_M):
        sl = slice(h * HEAD_DIM_M, (h + 1) * HEAD_DIM_M)
        s = _dot_nt(q[:, sl], mk_ref[0, :, sl])
        outs.append(_softmax_pv([s], [mv_ref[0, :, sl]]).astype(BF16))
    o = jnp.concatenate(outs, axis=1)
    h2 = h1 + jnp.dot(o, wo_ref[...], preferred_element_type=F32)
    h_ref[...] = h2
    _store_row_slabs(xn_ref, _pack_bf16_pairs(_rms(h2, gm_ref[...])))


def _post(x2d, oa, ob, woa, wob, gq, wq, mk, mv, wo, gm, *, tm, tiles_per_batch, total_rows,
          row_offset=0, into=None):
    r = x2d.shape[0]
    n_real = r // tm
    n_steps = n_real if into is not None else total_rows // tm
    mlen = mk.shape[1]
    row = lambda i: (jnp.minimum(i, n_real - 1), 0)
    out_row = lambda i: (row_offset // tm + i, 0)
    const = lambda i: (0, 0)
    memb = lambda i: (jnp.minimum(i, n_real - 1) // tiles_per_batch, 0, 0)
    gspec = pl.BlockSpec((1, D_MODEL), const)
    wspec = pl.BlockSpec((D_MODEL, D_MODEL), const)
    hspec = pl.BlockSpec((A_WIDTH, D_MODEL), const)
    obt_blk = lambda i: (jnp.minimum(i, n_real - 1) // tiles_per_batch, 0, 0,
                         jnp.minimum(i, n_real - 1) % tiles_per_batch)
    oat_blk = lambda i: (jnp.minimum(i, n_real - 1) // tiles_per_batch, 0,
                         jnp.minimum(i, n_real - 1) % tiles_per_batch)
    in_specs = [pl.BlockSpec((tm, D_MODEL), row),
                pl.BlockSpec((1, A_WIDTH, tm), oat_blk),
                pl.BlockSpec((1, N_HEADS_B, 2 * HEAD_DIM, tm), obt_blk),
                hspec, hspec, gspec, wspec,
                pl.BlockSpec((1, mlen, D_MODEL), memb), pl.BlockSpec((1, mlen, D_MODEL), memb),
                wspec, gspec]
    args = [x2d, oa, ob, woa, wob, gq, wq, mk, mv, wo, gm]
    aliases = {}
    if into is not None:
        aliases = {len(args): 0, len(args) + 1: 1}
        in_specs += [pl.BlockSpec(memory_space=pl.ANY)] * 2
        args += list(into)
    return pl.pallas_call(
        functools.partial(_post_kernel, n_real=n_real),
        grid=(n_steps,),
        in_specs=in_specs,
        out_specs=[pl.BlockSpec((tm, D_MODEL), out_row), pl.BlockSpec((tm * X_SLABS, LANES), out_row)],
        out_shape=[jax.ShapeDtypeStruct((total_rows, D_MODEL), F32),
                   jax.ShapeDtypeStruct((total_rows * X_SLABS, LANES), jnp.uint32)],
        input_output_aliases=aliases,
        compiler_params=_cparams(("arbitrary",)),
        name="post_attn",
    )(*args)


def _router_kernel(xn_ref, wr_ref, br_ref, tri_ref, idx_ref, gate_ref, rank_ref, cnt_ref, base_sc):
    @pl.when(pl.program_id(0) == 0)
    def _():
        base_sc[...] = jnp.zeros(base_sc.shape, F32)

    tm = xn_ref.shape[0] // X_SLABS
    xn = _unpack_bf16_pairs(_load_row_slabs(xn_ref, tm, X_SLABS))
    logits = _dot_nt(wr_ref[...], xn) + br_ref[...]
    iota_e = lax.broadcasted_iota(jnp.int32, (N_EXPERTS, tm), 0).astype(F32)
    work = logits
    vals, idxs, hots = [], [], []
    for _k in range(TOP_K):
        mx = work.max(axis=0, keepdims=True)
        ix = jnp.where(work == mx, iota_e, float(N_EXPERTS)).min(axis=0, keepdims=True)
        hot = iota_e == ix
        vals.append(mx)
        idxs.append(ix)
        hots.append(hot)
        work = jnp.where(hot, -jnp.inf, work)
    es = [jnp.exp(v - vals[0]) for v in vals]
    den = es[0] + es[1] + es[2] + es[3]
    cnt = (hots[0] | hots[1] | hots[2] | hots[3]).astype(F32)
    before = jnp.dot(cnt.astype(BF16), tri_ref[...], preferred_element_type=F32) + base_sc[...]
    for k in range(TOP_K):
        idx_ref[k:k + 1, :] = idxs[k].astype(jnp.int32)
        gate_ref[k:k + 1, :] = es[k] / den
        rank_ref[k:k + 1, :] = jnp.where(hots[k], before, 0.0).sum(axis=0, keepdims=True).astype(jnp.int32)
    base_sc[...] = base_sc[...] + cnt.sum(axis=1, keepdims=True)
    cnt_ref[...] = jnp.broadcast_to(base_sc[...], cnt_ref.shape).astype(jnp.int32)


def _router(xn2d, wr_t, br, tri, *, tm):
    r = xn2d.shape[0] // X_SLABS
    col = lambda i: (0, i)
    const = lambda i: (0, 0)
    return pl.pallas_call(
        _router_kernel,
        grid=(r // tm,),
        in_specs=[pl.BlockSpec((tm * X_SLABS, LANES), lambda i: (i, 0)),
                  pl.BlockSpec((N_EXPERTS, D_MODEL), const),
                  pl.BlockSpec((N_EXPERTS, 1), const),
                  pl.BlockSpec((tm, tm), const)],
        out_specs=[pl.BlockSpec((TOP_K, tm), col)] * 3 + [pl.BlockSpec((N_EXPERTS, LANES), const)],
        out_shape=[jax.ShapeDtypeStruct((TOP_K, r), jnp.int32),
                   jax.ShapeDtypeStruct((TOP_K, r), F32),
                   jax.ShapeDtypeStruct((TOP_K, r), jnp.int32),
                   jax.ShapeDtypeStruct((N_EXPERTS, LANES), jnp.int32)],
        scratch_shapes=[pltpu.VMEM((N_EXPERTS, 1), F32)],
        compiler_params=_cparams(("arbitrary",)),
        name="router",
    )(xn2d, wr_t, br, tri)


def _row_copy(src, dst, sem):
    return pltpu.make_async_copy(src, dst, sem)


def _slab(ref, row, n, lead=()):
    return ref.at[lead + (pl.ds(pl.multiple_of(row * n, n), n), slice(None))]


def _scatter_kernel(pos_ref, xn_ref, xs_ref, sem):
    tm = xn_ref.shape[0] // X_SLABS

    def start(tt, c):
        for u in range(POS_GROUP):
            for k in range(TOP_K):
                _row_copy(_slab(xn_ref, tt * POS_GROUP + u, X_SLABS),
                          _slab(xs_ref, pos_ref[0, tt, k * POS_GROUP + u], X_SLABS),
                          sem).start(priority=k % 2)
        return c

    lax.fori_loop(0, tm // POS_GROUP, start, 0)

    def wait(t, c):
        for k in range(TOP_K):
            _row_copy(_slab(xn_ref, 0, X_SLABS), _slab(xs_ref, 0, X_SLABS), sem).wait()
        return c

    lax.fori_loop(0, tm, wait, 0, unroll=4)


def _scatter(pos, xn2d, *, tm):
    r = xn2d.shape[0] // X_SLABS
    return pl.pallas_call(
        _scatter_kernel,
        grid=(r // tm,),
        in_specs=[pl.BlockSpec((1, tm // POS_GROUP, TOP_K * POS_GROUP), lambda i: (i, 0, 0),
                               memory_space=pltpu.SMEM),
                  pl.BlockSpec((tm * X_SLABS, LANES), lambda i: (i, 0))],
        out_specs=pl.BlockSpec(memory_space=pl.ANY),
        out_shape=jax.ShapeDtypeStruct((TOP_K * r * X_SLABS, LANES), jnp.uint32),
        scratch_shapes=[pltpu.SemaphoreType.DMA],
        compiler_params=_cparams(("arbitrary",)),
        name="moe_scatter",
    )(pos, xn2d)


def _experts_kernel(blk_ref, exp_ref, lo_ref, hi_ref, first_ref, newexp_ref,
                    x_ref, wgu_ref, bgu_ref, wd_ref, bd_ref, y_ref, wgu_bf, wd_bf):
    i = pl.program_id(0)

    @pl.when(newexp_ref[i] == 1)
    def _():
        wgu_bf[...] = wgu_ref[0].astype(BF16)
        wd_bf[...] = wd_ref[0].astype(BF16)

    x = _unpack_bf16_pairs(_load_row_slabs(x_ref, MOE_ROWS, X_SLABS))
    gu = jnp.dot(x, wgu_bf[...], preferred_element_type=F32) + bgu_ref[0]
    gate = jnp.minimum(gu[:, :EXPERT_FF], SWIGLU_LIMIT)
    up = jnp.clip(gu[:, EXPERT_FF:], -SWIGLU_LIMIT, SWIGLU_LIMIT)
    act = (up + 1.0) * (gate * jax.nn.sigmoid(SWIGLU_ALPHA * gate))
    y = jnp.dot(act.astype(BF16), wd_bf[...], preferred_element_type=F32) + bd_ref[0]
    row0 = blk_ref[i] * MOE_ROWS
    whole = (lo_ref[i] <= row0) & (hi_ref[i] >= row0 + MOE_ROWS)

    @pl.when(whole)
    def _():
        _store_row_slabs(y_ref, y)

    @pl.when(jnp.logical_not(whole))
    def _():
        @pl.when(first_ref[i] == 1)
        def _():
            y_ref[...] = jnp.zeros(y_ref.shape, F32)

        rows = row0 + lax.broadcasted_iota(jnp.int32, (MOE_ROWS, 1), 0)
        mine = (rows >= lo_ref[i]) & (rows < hi_ref[i])
        _store_row_slabs(y_ref, jnp.where(mine, y, _load_row_slabs(y_ref, MOE_ROWS, Y_SLABS)))


def _experts(items, xs, wgu, bgu, wd, bd):
    blk, exp, lo, hi, first = items
    newexp = jnp.concatenate([jnp.ones((1,), jnp.int32), (exp[1:] != exp[:-1]).astype(jnp.int32)])
    n_items = blk.shape[0]
    n_rows = xs.shape[0] // X_SLABS
    return pl.pallas_call(
        _experts_kernel,
        grid_spec=pltpu.PrefetchScalarGridSpec(
            num_scalar_prefetch=6,
            grid=(n_items,),
            in_specs=[pl.BlockSpec((MOE_ROWS * X_SLABS, LANES), lambda i, b, e, *_: (b[i], 0)),
                      pl.BlockSpec((1, D_MODEL, 2 * EXPERT_FF), lambda i, b, e, *_: (e[i], 0, 0)),
                      pl.BlockSpec((1, 1, 2 * EXPERT_FF), lambda i, b, e, *_: (e[i], 0, 0)),
                      pl.BlockSpec((1, EXPERT_FF, D_MODEL), lambda i, b, e, *_: (e[i], 0, 0)),
                      pl.BlockSpec((1, 1, D_MODEL), lambda i, b, e, *_: (e[i], 0, 0))],
            out_specs=pl.BlockSpec((MOE_ROWS * Y_SLABS, LANES), lambda i, b, e, *_: (b[i], 0)),
            scratch_shapes=[pltpu.VMEM((D_MODEL, 2 * EXPERT_FF), BF16), pltpu.VMEM((EXPERT_FF, D_MODEL), BF16)]),
        out_shape=jax.ShapeDtypeStruct((n_rows * Y_SLABS, LANES), F32),
        compiler_params=pltpu.CompilerParams(dimension_semantics=("arbitrary",),
                                             vmem_limit_bytes=EXPERTS_VMEM_LIMIT),
        name="moe_experts",
    )(blk, exp, lo, hi, first, newexp, xs, wgu, bgu, wd, bd)


def _work_items(counts, n_rows):
    n_blocks = n_rows // MOE_ROWS
    n_items = n_blocks + N_EXPERTS - 1
    ends = jnp.cumsum(counts)
    starts = ends - counts
    nb = jnp.where(counts > 0, (ends - 1) // MOE_ROWS - starts // MOE_ROWS + 1, 0)
    item_end = jnp.cumsum(nb)
    item_start = item_end - nb
    total = item_end[-1]
    i = jnp.arange(n_items, dtype=jnp.int32)
    e = jnp.minimum(jnp.sum(item_end[None, :] <= i[:, None], axis=1), N_EXPERTS - 1).astype(jnp.int32)
    blk = (starts[e] // MOE_ROWS + (i - item_start[e])).astype(jnp.int32)
    valid = i < total
    lo = jnp.maximum(starts[e], blk * MOE_ROWS)
    hi = jnp.minimum(ends[e], (blk + 1) * MOE_ROWS)
    last = jnp.maximum(total - 1, 0)
    blk = jnp.where(valid, blk, blk[last])
    e = jnp.where(valid, e, e[last])
    lo = jnp.where(valid, lo, 0).astype(jnp.int32)
    hi = jnp.where(valid, hi, 0).astype(jnp.int32)
    prev = jnp.concatenate([jnp.full((1,), -1, jnp.int32), blk[:-1]])
    first = (blk != prev).astype(jnp.int32)
    return blk, e, lo, hi, first, starts


def _combine_kernel(pos_ref, pos_next_ref, h_ref, gate_ref, g_ref, ys_ref, o_first_ref, o_rest_ref, buf, sem,
                    *, n_first):
    tm = h_ref.shape[0]
    i = pl.program_id(0)
    n = pl.num_programs(0)
    cur = i % 2

    def fetch(p_ref, slot):
        def start(tt, c):
            for u in range(POS_GROUP):
                for k in range(TOP_K):
                    _row_copy(_slab(ys_ref, p_ref[0, tt, k * POS_GROUP + u], Y_SLABS),
                              buf.at[slot, k, pl.ds((tt * POS_GROUP + u) * BUF_PITCH, Y_SLABS), :],
                              sem.at[slot]).start(priority=k % 2)
            return c

        lax.fori_loop(0, tm // POS_GROUP, start, 0)

    @pl.when(i == 0)
    def _():
        fetch(pos_ref, 0)

    for slot in range(2):
        @pl.when((i + 1 < n) & (cur != slot))
        def _():
            fetch(pos_next_ref, slot)

    def wait(t, c):
        for k in range(TOP_K):
            _row_copy(_slab(ys_ref, 0, Y_SLABS), _slab(buf, 0, Y_SLABS, (cur, 0)), sem.at[cur]).wait()
        return c

    lax.fori_loop(0, tm, wait, 0, unroll=4)
    acc = h_ref[...]
    for k in range(TOP_K):
        rows = jnp.concatenate([buf[cur, k, pl.ds(c, tm, stride=BUF_PITCH), :] for c in range(Y_SLABS)], axis=1)
        acc = acc + gate_ref[:, k:k + 1] * rows
    res = _rms(acc, g_ref[...])

    @pl.when(i < n_first)
    def _():
        o_first_ref[...] = res

    @pl.when(i >= n_first)
    def _():
        o_rest_ref[...] = res


def _combine(pos, h2d, gates_t, g_final, ys, *, tm, rows_first):
    r = h2d.shape[0]
    n_tiles = r // tm
    n_first = rows_first // tm
    row = lambda i: (i, 0)
    return pl.pallas_call(
        functools.partial(_combine_kernel, n_first=n_first),
        grid=(n_tiles,),
        in_specs=[pl.BlockSpec((1, tm // POS_GROUP, TOP_K * POS_GROUP), lambda i: (i, 0, 0),
                               memory_space=pltpu.SMEM),
                  pl.BlockSpec((1, tm // POS_GROUP, TOP_K * POS_GROUP),
                               lambda i: (jnp.minimum(i + 1, n_tiles - 1), 0, 0), memory_space=pltpu.SMEM),
                  pl.BlockSpec((tm, D_MODEL), row),
                  pl.BlockSpec((tm, TOP_K), row),
                  pl.BlockSpec((1, D_MODEL), lambda i: (0, 0)),
                  pl.BlockSpec(memory_space=pl.ANY)],
        out_specs=[pl.BlockSpec((tm, D_MODEL), lambda i: (jnp.minimum(i, n_first - 1), 0)),
                   pl.BlockSpec((tm, D_MODEL), lambda i: (jnp.maximum(i - n_first, 0), 0))],
        out_shape=[jax.ShapeDtypeStruct((rows_first, D_MODEL), F32),
                   jax.ShapeDtypeStruct((r - rows_first, D_MODEL), F32)],
        scratch_shapes=[pltpu.VMEM((2, TOP_K, tm * BUF_PITCH, LANES), F32), pltpu.SemaphoreType.DMA((2,))],
        compiler_params=_cparams(("arbitrary",)),
        name="moe_combine",
    )(pos, pos, h2d, gates_t, g_final, ys)


def _moe(h2d, xn2d, w, rows_first):
    r = h2d.shape[0]
    tm_router = _pick_tile(r, 512)
    tm_scatter = _pick_tile(r, 512)
    tm_combine = _pick_tile(r, 256)
    idx, gates, rank, cnt = _router(xn2d, w["wr_t"], w["br"], w["tri"][:tm_router, :tm_router], tm=tm_router)
    counts = cnt[:, 0]
    blk, e, lo, hi, first, starts = _work_items(counts, TOP_K * r)
    hot = idx[:, :, None] == jnp.arange(N_EXPERTS, dtype=jnp.int32)
    pos = jnp.sum(jnp.where(hot, starts.astype(jnp.int32), 0), axis=-1) + rank
    tiles = lambda tm: (pos.reshape(TOP_K, -1, tm // POS_GROUP, POS_GROUP).transpose(1, 2, 0, 3)
                        .reshape(-1, tm // POS_GROUP, TOP_K * POS_GROUP))
    xs = _scatter(tiles(tm_scatter), xn2d, tm=tm_scatter)
    ys = _experts((blk, e, lo, hi, first), xs, w["wgu"], w["bgu"], w["wd"], w["bd"])
    return _combine(tiles(tm_combine), h2d, gates.T, w["g_final"], ys, tm=tm_combine, rows_first=rows_first)


def _pick_tile(n, cap):
    t = cap
    while n % t:
        t //= 2
    return t


def kernel(x_prompt, x_sample, mem_prompt, cache_a_k, cache_a_v, cache_b_k, cache_b_v, cache_mem_k, cache_mem_v, g_mix, w_in, rel_bias, lam_q1, lam_k1, lam_q2, lam_k2, g_subln, w_out, g_xq, g_xmem, w_xq, w_xk, w_xv, w_xo, g_moe, w_router, b_router, w_gate_up, b_gate_up, w_down, b_down, g_final):
    assert g_mix.shape[0] == 1, "single-layer kernel"
    b, s, d = x_prompt.shape
    db, dl, _ = x_sample.shape
    past = cache_b_k.shape[2]
    a_len = cache_a_k.shape[2]
    mlen = mem_prompt.shape[1]
    keep_a = min(BAND, s)
    assert d == D_MODEL and s % B_QBLK == 0 and s >= A_WIN and dl == CHUNK and past % CHUNK == 0
    assert B_QBLK == 2 * B_KBLK

    row1 = lambda v: v.reshape(1, -1).astype(F32)
    w_in_bf = w_in[0].astype(BF16)
    w_out_bf = w_out[0].astype(BF16)
    weights = dict(
        wr_t=w_router[0].T.astype(BF16), br=b_router[0].reshape(N_EXPERTS, 1).astype(F32),
        tri=jnp.asarray(np.triu(np.ones((512, 512), np.float32), 1), BF16),
        wgu=w_gate_up[0].astype(F32), bgu=b_gate_up[0].reshape(N_EXPERTS, 1, -1).astype(F32),
        wd=w_down[0].astype(F32), bd=b_down[0].reshape(N_EXPERTS, 1, -1).astype(F32),
        g_final=row1(g_final))
    lam = (jnp.exp(jnp.sum(lam_q1[0].astype(F32) * lam_k1[0].astype(F32)))
           - jnp.exp(jnp.sum(lam_q2[0].astype(F32) * lam_k2[0].astype(F32))) + LAM_INIT).reshape(1)
    table = _band_table(rel_bias[0])
    g_sub = row1(g_subln[0])

    tm_p = _pick_tile(s, 512)
    tabs_p = _rope_tables(np.arange(s))
    p_in = _inproj(x_prompt.reshape(b * s, d), row1(g_mix[0]), w_in_bf, *tabs_p,
                   tm=tm_p, n_pos_blocks=s // tm_p, tail_div=s // tm_p if keep_a == tm_p else 1, emit_vt=True)
    tabs_s = _rope_tables(np.tile(past + np.arange(dl), db))
    s_in = _inproj(x_sample.reshape(db * dl, d), row1(g_mix[0]), w_in_bf, *tabs_s,
                   tm=db * dl, n_pos_blocks=1, tail_div=1, emit_vt=False)

    def split(outs, nb, nl):
        return [o.reshape(nb, nl, A_WIDTH) for o in outs[:6]]

    aq, ak, _, bq, bk, _ = split(p_in, b, s)
    avt = p_in[10].reshape((b, s // A_QBLK) + p_in[10].shape[1:])
    bvt = p_in[11].reshape((b, s // B_KBLK) + p_in[11].shape[1:])
    saq, sak, sav, sbq, sbk, sbv = split(s_in, db, dl)

    oa = _attn_a_prompt(aq, ak, avt, table.transpose(0, 2, 1))
    ob = _attn_b_prompt(lam, bq, bk, bvt, g_sub)
    soa = _attn_a_sample(saq, cache_a_k[0].reshape(db, a_len * N_HEADS_A, HEAD_DIM),
                         cache_a_v[0].reshape(db, a_len * N_HEADS_A, HEAD_DIM), sak, sav, table[:, :dl, :BAND + dl])
    sob = _attn_b_sample(lam, sbq, cache_b_k[0].reshape(db, past, B_WIDTH),
                         cache_b_v[0].reshape(db, past * N_HEADS_B, 2 * HEAD_DIM), sbk, sbv, g_sub)

    mkf, mvf, mkb, mvb = _memkv(mem_prompt.reshape(b * mlen, d), row1(g_xmem[0]),
                                w_xk[0].astype(BF16), w_xv[0].astype(BF16), tm=mlen)
    post_w = (w_out_bf[:A_WIDTH], w_out_bf[A_WIDTH:], row1(g_xq[0]), w_xq[0].astype(BF16))
    tm_post = _pick_tile(math.gcd(s, b * s + db * dl), 512)
    rows_p, rows_s = b * s, db * dl
    sob_t = sob.transpose(0, 2, 1).reshape(db, N_HEADS_B, 2 * HEAD_DIM, dl)
    h_xn = _post(x_prompt.reshape(rows_p, d), oa, ob,
                 *post_w, mkb.reshape(b, mlen, d), mvb.reshape(b, mlen, d),
                 w_xo[0].astype(BF16), row1(g_moe[0]), tm=tm_post, tiles_per_batch=s // tm_post,
                 total_rows=rows_p + rows_s)
    h_all, xn_all = _post(x_sample.reshape(rows_s, d), soa.transpose(0, 2, 1), sob_t,
                          *post_w, cache_mem_k[0].reshape(db, mlen, d).astype(BF16),
                          cache_mem_v[0].reshape(db, mlen, d).astype(BF16),
                          w_xo[0].astype(BF16), row1(g_moe[0]), tm=dl, tiles_per_batch=1,
                          total_rows=rows_p + rows_s, row_offset=rows_p, into=h_xn)

    y_prompt, y_sample = _moe(h_all, xn_all, weights, rows_p)
    y_prompt = y_prompt.reshape(b, s, d)
    y_sample = y_sample.reshape(db, dl, d)

    def heads(x2d, nb, nl, nh):
        return x2d.reshape(1, nb, nl, nh, -1)

    akf, avf, bkf, bvf = p_in[6:10]
    if keep_a != tm_p:
        akf = akf.reshape(b, s, A_WIDTH)[:, s - keep_a:]
        avf = avf.reshape(b, s, A_WIDTH)[:, s - keep_a:]
    sakf, savf, sbkf, sbvf = s_in[6:]
    return (y_prompt, y_sample,
            heads(akf, b, keep_a, N_HEADS_A), heads(avf, b, keep_a, N_HEADS_A),
            heads(bkf, b, s, 2 * N_HEADS_B), heads(bvf, b, s, N_HEADS_B),
            heads(mkf, b, mlen, N_HEADS_M), heads(mvf, b, mlen, N_HEADS_M),
            heads(sakf, db, dl, N_HEADS_A), heads(savf, db, dl, N_HEADS_A),
            heads(sbkf, db, dl, 2 * N_HEADS_B), heads(sbvf, db, dl, N_HEADS_B))
```

```python
import functools
import math

import numpy as np
import jax
import jax.numpy as jnp
from jax import lax
from jax.experimental import pallas as pl
from jax.experimental.pallas import tpu as pltpu

F32 = jnp.float32
BF16 = jnp.bfloat16

D_MODEL = 1024
CHUNK = 64
LEFT_CHUNKS = 8
BAND = LEFT_CHUNKS * CHUNK
HEAD_DIM = 64
A_WIDTH = 512
B_WIDTH = 512
N_HEADS_A = 8
N_HEADS_B = 4
REL_CLIP = 128
ROT_DIM = 16
ROPE_THETA = 500000.0
N_HEADS_M = 4
HEAD_DIM_M = 256
N_EXPERTS = 32
TOP_K = 4
EXPERT_FF = 1024
SWIGLU_LIMIT = 7.0
SWIGLU_ALPHA = 1.702
RMS_EPS = 1e-5
NEG_INF = -1e30
LAM_INIT = 0.8 - 0.6 * math.exp(-0.3 * 0)
LOG2E = math.log2(math.e)

LANES = 128
A_QBLK = 4 * CHUNK
A_WIN = BAND + A_QBLK
B_QBLK = 512
B_KBLK = 256
VT_ONES = 16
BUF_PITCH = 9
POS_GROUP = 16
MOE_ROWS = 512
X_SLABS = D_MODEL // 2 // LANES
Y_SLABS = D_MODEL // LANES
VMEM_LIMIT = 48 * 1024 * 1024
EXPERTS_VMEM_LIMIT = 56 * 1024 * 1024


def _cparams(sem):
    return pltpu.CompilerParams(dimension_semantics=sem, vmem_limit_bytes=VMEM_LIMIT)


def _rms(x, g):
    return x * lax.rsqrt(jnp.mean(x * x, axis=-1, keepdims=True) + RMS_EPS) * g


HI16 = 0xFFFF0000


def _pack_bf16_pairs(x):
    n = x.shape[1] // 2
    bits = lambda v: lax.bitcast_convert_type(v.astype(BF16).astype(F32), jnp.uint32)
    return (bits(x[:, :n]) >> 16) | (bits(x[:, n:]) & jnp.uint32(HI16))


def _unpack_bf16_pairs(w):
    lo = lax.bitcast_convert_type(w << 16, F32)
    hi = lax.bitcast_convert_type(w & jnp.uint32(HI16), F32)
    return jnp.concatenate([lo, hi], axis=1).astype(BF16)


def _store_row_slabs(ref, x, lead=()):
    m, n = x.shape[0], x.shape[1] // LANES
    for c in range(n):
        ref[lead + (pl.ds(c, m, stride=n), slice(None))] = x[:, c * LANES:(c + 1) * LANES]


def _load_row_slabs(ref, m, n, lead=()):
    return jnp.concatenate([ref[lead + (pl.ds(c, m, stride=n), slice(None))] for c in range(n)], axis=1)


def _dot_nt(a, b):
    return lax.dot_general(a, b, (((1,), (1,)), ((), ())), preferred_element_type=F32)


def _store_heads64(ref, x):
    m, n_heads = x.shape[0], x.shape[1] // HEAD_DIM
    for c in range(x.shape[1] // LANES):
        grp = x[:, c * LANES:(c + 1) * LANES]
        ref[pl.ds(2 * c, m, stride=n_heads), :] = grp[:, :HEAD_DIM]
        ref[pl.ds(2 * c + 1, m, stride=n_heads), :] = pltpu.roll(grp, HEAD_DIM, 1)[:, :HEAD_DIM]


def _inproj_kernel(x_ref, g_ref, w_ref, cos_ref, sa_ref, sb_ref,
                   aq_ref, ak_ref, av_ref, bq_ref, bk_ref, bv_ref,
                   akf_ref, avf_ref, bkf_ref, bvf_ref, *maybe_vt_refs, tail_div):
    xn = _rms(x_ref[...], g_ref[...]).astype(BF16)
    is_tail = pl.program_id(0) % tail_div == tail_div - 1

    def seg(s):
        return jnp.dot(xn, w_ref[:, s * A_WIDTH:(s + 1) * A_WIDTH], preferred_element_type=F32)

    def rope(z):
        cols = []
        for c in range(z.shape[1] // LANES):
            zc = z[:, c * LANES:(c + 1) * LANES]
            cols.append(zc * cos_ref[...] + pltpu.roll(zc, 8, 1) * sa_ref[...]
                        + pltpu.roll(zc, LANES - 8, 1) * sb_ref[...])
        return jnp.concatenate(cols, axis=1)

    scale = HEAD_DIM ** -0.5
    aq_ref[...] = (seg(0) * (scale * LOG2E)).astype(BF16)
    ak = seg(1)
    ak_ref[...] = ak.astype(BF16)
    av = seg(2)
    av_ref[...] = av.astype(BF16)
    bq_ref[...] = (rope(seg(3)) * (scale * LOG2E)).astype(BF16)
    bk = rope(seg(4))
    bk_ref[...] = bk.astype(BF16)
    _store_heads64(bkf_ref, bk)
    bv = seg(5)
    bv_ref[...] = bv.astype(BF16)
    _store_row_slabs(bvf_ref, bv)
    for vt_ref, v in zip(maybe_vt_refs, (av, bv)):
        vt = v.T.astype(BF16)
        blk = vt_ref.shape[3]
        for c in range(vt_ref.shape[0]):
            for g in range(vt_ref.shape[1]):
                vt_ref[c, g, 0:LANES, :] = vt[g * LANES:(g + 1) * LANES, c * blk:(c + 1) * blk]
                vt_ref[c, g, LANES:LANES + VT_ONES, :] = jnp.ones((VT_ONES, blk), BF16)

    @pl.when(is_tail)
    def _():
        _store_heads64(akf_ref, ak)
        _store_heads64(avf_ref, av)


def _inproj(x2d, g, w_bf, cos_t, sa_t, sb_t, *, tm, n_pos_blocks, tail_div, emit_vt):
    r = x2d.shape[0]
    n_tiles = r // tm
    row = lambda i: (i, 0)
    const = lambda i: (0, 0)
    pos = lambda i: (i % n_pos_blocks, 0)
    tail = lambda i: (i // tail_div, 0)
    bf = jax.ShapeDtypeStruct((r, A_WIDTH), BF16)
    blk = lambda im: pl.BlockSpec((tm, A_WIDTH), im)
    h64 = lambda im: pl.BlockSpec((tm * N_HEADS_A, HEAD_DIM), im)
    f_tail = jax.ShapeDtypeStruct((r // tail_div * N_HEADS_A, HEAD_DIM), F32)
    out_specs = [blk(row)] * 6 + [h64(tail), h64(tail), h64(row),
                                  pl.BlockSpec((tm * N_HEADS_B, 2 * HEAD_DIM), row)]
    out_shape = [bf] * 6 + [f_tail, f_tail,
                            jax.ShapeDtypeStruct((r * 2 * N_HEADS_B, HEAD_DIM), F32),
                            jax.ShapeDtypeStruct((r * N_HEADS_B, 2 * HEAD_DIM), F32)]
    if emit_vt:
        for kb in (A_QBLK, B_KBLK):
            grp = (A_WIDTH // LANES, LANES + VT_ONES, kb)
            out_specs.append(pl.BlockSpec((tm // kb,) + grp, lambda i: (i, 0, 0, 0)))
            out_shape.append(jax.ShapeDtypeStruct((r // kb,) + grp, BF16))
    return pl.pallas_call(
        functools.partial(_inproj_kernel, tail_div=tail_div),
        grid=(n_tiles,),
        in_specs=[pl.BlockSpec((tm, D_MODEL), row), pl.BlockSpec((1, D_MODEL), const),
                  pl.BlockSpec((D_MODEL, 3 * D_MODEL), const),
                  pl.BlockSpec((tm, LANES), pos), pl.BlockSpec((tm, LANES), pos),
                  pl.BlockSpec((tm, LANES), pos)],
        out_specs=out_specs,
        out_shape=out_shape,
        compiler_params=_cparams(("arbitrary",)),
        name="inproj",
    )(x2d, g, w_bf, cos_t, sa_t, sb_t)


def _rope_tables(positions):
    half = ROT_DIM // 2
    lane = np.arange(LANES) % HEAD_DIM
    inv_freq = np.power(ROPE_THETA, -np.arange(half, dtype=np.float64) / half)
    ang = np.asarray(positions, np.float64)[:, None] * inv_freq[None, :]
    cos = np.cos(ang)[:, lane % half]
    sin = np.sin(ang)[:, lane % half]
    in_rot = (lane < ROT_DIM)[None, :]
    first = (lane < half)[None, :]
    cos_t = np.where(in_rot, cos, 1.0)
    sa_t = np.where(in_rot & ~first, sin, 0.0)
    sb_t = np.where(first, -sin, 0.0)
    return tuple(jnp.asarray(t, F32) for t in (cos_t, sa_t, sb_t))


def _band_table(rel_bias):
    r = np.arange(A_QBLK)[:, None]
    j = np.arange(BAND + A_WIN)[None, :]
    dchunk = LEFT_CHUNKS + r // CHUNK - j // CHUNK
    allowed = (dchunk >= 0) & (dchunk <= LEFT_CHUNKS)
    bias = rel_bias.astype(F32) * LOG2E
    n_seq = BAND + A_WIN + A_QBLK - 1
    n_hi = BAND + A_QBLK - 1 - REL_CLIP
    seq = jnp.concatenate([jnp.broadcast_to(bias[:, -1:], (N_HEADS_A, n_hi)), bias[:, ::-1],
                           jnp.broadcast_to(bias[:, :1], (N_HEADS_A, n_seq - n_hi - bias.shape[1]))], axis=1)
    seq = jnp.roll(seq, -(A_QBLK - 1), axis=1)
    toe = jnp.tile(seq, (1, A_QBLK))[:, :A_QBLK * (n_seq - 1)].reshape(N_HEADS_A, A_QBLK, n_seq - 1)
    return jnp.where(jnp.asarray(allowed)[None], toe[:, :, :BAND + A_WIN], NEG_INF)


def _softmax_pv(s_list, v_list, exp_fn=jnp.exp):
    m = s_list[0].max(axis=-1, keepdims=True)
    for s in s_list[1:]:
        m = jnp.maximum(m, s.max(axis=-1, keepdims=True))
    l = None
    o = None
    for s, v in zip(s_list, v_list):
        p = exp_fn(s - m)
        ls = p.sum(axis=-1, keepdims=True)
        os_ = jnp.dot(p.astype(BF16), v, preferred_element_type=F32)
        l = ls if l is None else l + ls
        o = os_ if o is None else o + os_
    return o / l


def _attn_a_prompt_kernel(q_ref, k_ref, vt_ref, tab_ref, o_ref):
    i = pl.program_id(1)
    n_left = BAND // A_QBLK
    n_sub = A_WIN // A_QBLK
    blk0 = jnp.maximum(i - n_left, 0)
    off = pl.multiple_of(jnp.maximum(n_left - i, 0) * A_QBLK, A_QBLK)
    q = q_ref[0]
    lane = lax.broadcasted_iota(jnp.int32, (A_QBLK, LANES), 1)

    def scores(h):
        cols = slice(h // 2 * LANES, (h // 2 + 1) * LANES)
        qp = q[:, cols]
        qh = jnp.where((lane >= HEAD_DIM) == bool(h % 2), qp, jnp.zeros_like(qp))
        return [_dot_nt(k_ref[0, pl.ds(pl.multiple_of((blk0 + c) * A_QBLK, A_QBLK), A_QBLK), cols], qh)
                + tab_ref[h, pl.ds(off + c * A_QBLK, A_QBLK), :] for c in range(n_sub)]

    def attend(h, ss):
        m = _col_reduce(ss[0], jnp.maximum, jnp.max)
        for s in ss[1:]:
            m = jnp.maximum(m, _col_reduce(s, jnp.maximum, jnp.max))
        o_t = None
        for c in range(n_sub):
            p = jnp.exp2(ss[c] - m).astype(BF16)
            os_ = jnp.dot(vt_ref[0, blk0 + c, h // 2], p, preferred_element_type=F32)
            o_t = os_ if o_t is None else o_t + os_
        return o_t[h % 2 * HEAD_DIM:(h % 2 + 1) * HEAD_DIM] / o_t[LANES:LANES + 1]

    halves = []
    ss = scores(0)
    for h in range(N_HEADS_A):
        ss_next = scores(h + 1) if h + 1 < N_HEADS_A else None
        halves.append(attend(h, ss))
        ss = ss_next
    o_ref[0] = jnp.concatenate(halves, axis=0).astype(BF16)


def _attn_a_prompt(aq, ak, avt, table_t):
    b, l, _ = aq.shape
    n_kb = avt.shape[1]
    return pl.pallas_call(
        _attn_a_prompt_kernel,
        grid=(b, l // A_QBLK),
        in_specs=[pl.BlockSpec((1, A_QBLK, A_WIDTH), lambda bb, i: (bb, i, 0)),
                  pl.BlockSpec((1, l, A_WIDTH), lambda bb, i: (bb, 0, 0)),
                  pl.BlockSpec((1,) + avt.shape[1:], lambda bb, i: (bb, 0, 0, 0, 0)),
                  pl.BlockSpec(table_t.shape, lambda bb, i: (0, 0, 0))],
        out_specs=pl.BlockSpec((1, A_WIDTH, A_QBLK), lambda bb, i: (bb, 0, i)),
        out_shape=jax.ShapeDtypeStruct((b, A_WIDTH, l), BF16),
        compiler_params=_cparams(("arbitrary", "arbitrary")),
        name="attn_a_prompt",
    )(aq, ak, avt, table_t)


def _attn_a_sample_kernel(q_ref, kc_ref, vc_ref, k_ref, v_ref, tab_ref, o_ref):
    q = q_ref[0]
    ko = k_ref[0]
    vo = v_ref[0]
    a_len = kc_ref.shape[1] // N_HEADS_A
    lq = q.shape[0]
    c0 = BAND - a_len
    outs = []
    for h in range(N_HEADS_A):
        sl = slice(h * HEAD_DIM, (h + 1) * HEAD_DIM)
        kc = kc_ref[0, pl.ds(h, a_len, stride=N_HEADS_A), :].astype(BF16)
        vc = vc_ref[0, pl.ds(h, a_len, stride=N_HEADS_A), :].astype(BF16)
        s_c = _dot_nt(q[:, sl], kc) + tab_ref[h, :, c0:BAND]
        s_o = _dot_nt(q[:, sl], ko[:, sl]) + tab_ref[h, :, BAND:BAND + lq]
        outs.append(_softmax_pv([s_c, s_o], [vc, vo[:, sl]], exp_fn=jnp.exp2))
    o_ref[0] = jnp.concatenate(outs, axis=1).astype(BF16)


def _attn_a_sample(aq, cache_k, cache_v, ak, av, table):
    b, l, _ = aq.shape
    bs = lambda n: pl.BlockSpec((1, n, A_WIDTH), lambda bb: (bb, 0, 0))
    cache = pl.BlockSpec((1,) + cache_k.shape[1:], lambda bb: (bb, 0, 0))
    return pl.pallas_call(
        _attn_a_sample_kernel,
        grid=(b,),
        in_specs=[bs(l), cache, cache, bs(l), bs(l),
                  pl.BlockSpec(table.shape, lambda bb: (0, 0, 0))],
        out_specs=bs(l),
        out_shape=jax.ShapeDtypeStruct((b, l, A_WIDTH), BF16),
        compiler_params=_cparams(("arbitrary",)),
        name="attn_a_sample",
    )(aq, cache_k, cache_v, ak, av, table)


def _subnorm(o, g):
    return _rms(o, g) * (1.0 - LAM_INIT)


def _col_reduce(x, op, reduce_fn):
    n = x.shape[0]
    while n > 8:
        n //= 2
        x = op(x[:n], x[n:])
    return reduce_fn(x, axis=0, keepdims=True)


def _attn_b_prompt_kernel(lam_ref, q_ref, qn_ref, k_ref, vt_ref, mask_ref, g_ref, o_ref,
                          sa_sc, sb_sc, m_sc, acc_sc):
    i = pl.program_id(2)
    tk = vt_ref.shape[4]
    lane = lax.broadcasted_iota(jnp.int32, q_ref.shape[1:], 1)

    def split_maps(q):
        zero = jnp.zeros_like(q)
        return jnp.where(lane < HEAD_DIM, q, zero), jnp.where(lane >= HEAD_DIM, q, zero)

    qs = split_maps(q_ref[0])
    m_sc[...] = jnp.full(m_sc.shape, NEG_INF, F32)
    acc_sc[...] = jnp.zeros(acc_sc.shape, F32)

    def scores(j, s_sc, qmaps=qs):
        kb = k_ref[0, pl.ds(pl.multiple_of(j * tk, tk), tk), :]
        for a in range(2):
            s_sc[a] = _dot_nt(kb, qmaps[a])

    def consume(j, s_sc, mask):
        vt = vt_ref[0, j, 0]
        for a in range(2):
            s = s_sc[a]
            if mask is not None:
                s = s + mask
            m_old = m_sc[a]
            m_new = jnp.maximum(m_old, _col_reduce(s, jnp.maximum, jnp.max))
            alpha = jnp.exp2(m_old - m_new)
            p = jnp.exp2(s - m_new).astype(BF16)
            m_sc[a] = m_new
            acc_sc[a] = alpha * acc_sc[a] + jnp.dot(vt, p, preferred_element_type=F32)

    @pl.when(i == 0)
    def _():
        scores(0, sa_sc)

    def pair(j):
        scores(j + 1, sb_sc)
        consume(j, sa_sc, None)
        scores(j + 2, sa_sc)
        consume(j + 1, sb_sc, None)

    def body(jj, c):
        for r in range(4):
            pair(8 * jj + 2 * r)
        return c

    lax.fori_loop(0, i // 4, body, 0)
    done = (i // 4) * 8
    for rem in range(1, 4):
        @pl.when(i % 4 == rem)
        def _():
            for r in range(rem):
                pair(done + 2 * r)
    scores(2 * i + 1, sb_sc)
    consume(2 * i, sa_sc, mask_ref[0])
    scores(0, sa_sc, split_maps(qn_ref[0]))
    consume(2 * i + 1, sb_sc, mask_ref[1])
    o_t = (acc_sc[0, 0:LANES] / acc_sc[0, LANES:LANES + 1]
           - lam_ref[0] * (acc_sc[1, 0:LANES] / acc_sc[1, LANES:LANES + 1]))
    inv = lax.rsqrt(jnp.mean(o_t * o_t, axis=0, keepdims=True) + RMS_EPS)
    o_ref[0, 0] = (o_t * inv * (g_ref[...] * (1.0 - LAM_INIT))).astype(BF16)


def _chunk_causal_masks_t(tk, tq):
    kc = (np.arange(tq // tk)[:, None, None] * tk + np.arange(tk)[None, :, None]) // CHUNK
    qc = np.arange(tq)[None, None, :] // CHUNK
    return jnp.asarray(np.where(kc <= qc, 0.0, NEG_INF), F32)


def _attn_b_prompt(lam, bq, bk, bvt, g_sub):
    b, l, _ = bq.shape
    tq = B_QBLK
    tk = B_KBLK
    hw = 2 * HEAD_DIM
    n_kb = bvt.shape[1]
    return pl.pallas_call(
        _attn_b_prompt_kernel,
        grid_spec=pltpu.PrefetchScalarGridSpec(
            num_scalar_prefetch=1,
            grid=(b, N_HEADS_B, l // tq),
            in_specs=[pl.BlockSpec((1, tq, hw), lambda bb, h, i, lam_: (bb, i, h)),
                      pl.BlockSpec((1, tq, hw), lambda bb, h, i, lam_: (bb, jnp.minimum(i + 1, l // tq - 1), h)),
                      pl.BlockSpec((1, l, hw), lambda bb, h, i, lam_: (bb, 0, h)),
                      pl.BlockSpec((1, n_kb, 1, hw + VT_ONES, tk), lambda bb, h, i, lam_: (bb, 0, h, 0, 0)),
                      pl.BlockSpec((tq // tk, tk, tq), lambda bb, h, i, lam_: (0, 0, 0)),
                      pl.BlockSpec((hw, 1), lambda bb, h, i, lam_: (0, 0))],
            out_specs=pl.BlockSpec((1, 1, hw, tq), lambda bb, h, i, lam_: (bb, h, 0, i)),
            scratch_shapes=[pltpu.VMEM((2, tk, tq), F32), pltpu.VMEM((2, tk, tq), F32),
                            pltpu.VMEM((2, 1, tq), F32), pltpu.VMEM((2, hw + VT_ONES, tq), F32)]),
        out_shape=jax.ShapeDtypeStruct((b, N_HEADS_B, hw, l), BF16),
        compiler_params=_cparams(("arbitrary", "arbitrary", "arbitrary")),
        name="attn_b_prompt",
    )(lam, bq, bq, bk, bvt, _chunk_causal_masks_t(tk, tq), g_sub.reshape(hw, 1))


def _attn_b_sample_kernel(lam_ref, q_ref, kc_ref, vc_ref, k_ref, v_ref, g_ref, o_ref):
    q = q_ref[0]
    kc = kc_ref[0].astype(BF16)
    n_past = vc_ref.shape[1] // N_HEADS_B
    vc = vc_ref[0, pl.ds(pl.program_id(1), n_past, stride=N_HEADS_B), :].astype(BF16)
    ko = k_ref[0]
    vo = v_ref[0]
    outs = []
    for a in range(2):
        sl = slice(a * HEAD_DIM, (a + 1) * HEAD_DIM)
        s_c = _dot_nt(q[:, sl], kc[:, sl])
        s_o = _dot_nt(q[:, sl], ko[:, sl])
        outs.append(_softmax_pv([s_c, s_o], [vc, vo], exp_fn=jnp.exp2))
    o = outs[0] - lam_ref[0] * outs[1]
    o_ref[0] = _subnorm(o, g_ref[...]).astype(BF16)


def _attn_b_sample(lam, bq, cache_k, cache_v, bk, bv, g_sub):
    b, l, _ = bq.shape
    p = cache_k.shape[1]
    hw = 2 * HEAD_DIM
    bs = lambda n: pl.BlockSpec((1, n, hw), lambda bb, h, lam_: (bb, 0, h))
    return pl.pallas_call(
        _attn_b_sample_kernel,
        grid_spec=pltpu.PrefetchScalarGridSpec(
            num_scalar_prefetch=1,
            grid=(b, N_HEADS_B),
            in_specs=[bs(l), bs(p),
                      pl.BlockSpec((1, p * N_HEADS_B, hw), lambda bb, h, lam_: (bb, 0, 0)),
                      bs(l), bs(l),
                      pl.BlockSpec((1, hw), lambda bb, h, lam_: (0, 0))],
            out_specs=bs(l)),
        out_shape=jax.ShapeDtypeStruct((b, l, B_WIDTH), BF16),
        compiler_params=_cparams(("arbitrary", "arbitrary")),
        name="attn_b_sample",
    )(lam, bq, cache_k, cache_v, bk, bv, g_sub)


def _memkv_kernel(m_ref, g_ref, wk_ref, wv_ref, kf_ref, vf_ref, kb_ref, vb_ref):
    mn = _rms(m_ref[...], g_ref[...]).astype(BF16)
    k = jnp.dot(mn, wk_ref[...], preferred_element_type=F32)
    v = jnp.dot(mn, wv_ref[...], preferred_element_type=F32)
    kf_ref[...] = k
    vf_ref[...] = v
    kb_ref[...] = k.astype(BF16)
    vb_ref[...] = v.astype(BF16)


def _memkv(mem2d, g, wk, wv, *, tm):
    r = mem2d.shape[0]
    row = lambda i: (i, 0)
    const = lambda i: (0, 0)
    blk = pl.BlockSpec((tm, D_MODEL), row)
    wspec = pl.BlockSpec((D_MODEL, D_MODEL), const)
    f = jax.ShapeDtypeStruct((r, D_MODEL), F32)
    bf = jax.ShapeDtypeStruct((r, D_MODEL), BF16)
    return pl.pallas_call(
        _memkv_kernel,
        grid=(r // tm,),
        in_specs=[blk, pl.BlockSpec((1, D_MODEL), const), wspec, wspec],
        out_specs=[blk] * 4,
        out_shape=[f, f, bf, bf],
        compiler_params=_cparams(("arbitrary",)),
        name="memkv",
    )(mem2d, g, wk, wv)


def _post_kernel(*refs, n_real):
    h_ref, xn_ref = refs[-2:]

    @pl.when(pl.program_id(0) < n_real)
    def _():
        _post_tile(*refs[:11], h_ref, xn_ref)

    @pl.when(pl.program_id(0) >= n_real)
    def _():
        h_ref[...] = jnp.zeros(h_ref.shape, h_ref.dtype)
        xn_ref[...] = jnp.zeros(xn_ref.shape, xn_ref.dtype)


def _post_tile(x_ref, oa_ref, ob_ref, woa_ref, wob_ref, gq_ref, wq_ref, mk_ref, mv_ref, wo_ref,
               gm_ref, h_ref, xn_ref):
    tn = lambda a_t, w: lax.dot_general(a_t, w, (((0,), (0,)), ((), ())), preferred_element_type=F32)
    ob_t = ob_ref[0].reshape(B_WIDTH, ob_ref.shape[3])
    h1 = x_ref[...] + tn(oa_ref[0], woa_ref[...]) + tn(ob_t, wob_ref[...])
    xq = _rms(h1, gq_ref[...]).astype(BF16)
    q = (jnp.dot(xq, wq_ref[...], preferred_element_type=F32) * HEAD_DIM_M ** -0.5).astype(BF16)
    outs = []
    for h in range(N_HEADS_M):
        sl = slice(h * HEAD_DIM_M, (h + 1) * HEAD_DIM_M)
        s = _dot_nt(q[:, sl], mk_ref[0, :, sl])
        outs.append(_softmax_pv([s], [mv_ref[0, :, sl]]).astype(BF16))
    o = jnp.concatenate(outs, axis=1)
    h2 = h1 + jnp.dot(o, wo_ref[...], preferred_element_type=F32)
    h_ref[...] = h2
    _store_row_slabs(xn_ref, _pack_bf16_pairs(_rms(h2, gm_ref[...])))


def _post(x2d, oa, ob, woa, wob, gq, wq, mk, mv, wo, gm, *, tm, tiles_per_batch, total_rows,
          row_offset=0, into=None):
    r = x2d.shape[0]
    n_real = r // tm
    n_steps = n_real if into is not None else total_rows // tm
    mlen = mk.shape[1]
    row = lambda i: (jnp.minimum(i, n_real - 1), 0)
    out_row = lambda i: (row_offset // tm + i, 0)
    const = lambda i: (0, 0)
    memb = lambda i: (jnp.minimum(i, n_real - 1) // tiles_per_batch, 0, 0)
    gspec = pl.BlockSpec((1, D_MODEL), const)
    wspec = pl.BlockSpec((D_MODEL, D_MODEL), const)
    hspec = pl.BlockSpec((A_WIDTH, D_MODEL), const)
    obt_blk = lambda i: (jnp.minimum(i, n_real - 1) // tiles_per_batch, 0, 0,
                         jnp.minimum(i, n_real - 1) % tiles_per_batch)
    oat_blk = lambda i: (jnp.minimum(i, n_real - 1) // tiles_per_batch, 0,
                         jnp.minimum(i, n_real - 1) % tiles_per_batch)
    in_specs = [pl.BlockSpec((tm, D_MODEL), row),
                pl.BlockSpec((1, A_WIDTH, tm), oat_blk),
                pl.BlockSpec((1, N_HEADS_B, 2 * HEAD_DIM, tm), obt_blk),
                hspec, hspec, gspec, wspec,
                pl.BlockSpec((1, mlen, D_MODEL), memb), pl.BlockSpec((1, mlen, D_MODEL), memb),
                wspec, gspec]
    args = [x2d, oa, ob, woa, wob, gq, wq, mk, mv, wo, gm]
    aliases = {}
    if into is not None:
        aliases = {len(args): 0, len(args) + 1: 1}
        in_specs += [pl.BlockSpec(memory_space=pl.ANY)] * 2
        args += list(into)
    return pl.pallas_call(
        functools.partial(_post_kernel, n_real=n_real),
        grid=(n_steps,),
        in_specs=in_specs,
        out_specs=[pl.BlockSpec((tm, D_MODEL), out_row), pl.BlockSpec((tm * X_SLABS, LANES), out_row)],
        out_shape=[jax.ShapeDtypeStruct((total_rows, D_MODEL), F32),
                   jax.ShapeDtypeStruct((total_rows * X_SLABS, LANES), jnp.uint32)],
        input_output_aliases=aliases,
        compiler_params=_cparams(("arbitrary",)),
        name="post_attn",
    )(*args)


def _router_kernel(xn_ref, wr_ref, br_ref, tri_ref, idx_ref, gate_ref, rank_ref, cnt_ref, base_sc):
    @pl.when(pl.program_id(0) == 0)
    def _():
        base_sc[...] = jnp.zeros(base_sc.shape, F32)

    tm = xn_ref.shape[0] // X_SLABS
    xn = _unpack_bf16_pairs(_load_row_slabs(xn_ref, tm, X_SLABS))
    logits = _dot_nt(wr_ref[...], xn) + br_ref[...]
    iota_e = lax.broadcasted_iota(jnp.int32, (N_EXPERTS, tm), 0).astype(F32)
    work = logits
    vals, idxs, hots = [], [], []
    for _k in range(TOP_K):
        mx = work.max(axis=0, keepdims=True)
        ix = jnp.where(work == mx, iota_e, float(N_EXPERTS)).min(axis=0, keepdims=True)
        hot = iota_e == ix
        vals.append(mx)
        idxs.append(ix)
        hots.append(hot)
        work = jnp.where(hot, -jnp.inf, work)
    es = [jnp.exp(v - vals[0]) for v in vals]
    den = es[0] + es[1] + es[2] + es[3]
    cnt = (hots[0] | hots[1] | hots[2] | hots[3]).astype(F32)
    before = jnp.dot(cnt.astype(BF16), tri_ref[...], preferred_element_type=F32) + base_sc[...]
    for k in range(TOP_K):
        idx_ref[k:k + 1, :] = idxs[k].astype(jnp.int32)
        gate_ref[k:k + 1, :] = es[k] / den
        rank_ref[k:k + 1, :] = jnp.where(hots[k], before, 0.0).sum(axis=0, keepdims=True).astype(jnp.int32)
    base_sc[...] = base_sc[...] + cnt.sum(axis=1, keepdims=True)
    cnt_ref[...] = jnp.broadcast_to(base_sc[...], cnt_ref.shape).astype(jnp.int32)


def _router(xn2d, wr_t, br, tri, *, tm):
    r = xn2d.shape[0] // X_SLABS
    col = lambda i: (0, i)
    const = lambda i: (0, 0)
    return pl.pallas_call(
        _router_kernel,
        grid=(r // tm,),
        in_specs=[pl.BlockSpec((tm * X_SLABS, LANES), lambda i: (i, 0)),
                  pl.BlockSpec((N_EXPERTS, D_MODEL), const),
                  pl.BlockSpec((N_EXPERTS, 1), const),
                  pl.BlockSpec((tm, tm), const)],
        out_specs=[pl.BlockSpec((TOP_K, tm), col)] * 3 + [pl.BlockSpec((N_EXPERTS, LANES), const)],
        out_shape=[jax.ShapeDtypeStruct((TOP_K, r), jnp.int32),
                   jax.ShapeDtypeStruct((TOP_K, r), F32),
                   jax.ShapeDtypeStruct((TOP_K, r), jnp.int32),
                   jax.ShapeDtypeStruct((N_EXPERTS, LANES), jnp.int32)],
        scratch_shapes=[pltpu.VMEM((N_EXPERTS, 1), F32)],
        compiler_params=_cparams(("arbitrary",)),
        name="router",
    )(xn2d, wr_t, br, tri)


def _row_copy(src, dst, sem):
    return pltpu.make_async_copy(src, dst, sem)


def _slab(ref, row, n, lead=()):
    return ref.at[lead + (pl.ds(pl.multiple_of(row * n, n), n), slice(None))]


def _scatter_kernel(pos_ref, xn_ref, xs_ref, sem):
    tm = xn_ref.shape[0] // X_SLABS

    def start(tt, c):
        for u in range(POS_GROUP):
            for k in range(TOP_K):
                _row_copy(_slab(xn_ref, tt * POS_GROUP + u, X_SLABS),
                          _slab(xs_ref, pos_ref[0, tt, k * POS_GROUP + u], X_SLABS),
                          sem).start(priority=k % 2)
        return c

    lax.fori_loop(0, tm // POS_GROUP, start, 0)

    def wait(t, c):
        for k in range(TOP_K):
            _row_copy(_slab(xn_ref, 0, X_SLABS), _slab(xs_ref, 0, X_SLABS), sem).wait()
        return c

    lax.fori_loop(0, tm, wait, 0, unroll=4)


def _scatter(pos, xn2d, *, tm):
    r = xn2d.shape[0] // X_SLABS
    return pl.pallas_call(
        _scatter_kernel,
        grid=(r // tm,),
        in_specs=[pl.BlockSpec((1, tm // POS_GROUP, TOP_K * POS_GROUP), lambda i: (i, 0, 0),
                               memory_space=pltpu.SMEM),
                  pl.BlockSpec((tm * X_SLABS, LANES), lambda i: (i, 0))],
        out_specs=pl.BlockSpec(memory_space=pl.ANY),
        out_shape=jax.ShapeDtypeStruct((TOP_K * r * X_SLABS, LANES), jnp.uint32),
        scratch_shapes=[pltpu.SemaphoreType.DMA],
        compiler_params=_cparams(("arbitrary",)),
        name="moe_scatter",
    )(pos, xn2d)


def _experts_kernel(blk_ref, exp_ref, lo_ref, hi_ref, first_ref, newexp_ref,
                    x_ref, wgu_ref, bgu_ref, wd_ref, bd_ref, y_ref, wgu_bf, wd_bf):
    i = pl.program_id(0)

    @pl.when(newexp_ref[i] == 1)
    def _():
        wgu_bf[...] = wgu_ref[0].astype(BF16)
        wd_bf[...] = wd_ref[0].astype(BF16)

    x = _unpack_bf16_pairs(_load_row_slabs(x_ref, MOE_ROWS, X_SLABS))
    gu = jnp.dot(x, wgu_bf[...], preferred_element_type=F32) + bgu_ref[0]
    gate = jnp.minimum(gu[:, :EXPERT_FF], SWIGLU_LIMIT)
    up = jnp.clip(gu[:, EXPERT_FF:], -SWIGLU_LIMIT, SWIGLU_LIMIT)
    act = (up + 1.0) * (gate * jax.nn.sigmoid(SWIGLU_ALPHA * gate))
    y = jnp.dot(act.astype(BF16), wd_bf[...], preferred_element_type=F32) + bd_ref[0]
    row0 = blk_ref[i] * MOE_ROWS
    whole = (lo_ref[i] <= row0) & (hi_ref[i] >= row0 + MOE_ROWS)

    @pl.when(whole)
    def _():
        _store_row_slabs(y_ref, y)

    @pl.when(jnp.logical_not(whole))
    def _():
        @pl.when(first_ref[i] == 1)
        def _():
            y_ref[...] = jnp.zeros(y_ref.shape, F32)

        rows = row0 + lax.broadcasted_iota(jnp.int32, (MOE_ROWS, 1), 0)
        mine = (rows >= lo_ref[i]) & (rows < hi_ref[i])
        _store_row_slabs(y_ref, jnp.where(mine, y, _load_row_slabs(y_ref, MOE_ROWS, Y_SLABS)))


def _experts(items, xs, wgu, bgu, wd, bd):
    blk, exp, lo, hi, first = items
    newexp = jnp.concatenate([jnp.ones((1,), jnp.int32), (exp[1:] != exp[:-1]).astype(jnp.int32)])
    n_items = blk.shape[0]
    n_rows = xs.shape[0] // X_SLABS
    return pl.pallas_call(
        _experts_kernel,
        grid_spec=pltpu.PrefetchScalarGridSpec(
            num_scalar_prefetch=6,
            grid=(n_items,),
            in_specs=[pl.BlockSpec((MOE_ROWS * X_SLABS, LANES), lambda i, b, e, *_: (b[i], 0)),
                      pl.BlockSpec((1, D_MODEL, 2 * EXPERT_FF), lambda i, b, e, *_: (e[i], 0, 0)),
                      pl.BlockSpec((1, 1, 2 * EXPERT_FF), lambda i, b, e, *_: (e[i], 0, 0)),
                      pl.BlockSpec((1, EXPERT_FF, D_MODEL), lambda i, b, e, *_: (e[i], 0, 0)),
                      pl.BlockSpec((1, 1, D_MODEL), lambda i, b, e, *_: (e[i], 0, 0))],
            out_specs=pl.BlockSpec((MOE_ROWS * Y_SLABS, LANES), lambda i, b, e, *_: (b[i], 0)),
            scratch_shapes=[pltpu.VMEM((D_MODEL, 2 * EXPERT_FF), BF16), pltpu.VMEM((EXPERT_FF, D_MODEL), BF16)]),
        out_shape=jax.ShapeDtypeStruct((n_rows * Y_SLABS, LANES), F32),
        compiler_params=pltpu.CompilerParams(dimension_semantics=("arbitrary",),
                                             vmem_limit_bytes=EXPERTS_VMEM_LIMIT),
        name="moe_experts",
    )(blk, exp, lo, hi, first, newexp, xs, wgu, bgu, wd, bd)


def _work_items(counts, n_rows):
    n_blocks = n_rows // MOE_ROWS
    n_items = n_blocks + N_EXPERTS - 1
    ends = jnp.cumsum(counts)
    starts = ends - counts
    nb = jnp.where(counts > 0, (ends - 1) // MOE_ROWS - starts // MOE_ROWS + 1, 0)
    item_end = jnp.cumsum(nb)
    item_start = item_end - nb
    total = item_end[-1]
    i = jnp.arange(n_items, dtype=jnp.int32)
    e = jnp.minimum(jnp.sum(item_end[None, :] <= i[:, None], axis=1), N_EXPERTS - 1).astype(jnp.int32)
    blk = (starts[e] // MOE_ROWS + (i - item_start[e])).astype(jnp.int32)
    valid = i < total
    lo = jnp.maximum(starts[e], blk * MOE_ROWS)
    hi = jnp.minimum(ends[e], (blk + 1) * MOE_ROWS)
    last = jnp.maximum(total - 1, 0)
    blk = jnp.where(valid, blk, blk[last])
    e = jnp.where(valid, e, e[last])
    lo = jnp.where(valid, lo, 0).astype(jnp.int32)
    hi = jnp.where(valid, hi, 0).astype(jnp.int32)
    prev = jnp.concatenate([jnp.full((1,), -1, jnp.int32), blk[:-1]])
    first = (blk != prev).astype(jnp.int32)
    return blk, e, lo, hi, first, starts


def _combine_kernel(pos_ref, pos_next_ref, h_ref, gate_ref, g_ref, ys_ref, o_first_ref, o_rest_ref, buf, sem,
                    *, n_first):
    tm = h_ref.shape[0]
    i = pl.program_id(0)
    n = pl.num_programs(0)
    cur = i % 2

    def fetch(p_ref, slot):
        def start(tt, c):
            for u in range(POS_GROUP):
                for k in range(TOP_K):
                    _row_copy(_slab(ys_ref, p_ref[0, tt, k * POS_GROUP + u], Y_SLABS),
                              buf.at[slot, k, pl.ds((tt * POS_GROUP + u) * BUF_PITCH, Y_SLABS), :],
                              sem.at[slot]).start(priority=k % 2)
            return c

        lax.fori_loop(0, tm // POS_GROUP, start, 0)

    @pl.when(i == 0)
    def _():
        fetch(pos_ref, 0)

    for slot in range(2):
        @pl.when((i + 1 < n) & (cur != slot))
        def _():
            fetch(pos_next_ref, slot)

    def wait(t, c):
        for k in range(TOP_K):
            _row_copy(_slab(ys_ref, 0, Y_SLABS), _slab(buf, 0, Y_SLABS, (cur, 0)), sem.at[cur]).wait()
        return c

    lax.fori_loop(0, tm, wait, 0, unroll=4)
    acc = h_ref[...]
    for k in range(TOP_K):
        rows = jnp.concatenate([buf[cur, k, pl.ds(c, tm, stride=BUF_PITCH), :] for c in range(Y_SLABS)], axis=1)
        acc = acc + gate_ref[:, k:k + 1] * rows
    res = _rms(acc, g_ref[...])

    @pl.when(i < n_first)
    def _():
        o_first_ref[...] = res

    @pl.when(i >= n_first)
    def _():
        o_rest_ref[...] = res


def _combine(pos, h2d, gates_t, g_final, ys, *, tm, rows_first):
    r = h2d.shape[0]
    n_tiles = r // tm
    n_first = rows_first // tm
    row = lambda i: (i, 0)
    return pl.pallas_call(
        functools.partial(_combine_kernel, n_first=n_first),
        grid=(n_tiles,),
        in_specs=[pl.BlockSpec((1, tm // POS_GROUP, TOP_K * POS_GROUP), lambda i: (i, 0, 0),
                               memory_space=pltpu.SMEM),
                  pl.BlockSpec((1, tm // POS_GROUP, TOP_K * POS_GROUP),
                               lambda i: (jnp.minimum(i + 1, n_tiles - 1), 0, 0), memory_space=pltpu.SMEM),
                  pl.BlockSpec((tm, D_MODEL), row),
                  pl.BlockSpec((tm, TOP_K), row),
                  pl.BlockSpec((1, D_MODEL), lambda i: (0, 0)),
                  pl.BlockSpec(memory_space=pl.ANY)],
        out_specs=[pl.BlockSpec((tm, D_MODEL), lambda i: (jnp.minimum(i, n_first - 1), 0)),
                   pl.BlockSpec((tm, D_MODEL), lambda i: (jnp.maximum(i - n_first, 0), 0))],
        out_shape=[jax.ShapeDtypeStruct((rows_first, D_MODEL), F32),
                   jax.ShapeDtypeStruct((r - rows_first, D_MODEL), F32)],
        scratch_shapes=[pltpu.VMEM((2, TOP_K, tm * BUF_PITCH, LANES), F32), pltpu.SemaphoreType.DMA((2,))],
        compiler_params=_cparams(("arbitrary",)),
        name="moe_combine",
    )(pos, pos, h2d, gates_t, g_final, ys)


def _moe(h2d, xn2d, w, rows_first):
    r = h2d.shape[0]
    tm_router = _pick_tile(r, 512)
    tm_scatter = _pick_tile(r, 512)
    tm_combine = _pick_tile(r, 256)
    idx, gates, rank, cnt = _router(xn2d, w["wr_t"], w["br"], w["tri"][:tm_router, :tm_router], tm=tm_router)
    counts = cnt[:, 0]
    blk, e, lo, hi, first, starts = _work_items(counts, TOP_K * r)
    hot = idx[:, :, None] == jnp.arange(N_EXPERTS, dtype=jnp.int32)
    pos = jnp.sum(jnp.where(hot, starts.astype(jnp.int32), 0), axis=-1) + rank
    tiles = lambda tm: (pos.reshape(TOP_K, -1, tm // POS_GROUP, POS_GROUP).transpose(1, 2, 0, 3)
                        .reshape(-1, tm // POS_GROUP, TOP_K * POS_GROUP))
    xs = _scatter(tiles(tm_scatter), xn2d, tm=tm_scatter)
    ys = _experts((blk, e, lo, hi, first), xs, w["wgu"], w["bgu"], w["wd"], w["bd"])
    return _combine(tiles(tm_combine), h2d, gates.T, w["g_final"], ys, tm=tm_combine, rows_first=rows_first)


def _pick_tile(n, cap):
    t = cap
    while n % t:
        t //= 2
    return t


def kernel(x_prompt, x_sample, mem_prompt, cache_a_k, cache_a_v, cache_b_k, cache_b_v, cache_mem_k, cache_mem_v, g_mix, w_in, rel_bias, lam_q1, lam_k1, lam_q2, lam_k2, g_subln, w_out, g_xq, g_xmem, w_xq, w_xk, w_xv, w_xo, g_moe, w_router, b_router, w_gate_up, b_gate_up, w_down, b_down, g_final):
    assert g_mix.shape[0] == 1, "single-layer kernel"
    b, s, d = x_prompt.shape
    db, dl, _ = x_sample.shape
    past = cache_b_k.shape[2]
    a_len = cache_a_k.shape[2]
    mlen = mem_prompt.shape[1]
    keep_a = min(BAND, s)
    assert d == D_MODEL and s % B_QBLK == 0 and s >= A_WIN and dl == CHUNK and past % CHUNK == 0
    assert B_QBLK == 2 * B_KBLK

    row1 = lambda v: v.reshape(1, -1).astype(F32)
    w_in_bf = w_in[0].astype(BF16)
    w_out_bf = w_out[0].astype(BF16)
    weights = dict(
        wr_t=w_router[0].T.astype(BF16), br=b_router[0].reshape(N_EXPERTS, 1).astype(F32),
        tri=jnp.asarray(np.triu(np.ones((512, 512), np.float32), 1), BF16),
        wgu=w_gate_up[0].astype(F32), bgu=b_gate_up[0].reshape(N_EXPERTS, 1, -1).astype(F32),
        wd=w_down[0].astype(F32), bd=b_down[0].reshape(N_EXPERTS, 1, -1).astype(F32),
        g_final=row1(g_final))
    lam = (jnp.exp(jnp.sum(lam_q1[0].astype(F32) * lam_k1[0].astype(F32)))
           - jnp.exp(jnp.sum(lam_q2[0].astype(F32) * lam_k2[0].astype(F32))) + LAM_INIT).reshape(1)
    table = _band_table(rel_bias[0])
    g_sub = row1(g_subln[0])

    tm_p = _pick_tile(s, 512)
    tabs_p = _rope_tables(np.arange(s))
    p_in = _inproj(x_prompt.reshape(b * s, d), row1(g_mix[0]), w_in_bf, *tabs_p,
                   tm=tm_p, n_pos_blocks=s // tm_p, tail_div=s // tm_p if keep_a == tm_p else 1, emit_vt=True)
    tabs_s = _rope_tables(np.tile(past + np.arange(dl), db))
    s_in = _inproj(x_sample.reshape(db * dl, d), row1(g_mix[0]), w_in_bf, *tabs_s,
                   tm=db * dl, n_pos_blocks=1, tail_div=1, emit_vt=False)

    def split(outs, nb, nl):
        return [o.reshape(nb, nl, A_WIDTH) for o in outs[:6]]

    aq, ak, _, bq, bk, _ = split(p_in, b, s)
    avt = p_in[10].reshape((b, s // A_QBLK) + p_in[10].shape[1:])
    bvt = p_in[11].reshape((b, s // B_KBLK) + p_in[11].shape[1:])
    saq, sak, sav, sbq, sbk, sbv = split(s_in, db, dl)

    oa = _attn_a_prompt(aq, ak, avt, table.transpose(0, 2, 1))
    ob = _attn_b_prompt(lam, bq, bk, bvt, g_sub)
    soa = _attn_a_sample(saq, cache_a_k[0].reshape(db, a_len * N_HEADS_A, HEAD_DIM),
                         cache_a_v[0].reshape(db, a_len * N_HEADS_A, HEAD_DIM), sak, sav, table[:, :dl, :BAND + dl])
    sob = _attn_b_sample(lam, sbq, cache_b_k[0].reshape(db, past, B_WIDTH),
                         cache_b_v[0].reshape(db, past * N_HEADS_B, 2 * HEAD_DIM), sbk, sbv, g_sub)

    mkf, mvf, mkb, mvb = _memkv(mem_prompt.reshape(b * mlen, d), row1(g_xmem[0]),
                                w_xk[0].astype(BF16), w_xv[0].astype(BF16), tm=mlen)
    post_w = (w_out_bf[:A_WIDTH], w_out_bf[A_WIDTH:], row1(g_xq[0]), w_xq[0].astype(BF16))
    tm_post = _pick_tile(math.gcd(s, b * s + db * dl), 512)
    rows_p, rows_s = b * s, db * dl
    sob_t = sob.transpose(0, 2, 1).reshape(db, N_HEADS_B, 2 * HEAD_DIM, dl)
    h_xn = _post(x_prompt.reshape(rows_p, d), oa, ob,
                 *post_w, mkb.reshape(b, mlen, d), mvb.reshape(b, mlen, d),
                 w_xo[0].astype(BF16), row1(g_moe[0]), tm=tm_post, tiles_per_batch=s // tm_post,
                 total_rows=rows_p + rows_s)
    h_all, xn_all = _post(x_sample.reshape(rows_s, d), soa.transpose(0, 2, 1), sob_t,
                          *post_w, cache_mem_k[0].reshape(db, mlen, d).astype(BF16),
                          cache_mem_v[0].reshape(db, mlen, d).astype(BF16),
                          w_xo[0].astype(BF16), row1(g_moe[0]), tm=dl, tiles_per_batch=1,
                          total_rows=rows_p + rows_s, row_offset=rows_p, into=h_xn)

    y_prompt, y_sample = _moe(h_all, xn_all, weights, rows_p)
    y_prompt = y_prompt.reshape(b, s, d)
    y_sample = y_sample.reshape(db, dl, d)

    def heads(x2d, nb, nl, nh):
        return x2d.reshape(1, nb, nl, nh, -1)

    akf, avf, bkf, bvf = p_in[6:10]
    if keep_a != tm_p:
        akf = akf.reshape(b, s, A_WIDTH)[:, s - keep_a:]
        avf = avf.reshape(b, s, A_WIDTH)[:, s - keep_a:]
    sakf, savf, sbkf, sbvf = s_in[6:]
    return (y_prompt, y_sample,
            heads(akf, b, keep_a, N_HEADS_A), heads(avf, b, keep_a, N_HEADS_A),
            heads(bkf, b, s, 2 * N_HEADS_B), heads(bvf, b, s, N_HEADS_B),
            heads(mkf, b, mlen, N_HEADS_M), heads(mvf, b, mlen, N_HEADS_M),
            heads(sakf, db, dl, N_HEADS_A), heads(savf, db, dl, N_HEADS_A),
            heads(sbkf, db, dl, 2 * N_HEADS_B), heads(sbvf, db, dl, N_HEADS_B))
```
